```python
import math
import jax, jax.numpy as jnp
from jax import lax
import numpy as np

D_MODEL = 2048
BATCH = 4
SEQ = 2048
DEPTH = 4
DEC_BATCH = 8
DEC_SEQ = 8
PAST_LEN = 16384
PAGE_SIZE = 128

N_MIXERS = 4
L_SSD = (DEPTH + 3) // 4
L_MLSTM = (DEPTH + 2) // 4
L_S5 = (DEPTH + 1) // 4
L_NSA = DEPTH // 4

DEEPNORM_ALPHA = (2.0 * DEPTH) ** 0.25
DEEPNORM_BETA = (8.0 * DEPTH) ** -0.25
LN_EPS = 1e-5
RMS_EPS = 1e-5
NEG_INF = -1e30
FORCE_SCORE = 1e4

SSD_D_INNER = 2 * D_MODEL
SSD_HEADDIM = 64
SSD_N_HEADS = SSD_D_INNER // SSD_HEADDIM
SSD_N_GROUPS = 8
SSD_D_STATE = 128
SSD_CONV_W = 4
SSD_CHUNK = 256
SSD_CONV_DIM = SSD_D_INNER + 2 * SSD_N_GROUPS * SSD_D_STATE
SSD_IN_DIM = SSD_D_INNER + SSD_CONV_DIM + SSD_N_HEADS

MLSTM_D_INNER = 2 * D_MODEL
MLSTM_N_HEADS = 4
MLSTM_HEAD_DIM = MLSTM_D_INNER // MLSTM_N_HEADS
MLSTM_QKV_BLOCK = 4
MLSTM_CONV_W = 4
MLSTM_CHUNK = 64

S5_GROUP = 16
S5_N_GROUPS = D_MODEL // S5_GROUP
S5_STATE = 64

NSA_N_HEADS = 16
NSA_N_KV = 4
NSA_HEAD_DIM = D_MODEL // NSA_N_HEADS
NSA_GQA = NSA_N_HEADS // NSA_N_KV
NSA_CMP_BLOCK = 32
NSA_CMP_STRIDE = 16
NSA_CMP_HIDDEN = NSA_HEAD_DIM
NSA_SEL_BLOCK = 64
NSA_N_SELECT = 16
NSA_WINDOW = 512
NSA_QBLOCK = 32
NSA_WBLOCK = 128

FFN_DIM = 5632
FFN_CONV_W = 3

kernel_name = 'hybrid_ssd_mlstm_s5_nsa_step'


def layer_norm(x, g, b):
    xf = x.astype(jnp.float32)
    mu = jnp.mean(xf, axis=-1, keepdims=True)
    var = jnp.mean(jnp.square(xf - mu), axis=-1, keepdims=True)
    return ((xf - mu) * lax.rsqrt(var + LN_EPS) * g + b).astype(x.dtype)


def group_rms_norm(y, g, n_groups):
    yf = y.astype(jnp.float32)
    yg = yf.reshape(yf.shape[:-1] + (n_groups, -1))
    yg = yg * lax.rsqrt(jnp.mean(yg * yg, axis=-1, keepdims=True) + RMS_EPS)
    return (yg.reshape(yf.shape) * g).astype(y.dtype)


def head_layer_norm(h, g):
    mu = jnp.mean(h, axis=-1, keepdims=True)
    var = jnp.mean(jnp.square(h - mu), axis=-1, keepdims=True)
    hn = (h - mu) * lax.rsqrt(var + LN_EPS)
    return hn.reshape(h.shape[:2] + (-1,)) * g


def causal_dwconv(x_hist, w, b):
    width = w.shape[0]
    T = x_hist.shape[1] - (width - 1)
    out = b
    for k in range(width):
        out = out + w[k] * x_hist[:, k:k + T]
    return out


def blockdiag(x, w):
    nb, bs, _ = w.shape
    y = jnp.einsum('btnc,ncd->btnd', x.reshape(x.shape[:2] + (nb, bs)), w)
    return y.reshape(x.shape[:2] + (nb * bs,))


def segsum(x):
    T = x.shape[-1]
    xr = jnp.broadcast_to(x[..., :, None], x.shape + (T,))
    strict = jnp.tril(jnp.ones((T, T), bool), -1)
    xs = jnp.cumsum(jnp.where(strict, xr, 0), axis=-2)
    return jnp.where(jnp.tril(jnp.ones((T, T), bool)), xs, -jnp.inf)


def ssd_scan(xs, dt, a, bm, cm, h0):
    b_, T = xs.shape[:2]
    cl = math.gcd(T, SSD_CHUNK)
    nc = T // cl

    def chunk(t):
        return t.reshape((b_, nc, cl) + t.shape[2:])

    xc, dtc, bc, cc = chunk(xs), chunk(dt), chunk(bm), chunk(cm)
    dt_t = jnp.moveaxis(dtc, 2, -1)
    da = dt_t * a[:, :, None]
    acs = jnp.cumsum(da, axis=-1)
    decay_in = jnp.exp(segsum(da))
    cb = jnp.einsum('bclgn,bcsgn->bcgls', cc, bc)
    w_diag = cb[:, :, :, None] * decay_in * dt_t[..., None, :]
    y_diag = jnp.einsum('bcgrls,bcsgrp->bclgrp', w_diag, xc)
    w_state = jnp.exp(acs[..., -1:] - acs) * dt_t
    states = jnp.einsum('bclgn,bcgrl,bclgrp->bcgrpn', bc, w_state, xc)
    states = jnp.concatenate([h0[:, None], states], axis=1)
    tot = jnp.pad(jnp.moveaxis(acs[..., -1], 1, -1), ((0, 0), (0, 0), (0, 0), (1, 0)))
    decay_chunk = jnp.exp(segsum(tot))
    new_states = jnp.einsum('bgrzc,bcgrpn->bzgrpn', decay_chunk, states)
    y_off = jnp.einsum('bclgn,bcgrpn,bcgrl->bclgrp', cc, new_states[:, :-1], jnp.exp(acs))
    return (y_diag + y_off).reshape(xs.shape), new_states[:, -1]


def ssd_mixer(x, conv_hist, h0, w_in, conv_w, conv_b, dt_bias, a_log, d_skip, norm_g, w_out):
    b_, T, _ = x.shape
    gn = SSD_N_GROUPS * SSD_D_STATE
    r = SSD_N_HEADS // SSD_N_GROUPS
    z, xbc, dt = jnp.split(x @ w_in, [SSD_D_INNER, SSD_D_INNER + SSD_CONV_DIM], axis=-1)
    hist = jnp.concatenate([conv_hist.astype(xbc.dtype), xbc], axis=1)
    xbc = jax.nn.silu(causal_dwconv(hist, conv_w, conv_b))
    xs, bm, cm = jnp.split(xbc, [SSD_D_INNER, SSD_D_INNER + gn], axis=-1)
    xs = xs.reshape(b_, T, SSD_N_GROUPS, r, SSD_HEADDIM)
    bm = bm.reshape(b_, T, SSD_N_GROUPS, SSD_D_STATE)
    cm = cm.reshape(b_, T, SSD_N_GROUPS, SSD_D_STATE)
    dt = jax.nn.softplus(dt + dt_bias).reshape(b_, T, SSD_N_GROUPS, r)
    a = -jnp.exp(a_log).reshape(SSD_N_GROUPS, r)
    h0 = h0.reshape(b_, SSD_N_GROUPS, r, SSD_HEADDIM, SSD_D_STATE).astype(xs.dtype)
    y, h_new = ssd_scan(xs, dt, a, bm, cm, h0)
    y = y + d_skip.reshape(SSD_N_GROUPS, r)[:, :, None] * xs
    y = group_rms_norm(y.reshape(b_, T, SSD_D_INNER) * jax.nn.silu(z), norm_g, SSD_N_GROUPS)
    return (y @ w_out, hist[:, -(SSD_CONV_W - 1):],
            h_new.reshape(b_, SSD_N_HEADS, SSD_HEADDIM, SSD_D_STATE))


def mlstm_chunked(q, k, v, i_pre, logf, c0, n0, m0):
    b_, T, H, _ = q.shape
    cl = math.gcd(T, MLSTM_CHUNK)
    nc = T // cl
    causal = jnp.tril(jnp.ones((cl, cl), bool))

    def chunk(t):
        return jnp.moveaxis(t.reshape((b_, nc, cl) + t.shape[2:]), 1, 0)

    def step(carry, inp):
        c, n, m = carry
        qc, kc, vc, ic, fc = inp
        bcum = jnp.cumsum(fc, axis=1)
        dmat = bcum[:, :, None, :] - bcum[:, None, :, :] + ic[:, None, :, :]
        dmat = jnp.where(causal[None, :, :, None], dmat, -jnp.inf)
        inter = m[:, None, :] + bcum
        m_t = jnp.maximum(inter, jnp.max(dmat, axis=2))
        w = jnp.exp(dmat - m_t[:, :, None, :])
        s = jnp.einsum('bthd,bshd->btsh', qc, kc) * w
        sc_inter = jnp.exp(inter - m_t)
        num = (jnp.einsum('btsh,bshe->bthe', s, vc)
               + sc_inter[..., None] * jnp.einsum('bthd,bhde->bthe', qc, c))
        den = jnp.sum(s, axis=2) + sc_inter * jnp.einsum('bthd,bhd->bth', qc, n)
        h = num / jnp.maximum(jnp.abs(den), jnp.exp(-m_t))[..., None]
        m_new = m_t[:, -1]
        decay_s = jnp.exp(bcum[:, -1:] - bcum + ic - m_new[:, None])
        sc_c = jnp.exp(m + bcum[:, -1] - m_new)
        c_new = sc_c[..., None, None] * c + jnp.einsum('bsh,bshd,bshe->bhde', decay_s, kc, vc)
        n_new = sc_c[..., None] * n + jnp.einsum('bsh,bshd->bhd', decay_s, kc)
        return (c_new, n_new, m_new), h

    f32 = jnp.float32
    carry0 = (c0.astype(f32), n0.astype(f32), m0.astype(f32))
    (c, n, m), hs = lax.scan(step, carry0, (chunk(q), chunk(k), chunk(v), chunk(i_pre), chunk(logf)))
    h = jnp.moveaxis(hs, 0, 1).reshape(b_, T, H, -1)
    return h, c, n, m


def mlstm_mixer(x, conv_hist, c0, n0, m0, w_up, conv_w, conv_b, w_q, w_k, w_v, w_if, b_if,
                skip, norm_g, w_down):
    b_, T, _ = x.shape
    f32 = jnp.float32
    xm, z = jnp.split(x @ w_up, 2, axis=-1)
    hist = jnp.concatenate([conv_hist.astype(xm.dtype), xm], axis=1)
    xc = jax.nn.silu(causal_dwconv(hist, conv_w, conv_b))
    q, k, v = blockdiag(xc, w_q), blockdiag(xc, w_k), blockdiag(xm, w_v)
    gates = (jnp.concatenate([q, k, v], axis=-1) @ w_if + b_if).astype(f32)
    i_pre, f_pre = gates[..., :MLSTM_N_HEADS], gates[..., MLSTM_N_HEADS:]

    def heads(t):
        return t.reshape(b_, T, MLSTM_N_HEADS, MLSTM_HEAD_DIM).astype(f32)

    h, c, n, m = mlstm_chunked(heads(q), heads(k) * (MLSTM_HEAD_DIM ** -0.5), heads(v),
                               i_pre, jax.nn.log_sigmoid(f_pre), c0, n0, m0)
    h = head_layer_norm(h, norm_g).astype(x.dtype)
    h = (h + skip * xc) * jax.nn.silu(z)
    return h @ w_down, hist[:, -(MLSTM_CONV_W - 1):], c, n, m


def complex_affine_combine(e1, e2):
    a1r, a1i, b1r, b1i = e1
    a2r, a2i, b2r, b2i = e2
    return (a1r * a2r - a1i * a2i, a1r * a2i + a1i * a2r,
            a2r * b1r - a2i * b1i + b2r, a2r * b1i + a2i * b1r + b2i)


def s5_mixer(x, h0, a_re, a_im, log_dt, b_re, b_im, c_re, c_im, d_skip, w_glu_a, w_glu_b):
    b_, T, _ = x.shape
    u = x.reshape(b_, T, S5_N_GROUPS, S5_GROUP)
    step = jnp.exp(log_dt)[:, None]
    mag = jnp.exp(a_re * step)
    ab_re, ab_im = mag * jnp.cos(a_im * step), mag * jnp.sin(a_im * step)
    den = a_re * a_re + a_im * a_im
    nr, ni = ab_re - 1.0, ab_im
    f_re = (nr * a_re + ni * a_im) / den
    f_im = (ni * a_re - nr * a_im) / den
    bb_re = f_re[..., None] * b_re - f_im[..., None] * b_im
    bb_im = f_re[..., None] * b_im + f_im[..., None] * b_re
    bu_re = jnp.einsum('gpc,btgc->btgp', bb_re, u)
    bu_im = jnp.einsum('gpc,btgc->btgp', bb_im, u)
    h0r, h0i = h0[..., 0].astype(bu_re.dtype), h0[..., 1].astype(bu_re.dtype)
    bu_re = bu_re.at[:, 0].add(ab_re * h0r - ab_im * h0i)
    bu_im = bu_im.at[:, 0].add(ab_re * h0i + ab_im * h0r)
    ar = jnp.broadcast_to(ab_re, bu_re.shape)
    ai = jnp.broadcast_to(ab_im, bu_im.shape)
    _, _, hr, hi = lax.associative_scan(complex_affine_combine, (ar, ai, bu_re, bu_im), axis=1)
    y = jnp.einsum('gcp,btgp->btgc', c_re, hr) - jnp.einsum('gcp,btgp->btgc', c_im, hi)
    y = y.reshape(b_, T, D_MODEL) + d_skip * x
    g = jax.nn.gelu(y)
    out = (g @ w_glu_a) * jax.nn.sigmoid(g @ w_glu_b)
    return out, jnp.stack([hr[:, -1], hi[:, -1]], axis=-1)


def nsa_compress(kv, w1, w2, pe):
    b_, L = kv.shape[:2]
    span = NSA_CMP_BLOCK // NSA_CMP_STRIDE
    n_str = L // NSA_CMP_STRIDE
    n_cmp = n_str - span + 1
    chunks = kv.reshape(b_, n_str, NSA_CMP_STRIDE, NSA_N_KV, NSA_HEAD_DIM)
    blocks = jnp.concatenate([chunks[:, s:s + n_cmp] for s in range(span)], axis=2) + pe[:, None, :]
    flat = jnp.moveaxis(blocks, 3, 2).reshape(b_, n_cmp, NSA_N_KV, NSA_CMP_BLOCK * NSA_HEAD_DIM)
    return jax.nn.gelu(flat @ w1) @ w2


def cmp_to_sel(imp, n_sel):
    r = NSA_SEL_BLOCK // NSA_CMP_STRIDE
    span = NSA_CMP_BLOCK // NSA_CMP_STRIDE
    pad = jnp.pad(imp, [(0, 0)] * (imp.ndim - 1) + [(span - 1, span - 1)])
    return sum(pad[..., s:s + r * (n_sel - 1) + 1:r] for s in range(r + span - 1))


def nsa_cmp_slc_block(qb, tq, kcmp, vcmp, ks_blk, vs_blk, n_top):
    scale = NSA_HEAD_DIM ** -0.5
    b_ = qb.shape[0]
    n_cmp = kcmp.shape[1]
    n_sel = ks_blk.shape[2]
    cmp_end = jnp.arange(n_cmp) * NSA_CMP_STRIDE + (NSA_CMP_BLOCK - 1)
    cmask = (cmp_end[None, :] <= tq[:, None])[None, :, None, None, :]
    s = jnp.einsum('bqhgd,bnhd->bqhgn', qb, kcmp).astype(jnp.float32) * scale
    p = jax.nn.softmax(jnp.where(cmask, s, NEG_INF), axis=-1)
    p = jnp.where(cmask, p, 0.0)
    o_cmp = jnp.einsum('bqhgn,bnhd->bqhgd', p.astype(vcmp.dtype), vcmp)
    imp = cmp_to_sel(jnp.sum(p, axis=3), n_sel)
    blk = jnp.arange(n_sel)[None, :]
    cur = (tq // NSA_SEL_BLOCK)[:, None]
    forced = (blk == 0) | (blk == cur) | (blk == cur - 1)
    valid = blk <= cur
    score = jnp.where(forced[None, :, None, :], FORCE_SCORE,
                      jnp.where(valid[None, :, None, :], imp, -FORCE_SCORE))
    _, idx = lax.top_k(score, n_top)
    bi = jnp.arange(b_)[:, None, None, None]
    hi = jnp.arange(NSA_N_KV)[None, None, :, None]
    kg = ks_blk[bi, hi, idx]
    vg = vs_blk[bi, hi, idx]
    kpos = idx[..., None] * NSA_SEL_BLOCK + jnp.arange(NSA_SEL_BLOCK)
    smask = (kpos <= tq[None, :, None, None, None])[:, :, :, None]
    s2 = jnp.einsum('bqhgd,bqhnsd->bqhgns', qb, kg).astype(jnp.float32) * scale
    s2 = jnp.where(smask, s2, NEG_INF)
    p2 = jax.nn.softmax(s2.reshape(s2.shape[:4] + (-1,)), axis=-1).reshape(s2.shape)
    o_slc = jnp.einsum('bqhgns,bqhnsd->bqhgd', p2.astype(vg.dtype), vg)
    return o_cmp, o_slc


def nsa_cmp_slc(q, rows, q_pos, w_cmp1, w_cmp2, cmp_pe):
    b_, L = rows.shape[:2]
    lp = -(-L // NSA_SEL_BLOCK) * NSA_SEL_BLOCK
    rows = jnp.pad(rows, ((0, 0), (0, lp - L), (0, 0), (0, 0), (0, 0)))
    kcmp = nsa_compress(rows[:, :, 0], w_cmp1[0], w_cmp2[0], cmp_pe[0])
    vcmp = nsa_compress(rows[:, :, 1], w_cmp1[1], w_cmp2[1], cmp_pe[1])
    n_sel = lp // NSA_SEL_BLOCK

    def sel_blocks(t):
        return jnp.moveaxis(t.reshape(b_, n_sel, NSA_SEL_BLOCK, NSA_N_KV, NSA_HEAD_DIM), 3, 1)

    ks_blk, vs_blk = sel_blocks(rows[:, :, 2]), sel_blocks(rows[:, :, 3])
    n_top = min(NSA_N_SELECT, n_sel)
    T = q.shape[1]
    qbs = math.gcd(T, NSA_QBLOCK)
    nqb = T // qbs
    qb = jnp.moveaxis(q.reshape((b_, nqb, qbs) + q.shape[2:]), 1, 0)
    pb = q_pos.reshape(nqb, qbs)
    o_cmp, o_slc = lax.map(
        lambda a: nsa_cmp_slc_block(a[0], a[1], kcmp, vcmp, ks_blk, vs_blk, n_top), (qb, pb))
    return (jnp.moveaxis(o_cmp, 0, 1).reshape(q.shape), jnp.moveaxis(o_slc, 0, 1).reshape(q.shape))


def window_attend(qb, kb, vb, q_pos, k_pos):
    s = jnp.einsum('bnqhgd,bnshd->bnqhgs', qb, kb).astype(jnp.float32) * (NSA_HEAD_DIM ** -0.5)
    diff = q_pos[:, :, None] - k_pos[:, None, :]
    mask = ((diff >= 0) & (diff < NSA_WINDOW) & (k_pos[:, None, :] >= 0))[None, :, :, None, None, :]
    p = jax.nn.softmax(jnp.where(mask, s, NEG_INF), axis=-1)
    return jnp.einsum('bnqhgs,bnshd->bnqhgd', p.astype(vb.dtype), vb)


def nsa_window_prompt(q, win_rows):
    b_, T = q.shape[:2]
    qbs = math.gcd(T, NSA_WBLOCK)
    nqb = T // qbs
    pad = jnp.pad(win_rows, ((0, 0), (NSA_WINDOW, 0), (0, 0), (0, 0), (0, 0)))
    idx = jnp.arange(nqb)[:, None] * qbs + jnp.arange(NSA_WINDOW + qbs)[None, :]
    kvb = pad[:, idx]
    q_pos = jnp.arange(T).reshape(nqb, qbs)
    o = window_attend(q.reshape((b_, nqb, qbs) + q.shape[2:]), kvb[:, :, :, 0], kvb[:, :, :, 1],
                      q_pos, idx - NSA_WINDOW)
    return o.reshape(q.shape)


def nsa_mixer(x, past_rows, win_buf, pos0, w_q, w_kv, w_gate, b_gate, w_cmp1, w_cmp2, cmp_pe, w_out):
    b_, T, _ = x.shape
    q = (x @ w_q).reshape(b_, T, NSA_N_KV, NSA_GQA, NSA_HEAD_DIM)
    kv = (x @ w_kv).reshape(b_, T, 6, NSA_N_KV, NSA_HEAD_DIM)
    rows = kv[:, :, :4]
    full = rows if past_rows is None else jnp.concatenate([past_rows.astype(rows.dtype), rows], axis=1)
    q_pos = pos0 + jnp.arange(T)
    o_cmp, o_slc = nsa_cmp_slc(q, full, q_pos, w_cmp1, w_cmp2, cmp_pe)
    win_rows = kv[:, :, 4:6]
    if win_buf is None:
        o_win = nsa_window_prompt(q, win_rows)
        win_new = win_rows[:, -min(NSA_WINDOW, T):]
    else:
        w_b = win_buf.shape[1]
        wk = jnp.concatenate([win_buf.astype(win_rows.dtype), win_rows], axis=1)
        k_pos = pos0 - w_b + jnp.arange(w_b + T)
        o_win = window_attend(q[:, None], wk[:, None, :, 0], wk[:, None, :, 1],
                              q_pos[None], k_pos[None])[:, 0]
        win_new = wk[:, -w_b:]
    gate = jax.nn.sigmoid(x @ w_gate + b_gate).reshape(b_, T, NSA_N_KV, NSA_GQA, 3)
    o = gate[..., 0:1] * o_cmp + gate[..., 1:2] * o_slc + gate[..., 2:3] * o_win
    return o.reshape(b_, T, NSA_N_HEADS * NSA_HEAD_DIM) @ w_out, rows, win_new


def conv_ffn(x, hist, w_up, conv_w, conv_b, w_down):
    a, g = jnp.split(x @ w_up, 2, axis=-1)
    ah = jnp.concatenate([hist.astype(a.dtype), a], axis=1)
    a = causal_dwconv(ah, conv_w, conv_b)
    return (jax.nn.gelu(a) * g) @ w_down, ah[:, -(FFN_CONV_W - 1):]


def setup_inputs(seed: int = 0) -> dict:
    key = jax.random.key(seed)
    keys = iter(jax.random.split(key, 96))
    f32 = jnp.float32

    def nrm(shape, scale=1.0):
        return jax.random.normal(next(keys), shape, f32) * scale

    def unif(shape, lo, hi):
        return jax.random.uniform(next(keys), shape, f32, lo, hi)

    beta = DEEPNORM_BETA
    n_pages = PAST_LEN // PAGE_SIZE
    n_used = DEC_BATCH * n_pages
    n_pool = n_used + max(1, n_used // 4)
    w_buf = min(NSA_WINDOW, PAST_LEN)
    page_table = jax.random.permutation(next(keys), n_pool)[:n_used].reshape(DEC_BATCH, n_pages).astype(jnp.int32)
    ssd_dt = jnp.exp(unif((L_SSD, SSD_N_HEADS), math.log(1e-3), math.log(1e-1)))
    s5_a_im = math.pi * jnp.arange(S5_STATE, dtype=f32) + nrm((L_S5, S5_N_GROUPS, S5_STATE), 0.01)
    mlstm_b_if = jnp.concatenate(
        [nrm((L_MLSTM, MLSTM_N_HEADS), 0.1),
         jnp.linspace(3.0, 6.0, MLSTM_N_HEADS, dtype=f32) + nrm((L_MLSTM, MLSTM_N_HEADS), 0.01)], axis=-1)
    nb = MLSTM_D_INNER // MLSTM_QKV_BLOCK
    hd_all = NSA_N_HEADS * NSA_HEAD_DIM
    return {
        'x_prompt': nrm((BATCH, SEQ, D_MODEL)),
        'x_sample': nrm((DEC_BATCH, DEC_SEQ, D_MODEL)),
        'cache_nsa': nrm((L_NSA, n_pool, PAGE_SIZE, 4, NSA_N_KV, NSA_HEAD_DIM)),
        'state_nsa_win': nrm((L_NSA, DEC_BATCH, w_buf, 2, NSA_N_KV, NSA_HEAD_DIM)),
        'state_ssd': nrm((L_SSD, DEC_BATCH, SSD_N_HEADS, SSD_HEADDIM, SSD_D_STATE), 0.1),
        'state_ssd_conv': nrm((L_SSD, DEC_BATCH, SSD_CONV_W - 1, SSD_CONV_DIM)),
        'state_mlstm_c': nrm((L_MLSTM, DEC_BATCH, MLSTM_N_HEADS, MLSTM_HEAD_DIM, MLSTM_HEAD_DIM), 0.05),
        'state_mlstm_n': nrm((L_MLSTM, DEC_BATCH, MLSTM_N_HEADS, MLSTM_HEAD_DIM), 0.05),
        'state_mlstm_m': nrm((L_MLSTM, DEC_BATCH, MLSTM_N_HEADS)),
        'state_mlstm_conv': nrm((L_MLSTM, DEC_BATCH, MLSTM_CONV_W - 1, MLSTM_D_INNER)),
        'state_s5': nrm((L_S5, DEC_BATCH, S5_N_GROUPS, S5_STATE, 2)),
        'state_ffn_conv': nrm((DEPTH, DEC_BATCH, FFN_CONV_W - 1, FFN_DIM)),
        'page_table': page_table,
        'ln_g': 1.0 + nrm((DEPTH, 2, D_MODEL), 0.02),
        'ln_b': nrm((DEPTH, 2, D_MODEL), 0.02),
        'ffn_w_up': nrm((DEPTH, D_MODEL, 2 * FFN_DIM), D_MODEL ** -0.5),
        'ffn_conv_w': nrm((DEPTH, FFN_CONV_W, FFN_DIM), FFN_CONV_W ** -0.5),
        'ffn_conv_b': nrm((DEPTH, FFN_DIM), 0.02),
        'ffn_w_down': nrm((DEPTH, FFN_DIM, D_MODEL), beta * FFN_DIM ** -0.5),
        'ssd_w_in': nrm((L_SSD, D_MODEL, SSD_IN_DIM), D_MODEL ** -0.5),
        'ssd_conv_w': nrm((L_SSD, SSD_CONV_W, SSD_CONV_DIM), SSD_CONV_W ** -0.5),
        'ssd_conv_b': nrm((L_SSD, SSD_CONV_DIM), 0.02),
        'ssd_dt_bias': ssd_dt + jnp.log(-jnp.expm1(-ssd_dt)),
        'ssd_a_log': jnp.log(unif((L_SSD, SSD_N_HEADS), 1.0, 16.0)),
        'ssd_d': 1.0 + nrm((L_SSD, SSD_N_HEADS), 0.02),
        'ssd_norm_g': 1.0 + nrm((L_SSD, SSD_D_INNER), 0.02),
        'ssd_w_out': nrm((L_SSD, SSD_D_INNER, D_MODEL), beta * SSD_D_INNER ** -0.5),
        'mlstm_w_up': nrm((L_MLSTM, D_MODEL, 2 * MLSTM_D_INNER), D_MODEL ** -0.5),
        'mlstm_conv_w': nrm((L_MLSTM, MLSTM_CONV_W, MLSTM_D_INNER), MLSTM_CONV_W ** -0.5),
        'mlstm_conv_b': nrm((L_MLSTM, MLSTM_D_INNER), 0.02),
        'mlstm_w_q': nrm((L_MLSTM, nb, MLSTM_QKV_BLOCK, MLSTM_QKV_BLOCK), MLSTM_QKV_BLOCK ** -0.5),
        'mlstm_w_k': nrm((L_MLSTM, nb, MLSTM_QKV_BLOCK, MLSTM_QKV_BLOCK), MLSTM_QKV_BLOCK ** -0.5),
        'mlstm_w_v': nrm((L_MLSTM, nb, MLSTM_QKV_BLOCK, MLSTM_QKV_BLOCK), MLSTM_QKV_BLOCK ** -0.5),
        'mlstm_w_if': nrm((L_MLSTM, 3 * MLSTM_D_INNER, 2 * MLSTM_N_HEADS), 0.1 * (3 * MLSTM_D_INNER) ** -0.5),
        'mlstm_b_if': mlstm_b_if,
        'mlstm_skip': 1.0 + nrm((L_MLSTM, MLSTM_D_INNER), 0.02),
        'mlstm_norm_g': 1.0 + nrm((L_MLSTM, MLSTM_D_INNER), 0.02),
        'mlstm_w_down': nrm((L_MLSTM, MLSTM_D_INNER, D_MODEL), beta * MLSTM_D_INNER ** -0.5),
        's5_a_re': -0.5 + nrm((L_S5, S5_N_GROUPS, S5_STATE), 0.01),
        's5_a_im': s5_a_im,
        's5_log_dt': unif((L_S5, S5_N_GROUPS), math.log(1e-3), math.log(1e-1)),
        's5_b_re': nrm((L_S5, S5_N_GROUPS, S5_STATE, S5_GROUP), (2 * S5_GROUP) ** -0.5),
        's5_b_im': nrm((L_S5, S5_N_GROUPS, S5_STATE, S5_GROUP), (2 * S5_GROUP) ** -0.5),
        's5_c_re': nrm((L_S5, S5_N_GROUPS, S5_GROUP, S5_STATE), (2 * S5_STATE) ** -0.5),
        's5_c_im': nrm((L_S5, S5_N_GROUPS, S5_GROUP, S5_STATE), (2 * S5_STATE) ** -0.5),
        's5_d': nrm((L_S5, D_MODEL), 0.5),
        's5_w_glu_a': nrm((L_S5, D_MODEL, D_MODEL), beta * D_MODEL ** -0.5),
        's5_w_glu_b': nrm((L_S5, D_MODEL, D_MODEL), D_MODEL ** -0.5),
        'nsa_w_q': nrm((L_NSA, D_MODEL, hd_all), D_MODEL ** -0.5),
        'nsa_w_kv': nrm((L_NSA, D_MODEL, 6 * NSA_N_KV * NSA_HEAD_DIM), D_MODEL ** -0.5),
        'nsa_w_gate': nrm((L_NSA, D_MODEL, 3 * NSA_N_HEADS), D_MODEL ** -0.5),
        'nsa_b_gate': nrm((L_NSA, 3 * NSA_N_HEADS), 0.02),
        'nsa_w_cmp1': nrm((L_NSA, 2, NSA_CMP_BLOCK * NSA_HEAD_DIM, NSA_CMP_HIDDEN), (NSA_CMP_BLOCK * NSA_HEAD_DIM) ** -0.5),
        'nsa_w_cmp2': nrm((L_NSA, 2, NSA_CMP_HIDDEN, NSA_HEAD_DIM), NSA_CMP_HIDDEN ** -0.5),
        'nsa_cmp_pe': nrm((L_NSA, 2, NSA_CMP_BLOCK, NSA_HEAD_DIM), 0.1),
        'nsa_w_out': nrm((L_NSA, hd_all, D_MODEL), beta * hd_all ** -0.5),
    }


def reference(x_prompt, x_sample, cache_nsa, state_nsa_win, state_ssd, state_ssd_conv, state_mlstm_c,
              state_mlstm_n, state_mlstm_m, state_mlstm_conv, state_s5, state_ffn_conv, page_table,
              ln_g, ln_b, ffn_w_up, ffn_conv_w, ffn_conv_b, ffn_w_down,
              ssd_w_in, ssd_conv_w, ssd_conv_b, ssd_dt_bias, ssd_a_log, ssd_d, ssd_norm_g, ssd_w_out,
              mlstm_w_up, mlstm_conv_w, mlstm_conv_b, mlstm_w_q, mlstm_w_k, mlstm_w_v, mlstm_w_if,
              mlstm_b_if, mlstm_skip, mlstm_norm_g, mlstm_w_down,
              s5_a_re, s5_a_im, s5_log_dt, s5_b_re, s5_b_im, s5_c_re, s5_c_im, s5_d, s5_w_glu_a, s5_w_glu_b,
              nsa_w_q, nsa_w_kv, nsa_w_gate, nsa_b_gate, nsa_w_cmp1, nsa_w_cmp2, nsa_cmp_pe, nsa_w_out):

    def trunk(x, sample):
        b_, T, _ = x.shape
        dt_ = x.dtype
        pos0 = PAST_LEN if sample else 0
        o_nsa, o_win, o_ssd, o_ssdc, o_mc, o_mn, o_mm, o_mconv, o_s5, o_ffn = ([] for _ in range(10))
        for i in range(DEPTH):
            kind, j = i % N_MIXERS, i // N_MIXERS
            if kind == 0:
                hist = state_ssd_conv[j] if sample else jnp.zeros((b_, SSD_CONV_W - 1, SSD_CONV_DIM), dt_)
                h0 = state_ssd[j] if sample else jnp.zeros((b_, SSD_N_HEADS, SSD_HEADDIM, SSD_D_STATE), dt_)
                y, hist_new, h_new = ssd_mixer(x, hist, h0, ssd_w_in[j], ssd_conv_w[j], ssd_conv_b[j],
                                               ssd_dt_bias[j], ssd_a_log[j], ssd_d[j], ssd_norm_g[j], ssd_w_out[j])
                o_ssd.append(h_new)
                o_ssdc.append(hist_new)
            elif kind == 1:
                hist = state_mlstm_conv[j] if sample else jnp.zeros((b_, MLSTM_CONV_W - 1, MLSTM_D_INNER), dt_)
                c0 = state_mlstm_c[j] if sample else jnp.zeros((b_, MLSTM_N_HEADS, MLSTM_HEAD_DIM, MLSTM_HEAD_DIM), dt_)
                n0 = state_mlstm_n[j] if sample else jnp.zeros((b_, MLSTM_N_HEADS, MLSTM_HEAD_DIM), dt_)
                m0 = state_mlstm_m[j] if sample else jnp.zeros((b_, MLSTM_N_HEADS), dt_)
                y, hist_new, c, n, m = mlstm_mixer(x, hist, c0, n0, m0, mlstm_w_up[j], mlstm_conv_w[j],
                                                   mlstm_conv_b[j], mlstm_w_q[j], mlstm_w_k[j], mlstm_w_v[j],
                                                   mlstm_w_if[j], mlstm_b_if[j], mlstm_skip[j],
                                                   mlstm_norm_g[j], mlstm_w_down[j])
                o_mc.append(c)
                o_mn.append(n)
                o_mm.append(m)
                o_mconv.append(hist_new)
            elif kind == 2:
                h0 = state_s5[j] if sample else jnp.zeros((b_, S5_N_GROUPS, S5_STATE, 2), dt_)
                y, h_new = s5_mixer(x, h0, s5_a_re[j], s5_a_im[j], s5_log_dt[j], s5_b_re[j], s5_b_im[j],
                                    s5_c_re[j], s5_c_im[j], s5_d[j], s5_w_glu_a[j], s5_w_glu_b[j])
                o_s5.append(h_new)
            else:
                past = cache_nsa[j, page_table].reshape((b_, -1) + cache_nsa.shape[3:]) if sample else None
                win = state_nsa_win[j] if sample else None
                y, rows, win_new = nsa_mixer(x, past, win, pos0, nsa_w_q[j], nsa_w_kv[j], nsa_w_gate[j],
                                             nsa_b_gate[j], nsa_w_cmp1[j], nsa_w_cmp2[j], nsa_cmp_pe[j],
                                             nsa_w_out[j])
                o_nsa.append(rows)
                o_win.append(win_new)
            x = layer_norm(DEEPNORM_ALPHA * x + y.astype(dt_), ln_g[i, 0], ln_b[i, 0])
            fhist = state_ffn_conv[i] if sample else jnp.zeros((b_, FFN_CONV_W - 1, FFN_DIM), dt_)
            y, fhist_new = conv_ffn(x, fhist, ffn_w_up[i], ffn_conv_w[i], ffn_conv_b[i], ffn_w_down[i])
            o_ffn.append(fhist_new)
            x = layer_norm(DEEPNORM_ALPHA * x + y.astype(dt_), ln_g[i, 1], ln_b[i, 1])
        st = jnp.stack
        return (x, st(o_nsa), st(o_win), st(o_ssd), st(o_ssdc), st(o_mc), st(o_mn), st(o_mm),
                st(o_mconv), st(o_s5), st(o_ffn))

    (y_prompt, nsa_p, win_p, ssd_p, ssdc_p, mc_p, mn_p, mm_p, mconv_p, s5_p, ffn_p) = trunk(x_prompt, False)
    (y_sample, nsa_s, win_s, ssd_s, ssdc_s, mc_s, mn_s, mm_s, mconv_s, s5_s, ffn_s) = trunk(x_sample, True)
    return (y_prompt, y_sample, nsa_p, nsa_s, win_p, win_s, ssd_p, ssd_s, ssdc_p, ssdc_s, mc_p, mc_s,
            mn_p, mn_s, mm_p, mm_s, mconv_p, mconv_s, s5_p, s5_s, ffn_p, ffn_s)
```

```python
import functools
import math

import jax
import jax.numpy as jnp
from jax import lax
from jax.experimental import pallas as pl
from jax.experimental.pallas import tpu as pltpu

D_MODEL = 2048
DEPTH = 4
PAST_LEN = 16384
N_MIXERS = 4

DEEPNORM_ALPHA = (2.0 * DEPTH) ** 0.25
LN_EPS = 1e-5
RMS_EPS = 1e-5
NEG_INF = -1e30
FORCE_SCORE = 1e4

SSD_D_INNER = 2 * D_MODEL
SSD_HEADDIM = 64
SSD_N_HEADS = SSD_D_INNER // SSD_HEADDIM
SSD_N_GROUPS = 8
SSD_D_STATE = 128
SSD_CONV_W = 4
SSD_CHUNK = 256
SSD_CONV_DIM = SSD_D_INNER + 2 * SSD_N_GROUPS * SSD_D_STATE

MLSTM_D_INNER = 2 * D_MODEL
MLSTM_N_HEADS = 4
MLSTM_HEAD_DIM = MLSTM_D_INNER // MLSTM_N_HEADS
MLSTM_CONV_W = 4
MLSTM_CHUNK = 64

S5_GROUP = 16
S5_N_GROUPS = D_MODEL // S5_GROUP
S5_STATE = 64

NSA_N_HEADS = 16
NSA_N_KV = 4
NSA_HEAD_DIM = D_MODEL // NSA_N_HEADS
NSA_GQA = NSA_N_HEADS // NSA_N_KV
NSA_CMP_BLOCK = 32
NSA_CMP_STRIDE = 16
NSA_SEL_BLOCK = 64
NSA_N_SELECT = 16
NSA_WINDOW = 512
NSA_QBLOCK = 32
NSA_WBLOCK = 128

FFN_DIM = 5632
FFN_CONV_W = 3

V7X_VMEM_LIMIT_BYTES = 48 * 1024 * 1024


def _mm_kernel(x_ref, w_ref, o_ref, acc_ref):
    k = pl.program_id(2)

    @pl.when(k == 0)
    def _():
        acc_ref[...] = jnp.zeros_like(acc_ref)

    acc_ref[...] += jnp.dot(x_ref[...].astype(jnp.bfloat16), w_ref[...].astype(jnp.bfloat16),
                            preferred_element_type=jnp.float32)

    @pl.when(k == pl.num_programs(2) - 1)
    def _():
        o_ref[...] = acc_ref[...]


def _pick(dim, target):
    if dim <= target:
        return dim
    t = target
    while dim % t:
        t //= 2
    return t


def matmul(x, w):
    lead = x.shape[:-1]
    K, N = w.shape
    x2 = x.reshape(-1, K)
    M = x2.shape[0]
    tm = _pick(M, 1024)
    tk = _pick(K, 512)
    tn = N if N <= 1024 else 1024
    grid = (M // tm, pl.cdiv(N, tn), K // tk)
    out = pl.pallas_call(
        _mm_kernel,
        grid=grid,
        in_specs=[pl.BlockSpec((tm, tk), lambda i, j, k: (i, k)),
                  pl.BlockSpec((tk, tn), lambda i, j, k: (k, j))],
        out_specs=pl.BlockSpec((tm, tn), lambda i, j, k: (i, j)),
        out_shape=jax.ShapeDtypeStruct((M, N), jnp.float32),
        scratch_shapes=[pltpu.VMEM((tm, tn), jnp.float32)],
        compiler_params=pltpu.CompilerParams(
            dimension_semantics=("parallel", "parallel", "arbitrary"),
            vmem_limit_bytes=V7X_VMEM_LIMIT_BYTES),
        name="matmul",
    )(x2, w)
    return out.reshape(lead + (N,))


def layer_norm(x, g, b):
    mu = jnp.mean(x, axis=-1, keepdims=True)
    var = jnp.mean(jnp.square(x - mu), axis=-1, keepdims=True)
    return (x - mu) * lax.rsqrt(var + LN_EPS) * g + b


def group_rms_norm(y, g, n_groups):
    yg = y.reshape(y.shape[:-1] + (n_groups, -1))
    yg = yg * lax.rsqrt(jnp.mean(yg * yg, axis=-1, keepdims=True) + RMS_EPS)
    return yg.reshape(y.shape) * g


def head_layer_norm(h, g):
    mu = jnp.mean(h, axis=-1, keepdims=True)
    var = jnp.mean(jnp.square(h - mu), axis=-1, keepdims=True)
    hn = (h - mu) * lax.rsqrt(var + LN_EPS)
    return hn.reshape(h.shape[:2] + (-1,)) * g


def causal_dwconv(x_hist, w, b):
    width = w.shape[0]
    T = x_hist.shape[1] - (width - 1)
    out = b
    for k in range(width):
        out = out + w[k] * x_hist[:, k:k + T]
    return out


def blockdiag(x, w):
    nb, bs, _ = w.shape
    y = jnp.einsum('btnc,ncd->btnd', x.reshape(x.shape[:2] + (nb, bs)), w)
    return y.reshape(x.shape[:2] + (nb * bs,))


def segsum(x):
    T = x.shape[-1]
    xr = jnp.broadcast_to(x[..., :, None], x.shape + (T,))
    strict = jnp.tril(jnp.ones((T, T), bool), -1)
    xs = jnp.cumsum(jnp.where(strict, xr, 0), axis=-2)
    return jnp.where(jnp.tril(jnp.ones((T, T), bool)), xs, -jnp.inf)


def ssd_scan(xs, dt, a, bm, cm, h0):
    b_, T = xs.shape[:2]
    cl = math.gcd(T, SSD_CHUNK)
    nc = T // cl

    def chunk(t):
        return t.reshape((b_, nc, cl) + t.shape[2:])

    xc, dtc, bc, cc = chunk(xs), chunk(dt), chunk(bm), chunk(cm)
    dt_t = jnp.moveaxis(dtc, 2, -1)
    da = dt_t * a[:, :, None]
    acs = jnp.cumsum(da, axis=-1)
    decay_in = jnp.exp(segsum(da))
    cb = jnp.einsum('bclgn,bcsgn->bcgls', cc, bc)
    w_diag = cb[:, :, :, None] * decay_in * dt_t[..., None, :]
    y_diag = jnp.einsum('bcgrls,bcsgrp->bclgrp', w_diag, xc)
    w_state = jnp.exp(acs[..., -1:] - acs) * dt_t
    states = jnp.einsum('bclgn,bcgrl,bclgrp->bcgrpn', bc, w_state, xc)
    states = jnp.concatenate([h0[:, None], states], axis=1)
    tot = jnp.pad(jnp.moveaxis(acs[..., -1], 1, -1), ((0, 0), (0, 0), (0, 0), (1, 0)))
    decay_chunk = jnp.exp(segsum(tot))
    new_states = jnp.einsum('bgrzc,bcgrpn->bzgrpn', decay_chunk, states)
    y_off = jnp.einsum('bclgn,bcgrpn,bcgrl->bclgrp', cc, new_states[:, :-1], jnp.exp(acs))
    return (y_diag + y_off).reshape(xs.shape), new_states[:, -1]


def ssd_mixer(x, conv_hist, h0, w_in, conv_w, conv_b, dt_bias, a_log, d_skip, norm_g, w_out):
    b_, T, _ = x.shape
    gn = SSD_N_GROUPS * SSD_D_STATE
    r = SSD_N_HEADS // SSD_N_GROUPS
    z, xbc, dt = jnp.split(matmul(x, w_in), [SSD_D_INNER, SSD_D_INNER + SSD_CONV_DIM], axis=-1)
    hist = jnp.concatenate([conv_hist, xbc], axis=1)
    xbc = jax.nn.silu(causal_dwconv(hist, conv_w, conv_b))
    xs, bm, cm = jnp.split(xbc, [SSD_D_INNER, SSD_D_INNER + gn], axis=-1)
    xs = xs.reshape(b_, T, SSD_N_GROUPS, r, SSD_HEADDIM)
    bm = bm.reshape(b_, T, SSD_N_GROUPS, SSD_D_STATE)
    cm = cm.reshape(b_, T, SSD_N_GROUPS, SSD_D_STATE)
    dt = jax.nn.softplus(dt + dt_bias).reshape(b_, T, SSD_N_GROUPS, r)
    a = -jnp.exp(a_log).reshape(SSD_N_GROUPS, r)
    h0 = h0.reshape(b_, SSD_N_GROUPS, r, SSD_HEADDIM, SSD_D_STATE)
    y, h_new = ssd_scan(xs, dt, a, bm, cm, h0)
    y = y + d_skip.reshape(SSD_N_GROUPS, r)[:, :, None] * xs
    y = group_rms_norm(y.reshape(b_, T, SSD_D_INNER) * jax.nn.silu(z), norm_g, SSD_N_GROUPS)
    return (matmul(y, w_out), hist[:, -(SSD_CONV_W - 1):],
            h_new.reshape(b_, SSD_N_HEADS, SSD_HEADDIM, SSD_D_STATE))


def mlstm_chunked(q, k, v, i_pre, logf, c0, n0, m0):
    b_, T, H, _ = q.shape
    cl = math.gcd(T, MLSTM_CHUNK)
    nc = T // cl
    causal = jnp.tril(jnp.ones((cl, cl), bool))

    def chunk(t):
        return jnp.moveaxis(t.reshape((b_, nc, cl) + t.shape[2:]), 1, 0)

    def step(carry, inp):
        c, n, m = carry
        qc, kc, vc, ic, fc = inp
        bcum = jnp.cumsum(fc, axis=1)
        dmat = bcum[:, :, None, :] - bcum[:, None, :, :] + ic[:, None, :, :]
        dmat = jnp.where(causal[None, :, :, None], dmat, -jnp.inf)
        inter = m[:, None, :] + bcum
        m_t = jnp.maximum(inter, jnp.max(dmat, axis=2))
        w = jnp.exp(dmat - m_t[:, :, None, :])
        s = jnp.einsum('bthd,bshd->btsh', qc, kc) * w
        sc_inter = jnp.exp(inter - m_t)
        num = (jnp.einsum('btsh,bshe->bthe', s, vc)
               + sc_inter[..., None] * jnp.einsum('bthd,bhde->bthe', qc, c))
        den = jnp.sum(s, axis=2) + sc_inter * jnp.einsum('bthd,bhd->bth', qc, n)
        h = num / jnp.maximum(jnp.abs(den), jnp.exp(-m_t))[..., None]
        m_new = m_t[:, -1]
        decay_s = jnp.exp(bcum[:, -1:] - bcum + ic - m_new[:, None])
        sc_c = jnp.exp(m + bcum[:, -1] - m_new)
        c_new = sc_c[..., None, None] * c + jnp.einsum('bsh,bshd,bshe->bhde', decay_s, kc, vc)
        n_new = sc_c[..., None] * n + jnp.einsum('bsh,bshd->bhd', decay_s, kc)
        return (c_new, n_new, m_new), h

    (c, n, m), hs = lax.scan(step, (c0, n0, m0),
                             (chunk(q), chunk(k), chunk(v), chunk(i_pre), chunk(logf)))
    h = jnp.moveaxis(hs, 0, 1).reshape(b_, T, H, -1)
    return h, c, n, m


def mlstm_mixer(x, conv_hist, c0, n0, m0, w_up, conv_w, conv_b, w_q, w_k, w_v, w_if, b_if,
                skip, norm_g, w_down):
    b_, T, _ = x.shape
    xm, z = jnp.split(matmul(x, w_up), 2, axis=-1)
    hist = jnp.concatenate([conv_hist, xm], axis=1)
    xc = jax.nn.silu(causal_dwconv(hist, conv_w, conv_b))
    q, k, v = blockdiag(xc, w_q), blockdiag(xc, w_k), blockdiag(xm, w_v)
    gates = matmul(jnp.concatenate([q, k, v], axis=-1), w_if) + b_if
    i_pre, f_pre = gates[..., :MLSTM_N_HEADS], gates[..., MLSTM_N_HEADS:]

    def heads(t):
        return t.reshape(b_, T, MLSTM_N_HEADS, MLSTM_HEAD_DIM)

    h, c, n, m = mlstm_chunked(heads(q), heads(k) * (MLSTM_HEAD_DIM ** -0.5), heads(v),
                               i_pre, jax.nn.log_sigmoid(f_pre), c0, n0, m0)
    h = head_layer_norm(h, norm_g)
    h = (h + skip * xc) * jax.nn.silu(z)
    return matmul(h, w_down), hist[:, -(MLSTM_CONV_W - 1):], c, n, m


def complex_affine_combine(e1, e2):
    a1r, a1i, b1r, b1i = e1
    a2r, a2i, b2r, b2i = e2
    return (a1r * a2r - a1i * a2i, a1r * a2i + a1i * a2r,
            a2r * b1r - a2i * b1i + b2r, a2r * b1i + a2i * b1r + b2i)


def s5_mixer(x, h0, a_re, a_im, log_dt, b_re, b_im, c_re, c_im, d_skip, w_glu_a, w_glu_b):
    b_, T, _ = x.shape
    u = x.reshape(b_, T, S5_N_GROUPS, S5_GROUP)
    step = jnp.exp(log_dt)[:, None]
    mag = jnp.exp(a_re * step)
    ab_re, ab_im = mag * jnp.cos(a_im * step), mag * jnp.sin(a_im * step)
    den = a_re * a_re + a_im * a_im
    nr, ni = ab_re - 1.0, ab_im
    f_re = (nr * a_re + ni * a_im) / den
    f_im = (ni * a_re - nr * a_im) / den
    bb_re = f_re[..., None] * b_re - f_im[..., None] * b_im
    bb_im = f_re[..., None] * b_im + f_im[..., None] * b_re
    bu_re = jnp.einsum('gpc,btgc->btgp', bb_re, u)
    bu_im = jnp.einsum('gpc,btgc->btgp', bb_im, u)
    h0r, h0i = h0[..., 0], h0[..., 1]
    bu_re = bu_re.at[:, 0].add(ab_re * h0r - ab_im * h0i)
    bu_im = bu_im.at[:, 0].add(ab_re * h0i + ab_im * h0r)
    ar = jnp.broadcast_to(ab_re, bu_re.shape)
    ai = jnp.broadcast_to(ab_im, bu_im.shape)
    _, _, hr, hi = lax.associative_scan(complex_affine_combine, (ar, ai, bu_re, bu_im), axis=1)
    y = jnp.einsum('gcp,btgp->btgc', c_re, hr) - jnp.einsum('gcp,btgp->btgc', c_im, hi)
    y = y.reshape(b_, T, D_MODEL) + d_skip * x
    g = jax.nn.gelu(y)
    out = matmul(g, w_glu_a) * jax.nn.sigmoid(matmul(g, w_glu_b))
    return out, jnp.stack([hr[:, -1], hi[:, -1]], axis=-1)


def nsa_compress(kv, w1, w2, pe):
    b_, L = kv.shape[:2]
    span = NSA_CMP_BLOCK // NSA_CMP_STRIDE
    n_str = L // NSA_CMP_STRIDE
    n_cmp = n_str - span + 1
    chunks = kv.reshape(b_, n_str, NSA_CMP_STRIDE, NSA_N_KV, NSA_HEAD_DIM)
    blocks = jnp.concatenate([chunks[:, s:s + n_cmp] for s in range(span)], axis=2) + pe[:, None, :]
    flat = jnp.moveaxis(blocks, 3, 2).reshape(b_, n_cmp, NSA_N_KV, NSA_CMP_BLOCK * NSA_HEAD_DIM)
    return jax.nn.gelu(flat @ w1) @ w2


def cmp_to_sel(imp, n_sel):
    r = NSA_SEL_BLOCK // NSA_CMP_STRIDE
    span = NSA_CMP_BLOCK // NSA_CMP_STRIDE
    pad = jnp.pad(imp, [(0, 0)] * (imp.ndim - 1) + [(span - 1, span - 1)])
    return sum(pad[..., s:s + r * (n_sel - 1) + 1:r] for s in range(r + span - 1))


def nsa_cmp_slc_block(qb, tq, kcmp, vcmp, ks_blk, vs_blk, n_top):
    scale = NSA_HEAD_DIM ** -0.5
    b_ = qb.shape[0]
    n_cmp = kcmp.shape[1]
    n_sel = ks_blk.shape[2]
    cmp_end = jnp.arange(n_cmp) * NSA_CMP_STRIDE + (NSA_CMP_BLOCK - 1)
    cmask = (cmp_end[None, :] <= tq[:, None])[None, :, None, None, :]
    s = jnp.einsum('bqhgd,bnhd->bqhgn', qb, kcmp) * scale
    p = jax.nn.softmax(jnp.where(cmask, s, NEG_INF), axis=-1)
    p = jnp.where(cmask, p, 0.0)
    o_cmp = jnp.einsum('bqhgn,bnhd->bqhgd', p, vcmp)
    imp = cmp_to_sel(jnp.sum(p, axis=3), n_sel)
    blk = jnp.arange(n_sel)[None, :]
    cur = (tq // NSA_SEL_BLOCK)[:, None]
    forced = (blk == 0) | (blk == cur) | (blk == cur - 1)
    valid = blk <= cur
    score = jnp.where(forced[None, :, None, :], FORCE_SCORE,
                      jnp.where(valid[None, :, None, :], imp, -FORCE_SCORE))
    _, idx = lax.top_k(score, n_top)
    bi = jnp.arange(b_)[:, None, None, None]
    hi = jnp.arange(NSA_N_KV)[None, None, :, None]
    kg = ks_blk[bi, hi, idx]
    vg = vs_blk[bi, hi, idx]
    kpos = idx[..., None] * NSA_SEL_BLOCK + jnp.arange(NSA_SEL_BLOCK)
    smask = (kpos <= tq[None, :, None, None, None])[:, :, :, None]
    s2 = jnp.einsum('bqhgd,bqhnsd->bqhgns', qb, kg) * scale
    s2 = jnp.where(smask, s2, NEG_INF)
    p2 = jax.nn.softmax(s2.reshape(s2.shape[:4] + (-1,)), axis=-1).reshape(s2.shape)
    o_slc = jnp.einsum('bqhgns,bqhnsd->bqhgd', p2, vg)
    return o_cmp, o_slc


def nsa_cmp_slc(q, rows, q_pos, w_cmp1, w_cmp2, cmp_pe):
    b_, L = rows.shape[:2]
    lp = -(-L // NSA_SEL_BLOCK) * NSA_SEL_BLOCK
    rows = jnp.pad(rows, ((0, 0), (0, lp - L), (0, 0), (0, 0), (0, 0)))
    kcmp = nsa_compress(rows[:, :, 0], w_cmp1[0], w_cmp2[0], cmp_pe[0])
    vcmp = nsa_compress(rows[:, :, 1], w_cmp1[1], w_cmp2[1], cmp_pe[1])
    n_sel = lp // NSA_SEL_BLOCK

    def sel_blocks(t):
        return jnp.moveaxis(t.reshape(b_, n_sel, NSA_SEL_BLOCK, NSA_N_KV, NSA_HEAD_DIM), 3, 1)

    ks_blk, vs_blk = sel_blocks(rows[:, :, 2]), sel_blocks(rows[:, :, 3])
    n_top = min(NSA_N_SELECT, n_sel)
    T = q.shape[1]
    qbs = math.gcd(T, NSA_QBLOCK)
    nqb = T // qbs
    qb = jnp.moveaxis(q.reshape((b_, nqb, qbs) + q.shape[2:]), 1, 0)
    pb = q_pos.reshape(nqb, qbs)
    o_cmp, o_slc = lax.map(
        lambda a: nsa_cmp_slc_block(a[0], a[1], kcmp, vcmp, ks_blk, vs_blk, n_top), (qb, pb))
    return (jnp.moveaxis(o_cmp, 0, 1).reshape(q.shape), jnp.moveaxis(o_slc, 0, 1).reshape(q.shape))


def window_attend(qb, kb, vb, q_pos, k_pos):
    s = jnp.einsum('bnqhgd,bnshd->bnqhgs', qb, kb) * (NSA_HEAD_DIM ** -0.5)
    diff = q_pos[:, :, None] - k_pos[:, None, :]
    mask = ((diff >= 0) & (diff < NSA_WINDOW) & (k_pos[:, None, :] >= 0))[None, :, :, None, None, :]
    p = jax.nn.softmax(jnp.where(mask, s, NEG_INF), axis=-1)
    return jnp.einsum('bnqhgs,bnshd->bnqhgd', p, vb)


def nsa_window_prompt(q, win_rows):
    b_, T = q.shape[:2]
    qbs = math.gcd(T, NSA_WBLOCK)
    nqb = T // qbs
    pad = jnp.pad(win_rows, ((0, 0), (NSA_WINDOW, 0), (0, 0), (0, 0), (0, 0)))
    idx = jnp.arange(nqb)[:, None] * qbs + jnp.arange(NSA_WINDOW + qbs)[None, :]
    kvb = pad[:, idx]
    q_pos = jnp.arange(T).reshape(nqb, qbs)
    o = window_attend(q.reshape((b_, nqb, qbs) + q.shape[2:]), kvb[:, :, :, 0], kvb[:, :, :, 1],
                      q_pos, idx - NSA_WINDOW)
    return o.reshape(q.shape)


def nsa_mixer(x, past_rows, win_buf, pos0, w_q, w_kv, w_gate, b_gate, w_cmp1, w_cmp2, cmp_pe, w_out):
    b_, T, _ = x.shape
    q = matmul(x, w_q).reshape(b_, T, NSA_N_KV, NSA_GQA, NSA_HEAD_DIM)
    kv = matmul(x, w_kv).reshape(b_, T, 6, NSA_N_KV, NSA_HEAD_DIM)
    rows = kv[:, :, :4]
    full = rows if past_rows is None else jnp.concatenate([past_rows, rows], axis=1)
    q_pos = pos0 + jnp.arange(T)
    o_cmp, o_slc = nsa_cmp_slc(q, full, q_pos, w_cmp1, w_cmp2, cmp_pe)
    win_rows = kv[:, :, 4:6]
    if win_buf is None:
        o_win = nsa_window_prompt(q, win_rows)
        win_new = win_rows[:, -min(NSA_WINDOW, T):]
    else:
        w_b = win_buf.shape[1]
        wk = jnp.concatenate([win_buf, win_rows], axis=1)
        k_pos = pos0 - w_b + jnp.arange(w_b + T)
        o_win = window_attend(q[:, None], wk[:, None, :, 0], wk[:, None, :, 1],
                              q_pos[None], k_pos[None])[:, 0]
        win_new = wk[:, -w_b:]
    gate = jax.nn.sigmoid(matmul(x, w_gate) + b_gate).reshape(b_, T, NSA_N_KV, NSA_GQA, 3)
    o = gate[..., 0:1] * o_cmp + gate[..., 1:2] * o_slc + gate[..., 2:3] * o_win
    return matmul(o.reshape(b_, T, NSA_N_HEADS * NSA_HEAD_DIM), w_out), rows, win_new


def conv_ffn(x, hist, w_up, conv_w, conv_b, w_down):
    a, g = jnp.split(matmul(x, w_up), 2, axis=-1)
    ah = jnp.concatenate([hist, a], axis=1)
    a = causal_dwconv(ah, conv_w, conv_b)
    return matmul(jax.nn.gelu(a) * g, w_down), ah[:, -(FFN_CONV_W - 1):]


def kernel(x_prompt, x_sample, cache_nsa, state_nsa_win, state_ssd, state_ssd_conv, state_mlstm_c,
           state_mlstm_n, state_mlstm_m, state_mlstm_conv, state_s5, state_ffn_conv, page_table,
           ln_g, ln_b, ffn_w_up, ffn_conv_w, ffn_conv_b, ffn_w_down,
           ssd_w_in, ssd_conv_w, ssd_conv_b, ssd_dt_bias, ssd_a_log, ssd_d, ssd_norm_g, ssd_w_out,
           mlstm_w_up, mlstm_conv_w, mlstm_conv_b, mlstm_w_q, mlstm_w_k, mlstm_w_v, mlstm_w_if,
           mlstm_b_if, mlstm_skip, mlstm_norm_g, mlstm_w_down,
           s5_a_re, s5_a_im, s5_log_dt, s5_b_re, s5_b_im, s5_c_re, s5_c_im, s5_d, s5_w_glu_a, s5_w_glu_b,
           nsa_w_q, nsa_w_kv, nsa_w_gate, nsa_b_gate, nsa_w_cmp1, nsa_w_cmp2, nsa_cmp_pe, nsa_w_out):

    def trunk(x, sample):
        b_, T, _ = x.shape
        dt_ = x.dtype
        pos0 = PAST_LEN if sample else 0
        o_nsa, o_win, o_ssd, o_ssdc, o_mc, o_mn, o_mm, o_mconv, o_s5, o_ffn = ([] for _ in range(10))
        for i in range(DEPTH):
            kind, j = i % N_MIXERS, i // N_MIXERS
            if kind == 0:
                hist = state_ssd_conv[j] if sample else jnp.zeros((b_, SSD_CONV_W - 1, SSD_CONV_DIM), dt_)
                h0 = state_ssd[j] if sample else jnp.zeros((b_, SSD_N_HEADS, SSD_HEADDIM, SSD_D_STATE), dt_)
                y, hist_new, h_new = ssd_mixer(x, hist, h0, ssd_w_in[j], ssd_conv_w[j], ssd_conv_b[j],
                                               ssd_dt_bias[j], ssd_a_log[j], ssd_d[j], ssd_norm_g[j], ssd_w_out[j])
                o_ssd.append(h_new)
                o_ssdc.append(hist_new)
            elif kind == 1:
                hist = state_mlstm_conv[j] if sample else jnp.zeros((b_, MLSTM_CONV_W - 1, MLSTM_D_INNER), dt_)
                c0 = state_mlstm_c[j] if sample else jnp.zeros((b_, MLSTM_N_HEADS, MLSTM_HEAD_DIM, MLSTM_HEAD_DIM), dt_)
                n0 = state_mlstm_n[j] if sample else jnp.zeros((b_, MLSTM_N_HEADS, MLSTM_HEAD_DIM), dt_)
                m0 = state_mlstm_m[j] if sample else jnp.zeros((b_, MLSTM_N_HEADS), dt_)
                y, hist_new, c, n, m = mlstm_mixer(x, hist, c0, n0, m0, mlstm_w_up[j], mlstm_conv_w[j],
                                                   mlstm_conv_b[j], mlstm_w_q[j], mlstm_w_k[j], mlstm_w_v[j],
                                                   mlstm_w_if[j], mlstm_b_if[j], mlstm_skip[j],
                                                   mlstm_norm_g[j], mlstm_w_down[j])
                o_mc.append(c)
                o_mn.append(n)
                o_mm.append(m)
                o_mconv.append(hist_new)
            elif kind == 2:
                h0 = state_s5[j] if sample else jnp.zeros((b_, S5_N_GROUPS, S5_STATE, 2), dt_)
                y, h_new = s5_mixer(x, h0, s5_a_re[j], s5_a_im[j], s5_log_dt[j], s5_b_re[j], s5_b_im[j],
                                    s5_c_re[j], s5_c_im[j], s5_d[j], s5_w_glu_a[j], s5_w_glu_b[j])
                o_s5.append(h_new)
            else:
                past = cache_nsa[j, page_table].reshape((b_, -1) + cache_nsa.shape[3:]) if sample else None
                win = state_nsa_win[j] if sample else None
                y, rows, win_new = nsa_mixer(x, past, win, pos0, nsa_w_q[j], nsa_w_kv[j], nsa_w_gate[j],
                                             nsa_b_gate[j], nsa_w_cmp1[j], nsa_w_cmp2[j], nsa_cmp_pe[j],
                                             nsa_w_out[j])
                o_nsa.append(rows)
                o_win.append(win_new)
            x = layer_norm(DEEPNORM_ALPHA * x + y, ln_g[i, 0], ln_b[i, 0])
            fhist = state_ffn_conv[i] if sample else jnp.zeros((b_, FFN_CONV_W - 1, FFN_DIM), dt_)
            y, fhist_new = conv_ffn(x, fhist, ffn_w_up[i], ffn_conv_w[i], ffn_conv_b[i], ffn_w_down[i])
            o_ffn.append(fhist_new)
            x = layer_norm(DEEPNORM_ALPHA * x + y, ln_g[i, 1], ln_b[i, 1])
        st = jnp.stack
        return (x, st(o_nsa), st(o_win), st(o_ssd), st(o_ssdc), st(o_mc), st(o_mn), st(o_mm),
                st(o_mconv), st(o_s5), st(o_ffn))

    (y_prompt, nsa_p, win_p, ssd_p, ssdc_p, mc_p, mn_p, mm_p, mconv_p, s5_p, ffn_p) = trunk(x_prompt, False)
    (y_sample, nsa_s, win_s, ssd_s, ssdc_s, mc_s, mn_s, mm_s, mconv_s, s5_s, ffn_s) = trunk(x_sample, True)
    return (y_prompt, y_sample, nsa_p, nsa_s, win_p, win_s, ssd_p, ssd_s, ssdc_p, ssdc_s, mc_p, mc_s,
            mn_p, mn_s, mm_p, mm_s, mconv_p, mconv_s, s5_p, s5_s, ffn_p, ffn_s)
```

```python
import functools
import math

import jax
import jax.numpy as jnp
from jax import lax
from jax.experimental import pallas as pl
from jax.experimental.pallas import tpu as pltpu

D_MODEL = 2048
DEPTH = 4
PAST_LEN = 16384
N_MIXERS = 4

DEEPNORM_ALPHA = (2.0 * DEPTH) ** 0.25
LN_EPS = 1e-5
RMS_EPS = 1e-5
NEG_INF = -1e30
FORCE_SCORE = 1e4

SSD_D_INNER = 2 * D_MODEL
SSD_HEADDIM = 64
SSD_N_HEADS = SSD_D_INNER // SSD_HEADDIM
SSD_N_GROUPS = 8
SSD_D_STATE = 128
SSD_CONV_W = 4
SSD_CHUNK = 256
SSD_CONV_DIM = SSD_D_INNER + 2 * SSD_N_GROUPS * SSD_D_STATE

MLSTM_D_INNER = 2 * D_MODEL
MLSTM_N_HEADS = 4
MLSTM_HEAD_DIM = MLSTM_D_INNER // MLSTM_N_HEADS
MLSTM_CONV_W = 4
MLSTM_CHUNK = 64

S5_GROUP = 16
S5_N_GROUPS = D_MODEL // S5_GROUP
S5_STATE = 64

NSA_N_HEADS = 16
NSA_N_KV = 4
NSA_HEAD_DIM = D_MODEL // NSA_N_HEADS
NSA_GQA = NSA_N_HEADS // NSA_N_KV
NSA_CMP_BLOCK = 32
NSA_CMP_STRIDE = 16
NSA_SEL_BLOCK = 64
NSA_N_SELECT = 16
NSA_WINDOW = 512
NSA_QBLOCK = 32
NSA_WBLOCK = 128

FFN_DIM = 5632
FFN_CONV_W = 3

V7X_VMEM_LIMIT_BYTES = 48 * 1024 * 1024


def _mm_kernel(x_ref, w_ref, o_ref, acc_ref):
    k = pl.program_id(2)

    @pl.when(k == 0)
    def _():
        acc_ref[...] = jnp.zeros_like(acc_ref)

    acc_ref[...] += jnp.dot(x_ref[...].astype(jnp.bfloat16), w_ref[...].astype(jnp.bfloat16),
                            preferred_element_type=jnp.float32)

    @pl.when(k == pl.num_programs(2) - 1)
    def _():
        o_ref[...] = acc_ref[...]


def _pick(dim, target):
    if dim <= target:
        return dim
    t = target
    while dim % t:
        t //= 2
    return t


def matmul(x, w):
    lead = x.shape[:-1]
    K, N = w.shape
    x2 = x.reshape(-1, K)
    M = x2.shape[0]
    tm = _pick(M, 1024)
    tk = _pick(K, 512)
    tn = N if N <= 1024 else 1024
    grid = (M // tm, pl.cdiv(N, tn), K // tk)
    out = pl.pallas_call(
        _mm_kernel,
        grid=grid,
        in_specs=[pl.BlockSpec((tm, tk), lambda i, j, k: (i, k)),
                  pl.BlockSpec((tk, tn), lambda i, j, k: (k, j))],
        out_specs=pl.BlockSpec((tm, tn), lambda i, j, k: (i, j)),
        out_shape=jax.ShapeDtypeStruct((M, N), jnp.float32),
        scratch_shapes=[pltpu.VMEM((tm, tn), jnp.float32)],
        compiler_params=pltpu.CompilerParams(
            dimension_semantics=("parallel", "parallel", "arbitrary"),
            vmem_limit_bytes=V7X_VMEM_LIMIT_BYTES),
        name="matmul",
    )(x2, w)
    return out.reshape(lead + (N,))


def layer_norm(x, g, b):
    mu = jnp.mean(x, axis=-1, keepdims=True)
    var = jnp.mean(jnp.square(x - mu), axis=-1, keepdims=True)
    return (x - mu) * lax.rsqrt(var + LN_EPS) * g + b


def group_rms_norm(y, g, n_groups):
    yg = y.reshape(y.shape[:-1] + (n_groups, -1))
    yg = yg * lax.rsqrt(jnp.mean(yg * yg, axis=-1, keepdims=True) + RMS_EPS)
    return yg.reshape(y.shape) * g


def head_layer_norm(h, g):
    mu = jnp.mean(h, axis=-1, keepdims=True)
    var = jnp.mean(jnp.square(h - mu), axis=-1, keepdims=True)
    hn = (h - mu) * lax.rsqrt(var + LN_EPS)
    return hn.reshape(h.shape[:2] + (-1,)) * g


def causal_dwconv(x_hist, w, b):
    width = w.shape[0]
    T = x_hist.shape[1] - (width - 1)
    out = b
    for k in range(width):
        out = out + w[k] * x_hist[:, k:k + T]
    return out


def blockdiag(x, w):
    nb, bs, _ = w.shape
    y = jnp.einsum('btnc,ncd->btnd', x.reshape(x.shape[:2] + (nb, bs)), w)
    return y.reshape(x.shape[:2] + (nb * bs,))


def segsum(x):
    T = x.shape[-1]
    xr = jnp.broadcast_to(x[..., :, None], x.shape + (T,))
    strict = jnp.tril(jnp.ones((T, T), bool), -1)
    xs = jnp.cumsum(jnp.where(strict, xr, 0), axis=-2)
    return jnp.where(jnp.tril(jnp.ones((T, T), bool)), xs, -jnp.inf)


def ssd_scan(xs, dt, a, bm, cm, h0):
    b_, T = xs.shape[:2]
    cl = math.gcd(T, SSD_CHUNK)
    nc = T // cl

    def chunk(t):
        return t.reshape((b_, nc, cl) + t.shape[2:])

    xc, dtc, bc, cc = chunk(xs), chunk(dt), chunk(bm), chunk(cm)
    dt_t = jnp.moveaxis(dtc, 2, -1)
    da = dt_t * a[:, :, None]
    acs = jnp.cumsum(da, axis=-1)
    decay_in = jnp.exp(segsum(da))
    cb = jnp.einsum('bclgn,bcsgn->bcgls', cc, bc)
    w_diag = cb[:, :, :, None] * decay_in * dt_t[..., None, :]
    y_diag = jnp.einsum('bcgrls,bcsgrp->bclgrp', w_diag, xc)
    w_state = jnp.exp(acs[..., -1:] - acs) * dt_t
    states = jnp.einsum('bclgn,bcgrl,bclgrp->bcgrpn', bc, w_state, xc)
    states = jnp.concatenate([h0[:, None], states], axis=1)
    tot = jnp.pad(jnp.moveaxis(acs[..., -1], 1, -1), ((0, 0), (0, 0), (0, 0), (1, 0)))
    decay_chunk = jnp.exp(segsum(tot))
    new_states = jnp.einsum('bgrzc,bcgrpn->bzgrpn', decay_chunk, states)
    y_off = jnp.einsum('bclgn,bcgrpn,bcgrl->bclgrp', cc, new_states[:, :-1], jnp.exp(acs))
    return (y_diag + y_off).reshape(xs.shape), new_states[:, -1]


def ssd_mixer(x, conv_hist, h0, w_in, conv_w, conv_b, dt_bias, a_log, d_skip, norm_g, w_out):
    b_, T, _ = x.shape
    gn = SSD_N_GROUPS * SSD_D_STATE
    r = SSD_N_HEADS // SSD_N_GROUPS
    z, xbc, dt = jnp.split(matmul(x, w_in), [SSD_D_INNER, SSD_D_INNER + SSD_CONV_DIM], axis=-1)
    hist = jnp.concatenate([conv_hist, xbc], axis=1)
    xbc = jax.nn.silu(causal_dwconv(hist, conv_w, conv_b))
    xs, bm, cm = jnp.split(xbc, [SSD_D_INNER, SSD_D_INNER + gn], axis=-1)
    xs = xs.reshape(b_, T, SSD_N_GROUPS, r, SSD_HEADDIM)
    bm = bm.reshape(b_, T, SSD_N_GROUPS, SSD_D_STATE)
    cm = cm.reshape(b_, T, SSD_N_GROUPS, SSD_D_STATE)
    dt = jax.nn.softplus(dt + dt_bias).reshape(b_, T, SSD_N_GROUPS, r)
    a = -jnp.exp(a_log).reshape(SSD_N_GROUPS, r)
    h0 = h0.reshape(b_, SSD_N_GROUPS, r, SSD_HEADDIM, SSD_D_STATE)
    y, h_new = ssd_scan(xs, dt, a, bm, cm, h0)
    y = y + d_skip.reshape(SSD_N_GROUPS, r)[:, :, None] * xs
    y = group_rms_norm(y.reshape(b_, T, SSD_D_INNER) * jax.nn.silu(z), norm_g, SSD_N_GROUPS)
    return (matmul(y, w_out), hist[:, -(SSD_CONV_W - 1):],
            h_new.reshape(b_, SSD_N_HEADS, SSD_HEADDIM, SSD_D_STATE))


def mlstm_chunked(q, k, v, i_pre, logf, c0, n0, m0):
    b_, T, H, _ = q.shape
    cl = math.gcd(T, MLSTM_CHUNK)
    nc = T // cl
    causal = jnp.tril(jnp.ones((cl, cl), bool))

    def chunk(t):
        return jnp.moveaxis(t.reshape((b_, nc, cl) + t.shape[2:]), 1, 0)

    def step(carry, inp):
        c, n, m = carry
        qc, kc, vc, ic, fc = inp
        bcum = jnp.cumsum(fc, axis=1)
        dmat = bcum[:, :, None, :] - bcum[:, None, :, :] + ic[:, None, :, :]
        dmat = jnp.where(causal[None, :, :, None], dmat, -jnp.inf)
        inter = m[:, None, :] + bcum
        m_t = jnp.maximum(inter, jnp.max(dmat, axis=2))
        w = jnp.exp(dmat - m_t[:, :, None, :])
        s = jnp.einsum('bthd,bshd->btsh', qc, kc) * w
        sc_inter = jnp.exp(inter - m_t)
        num = (jnp.einsum('btsh,bshe->bthe', s, vc)
               + sc_inter[..., None] * jnp.einsum('bthd,bhde->bthe', qc, c))
        den = jnp.sum(s, axis=2) + sc_inter * jnp.einsum('bthd,bhd->bth', qc, n)
        h = num / jnp.maximum(jnp.abs(den), jnp.exp(-m_t))[..., None]
        m_new = m_t[:, -1]
        decay_s = jnp.exp(bcum[:, -1:] - bcum + ic - m_new[:, None])
        sc_c = jnp.exp(m + bcum[:, -1] - m_new)
        c_new = sc_c[..., None, None] * c + jnp.einsum('bsh,bshd,bshe->bhde', decay_s, kc, vc)
        n_new = sc_c[..., None] * n + jnp.einsum('bsh,bshd->bhd', decay_s, kc)
        return (c_new, n_new, m_new), h

    (c, n, m), hs = lax.scan(step, (c0, n0, m0),
                             (chunk(q), chunk(k), chunk(v), chunk(i_pre), chunk(logf)))
    h = jnp.moveaxis(hs, 0, 1).reshape(b_, T, H, -1)
    return h, c, n, m


def mlstm_mixer(x, conv_hist, c0, n0, m0, w_up, conv_w, conv_b, w_q, w_k, w_v, w_if, b_if,
                skip, norm_g, w_down):
    b_, T, _ = x.shape
    xm, z = jnp.split(matmul(x, w_up), 2, axis=-1)
    hist = jnp.concatenate([conv_hist, xm], axis=1)
    xc = jax.nn.silu(causal_dwconv(hist, conv_w, conv_b))
    q, k, v = blockdiag(xc, w_q), blockdiag(xc, w_k), blockdiag(xm, w_v)
    gates = matmul(jnp.concatenate([q, k, v], axis=-1), w_if) + b_if
    i_pre, f_pre = gates[..., :MLSTM_N_HEADS], gates[..., MLSTM_N_HEADS:]

    def heads(t):
        return t.reshape(b_, T, MLSTM_N_HEADS, MLSTM_HEAD_DIM)

    h, c, n, m = mlstm_chunked(heads(q), heads(k) * (MLSTM_HEAD_DIM ** -0.5), heads(v),
                               i_pre, jax.nn.log_sigmoid(f_pre), c0, n0, m0)
    h = head_layer_norm(h, norm_g)
    h = (h + skip * xc) * jax.nn.silu(z)
    return matmul(h, w_down), hist[:, -(MLSTM_CONV_W - 1):], c, n, m


S5_D_STATE = S5_N_GROUPS * S5_STATE
S5_PACK = 8
S5_N_PACKS = S5_N_GROUPS // S5_PACK
S5_SCAN_ROWS = 8
S5_SCAN_LANES = 256
S5_TIME_TILE = 256


def _cmul(ar, ai, br, bi):
    return ar * br - ai * bi, ar * bi + ai * br


def _s5_kernel(x_ref, bre_ref, bim_ref, cre_ref, cim_ref, pw_ref, h0r_ref, h0i_ref, d_ref,
               g_ref, hr_out_ref, hi_out_ref, bur_ref, bui_ref, hr_ref, hi_ref):
    tt = pl.program_id(1)
    rows = x_ref.shape[0]
    pk_in = S5_PACK * S5_GROUP
    pk_st = S5_PACK * S5_STATE

    @pl.when(tt == 0)
    def _():
        hr_ref[...] = jnp.broadcast_to(h0r_ref[...], hr_ref.shape)
        hi_ref[...] = jnp.broadcast_to(h0i_ref[...], hi_ref.shape)

    for c in range(S5_N_PACKS):
        xc = x_ref[:, c * pk_in:(c + 1) * pk_in].astype(jnp.bfloat16)
        bur_ref[:, c * pk_st:(c + 1) * pk_st] = jnp.dot(xc, bre_ref[c], preferred_element_type=jnp.float32)
        bui_ref[:, c * pk_st:(c + 1) * pk_st] = jnp.dot(xc, bim_ref[c], preferred_element_type=jnp.float32)

    def col_body(cb, carry):
        cs = pl.ds(pl.multiple_of(cb * S5_SCAN_LANES, S5_SCAN_LANES), S5_SCAN_LANES)
        stages = [(pw_ref[2 * k, :, cs], pw_ref[2 * k + 1, :, cs], 1 << k) for k in range(3)]
        lr, li = pw_ref[6, :, cs], pw_ref[7, :, cs]

        def row_body(r, h):
            hr, hi = h
            rs = pl.ds(pl.multiple_of(r * S5_SCAN_ROWS, S5_SCAN_ROWS), S5_SCAN_ROWS)
            vr, vi = bur_ref[rs, cs], bui_ref[rs, cs]
            for mr, mi, s in stages:
                pr, pi = _cmul(mr, mi, pltpu.roll(vr, s, 0), pltpu.roll(vi, s, 0))
                vr, vi = vr + pr, vi + pi
            pr, pi = _cmul(lr, li, hr, hi)
            vr, vi = vr + pr, vi + pi
            bur_ref[rs, cs] = vr
            bui_ref[rs, cs] = vi
            last = S5_SCAN_ROWS - 1
            return (jnp.broadcast_to(vr[last:, :], vr.shape), jnp.broadcast_to(vi[last:, :], vi.shape))

        hr, hi = lax.fori_loop(0, rows // S5_SCAN_ROWS, row_body, (hr_ref[:, cs], hi_ref[:, cs]))
        hr_ref[:, cs] = hr
        hi_ref[:, cs] = hi
        return carry

    lax.fori_loop(0, S5_D_STATE // S5_SCAN_LANES, col_body, 0)

    for c in range(S5_N_PACKS):
        hr = bur_ref[:, c * pk_st:(c + 1) * pk_st].astype(jnp.bfloat16)
        hi = bui_ref[:, c * pk_st:(c + 1) * pk_st].astype(jnp.bfloat16)
        y = (jnp.dot(hr, cre_ref[c], preferred_element_type=jnp.float32)
             - jnp.dot(hi, cim_ref[c], preferred_element_type=jnp.float32))
        cols = slice(c * pk_in, (c + 1) * pk_in)
        g_ref[:, cols] = jax.nn.gelu(y + d_ref[:, cols] * x_ref[:, cols])

    @pl.when(tt == pl.num_programs(1) - 1)
    def _():
        hr_out_ref[...] = hr_ref[0:1, :]
        hi_out_ref[...] = hi_ref[0:1, :]


def _block_diag_packs(w):
    g, r, c = w.shape
    eye = jnp.eye(S5_PACK, dtype=w.dtype)
    wb = jnp.einsum('kgrc,gh->kgrhc', w.reshape(g // S5_PACK, S5_PACK, r, c), eye)
    return wb.reshape(g // S5_PACK, S5_PACK * r, S5_PACK * c).astype(jnp.bfloat16)


def s5_mixer(x, h0, a_re, a_im, log_dt, b_re, b_im, c_re, c_im, d_skip, w_glu_a, w_glu_b):
    b_, T, _ = x.shape
    step = jnp.exp(log_dt)[:, None]
    mag = jnp.exp(a_re * step)
    ab_re, ab_im = mag * jnp.cos(a_im * step), mag * jnp.sin(a_im * step)
    den = a_re * a_re + a_im * a_im
    nr, ni = ab_re - 1.0, ab_im
    f_re = (nr * a_re + ni * a_im) / den
    f_im = (ni * a_re - nr * a_im) / den
    bb_re = f_re[..., None] * b_re - f_im[..., None] * b_im
    bb_im = f_re[..., None] * b_im + f_im[..., None] * b_re
    bre = _block_diag_packs(jnp.swapaxes(bb_re, 1, 2))
    bim = _block_diag_packs(jnp.swapaxes(bb_im, 1, 2))
    cre = _block_diag_packs(jnp.swapaxes(c_re, 1, 2))
    cim = _block_diag_packs(jnp.swapaxes(c_im, 1, 2))
    l1 = (ab_re.reshape(-1), ab_im.reshape(-1))
    l2 = _cmul(*l1, *l1)
    l4 = _cmul(*l2, *l2)
    row = jnp.arange(S5_SCAN_ROWS)[:, None]
    pw = []
    for s, (pr, pi) in ((1, l1), (2, l2), (4, l4)):
        pw += [jnp.where(row >= s, pr[None, :], 0.0), jnp.where(row >= s, pi[None, :], 0.0)]
    acc = [l1]
    for _ in range(S5_SCAN_ROWS - 1):
        acc.append(_cmul(*acc[-1], *l1))
    pw += [jnp.stack([a[0] for a in acc]), jnp.stack([a[1] for a in acc])]
    pw = jnp.stack(pw)

    tt = min(T, S5_TIME_TILE)
    h0r = h0[..., 0].reshape(b_, 1, S5_D_STATE)
    h0i = h0[..., 1].reshape(b_, 1, S5_D_STATE)
    pk_in, pk_st = S5_PACK * S5_GROUP, S5_PACK * S5_STATE

    def const3(b, t):
        return (0, 0, 0)

    state_spec = pl.BlockSpec((None, 1, S5_D_STATE), lambda b, t: (b, 0, 0))
    g, hr, hi = pl.pallas_call(
        _s5_kernel,
        grid=(b_, T // tt),
        in_specs=[pl.BlockSpec((None, tt, D_MODEL), lambda b, t: (b, t, 0)),
                  pl.BlockSpec((S5_N_PACKS, pk_in, pk_st), const3),
                  pl.BlockSpec((S5_N_PACKS, pk_in, pk_st), const3),
                  pl.BlockSpec((S5_N_PACKS, pk_st, pk_in), const3),
                  pl.BlockSpec((S5_N_PACKS, pk_st, pk_in), const3),
                  pl.BlockSpec((8, S5_SCAN_ROWS, S5_D_STATE), const3),
                  state_spec, state_spec,
                  pl.BlockSpec((1, D_MODEL), lambda b, t: (0, 0))],
        out_specs=[pl.BlockSpec((None, tt, D_MODEL), lambda b, t: (b, t, 0)), state_spec, state_spec],
        out_shape=[jax.ShapeDtypeStruct((b_, T, D_MODEL), jnp.float32),
                   jax.ShapeDtypeStruct((b_, 1, S5_D_STATE), jnp.float32),
                   jax.ShapeDtypeStruct((b_, 1, S5_D_STATE), jnp.float32)],
        scratch_shapes=[pltpu.VMEM((tt, S5_D_STATE), jnp.float32),
                        pltpu.VMEM((tt, S5_D_STATE), jnp.float32),
                        pltpu.VMEM((S5_SCAN_ROWS, S5_D_STATE), jnp.float32),
                        pltpu.VMEM((S5_SCAN_ROWS, S5_D_STATE), jnp.float32)],
        compiler_params=pltpu.CompilerParams(
            dimension_semantics=("parallel", "arbitrary"),
            vmem_limit_bytes=V7X_VMEM_LIMIT_BYTES),
        name="s5_scan",
    )(x, bre, bim, cre, cim, pw, h0r, h0i, d_skip.reshape(1, D_MODEL))
    out = matmul(g, w_glu_a) * jax.nn.sigmoid(matmul(g, w_glu_b))
    h_new = jnp.stack([hr.reshape(b_, S5_N_GROUPS, S5_STATE), hi.reshape(b_, S5_N_GROUPS, S5_STATE)], axis=-1)
    return out, h_new


def nsa_compress(kv, w1, w2, pe):
    b_, L = kv.shape[:2]
    span = NSA_CMP_BLOCK // NSA_CMP_STRIDE
    n_str = L // NSA_CMP_STRIDE
    n_cmp = n_str - span + 1
    chunks = kv.reshape(b_, n_str, NSA_CMP_STRIDE, NSA_N_KV, NSA_HEAD_DIM)
    blocks = jnp.concatenate([chunks[:, s:s + n_cmp] for s in range(span)], axis=2) + pe[:, None, :]
    flat = jnp.moveaxis(blocks, 3, 2).reshape(b_, n_cmp, NSA_N_KV, NSA_CMP_BLOCK * NSA_HEAD_DIM)
    return jax.nn.gelu(flat @ w1) @ w2


def cmp_to_sel(imp, n_sel):
    r = NSA_SEL_BLOCK // NSA_CMP_STRIDE
    span = NSA_CMP_BLOCK // NSA_CMP_STRIDE
    pad = jnp.pad(imp, [(0, 0)] * (imp.ndim - 1) + [(span - 1, span - 1)])
    return sum(pad[..., s:s + r * (n_sel - 1) + 1:r] for s in range(r + span - 1))


def nsa_cmp_slc_block(qb, tq, kcmp, vcmp, ks_blk, vs_blk, n_top):
    scale = NSA_HEAD_DIM ** -0.5
    b_ = qb.shape[0]
    n_cmp = kcmp.shape[1]
    n_sel = ks_blk.shape[2]
    cmp_end = jnp.arange(n_cmp) * NSA_CMP_STRIDE + (NSA_CMP_BLOCK - 1)
    cmask = (cmp_end[None, :] <= tq[:, None])[None, :, None, None, :]
    s = jnp.einsum('bqhgd,bnhd->bqhgn', qb, kcmp) * scale
    p = jax.nn.softmax(jnp.where(cmask, s, NEG_INF), axis=-1)
    p = jnp.where(cmask, p, 0.0)
    o_cmp = jnp.einsum('bqhgn,bnhd->bqhgd', p, vcmp)
    imp = cmp_to_sel(jnp.sum(p, axis=3), n_sel)
    blk = jnp.arange(n_sel)[None, :]
    cur = (tq // NSA_SEL_BLOCK)[:, None]
    forced = (blk == 0) | (blk == cur) | (blk == cur - 1)
    valid = blk <= cur
    score = jnp.where(forced[None, :, None, :], FORCE_SCORE,
                      jnp.where(valid[None, :, None, :], imp, -FORCE_SCORE))
    _, idx = lax.top_k(score, n_top)
    bi = jnp.arange(b_)[:, None, None, None]
    hi = jnp.arange(NSA_N_KV)[None, None, :, None]
    kg = ks_blk[bi, hi, idx]
    vg = vs_blk[bi, hi, idx]
    kpos = idx[..., None] * NSA_SEL_BLOCK + jnp.arange(NSA_SEL_BLOCK)
    smask = (kpos <= tq[None, :, None, None, None])[:, :, :, None]
    s2 = jnp.einsum('bqhgd,bqhnsd->bqhgns', qb, kg) * scale
    s2 = jnp.where(smask, s2, NEG_INF)
    p2 = jax.nn.softmax(s2.reshape(s2.shape[:4] + (-1,)), axis=-1).reshape(s2.shape)
    o_slc = jnp.einsum('bqhgns,bqhnsd->bqhgd', p2, vg)
    return o_cmp, o_slc


def nsa_cmp_slc(q, rows, q_pos, w_cmp1, w_cmp2, cmp_pe):
    b_, L = rows.shape[:2]
    lp = -(-L // NSA_SEL_BLOCK) * NSA_SEL_BLOCK
    rows = jnp.pad(rows, ((0, 0), (0, lp - L), (0, 0), (0, 0), (0, 0)))
    kcmp = nsa_compress(rows[:, :, 0], w_cmp1[0], w_cmp2[0], cmp_pe[0])
    vcmp = nsa_compress(rows[:, :, 1], w_cmp1[1], w_cmp2[1], cmp_pe[1])
    n_sel = lp // NSA_SEL_BLOCK

    def sel_blocks(t):
        return jnp.moveaxis(t.reshape(b_, n_sel, NSA_SEL_BLOCK, NSA_N_KV, NSA_HEAD_DIM), 3, 1)

    ks_blk, vs_blk = sel_blocks(rows[:, :, 2]), sel_blocks(rows[:, :, 3])
    n_top = min(NSA_N_SELECT, n_sel)
    T = q.shape[1]
    qbs = math.gcd(T, NSA_QBLOCK)
    nqb = T // qbs
    qb = jnp.moveaxis(q.reshape((b_, nqb, qbs) + q.shape[2:]), 1, 0)
    pb = q_pos.reshape(nqb, qbs)
    o_cmp, o_slc = lax.map(
        lambda a: nsa_cmp_slc_block(a[0], a[1], kcmp, vcmp, ks_blk, vs_blk, n_top), (qb, pb))
    return (jnp.moveaxis(o_cmp, 0, 1).reshape(q.shape), jnp.moveaxis(o_slc, 0, 1).reshape(q.shape))


def window_attend(qb, kb, vb, q_pos, k_pos):
    s = jnp.einsum('bnqhgd,bnshd->bnqhgs', qb, kb) * (NSA_HEAD_DIM ** -0.5)
    diff = q_pos[:, :, None] - k_pos[:, None, :]
    mask = ((diff >= 0) & (diff < NSA_WINDOW) & (k_pos[:, None, :] >= 0))[None, :, :, None, None, :]
    p = jax.nn.softmax(jnp.where(mask, s, NEG_INF), axis=-1)
    return jnp.einsum('bnqhgs,bnshd->bnqhgd', p, vb)


def nsa_window_prompt(q, win_rows):
    b_, T = q.shape[:2]
    qbs = math.gcd(T, NSA_WBLOCK)
    nqb = T // qbs
    pad = jnp.pad(win_rows, ((0, 0), (NSA_WINDOW, 0), (0, 0), (0, 0), (0, 0)))
    idx = jnp.arange(nqb)[:, None] * qbs + jnp.arange(NSA_WINDOW + qbs)[None, :]
    kvb = pad[:, idx]
    q_pos = jnp.arange(T).reshape(nqb, qbs)
    o = window_attend(q.reshape((b_, nqb, qbs) + q.shape[2:]), kvb[:, :, :, 0], kvb[:, :, :, 1],
                      q_pos, idx - NSA_WINDOW)
    return o.reshape(q.shape)


NSA_TQ = NSA_WBLOCK
NSA_SLC_CHUNK = 256
NSA_WIN_CHUNK = 128
NSA_ROWS = NSA_GQA * NSA_TQ
NSA_SEL_SHIFT = NSA_SEL_BLOCK.bit_length() - 1
assert 1 << NSA_SEL_SHIFT == NSA_SEL_BLOCK


def _nsa_stream_softmax(q, k_ref, v_ref, c_lo, c_hi, chunk, mask_fn, m_ref, l_ref, acc_ref):
    scale = NSA_HEAD_DIM ** -0.5
    m_ref[...] = jnp.full(m_ref.shape, NEG_INF, jnp.float32)
    l_ref[...] = jnp.zeros(l_ref.shape, jnp.float32)
    acc_ref[...] = jnp.zeros(acc_ref.shape, jnp.float32)

    def body(c, carry):
        start = pl.multiple_of(c * chunk, chunk)
        k = k_ref[pl.ds(start, chunk), :]
        v = v_ref[pl.ds(start, chunk), :]
        s = lax.dot_general(q, k, (((1,), (1,)), ((), ())), preferred_element_type=jnp.float32) * scale
        s = jnp.where(mask_fn(start), s, NEG_INF)
        m_old = m_ref[...]
        m_new = jnp.maximum(m_old, jnp.max(s, axis=-1, keepdims=True))
        alpha = jnp.exp(m_old - m_new)
        p = jnp.exp(s - m_new)
        l_ref[...] = alpha * l_ref[...] + jnp.sum(p, axis=-1, keepdims=True)
        acc_ref[...] = alpha * acc_ref[...] + jnp.dot(p.astype(jnp.bfloat16), v,
                                                      preferred_element_type=jnp.float32)
        m_ref[...] = m_new
        return carry

    lax.fori_loop(c_lo, c_hi, body, 0)
    return acc_ref[...] / l_ref[...]


def _nsa_prompt_kernel(q_ref, kc_ref, vc_ref, ks_ref, vs_ref, kw_ref, vw_ref,
                       ocmp_ref, oslc_ref, owin_ref, selexp_ref, m_ref, l_ref, acc_ref,
                       *, n_cmp, n_top):
    f32, bf16 = jnp.float32, jnp.bfloat16
    i = pl.program_id(2)
    t0 = i * NSA_TQ
    n_cp = kc_ref.shape[0]
    t_len = selexp_ref.shape[1]
    n_sel = t_len // NSA_SEL_BLOCK
    nt = (((1,), (1,)), ((), ()))
    q = jnp.concatenate([q_ref[:, g * NSA_HEAD_DIM:(g + 1) * NSA_HEAD_DIM] for g in range(NSA_GQA)], axis=0)
    tq = t0 + (lax.broadcasted_iota(jnp.int32, (NSA_ROWS, 1), 0) & (NSA_TQ - 1))

    def store_heads(o_ref, o):
        for g in range(NSA_GQA):
            o_ref[:, g * NSA_HEAD_DIM:(g + 1) * NSA_HEAD_DIM] = o[g * NSA_TQ:(g + 1) * NSA_TQ]

    s = lax.dot_general(q, kc_ref[...], nt, preferred_element_type=f32) * (NSA_HEAD_DIM ** -0.5)
    n_idx = lax.broadcasted_iota(jnp.int32, (NSA_ROWS, n_cp), 1)
    cmask = (n_idx * NSA_CMP_STRIDE + (NSA_CMP_BLOCK - 1) <= tq) & (n_idx < n_cmp)
    s = jnp.where(cmask, s, NEG_INF)
    e = jnp.exp(s - jnp.max(s, axis=-1, keepdims=True))
    p = jnp.where(cmask, e / jnp.sum(e, axis=-1, keepdims=True), 0.0)
    store_heads(ocmp_ref, jnp.dot(p.astype(bf16), vc_ref[...], preferred_element_type=f32))

    psum = p[0:NSA_TQ]
    for g in range(1, NSA_GQA):
        psum = psum + p[g * NSA_TQ:(g + 1) * NSA_TQ]
    jn = lax.broadcasted_iota(jnp.int32, (n_sel, n_cp), 0)
    nn = lax.broadcasted_iota(jnp.int32, (n_sel, n_cp), 1)
    r = NSA_SEL_BLOCK // NSA_CMP_STRIDE
    span = NSA_CMP_BLOCK // NSA_CMP_STRIDE
    pool = jnp.where((nn >= r * jn - (span - 1)) & (nn <= r * jn + (r - 1)), 1.0, 0.0).astype(bf16)
    imp = jnp.zeros((n_sel, NSA_TQ), f32)
    rest = psum
    for _ in range(3):
        part = rest.astype(bf16)
        imp = imp + lax.dot_general(pool, part, nt, preferred_element_type=f32)
        rest = rest - part.astype(f32)

    jidx = lax.broadcasted_iota(jnp.int32, (n_sel, NSA_TQ), 0)
    cur = (t0 + lax.broadcasted_iota(jnp.int32, (n_sel, NSA_TQ), 1)) >> NSA_SEL_SHIFT
    forced = (jidx == 0) | (jidx == cur) | (jidx == cur - 1)
    score = jnp.where(forced, FORCE_SCORE, jnp.where(jidx <= cur, imp, -FORCE_SCORE))
    rank = jnp.zeros((n_sel, NSA_TQ), jnp.int32)
    for k in range(n_sel):
        row = score[k:k + 1, :]
        before = (row > score) | ((row == score) & (k < jidx))
        rank = rank + jnp.where(before, 1, 0)
    sel_t = jnp.where(rank < n_top, 1.0, 0.0).astype(bf16)
    eye = jnp.where(lax.broadcasted_iota(jnp.int32, (NSA_TQ, NSA_TQ), 0)
                    == lax.broadcasted_iota(jnp.int32, (NSA_TQ, NSA_TQ), 1), 1.0, 0.0).astype(bf16)
    sel = lax.dot_general(eye, sel_t, nt, preferred_element_type=f32).astype(bf16)
    expand = jnp.where((lax.broadcasted_iota(jnp.int32, (n_sel, t_len), 1) >> NSA_SEL_SHIFT)
                       == lax.broadcasted_iota(jnp.int32, (n_sel, t_len), 0), 1.0, 0.0).astype(bf16)
    selexp_ref[...] = jnp.dot(sel, expand, preferred_element_type=f32)

    def slc_mask(start):
        kpos = start + lax.broadcasted_iota(jnp.int32, (NSA_ROWS, NSA_SLC_CHUNK), 1)
        chosen = jnp.concatenate([selexp_ref[:, pl.ds(start, NSA_SLC_CHUNK)]] * NSA_GQA, axis=0)
        return (chosen > 0.5) & (kpos <= tq)

    store_heads(oslc_ref, _nsa_stream_softmax(
        q, ks_ref, vs_ref, 0, (t0 + NSA_TQ - 1) // NSA_SLC_CHUNK + 1, NSA_SLC_CHUNK, slc_mask,
        m_ref, l_ref, acc_ref))

    def win_mask(start):
        diff = tq - (start + lax.broadcasted_iota(jnp.int32, (NSA_ROWS, NSA_WIN_CHUNK), 1))
        return (diff >= 0) & (diff < NSA_WINDOW)

    store_heads(owin_ref, _nsa_stream_softmax(
        q, kw_ref, vw_ref, jnp.maximum(i - NSA_WINDOW // NSA_WIN_CHUNK, 0), i + 1, NSA_WIN_CHUNK,
        win_mask, m_ref, l_ref, acc_ref))


def nsa_prompt_attention(q, kv, kcmp, vcmp):
    b_, T, _ = q.shape
    assert T % NSA_SLC_CHUNK == 0 and (T // NSA_SEL_BLOCK) % 8 == 0
    n_cmp = kcmp.shape[1]
    n_cp = T // NSA_CMP_STRIDE
    n_sel = T // NSA_SEL_BLOCK
    bf16 = jnp.bfloat16

    def cmp_layout(t):
        return jnp.pad(jnp.moveaxis(t, 2, 1), ((0, 0), (0, 0), (0, n_cp - n_cmp), (0, 0))).astype(bf16)

    kvb = kv.astype(bf16)
    width = NSA_GQA * NSA_HEAD_DIM
    q_spec = pl.BlockSpec((None, NSA_TQ, width), lambda b, h, i: (b, i, h))
    cmp_spec = pl.BlockSpec((None, None, n_cp, NSA_HEAD_DIM), lambda b, h, i: (b, h, 0, 0))

    def kv_spec(comp):
        return pl.BlockSpec((None, T, NSA_HEAD_DIM), lambda b, h, i: (b, 0, comp * NSA_N_KV + h))

    out = jax.ShapeDtypeStruct((b_, T, NSA_N_HEADS * NSA_HEAD_DIM), jnp.float32)
    return pl.pallas_call(
        functools.partial(_nsa_prompt_kernel, n_cmp=n_cmp, n_top=min(NSA_N_SELECT, n_sel)),
        grid=(b_, NSA_N_KV, T // NSA_TQ),
        in_specs=[q_spec, cmp_spec, cmp_spec, kv_spec(2), kv_spec(3), kv_spec(4), kv_spec(5)],
        out_specs=[q_spec, q_spec, q_spec],
        out_shape=[out, out, out],
        scratch_shapes=[pltpu.VMEM((NSA_TQ, T), jnp.float32),
                        pltpu.VMEM((NSA_ROWS, 1), jnp.float32),
                        pltpu.VMEM((NSA_ROWS, 1), jnp.float32),
                        pltpu.VMEM((NSA_ROWS, NSA_HEAD_DIM), jnp.float32)],
        compiler_params=pltpu.CompilerParams(
            dimension_semantics=("parallel", "parallel", "arbitrary"),
            vmem_limit_bytes=V7X_VMEM_LIMIT_BYTES),
        name="nsa_prompt",
    )(q.astype(bf16), cmp_layout(kcmp), cmp_layout(vcmp), kvb, kvb, kvb, kvb)


def nsa_prompt_mixer(x, w_q, w_kv, w_gate, b_gate, w_cmp1, w_cmp2, cmp_pe, w_out):
    b_, T, _ = x.shape
    q = matmul(x, w_q)
    kv = matmul(x, w_kv)
    kv6 = kv.reshape(b_, T, 6, NSA_N_KV, NSA_HEAD_DIM)
    kcmp = nsa_compress(kv6[:, :, 0], w_cmp1[0], w_cmp2[0], cmp_pe[0])
    vcmp = nsa_compress(kv6[:, :, 1], w_cmp1[1], w_cmp2[1], cmp_pe[1])
    o_cmp, o_slc, o_win = nsa_prompt_attention(q, kv, kcmp, vcmp)
    gate = jax.nn.sigmoid(matmul(x, w_gate) + b_gate).reshape(b_, T, NSA_N_HEADS, 3)

    def heads(t):
        return t.reshape(b_, T, NSA_N_HEADS, NSA_HEAD_DIM)

    o = gate[..., 0:1] * heads(o_cmp) + gate[..., 1:2] * heads(o_slc) + gate[..., 2:3] * heads(o_win)
    y = matmul(o.reshape(b_, T, NSA_N_HEADS * NSA_HEAD_DIM), w_out)
    return y, kv6[:, :, :4], kv6[:, :, 4:6][:, -min(NSA_WINDOW, T):]


def nsa_mixer(x, past_rows, win_buf, pos0, w_q, w_kv, w_gate, b_gate, w_cmp1, w_cmp2, cmp_pe, w_out):
    b_, T, _ = x.shape
    if past_rows is None:
        assert win_buf is None and pos0 == 0 and T % NSA_SEL_BLOCK == 0
        return nsa_prompt_mixer(x, w_q, w_kv, w_gate, b_gate, w_cmp1, w_cmp2, cmp_pe, w_out)
    q = matmul(x, w_q).reshape(b_, T, NSA_N_KV, NSA_GQA, NSA_HEAD_DIM)
    kv = matmul(x, w_kv).reshape(b_, T, 6, NSA_N_KV, NSA_HEAD_DIM)
    rows = kv[:, :, :4]
    full = rows if past_rows is None else jnp.concatenate([past_rows, rows], axis=1)
    q_pos = pos0 + jnp.arange(T)
    o_cmp, o_slc = nsa_cmp_slc(q, full, q_pos, w_cmp1, w_cmp2, cmp_pe)
    win_rows = kv[:, :, 4:6]
    if win_buf is None:
        o_win = nsa_window_prompt(q, win_rows)
        win_new = win_rows[:, -min(NSA_WINDOW, T):]
    else:
        w_b = win_buf.shape[1]
        wk = jnp.concatenate([win_buf, win_rows], axis=1)
        k_pos = pos0 - w_b + jnp.arange(w_b + T)
        o_win = window_attend(q[:, None], wk[:, None, :, 0], wk[:, None, :, 1],
                              q_pos[None], k_pos[None])[:, 0]
        win_new = wk[:, -w_b:]
    gate = jax.nn.sigmoid(matmul(x, w_gate) + b_gate).reshape(b_, T, NSA_N_KV, NSA_GQA, 3)
    o = gate[..., 0:1] * o_cmp + gate[..., 1:2] * o_slc + gate[..., 2:3] * o_win
    return matmul(o.reshape(b_, T, NSA_N_HEADS * NSA_HEAD_DIM), w_out), rows, win_new


def conv_ffn(x, hist, w_up, conv_w, conv_b, w_down):
    a, g = jnp.split(matmul(x, w_up), 2, axis=-1)
    ah = jnp.concatenate([hist, a], axis=1)
    a = causal_dwconv(ah, conv_w, conv_b)
    return matmul(jax.nn.gelu(a) * g, w_down), ah[:, -(FFN_CONV_W - 1):]


def kernel(x_prompt, x_sample, cache_nsa, state_nsa_win, state_ssd, state_ssd_conv, state_mlstm_c,
           state_mlstm_n, state_mlstm_m, state_mlstm_conv, state_s5, state_ffn_conv, page_table,
           ln_g, ln_b, ffn_w_up, ffn_conv_w, ffn_conv_b, ffn_w_down,
           ssd_w_in, ssd_conv_w, ssd_conv_b, ssd_dt_bias, ssd_a_log, ssd_d, ssd_norm_g, ssd_w_out,
           mlstm_w_up, mlstm_conv_w, mlstm_conv_b, mlstm_w_q, mlstm_w_k, mlstm_w_v, mlstm_w_if,
           mlstm_b_if, mlstm_skip, mlstm_norm_g, mlstm_w_down,
           s5_a_re, s5_a_im, s5_log_dt, s5_b_re, s5_b_im, s5_c_re, s5_c_im, s5_d, s5_w_glu_a, s5_w_glu_b,
           nsa_w_q, nsa_w_kv, nsa_w_gate, nsa_b_gate, nsa_w_cmp1, nsa_w_cmp2, nsa_cmp_pe, nsa_w_out):

    def trunk(x, sample):
        b_, T, _ = x.shape
        dt_ = x.dtype
        pos0 = PAST_LEN if sample else 0
        o_nsa, o_win, o_ssd, o_ssdc, o_mc, o_mn, o_mm, o_mconv, o_s5, o_ffn = ([] for _ in range(10))
        for i in range(DEPTH):
            kind, j = i % N_MIXERS, i // N_MIXERS
            if kind == 0:
                hist = state_ssd_conv[j] if sample else jnp.zeros((b_, SSD_CONV_W - 1, SSD_CONV_DIM), dt_)
                h0 = state_ssd[j] if sample else jnp.zeros((b_, SSD_N_HEADS, SSD_HEADDIM, SSD_D_STATE), dt_)
                y, hist_new, h_new = ssd_mixer(x, hist, h0, ssd_w_in[j], ssd_conv_w[j], ssd_conv_b[j],
                                               ssd_dt_bias[j], ssd_a_log[j], ssd_d[j], ssd_norm_g[j], ssd_w_out[j])
                o_ssd.append(h_new)
                o_ssdc.append(hist_new)
            elif kind == 1:
                hist = state_mlstm_conv[j] if sample else jnp.zeros((b_, MLSTM_CONV_W - 1, MLSTM_D_INNER), dt_)
                c0 = state_mlstm_c[j] if sample else jnp.zeros((b_, MLSTM_N_HEADS, MLSTM_HEAD_DIM, MLSTM_HEAD_DIM), dt_)
                n0 = state_mlstm_n[j] if sample else jnp.zeros((b_, MLSTM_N_HEADS, MLSTM_HEAD_DIM), dt_)
                m0 = state_mlstm_m[j] if sample else jnp.zeros((b_, MLSTM_N_HEADS), dt_)
                y, hist_new, c, n, m = mlstm_mixer(x, hist, c0, n0, m0, mlstm_w_up[j], mlstm_conv_w[j],
                                                   mlstm_conv_b[j], mlstm_w_q[j], mlstm_w_k[j], mlstm_w_v[j],
                                                   mlstm_w_if[j], mlstm_b_if[j], mlstm_skip[j],
                                                   mlstm_norm_g[j], mlstm_w_down[j])
                o_mc.append(c)
                o_mn.append(n)
                o_mm.append(m)
                o_mconv.append(hist_new)
            elif kind == 2:
                h0 = state_s5[j] if sample else jnp.zeros((b_, S5_N_GROUPS, S5_STATE, 2), dt_)
                y, h_new = s5_mixer(x, h0, s5_a_re[j], s5_a_im[j], s5_log_dt[j], s5_b_re[j], s5_b_im[j],
                                    s5_c_re[j], s5_c_im[j], s5_d[j], s5_w_glu_a[j], s5_w_glu_b[j])
                o_s5.append(h_new)
            else:
                past = cache_nsa[j, page_table].reshape((b_, -1) + cache_nsa.shape[3:]) if sample else None
                win = state_nsa_win[j] if sample else None
                y, rows, win_new = nsa_mixer(x, past, win, pos0, nsa_w_q[j], nsa_w_kv[j], nsa_w_gate[j],
                                             nsa_b_gate[j], nsa_w_cmp1[j], nsa_w_cmp2[j], nsa_cmp_pe[j],
                                             nsa_w_out[j])
                o_nsa.append(rows)
                o_win.append(win_new)
            x = layer_norm(DEEPNORM_ALPHA * x + y, ln_g[i, 0], ln_b[i, 0])
            fhist = state_ffn_conv[i] if sample else jnp.zeros((b_, FFN_CONV_W - 1, FFN_DIM), dt_)
            y, fhist_new = conv_ffn(x, fhist, ffn_w_up[i], ffn_conv_w[i], ffn_conv_b[i], ffn_w_down[i])
            o_ffn.append(fhist_new)
            x = layer_norm(DEEPNORM_ALPHA * x + y, ln_g[i, 1], ln_b[i, 1])
        st = jnp.stack
        return (x, st(o_nsa), st(o_win), st(o_ssd), st(o_ssdc), st(o_mc), st(o_mn), st(o_mm),
                st(o_mconv), st(o_s5), st(o_ffn))

    (y_prompt, nsa_p, win_p, ssd_p, ssdc_p, mc_p, mn_p, mm_p, mconv_p, s5_p, ffn_p) = trunk(x_prompt, False)
    (y_sample, nsa_s, win_s, ssd_s, ssdc_s, mc_s, mn_s, mm_s, mconv_s, s5_s, ffn_s) = trunk(x_sample, True)
    return (y_prompt, y_sample, nsa_p, nsa_s, win_p, win_s, ssd_p, ssd_s, ssdc_p, ssdc_s, mc_p, mc_s,
            mn_p, mn_s, mm_p, mm_s, mconv_p, mconv_s, s5_p, s5_s, ffn_p, ffn_s)
```

```python
import functools
import math

import jax
import jax.numpy as jnp
from jax import lax
from jax.experimental import pallas as pl
from jax.experimental.pallas import tpu as pltpu

D_MODEL = 2048
DEPTH = 4
PAST_LEN = 16384
N_MIXERS = 4

DEEPNORM_ALPHA = (2.0 * DEPTH) ** 0.25
LN_EPS = 1e-5
RMS_EPS = 1e-5
NEG_INF = -1e30
FORCE_SCORE = 1e4

SSD_D_INNER = 2 * D_MODEL
SSD_HEADDIM = 64
SSD_N_HEADS = SSD_D_INNER // SSD_HEADDIM
SSD_N_GROUPS = 8
SSD_D_STATE = 128
SSD_CONV_W = 4
SSD_CHUNK = 256
SSD_CONV_DIM = SSD_D_INNER + 2 * SSD_N_GROUPS * SSD_D_STATE

MLSTM_D_INNER = 2 * D_MODEL
MLSTM_N_HEADS = 4
MLSTM_HEAD_DIM = MLSTM_D_INNER // MLSTM_N_HEADS
MLSTM_CONV_W = 4
MLSTM_CHUNK = 64

S5_GROUP = 16
S5_N_GROUPS = D_MODEL // S5_GROUP
S5_STATE = 64

NSA_N_HEADS = 16
NSA_N_KV = 4
NSA_HEAD_DIM = D_MODEL // NSA_N_HEADS
NSA_GQA = NSA_N_HEADS // NSA_N_KV
NSA_CMP_BLOCK = 32
NSA_CMP_STRIDE = 16
NSA_SEL_BLOCK = 64
NSA_N_SELECT = 16
NSA_WINDOW = 512
NSA_QBLOCK = 32
NSA_WBLOCK = 128

FFN_DIM = 5632
FFN_CONV_W = 3

V7X_VMEM_LIMIT_BYTES = 48 * 1024 * 1024


def _mm_kernel(x_ref, w_ref, o_ref):
    o_ref[...] = jnp.dot(x_ref[...], w_ref[...], preferred_element_type=jnp.float32)


def _pick(dim, target):
    if dim <= target:
        return dim
    t = target
    while dim % t:
        t //= 2
    return t


def _mm_tiles(M, K, N):
    tm = _pick(M, 1024)
    tn = N if N <= 512 else (1024 if K <= 2048 else 512)
    double_buffered = 2 * (tm * K * 2 + K * tn * 2 + tm * tn * 4)
    assert double_buffered <= V7X_VMEM_LIMIT_BYTES, (M, K, N)
    return tm, tn


def matmul(x, w):
    stack, s = w if isinstance(w, tuple) else (w[None], 0)
    _, K, N = stack.shape
    lead = x.shape[:-1]
    x2 = x.astype(jnp.bfloat16).reshape(-1, K)
    M = x2.shape[0]
    tm, tn = _mm_tiles(M, K, N)
    out = pl.pallas_call(
        _mm_kernel,
        grid=(M // tm, pl.cdiv(N, tn)),
        in_specs=[pl.BlockSpec((tm, K), lambda i, j: (i, 0)),
                  pl.BlockSpec((None, K, tn), lambda i, j: (s, 0, j))],
        out_specs=pl.BlockSpec((tm, tn), lambda i, j: (i, j)),
        out_shape=jax.ShapeDtypeStruct((M, N), jnp.float32),
        compiler_params=pltpu.CompilerParams(
            dimension_semantics=("parallel", "arbitrary"),
            vmem_limit_bytes=V7X_VMEM_LIMIT_BYTES),
        name="matmul",
    )(x2, stack.astype(jnp.bfloat16))
    return out.reshape(lead + (N,))


def layer_norm(x, g, b):
    mu = jnp.mean(x, axis=-1, keepdims=True)
    var = jnp.mean(jnp.square(x - mu), axis=-1, keepdims=True)
    return (x - mu) * lax.rsqrt(var + LN_EPS) * g + b


def group_rms_norm(y, g, n_groups):
    yg = y.reshape(y.shape[:-1] + (n_groups, -1))
    yg = yg * lax.rsqrt(jnp.mean(yg * yg, axis=-1, keepdims=True) + RMS_EPS)
    return yg.reshape(y.shape) * g


def head_layer_norm(h, g):
    mu = jnp.mean(h, axis=-1, keepdims=True)
    var = jnp.mean(jnp.square(h - mu), axis=-1, keepdims=True)
    hn = (h - mu) * lax.rsqrt(var + LN_EPS)
    return hn.reshape(h.shape[:2] + (-1,)) * g


def causal_dwconv(x_hist, w, b):
    width = w.shape[0]
    T = x_hist.shape[1] - (width - 1)
    out = b
    for k in range(width):
        out = out + w[k] * x_hist[:, k:k + T]
    return out


def blockdiag(x, w):
    nb, bs, _ = w.shape
    y = jnp.einsum('btnc,ncd->btnd', x.reshape(x.shape[:2] + (nb, bs)), w)
    return y.reshape(x.shape[:2] + (nb * bs,))


def segsum(x):
    T = x.shape[-1]
    xr = jnp.broadcast_to(x[..., :, None], x.shape + (T,))
    strict = jnp.tril(jnp.ones((T, T), bool), -1)
    xs = jnp.cumsum(jnp.where(strict, xr, 0), axis=-2)
    return jnp.where(jnp.tril(jnp.ones((T, T), bool)), xs, -jnp.inf)


def ssd_scan(xs, dt, a, bm, cm, h0):
    b_, T = xs.shape[:2]
    cl = math.gcd(T, SSD_CHUNK)
    nc = T // cl

    def chunk(t):
        return t.reshape((b_, nc, cl) + t.shape[2:])

    xc, dtc, bc, cc = chunk(xs), chunk(dt), chunk(bm), chunk(cm)
    dt_t = jnp.moveaxis(dtc, 2, -1)
    da = dt_t * a[:, :, None]
    acs = jnp.cumsum(da, axis=-1)
    decay_in = jnp.exp(segsum(da))
    cb = jnp.einsum('bclgn,bcsgn->bcgls', cc, bc)
    w_diag = cb[:, :, :, None] * decay_in * dt_t[..., None, :]
    y_diag = jnp.einsum('bcgrls,bcsgrp->bclgrp', w_diag, xc)
    w_state = jnp.exp(acs[..., -1:] - acs) * dt_t
    states = jnp.einsum('bclgn,bcgrl,bclgrp->bcgrpn', bc, w_state, xc)
    states = jnp.concatenate([h0[:, None], states], axis=1)
    tot = jnp.pad(jnp.moveaxis(acs[..., -1], 1, -1), ((0, 0), (0, 0), (0, 0), (1, 0)))
    decay_chunk = jnp.exp(segsum(tot))
    new_states = jnp.einsum('bgrzc,bcgrpn->bzgrpn', decay_chunk, states)
    y_off = jnp.einsum('bclgn,bcgrpn,bcgrl->bclgrp', cc, new_states[:, :-1], jnp.exp(acs))
    return (y_diag + y_off).reshape(xs.shape), new_states[:, -1]


SSD_HEADS_PER_GROUP = SSD_N_HEADS // SSD_N_GROUPS
SSD_GROUP_WIDTH = SSD_HEADS_PER_GROUP * SSD_HEADDIM
SSD_HEAD_SHIFT = SSD_HEADDIM.bit_length() - 1
assert 1 << SSD_HEAD_SHIFT == SSD_HEADDIM and SSD_GROUP_WIDTH == SSD_D_INNER // SSD_N_GROUPS


def _ssd_kernel(x_ref, z_ref, cm_ref, bmt_ref, acs_ref, dt_ref, rows_ref, h0_ref, d_ref, g_ref,
                y_ref, h_out_ref, h_ref):
    f32, bf16 = jnp.float32, jnp.bfloat16
    ck = pl.program_id(2)
    L, W = x_ref.shape

    @pl.when(ck == 0)
    def _():
        h_ref[...] = h0_ref[...]

    x = x_ref[...]
    xb = x.astype(bf16)
    cmb = cm_ref[...].astype(bf16)
    bmt = bmt_ref[...]
    acs = acs_ref[...]
    cb = jnp.dot(cmb, bmt, preferred_element_type=f32)
    causal = (lax.broadcasted_iota(jnp.int32, (L, L), 0) >= lax.broadcasted_iota(jnp.int32, (L, L), 1))
    lane_head = lax.broadcasted_iota(jnp.int32, (L, W), 1) >> SSD_HEAD_SHIFT
    y = jnp.zeros((L, W), f32)
    for r in range(SSD_HEADS_PER_GROUP):
        acs_col = acs[:, r * SSD_HEADDIM:r * SSD_HEADDIM + 1]
        acs_row = rows_ref[r:r + 1, :]
        dt_row = rows_ref[SSD_HEADS_PER_GROUP + r:SSD_HEADS_PER_GROUP + r + 1, :]
        decay = jnp.exp(jnp.where(causal, acs_col - acs_row, -jnp.inf))
        w = (cb * decay * dt_row).astype(bf16)
        y = jnp.where(lane_head == r, jnp.dot(w, xb, preferred_element_type=f32), y)
    total = acs[L - 1:L, :]
    xw = (x * (jnp.exp(total - acs) * dt_ref[...])).astype(bf16)
    h_t = h_ref[...]
    y = y + jnp.dot(cmb, h_t.astype(bf16), preferred_element_type=f32) * jnp.exp(acs)
    h_ref[...] = jnp.exp(total) * h_t + jnp.dot(bmt, xw, preferred_element_type=f32)
    y = (y + d_ref[...] * x) * jax.nn.silu(z_ref[...])
    y_ref[...] = y * lax.rsqrt(jnp.mean(y * y, axis=-1, keepdims=True) + RMS_EPS) * g_ref[...]

    @pl.when(ck == pl.num_programs(2) - 1)
    def _():
        h_out_ref[...] = h_ref[...]


def ssd_cell(zx, xbc, dt, a, d_skip, norm_g, h0):
    b_, T, _ = xbc.shape
    G, R, P, N, W = SSD_N_GROUPS, SSD_HEADS_PER_GROUP, SSD_HEADDIM, SSD_D_STATE, SSD_GROUP_WIDTH
    L = math.gcd(T, SSD_CHUNK)
    nc = T // L
    acs = jnp.cumsum((dt * a).reshape(b_, nc, L, SSD_N_HEADS), axis=2)
    dtc = dt.reshape(b_, nc, L, SSD_N_HEADS)

    def rows(t):
        return jnp.transpose(t.reshape(b_, nc, L, G, R), (0, 3, 1, 4, 2))

    rowpack = jnp.concatenate([rows(acs), rows(dtc)], axis=3)
    acs_e = jnp.repeat(acs.reshape(b_, T, SSD_N_HEADS), P, axis=-1)
    dt_e = jnp.repeat(dt, P, axis=-1)
    bmt = jnp.transpose(xbc[..., SSD_D_INNER:SSD_D_INNER + G * N].astype(jnp.bfloat16).reshape(b_, T, G, N),
                        (0, 2, 3, 1))
    h0t = jnp.transpose(h0.reshape(b_, G, R, P, N), (0, 1, 4, 2, 3)).reshape(b_, G, N, W)
    chan = lambda b, g, c: (b, c, g)
    cm_block0 = (SSD_D_INNER + G * N) // N
    state_spec = pl.BlockSpec((None, None, N, W), lambda b, g, c: (b, g, 0, 0))
    row_spec = pl.BlockSpec((1, W), lambda b, g, c: (0, g))
    y, ht = pl.pallas_call(
        _ssd_kernel,
        grid=(b_, G, nc),
        in_specs=[pl.BlockSpec((None, L, W), chan),
                  pl.BlockSpec((None, L, W), chan),
                  pl.BlockSpec((None, L, N), lambda b, g, c: (b, c, cm_block0 + g)),
                  pl.BlockSpec((None, None, N, L), lambda b, g, c: (b, g, 0, c)),
                  pl.BlockSpec((None, L, W), chan),
                  pl.BlockSpec((None, L, W), chan),
                  pl.BlockSpec((None, None, None, 2 * R, L), lambda b, g, c: (b, g, c, 0, 0)),
                  state_spec, row_spec, row_spec],
        out_specs=[pl.BlockSpec((None, L, W), chan), state_spec],
        out_shape=[jax.ShapeDtypeStruct((b_, T, SSD_D_INNER), jnp.float32),
                   jax.ShapeDtypeStruct((b_, G, N, W), jnp.float32)],
        scratch_shapes=[pltpu.VMEM((N, W), jnp.float32)],
        compiler_params=pltpu.CompilerParams(
            dimension_semantics=("parallel", "parallel", "arbitrary"),
            vmem_limit_bytes=V7X_VMEM_LIMIT_BYTES),
        name="ssd_cell",
    )(xbc, zx, xbc, bmt, acs_e, dt_e, rowpack, h0t,
      jnp.repeat(d_skip, P).reshape(1, SSD_D_INNER), norm_g.reshape(1, SSD_D_INNER))
    h_new = jnp.transpose(ht.reshape(b_, G, N, R, P), (0, 1, 3, 4, 2)).reshape(b_, SSD_N_HEADS, P, N)
    return y, h_new


def ssd_mixer(x, conv_hist, h0, w_in, conv_w, conv_b, dt_bias, a_log, d_skip, norm_g, w_out):
    zx = matmul(x, w_in)
    xbc_raw = zx[..., SSD_D_INNER:SSD_D_INNER + SSD_CONV_DIM]
    hist = jnp.concatenate([conv_hist, xbc_raw], axis=1)
    xbc = jax.nn.silu(causal_dwconv(hist, conv_w, conv_b))
    dt = jax.nn.softplus(zx[..., SSD_D_INNER + SSD_CONV_DIM:] + dt_bias)
    y, h_new = ssd_cell(zx, xbc, dt, -jnp.exp(a_log), d_skip, norm_g, h0)
    return matmul(y, w_out), hist[:, -(SSD_CONV_W - 1):], h_new


def mlstm_chunked(q, k, v, i_pre, logf, c0, n0, m0):
    b_, T, H, _ = q.shape
    cl = math.gcd(T, MLSTM_CHUNK)
    nc = T // cl
    causal = jnp.tril(jnp.ones((cl, cl), bool))

    def chunk(t):
        return jnp.moveaxis(t.reshape((b_, nc, cl) + t.shape[2:]), 1, 0)

    def step(carry, inp):
        c, n, m = carry
        qc, kc, vc, ic, fc = inp
        bcum = jnp.cumsum(fc, axis=1)
        dmat = bcum[:, :, None, :] - bcum[:, None, :, :] + ic[:, None, :, :]
        dmat = jnp.where(causal[None, :, :, None], dmat, -jnp.inf)
        inter = m[:, None, :] + bcum
        m_t = jnp.maximum(inter, jnp.max(dmat, axis=2))
        w = jnp.exp(dmat - m_t[:, :, None, :])
        s = jnp.einsum('bthd,bshd->btsh', qc, kc) * w
        sc_inter = jnp.exp(inter - m_t)
        num = (jnp.einsum('btsh,bshe->bthe', s, vc)
               + sc_inter[..., None] * jnp.einsum('bthd,bhde->bthe', qc, c))
        den = jnp.sum(s, axis=2) + sc_inter * jnp.einsum('bthd,bhd->bth', qc, n)
        h = num / jnp.maximum(jnp.abs(den), jnp.exp(-m_t))[..., None]
        m_new = m_t[:, -1]
        decay_s = jnp.exp(bcum[:, -1:] - bcum + ic - m_new[:, None])
        sc_c = jnp.exp(m + bcum[:, -1] - m_new)
        c_new = sc_c[..., None, None] * c + jnp.einsum('bsh,bshd,bshe->bhde', decay_s, kc, vc)
        n_new = sc_c[..., None] * n + jnp.einsum('bsh,bshd->bhd', decay_s, kc)
        return (c_new, n_new, m_new), h

    (c, n, m), hs = lax.scan(step, (c0, n0, m0),
                             (chunk(q), chunk(k), chunk(v), chunk(i_pre), chunk(logf)))
    h = jnp.moveaxis(hs, 0, 1).reshape(b_, T, H, -1)
    return h, c, n, m


MLSTM_STEP = 256
MLSTM_NORM_LANES = 128


def _mlstm_kernel(q_ref, kt_ref, v_ref, acol_ref, grow_ref, c0_ref, n0_ref, m0_ref, g_ref,
                  h_ref, c_out_ref, n_out_ref, m_out_ref, c_ref, m_ref):
    f32, bf16 = jnp.float32, jnp.bfloat16
    step = pl.program_id(2)
    L, D = q_ref.shape

    @pl.when(step == 0)
    def _():
        c_ref[:, :D] = c0_ref[...]
        c_ref[:, D:] = n0_ref[...]
        m_ref[...] = m0_ref[...]

    q = q_ref[...]
    kt = kt_ref[...]
    a_col = acol_ref[...]
    g_row = grow_ref[...]
    m_prev = m_ref[:, 0:1]
    causal = (lax.broadcasted_iota(jnp.int32, (L, L), 0) >= lax.broadcasted_iota(jnp.int32, (L, L), 1))
    dmat = jnp.where(causal, a_col + g_row, -jnp.inf)
    inter = m_prev + a_col
    m_t = jnp.maximum(inter, jnp.max(dmat, axis=1, keepdims=True))
    s = jnp.dot(q, kt, preferred_element_type=f32) * jnp.exp(dmat - m_t)
    sc_inter = jnp.exp(inter - m_t)
    qc = jnp.dot(q, c_ref[...].astype(bf16), preferred_element_type=f32)
    num = jnp.dot(s.astype(bf16), v_ref[...], preferred_element_type=f32) + sc_inter * qc[:, :D]
    den = jnp.sum(s, axis=1, keepdims=True) + sc_inter * qc[:, D:D + 1]
    h = num / jnp.maximum(jnp.abs(den), jnp.exp(-m_t))
    mu = jnp.mean(h, axis=-1, keepdims=True)
    var = jnp.mean(jnp.square(h - mu), axis=-1, keepdims=True)
    h_ref[...] = (h - mu) * lax.rsqrt(var + LN_EPS) * g_ref[...]

    m_new = m_t[L - 1:L, :]
    total = a_col[L - 1:L, :]
    decay = jnp.exp(total + g_row - m_new)
    sc_c = jnp.exp(m_prev + total - m_new)
    ktd = (kt.astype(f32) * decay).astype(bf16)
    one_hot = jnp.where(lax.broadcasted_iota(jnp.int32, (L, MLSTM_NORM_LANES), 1) == 0, 1.0, 0.0).astype(bf16)
    v_ext = jnp.concatenate([v_ref[...], one_hot], axis=1)
    c_ref[...] = sc_c * c_ref[...] + jnp.dot(ktd, v_ext, preferred_element_type=f32)
    m_ref[...] = jnp.broadcast_to(m_new, m_ref.shape)

    @pl.when(step == pl.num_programs(2) - 1)
    def _():
        c_out_ref[...] = c_ref[:, :D]
        n_out_ref[...] = c_ref[:, D:]
        m_out_ref[...] = m_ref[...]


def mlstm_cell(q, k, v, i_pre, logf, c0, n0, m0, norm_g):
    b_, T, _ = q.shape
    H, D = MLSTM_N_HEADS, MLSTM_HEAD_DIM
    L = min(T, MLSTM_STEP)
    ns = T // L
    bf16 = jnp.bfloat16
    kt = jnp.swapaxes((k * (D ** -0.5)).astype(bf16).reshape(b_, T, H, D), 1, 3)
    kt = jnp.swapaxes(kt, 1, 2)
    bcum = jnp.cumsum(logf.reshape(b_, ns, L, H), axis=2)
    acol = jnp.moveaxis(bcum, 3, 1).reshape(b_, H, T, 1)
    grow = jnp.moveaxis(i_pre.reshape(b_, ns, L, H) - bcum, 3, 1).reshape(b_, H, ns, 1, L)
    n0e = jnp.pad(n0[..., None], ((0, 0), (0, 0), (0, 0), (0, MLSTM_NORM_LANES - 1)))
    m0e = jnp.broadcast_to(m0[:, :, None, None], (b_, H, 1, MLSTM_NORM_LANES))

    tok_spec = pl.BlockSpec((None, L, D), lambda b, h, s: (b, s, h))
    state = lambda w: pl.BlockSpec((None, None, D, w), lambda b, h, s: (b, h, 0, 0))
    m_spec = pl.BlockSpec((None, None, 1, MLSTM_NORM_LANES), lambda b, h, s: (b, h, 0, 0))
    hn, c, n, m = pl.pallas_call(
        _mlstm_kernel,
        grid=(b_, H, ns),
        in_specs=[tok_spec,
                  pl.BlockSpec((None, None, D, L), lambda b, h, s: (b, h, 0, s)),
                  tok_spec,
                  pl.BlockSpec((None, None, L, 1), lambda b, h, s: (b, h, s, 0)),
                  pl.BlockSpec((None, None, None, 1, L), lambda b, h, s: (b, h, s, 0, 0)),
                  state(D), state(MLSTM_NORM_LANES), m_spec,
                  pl.BlockSpec((1, D), lambda b, h, s: (0, h))],
        out_specs=[tok_spec, state(D), state(MLSTM_NORM_LANES), m_spec],
        out_shape=[jax.ShapeDtypeStruct((b_, T, H * D), jnp.float32),
                   jax.ShapeDtypeStruct((b_, H, D, D), jnp.float32),
                   jax.ShapeDtypeStruct((b_, H, D, MLSTM_NORM_LANES), jnp.float32),
                   jax.ShapeDtypeStruct((b_, H, 1, MLSTM_NORM_LANES), jnp.float32)],
        scratch_shapes=[pltpu.VMEM((D, D + MLSTM_NORM_LANES), jnp.float32),
                        pltpu.VMEM((1, MLSTM_NORM_LANES), jnp.float32)],
        compiler_params=pltpu.CompilerParams(
            dimension_semantics=("parallel", "parallel", "arbitrary"),
            vmem_limit_bytes=V7X_VMEM_LIMIT_BYTES),
        name="mlstm_cell",
    )(q.astype(bf16), kt, v.astype(bf16), acol, grow, c0, n0e, m0e, norm_g.reshape(1, H * D))
    return hn, c, n[..., 0], m[:, :, 0, 0]


def mlstm_mixer(x, conv_hist, c0, n0, m0, w_up, conv_w, conv_b, w_q, w_k, w_v, w_if, b_if,
                skip, norm_g, w_down):
    b_, T, _ = x.shape
    xm, z = jnp.split(matmul(x, w_up), 2, axis=-1)
    hist = jnp.concatenate([conv_hist, xm], axis=1)
    xc = jax.nn.silu(causal_dwconv(hist, conv_w, conv_b))
    q, k, v = blockdiag(xc, w_q), blockdiag(xc, w_k), blockdiag(xm, w_v)
    gates = matmul(q, w_if[0]) + matmul(k, w_if[1]) + matmul(v, w_if[2]) + b_if
    i_pre, f_pre = gates[..., :MLSTM_N_HEADS], gates[..., MLSTM_N_HEADS:]
    h, c, n, m = mlstm_cell(q, k, v, i_pre, jax.nn.log_sigmoid(f_pre), c0, n0, m0, norm_g)
    h = (h + skip * xc) * jax.nn.silu(z)
    return matmul(h, w_down), hist[:, -(MLSTM_CONV_W - 1):], c, n, m


S5_D_STATE = S5_N_GROUPS * S5_STATE
S5_PACK = 8
S5_N_PACKS = S5_N_GROUPS // S5_PACK
S5_SCAN_ROWS = 8
S5_SCAN_LANES = 256
S5_TIME_TILE = 256


def _cmul(ar, ai, br, bi):
    return ar * br - ai * bi, ar * bi + ai * br


def _s5_kernel(x_ref, bre_ref, bim_ref, cre_ref, cim_ref, pw_ref, h0r_ref, h0i_ref, d_ref,
               g_ref, hr_out_ref, hi_out_ref, bur_ref, bui_ref, hr_ref, hi_ref):
    tt = pl.program_id(1)
    rows = x_ref.shape[0]
    pk_in = S5_PACK * S5_GROUP
    pk_st = S5_PACK * S5_STATE

    @pl.when(tt == 0)
    def _():
        hr_ref[...] = jnp.broadcast_to(h0r_ref[...], hr_ref.shape)
        hi_ref[...] = jnp.broadcast_to(h0i_ref[...], hi_ref.shape)

    for c in range(S5_N_PACKS):
        xc = x_ref[:, c * pk_in:(c + 1) * pk_in].astype(jnp.bfloat16)
        bur_ref[:, c * pk_st:(c + 1) * pk_st] = jnp.dot(xc, bre_ref[c], preferred_element_type=jnp.float32)
        bui_ref[:, c * pk_st:(c + 1) * pk_st] = jnp.dot(xc, bim_ref[c], preferred_element_type=jnp.float32)

    def col_body(cb, carry):
        cs = pl.ds(pl.multiple_of(cb * S5_SCAN_LANES, S5_SCAN_LANES), S5_SCAN_LANES)
        stages = [(pw_ref[2 * k, :, cs], pw_ref[2 * k + 1, :, cs], 1 << k) for k in range(3)]
        lr, li = pw_ref[6, :, cs], pw_ref[7, :, cs]

        def row_body(r, h):
            hr, hi = h
            rs = pl.ds(pl.multiple_of(r * S5_SCAN_ROWS, S5_SCAN_ROWS), S5_SCAN_ROWS)
            vr, vi = bur_ref[rs, cs], bui_ref[rs, cs]
            for mr, mi, s in stages:
                pr, pi = _cmul(mr, mi, pltpu.roll(vr, s, 0), pltpu.roll(vi, s, 0))
                vr, vi = vr + pr, vi + pi
            pr, pi = _cmul(lr, li, hr, hi)
            vr, vi = vr + pr, vi + pi
            bur_ref[rs, cs] = vr
            bui_ref[rs, cs] = vi
            last = S5_SCAN_ROWS - 1
            return (jnp.broadcast_to(vr[last:, :], vr.shape), jnp.broadcast_to(vi[last:, :], vi.shape))

        hr, hi = lax.fori_loop(0, rows // S5_SCAN_ROWS, row_body, (hr_ref[:, cs], hi_ref[:, cs]))
        hr_ref[:, cs] = hr
        hi_ref[:, cs] = hi
        return carry

    lax.fori_loop(0, S5_D_STATE // S5_SCAN_LANES, col_body, 0)

    for c in range(S5_N_PACKS):
        hr = bur_ref[:, c * pk_st:(c + 1) * pk_st].astype(jnp.bfloat16)
        hi = bui_ref[:, c * pk_st:(c + 1) * pk_st].astype(jnp.bfloat16)
        y = (jnp.dot(hr, cre_ref[c], preferred_element_type=jnp.float32)
             - jnp.dot(hi, cim_ref[c], preferred_element_type=jnp.float32))
        cols = slice(c * pk_in, (c + 1) * pk_in)
        g_ref[:, cols] = jax.nn.gelu(y + d_ref[:, cols] * x_ref[:, cols])

    @pl.when(tt == pl.num_programs(1) - 1)
    def _():
        hr_out_ref[...] = hr_ref[0:1, :]
        hi_out_ref[...] = hi_ref[0:1, :]


def _block_diag_packs(w):
    g, r, c = w.shape
    eye = jnp.eye(S5_PACK, dtype=w.dtype)
    wb = jnp.einsum('kgrc,gh->kgrhc', w.reshape(g // S5_PACK, S5_PACK, r, c), eye)
    return wb.reshape(g // S5_PACK, S5_PACK * r, S5_PACK * c).astype(jnp.bfloat16)


def s5_mixer(x, h0, a_re, a_im, log_dt, b_re, b_im, c_re, c_im, d_skip, w_glu_a, w_glu_b):
    b_, T, _ = x.shape
    step = jnp.exp(log_dt)[:, None]
    mag = jnp.exp(a_re * step)
    ab_re, ab_im = mag * jnp.cos(a_im * step), mag * jnp.sin(a_im * step)
    den = a_re * a_re + a_im * a_im
    nr, ni = ab_re - 1.0, ab_im
    f_re = (nr * a_re + ni * a_im) / den
    f_im = (ni * a_re - nr * a_im) / den
    bb_re = f_re[..., None] * b_re - f_im[..., None] * b_im
    bb_im = f_re[..., None] * b_im + f_im[..., None] * b_re
    bre = _block_diag_packs(jnp.swapaxes(bb_re, 1, 2))
    bim = _block_diag_packs(jnp.swapaxes(bb_im, 1, 2))
    cre = _block_diag_packs(jnp.swapaxes(c_re, 1, 2))
    cim = _block_diag_packs(jnp.swapaxes(c_im, 1, 2))
    l1 = (ab_re.reshape(-1), ab_im.reshape(-1))
    l2 = _cmul(*l1, *l1)
    l4 = _cmul(*l2, *l2)
    row = jnp.arange(S5_SCAN_ROWS)[:, None]
    pw = []
    for s, (pr, pi) in ((1, l1), (2, l2), (4, l4)):
        pw += [jnp.where(row >= s, pr[None, :], 0.0), jnp.where(row >= s, pi[None, :], 0.0)]
    acc = [l1]
    for _ in range(S5_SCAN_ROWS - 1):
        acc.append(_cmul(*acc[-1], *l1))
    pw += [jnp.stack([a[0] for a in acc]), jnp.stack([a[1] for a in acc])]
    pw = jnp.stack(pw)

    tt = min(T, S5_TIME_TILE)
    h0r = h0[..., 0].reshape(b_, 1, S5_D_STATE)
    h0i = h0[..., 1].reshape(b_, 1, S5_D_STATE)
    pk_in, pk_st = S5_PACK * S5_GROUP, S5_PACK * S5_STATE

    def const3(b, t):
        return (0, 0, 0)

    state_spec = pl.BlockSpec((None, 1, S5_D_STATE), lambda b, t: (b, 0, 0))
    g, hr, hi = pl.pallas_call(
        _s5_kernel,
        grid=(b_, T // tt),
        in_specs=[pl.BlockSpec((None, tt, D_MODEL), lambda b, t: (b, t, 0)),
                  pl.BlockSpec((S5_N_PACKS, pk_in, pk_st), const3),
                  pl.BlockSpec((S5_N_PACKS, pk_in, pk_st), const3),
                  pl.BlockSpec((S5_N_PACKS, pk_st, pk_in), const3),
                  pl.BlockSpec((S5_N_PACKS, pk_st, pk_in), const3),
                  pl.BlockSpec((8, S5_SCAN_ROWS, S5_D_STATE), const3),
                  state_spec, state_spec,
                  pl.BlockSpec((1, D_MODEL), lambda b, t: (0, 0))],
        out_specs=[pl.BlockSpec((None, tt, D_MODEL), lambda b, t: (b, t, 0)), state_spec, state_spec],
        out_shape=[jax.ShapeDtypeStruct((b_, T, D_MODEL), jnp.float32),
                   jax.ShapeDtypeStruct((b_, 1, S5_D_STATE), jnp.float32),
                   jax.ShapeDtypeStruct((b_, 1, S5_D_STATE), jnp.float32)],
        scratch_shapes=[pltpu.VMEM((tt, S5_D_STATE), jnp.float32),
                        pltpu.VMEM((tt, S5_D_STATE), jnp.float32),
                        pltpu.VMEM((S5_SCAN_ROWS, S5_D_STATE), jnp.float32),
                        pltpu.VMEM((S5_SCAN_ROWS, S5_D_STATE), jnp.float32)],
        compiler_params=pltpu.CompilerParams(
            dimension_semantics=("parallel", "arbitrary"),
            vmem_limit_bytes=V7X_VMEM_LIMIT_BYTES),
        name="s5_scan",
    )(x, bre, bim, cre, cim, pw, h0r, h0i, d_skip.reshape(1, D_MODEL))
    out = matmul(g, w_glu_a) * jax.nn.sigmoid(matmul(g, w_glu_b))
    h_new = jnp.stack([hr.reshape(b_, S5_N_GROUPS, S5_STATE), hi.reshape(b_, S5_N_GROUPS, S5_STATE)], axis=-1)
    return out, h_new


def nsa_compress(kv, w1, w2, pe):
    b_, L = kv.shape[:2]
    span = NSA_CMP_BLOCK // NSA_CMP_STRIDE
    n_str = L // NSA_CMP_STRIDE
    n_cmp = n_str - span + 1
    chunks = kv.reshape(b_, n_str, NSA_CMP_STRIDE, NSA_N_KV, NSA_HEAD_DIM)
    blocks = jnp.concatenate([chunks[:, s:s + n_cmp] for s in range(span)], axis=2) + pe[:, None, :]
    flat = jnp.moveaxis(blocks, 3, 2).reshape(b_, n_cmp, NSA_N_KV, NSA_CMP_BLOCK * NSA_HEAD_DIM)
    return jax.nn.gelu(flat @ w1) @ w2


def cmp_to_sel(imp, n_sel):
    r = NSA_SEL_BLOCK // NSA_CMP_STRIDE
    span = NSA_CMP_BLOCK // NSA_CMP_STRIDE
    pad = jnp.pad(imp, [(0, 0)] * (imp.ndim - 1) + [(span - 1, span - 1)])
    return sum(pad[..., s:s + r * (n_sel - 1) + 1:r] for s in range(r + span - 1))


def nsa_cmp_slc_block(qb, tq, kcmp, vcmp, ks_blk, vs_blk, n_top):
    scale = NSA_HEAD_DIM ** -0.5
    b_ = qb.shape[0]
    n_cmp = kcmp.shape[1]
    n_sel = ks_blk.shape[2]
    cmp_end = jnp.arange(n_cmp) * NSA_CMP_STRIDE + (NSA_CMP_BLOCK - 1)
    cmask = (cmp_end[None, :] <= tq[:, None])[None, :, None, None, :]
    s = jnp.einsum('bqhgd,bnhd->bqhgn', qb, kcmp) * scale
    p = jax.nn.softmax(jnp.where(cmask, s, NEG_INF), axis=-1)
    p = jnp.where(cmask, p, 0.0)
    o_cmp = jnp.einsum('bqhgn,bnhd->bqhgd', p, vcmp)
    imp = cmp_to_sel(jnp.sum(p, axis=3), n_sel)
    blk = jnp.arange(n_sel)[None, :]
    cur = (tq // NSA_SEL_BLOCK)[:, None]
    forced = (blk == 0) | (blk == cur) | (blk == cur - 1)
    valid = blk <= cur
    score = jnp.where(forced[None, :, None, :], FORCE_SCORE,
                      jnp.where(valid[None, :, None, :], imp, -FORCE_SCORE))
    _, idx = lax.top_k(score, n_top)
    bi = jnp.arange(b_)[:, None, None, None]
    hi = jnp.arange(NSA_N_KV)[None, None, :, None]
    kg = ks_blk[bi, hi, idx]
    vg = vs_blk[bi, hi, idx]
    kpos = idx[..., None] * NSA_SEL_BLOCK + jnp.arange(NSA_SEL_BLOCK)
    smask = (kpos <= tq[None, :, None, None, None])[:, :, :, None]
    s2 = jnp.einsum('bqhgd,bqhnsd->bqhgns', qb, kg) * scale
    s2 = jnp.where(smask, s2, NEG_INF)
    p2 = jax.nn.softmax(s2.reshape(s2.shape[:4] + (-1,)), axis=-1).reshape(s2.shape)
    o_slc = jnp.einsum('bqhgns,bqhnsd->bqhgd', p2, vg)
    return o_cmp, o_slc


def nsa_cmp_slc(q, rows, q_pos, w_cmp1, w_cmp2, cmp_pe):
    b_, L = rows.shape[:2]
    lp = -(-L // NSA_SEL_BLOCK) * NSA_SEL_BLOCK
    rows = jnp.pad(rows, ((0, 0), (0, lp - L), (0, 0), (0, 0), (0, 0)))
    kcmp = nsa_compress(rows[:, :, 0], w_cmp1[0], w_cmp2[0], cmp_pe[0])
    vcmp = nsa_compress(rows[:, :, 1], w_cmp1[1], w_cmp2[1], cmp_pe[1])
    n_sel = lp // NSA_SEL_BLOCK

    def sel_blocks(t):
        return jnp.moveaxis(t.reshape(b_, n_sel, NSA_SEL_BLOCK, NSA_N_KV, NSA_HEAD_DIM), 3, 1)

    ks_blk, vs_blk = sel_blocks(rows[:, :, 2]), sel_blocks(rows[:, :, 3])
    n_top = min(NSA_N_SELECT, n_sel)
    T = q.shape[1]
    qbs = math.gcd(T, NSA_QBLOCK)
    nqb = T // qbs
    qb = jnp.moveaxis(q.reshape((b_, nqb, qbs) + q.shape[2:]), 1, 0)
    pb = q_pos.reshape(nqb, qbs)
    o_cmp, o_slc = lax.map(
        lambda a: nsa_cmp_slc_block(a[0], a[1], kcmp, vcmp, ks_blk, vs_blk, n_top), (qb, pb))
    return (jnp.moveaxis(o_cmp, 0, 1).reshape(q.shape), jnp.moveaxis(o_slc, 0, 1).reshape(q.shape))


def window_attend(qb, kb, vb, q_pos, k_pos):
    s = jnp.einsum('bnqhgd,bnshd->bnqhgs', qb, kb) * (NSA_HEAD_DIM ** -0.5)
    diff = q_pos[:, :, None] - k_pos[:, None, :]
    mask = ((diff >= 0) & (diff < NSA_WINDOW) & (k_pos[:, None, :] >= 0))[None, :, :, None, None, :]
    p = jax.nn.softmax(jnp.where(mask, s, NEG_INF), axis=-1)
    return jnp.einsum('bnqhgs,bnshd->bnqhgd', p, vb)


def nsa_window_prompt(q, win_rows):
    b_, T = q.shape[:2]
    qbs = math.gcd(T, NSA_WBLOCK)
    nqb = T // qbs
    pad = jnp.pad(win_rows, ((0, 0), (NSA_WINDOW, 0), (0, 0), (0, 0), (0, 0)))
    idx = jnp.arange(nqb)[:, None] * qbs + jnp.arange(NSA_WINDOW + qbs)[None, :]
    kvb = pad[:, idx]
    q_pos = jnp.arange(T).reshape(nqb, qbs)
    o = window_attend(q.reshape((b_, nqb, qbs) + q.shape[2:]), kvb[:, :, :, 0], kvb[:, :, :, 1],
                      q_pos, idx - NSA_WINDOW)
    return o.reshape(q.shape)


NSA_TQ = NSA_WBLOCK
NSA_SLC_CHUNK = 256
NSA_WIN_CHUNK = 128
NSA_ROWS = NSA_GQA * NSA_TQ
NSA_SEL_SHIFT = NSA_SEL_BLOCK.bit_length() - 1
assert 1 << NSA_SEL_SHIFT == NSA_SEL_BLOCK


def _nsa_stream_softmax(q, k_ref, v_ref, c_lo, c_hi, chunk, mask_fn, m_ref, l_ref, acc_ref):
    scale = NSA_HEAD_DIM ** -0.5
    m_ref[...] = jnp.full(m_ref.shape, NEG_INF, jnp.float32)
    l_ref[...] = jnp.zeros(l_ref.shape, jnp.float32)
    acc_ref[...] = jnp.zeros(acc_ref.shape, jnp.float32)

    def body(c, carry):
        start = pl.multiple_of(c * chunk, chunk)
        k = k_ref[pl.ds(start, chunk), :]
        v = v_ref[pl.ds(start, chunk), :]
        s = lax.dot_general(q, k, (((1,), (1,)), ((), ())), preferred_element_type=jnp.float32) * scale
        s = jnp.where(mask_fn(start), s, NEG_INF)
        m_old = m_ref[...]
        m_new = jnp.maximum(m_old, jnp.max(s, axis=-1, keepdims=True))
        alpha = jnp.exp(m_old - m_new)
        p = jnp.exp(s - m_new)
        l_ref[...] = alpha * l_ref[...] + jnp.sum(p, axis=-1, keepdims=True)
        acc_ref[...] = alpha * acc_ref[...] + jnp.dot(p.astype(jnp.bfloat16), v,
                                                      preferred_element_type=jnp.float32)
        m_ref[...] = m_new
        return carry

    lax.fori_loop(c_lo, c_hi, body, 0)
    return acc_ref[...] / l_ref[...]


def _nsa_prompt_kernel(q_ref, kc_ref, vc_ref, ks_ref, vs_ref, kw_ref, vw_ref,
                       ocmp_ref, oslc_ref, owin_ref, selexp_ref, m_ref, l_ref, acc_ref,
                       *, n_cmp, n_top):
    f32, bf16 = jnp.float32, jnp.bfloat16
    i = pl.program_id(2)
    t0 = i * NSA_TQ
    n_cp = kc_ref.shape[0]
    t_len = selexp_ref.shape[1]
    n_sel = t_len // NSA_SEL_BLOCK
    nt = (((1,), (1,)), ((), ()))
    q = jnp.concatenate([q_ref[:, g * NSA_HEAD_DIM:(g + 1) * NSA_HEAD_DIM] for g in range(NSA_GQA)], axis=0)
    tq = t0 + (lax.broadcasted_iota(jnp.int32, (NSA_ROWS, 1), 0) & (NSA_TQ - 1))

    def store_heads(o_ref, o):
        for g in range(NSA_GQA):
            o_ref[:, g * NSA_HEAD_DIM:(g + 1) * NSA_HEAD_DIM] = o[g * NSA_TQ:(g + 1) * NSA_TQ]

    s = lax.dot_general(q, kc_ref[...], nt, preferred_element_type=f32) * (NSA_HEAD_DIM ** -0.5)
    n_idx = lax.broadcasted_iota(jnp.int32, (NSA_ROWS, n_cp), 1)
    cmask = (n_idx * NSA_CMP_STRIDE + (NSA_CMP_BLOCK - 1) <= tq) & (n_idx < n_cmp)
    s = jnp.where(cmask, s, NEG_INF)
    e = jnp.exp(s - jnp.max(s, axis=-1, keepdims=True))
    p = jnp.where(cmask, e / jnp.sum(e, axis=-1, keepdims=True), 0.0)
    store_heads(ocmp_ref, jnp.dot(p.astype(bf16), vc_ref[...], preferred_element_type=f32))

    psum = p[0:NSA_TQ]
    for g in range(1, NSA_GQA):
        psum = psum + p[g * NSA_TQ:(g + 1) * NSA_TQ]
    jn = lax.broadcasted_iota(jnp.int32, (n_sel, n_cp), 0)
    nn = lax.broadcasted_iota(jnp.int32, (n_sel, n_cp), 1)
    r = NSA_SEL_BLOCK // NSA_CMP_STRIDE
    span = NSA_CMP_BLOCK // NSA_CMP_STRIDE
    pool = jnp.where((nn >= r * jn - (span - 1)) & (nn <= r * jn + (r - 1)), 1.0, 0.0).astype(bf16)
    imp = jnp.zeros((n_sel, NSA_TQ), f32)
    rest = psum
    for _ in range(3):
        part = rest.astype(bf16)
        imp = imp + lax.dot_general(pool, part, nt, preferred_element_type=f32)
        rest = rest - part.astype(f32)

    jidx = lax.broadcasted_iota(jnp.int32, (n_sel, NSA_TQ), 0)
    cur = (t0 + lax.broadcasted_iota(jnp.int32, (n_sel, NSA_TQ), 1)) >> NSA_SEL_SHIFT
    forced = (jidx == 0) | (jidx == cur) | (jidx == cur - 1)
    score = jnp.where(forced, FORCE_SCORE, jnp.where(jidx <= cur, imp, -FORCE_SCORE))
    rank = jnp.zeros((n_sel, NSA_TQ), jnp.int32)
    for k in range(n_sel):
        row = score[k:k + 1, :]
        before = (row > score) | ((row == score) & (k < jidx))
        rank = rank + jnp.where(before, 1, 0)
    sel_t = jnp.where(rank < n_top, 1.0, 0.0).astype(bf16)
    eye = jnp.where(lax.broadcasted_iota(jnp.int32, (NSA_TQ, NSA_TQ), 0)
                    == lax.broadcasted_iota(jnp.int32, (NSA_TQ, NSA_TQ), 1), 1.0, 0.0).astype(bf16)
    sel = lax.dot_general(eye, sel_t, nt, preferred_element_type=f32).astype(bf16)
    expand = jnp.where((lax.broadcasted_iota(jnp.int32, (n_sel, t_len), 1) >> NSA_SEL_SHIFT)
                       == lax.broadcasted_iota(jnp.int32, (n_sel, t_len), 0), 1.0, 0.0).astype(bf16)
    selexp_ref[...] = jnp.dot(sel, expand, preferred_element_type=f32)

    def slc_mask(start):
        kpos = start + lax.broadcasted_iota(jnp.int32, (NSA_ROWS, NSA_SLC_CHUNK), 1)
        chosen = jnp.concatenate([selexp_ref[:, pl.ds(start, NSA_SLC_CHUNK)]] * NSA_GQA, axis=0)
        return (chosen > 0.5) & (kpos <= tq)

    store_heads(oslc_ref, _nsa_stream_softmax(
        q, ks_ref, vs_ref, 0, (t0 + NSA_TQ - 1) // NSA_SLC_CHUNK + 1, NSA_SLC_CHUNK, slc_mask,
        m_ref, l_ref, acc_ref))

    def win_mask(start):
        diff = tq - (start + lax.broadcasted_iota(jnp.int32, (NSA_ROWS, NSA_WIN_CHUNK), 1))
        return (diff >= 0) & (diff < NSA_WINDOW)

    store_heads(owin_ref, _nsa_stream_softmax(
        q, kw_ref, vw_ref, jnp.maximum(i - NSA_WINDOW // NSA_WIN_CHUNK, 0), i + 1, NSA_WIN_CHUNK,
        win_mask, m_ref, l_ref, acc_ref))


def nsa_prompt_attention(q, kv, kcmp, vcmp):
    b_, T, _ = q.shape
    assert T % NSA_SLC_CHUNK == 0 and (T // NSA_SEL_BLOCK) % 8 == 0
    n_cmp = kcmp.shape[1]
    n_cp = T // NSA_CMP_STRIDE
    n_sel = T // NSA_SEL_BLOCK
    bf16 = jnp.bfloat16

    def cmp_layout(t):
        return jnp.pad(jnp.moveaxis(t, 2, 1), ((0, 0), (0, 0), (0, n_cp - n_cmp), (0, 0))).astype(bf16)

    kvb = kv.astype(bf16)
    width = NSA_GQA * NSA_HEAD_DIM
    q_spec = pl.BlockSpec((None, NSA_TQ, width), lambda b, h, i: (b, i, h))
    cmp_spec = pl.BlockSpec((None, None, n_cp, NSA_HEAD_DIM), lambda b, h, i: (b, h, 0, 0))

    def kv_spec(comp):
        return pl.BlockSpec((None, T, NSA_HEAD_DIM), lambda b, h, i: (b, 0, comp * NSA_N_KV + h))

    out = jax.ShapeDtypeStruct((b_, T, NSA_N_HEADS * NSA_HEAD_DIM), jnp.float32)
    return pl.pallas_call(
        functools.partial(_nsa_prompt_kernel, n_cmp=n_cmp, n_top=min(NSA_N_SELECT, n_sel)),
        grid=(b_, NSA_N_KV, T // NSA_TQ),
        in_specs=[q_spec, cmp_spec, cmp_spec, kv_spec(2), kv_spec(3), kv_spec(4), kv_spec(5)],
        out_specs=[q_spec, q_spec, q_spec],
        out_shape=[out, out, out],
        scratch_shapes=[pltpu.VMEM((NSA_TQ, T), jnp.float32),
                        pltpu.VMEM((NSA_ROWS, 1), jnp.float32),
                        pltpu.VMEM((NSA_ROWS, 1), jnp.float32),
                        pltpu.VMEM((NSA_ROWS, NSA_HEAD_DIM), jnp.float32)],
        compiler_params=pltpu.CompilerParams(
            dimension_semantics=("parallel", "parallel", "arbitrary"),
            vmem_limit_bytes=V7X_VMEM_LIMIT_BYTES),
        name="nsa_prompt",
    )(q.astype(bf16), cmp_layout(kcmp), cmp_layout(vcmp), kvb, kvb, kvb, kvb)


def nsa_prompt_mixer(x, w_q, w_kv, w_gate, b_gate, w_cmp1, w_cmp2, cmp_pe, w_out):
    b_, T, _ = x.shape
    q = matmul(x, w_q)
    kv = matmul(x, w_kv)
    kv6 = kv.reshape(b_, T, 6, NSA_N_KV, NSA_HEAD_DIM)
    kcmp = nsa_compress(kv6[:, :, 0], w_cmp1[0], w_cmp2[0], cmp_pe[0])
    vcmp = nsa_compress(kv6[:, :, 1], w_cmp1[1], w_cmp2[1], cmp_pe[1])
    o_cmp, o_slc, o_win = nsa_prompt_attention(q, kv, kcmp, vcmp)
    gate = jax.nn.sigmoid(matmul(x, w_gate) + b_gate).reshape(b_, T, NSA_N_HEADS, 3)

    def heads(t):
        return t.reshape(b_, T, NSA_N_HEADS, NSA_HEAD_DIM)

    o = gate[..., 0:1] * heads(o_cmp) + gate[..., 1:2] * heads(o_slc) + gate[..., 2:3] * heads(o_win)
    y = matmul(o.reshape(b_, T, NSA_N_HEADS * NSA_HEAD_DIM), w_out)
    return y, kv6[:, :, :4], kv6[:, :, 4:6][:, -min(NSA_WINDOW, T):]


NSA_ROW_WIDTH = 4 * NSA_N_KV * NSA_HEAD_DIM
NSA_HALF_WIDTH = NSA_ROW_WIDTH // 2
NSA_KVW = NSA_N_KV * NSA_HEAD_DIM
NSA_CMP_PAGES = 8
NSA_SLC_PAGES = 4


def _round_up(n, m):
    return -(-n // m) * m


def _log2(n):
    assert n > 0 and n & (n - 1) == 0
    return n.bit_length() - 1


def _nsa_compress_kernel(pt_ref, *refs):
    del pt_ref
    pages = refs[:NSA_CMP_PAGES]
    w1_ref, pe_ref, a_ref, b_ref = refs[NSA_CMP_PAGES:]
    page_rows = pages[0].shape[0]
    per_page = page_rows // NSA_CMP_STRIDE
    rows = NSA_CMP_PAGES * NSA_N_KV * per_page
    half = NSA_CMP_STRIDE * NSA_HEAD_DIM
    for comp in range(2):
        acc_a = jnp.zeros((rows, NSA_HEAD_DIM), jnp.float32)
        acc_b = jnp.zeros((rows, NSA_HEAD_DIM), jnp.float32)
        by_row = [jnp.swapaxes(
            pg[:, pl.ds(comp * NSA_KVW + h * NSA_HEAD_DIM, NSA_HEAD_DIM)].reshape(
                per_page, NSA_CMP_STRIDE, NSA_HEAD_DIM), 0, 1)
            for pg in pages for h in range(NSA_N_KV)]
        for j0 in range(0, NSA_CMP_STRIDE, 2):
            xa, xb = [], []
            for j in (j0, j0 + 1):
                x = jnp.concatenate([t[j] for t in by_row], axis=0)
                xa.append((x + pe_ref[comp, j:j + 1, :]).astype(jnp.bfloat16))
                xb.append((x + pe_ref[comp, NSA_CMP_STRIDE + j:NSA_CMP_STRIDE + j + 1, :]).astype(jnp.bfloat16))
            lo = j0 * NSA_HEAD_DIM
            acc_a = acc_a + jnp.dot(jnp.concatenate(xa, axis=1), w1_ref[comp, lo:lo + 2 * NSA_HEAD_DIM, :],
                                    preferred_element_type=jnp.float32)
            acc_b = acc_b + jnp.dot(jnp.concatenate(xb, axis=1),
                                    w1_ref[comp, half + lo:half + lo + 2 * NSA_HEAD_DIM, :],
                                    preferred_element_type=jnp.float32)
        shape = (NSA_CMP_PAGES, NSA_N_KV, per_page, NSA_HEAD_DIM)
        a_ref[comp] = acc_a.reshape(shape)
        b_ref[comp] = acc_b.reshape(shape)


def nsa_decode_compress(cache, page_ids, new_rows, w_cmp1, w_cmp2, cmp_pe):
    b_, n_pages = page_ids.shape
    page = cache.shape[1]
    T = new_rows.shape[1]
    pos0 = n_pages * page
    lp = _round_up(pos0 + T, NSA_SEL_BLOCK)
    n_cmp = lp // NSA_CMP_STRIDE - (NSA_CMP_BLOCK // NSA_CMP_STRIDE - 1)
    per_page = page // NSA_CMP_STRIDE
    assert n_pages % NSA_CMP_PAGES == 0 and NSA_CMP_BLOCK == 2 * NSA_CMP_STRIDE
    w1 = w_cmp1.astype(jnp.bfloat16)

    def page_spec(k):
        return pl.BlockSpec((None, page, NSA_HALF_WIDTH), lambda b, s, pt: (pt[b, NSA_CMP_PAGES * s + k], 0, 0))

    ab_shape = jax.ShapeDtypeStruct((b_, 2, n_pages, NSA_N_KV, per_page, NSA_HEAD_DIM), jnp.float32)
    ab_spec = pl.BlockSpec((None, 2, NSA_CMP_PAGES, NSA_N_KV, per_page, NSA_HEAD_DIM),
                           lambda b, s, pt: (b, 0, s, 0, 0, 0))
    part_a, part_b = pl.pallas_call(
        _nsa_compress_kernel,
        grid_spec=pltpu.PrefetchScalarGridSpec(
            num_scalar_prefetch=1,
            grid=(b_, n_pages // NSA_CMP_PAGES),
            in_specs=[page_spec(k) for k in range(NSA_CMP_PAGES)]
            + [pl.BlockSpec(w1.shape, lambda b, s, pt: (0, 0, 0)),
               pl.BlockSpec(cmp_pe.shape, lambda b, s, pt: (0, 0, 0))],
            out_specs=[ab_spec, ab_spec]),
        out_shape=[ab_shape, ab_shape],
        compiler_params=pltpu.CompilerParams(
            dimension_semantics=("parallel", "arbitrary"),
            vmem_limit_bytes=V7X_VMEM_LIMIT_BYTES),
        name="nsa_compress_pages",
    )(page_ids, *([cache] * NSA_CMP_PAGES), w1, cmp_pe)

    def strides(t):
        return jnp.transpose(t, (0, 1, 3, 2, 4, 5)).reshape(b_, 2, NSA_N_KV, n_pages * per_page, NSA_HEAD_DIM)

    n_tail = (lp - pos0) // NSA_CMP_STRIDE
    tail = jnp.pad(new_rows, ((0, 0), (0, lp - pos0 - T), (0, 0), (0, 0), (0, 0)))
    tail = jnp.transpose(tail.reshape(b_, n_tail, NSA_CMP_STRIDE, 2, NSA_N_KV, NSA_HEAD_DIM), (0, 3, 4, 1, 2, 5))
    w1s = w_cmp1.reshape(2, 2, NSA_CMP_STRIDE, NSA_HEAD_DIM, -1)
    pes = cmp_pe.reshape(2, 2, NSA_CMP_STRIDE, NSA_HEAD_DIM)
    tail_a = jnp.einsum('bchsjd,cjdk->bchsk', tail + pes[None, :, 0, None, None], w1s[:, 0])
    tail_b = jnp.einsum('bchsjd,cjdk->bchsk', tail + pes[None, :, 1, None, None], w1s[:, 1])
    full_a = jnp.concatenate([strides(part_a), tail_a], axis=3)
    full_b = jnp.concatenate([strides(part_b), tail_b], axis=3)
    hidden = jax.nn.gelu(full_a[:, :, :, :n_cmp] + full_b[:, :, :, 1:n_cmp + 1])
    out = jnp.einsum('bchnk,ckd->bcnhd', hidden, w_cmp2).reshape(b_, 2, n_cmp, NSA_KVW)
    return out[:, 0], out[:, 1]


def _nsa_softmax_rows(s, mask):
    s = jnp.where(mask, s, NEG_INF)
    e = jnp.exp(s - jnp.max(s, axis=-1, keepdims=True))
    return e / jnp.sum(e, axis=-1, keepdims=True)


def _nsa_decode_select_kernel(q_ref, kc_ref, vc_ref, wk_ref, wv_ref, ocmp_ref, owin_ref, sel_ref,
                              score_ref, rank_ref, *, n_cmp, n_sel, n_top, n_win, w_buf, pos0, t_new):
    f32, bf16 = jnp.float32, jnp.bfloat16
    nt = (((1,), (1,)), ((), ()))
    scale = NSA_HEAD_DIM ** -0.5
    rows = q_ref.shape[0]
    per_head = NSA_GQA * t_new
    q = q_ref[...]
    tq = pos0 + (lax.broadcasted_iota(jnp.int32, (rows, 1), 0) & (t_new - 1))

    def heads_out(o_ref, p, v_ref):
        for h in range(NSA_N_KV):
            o_ref[h * per_head:(h + 1) * per_head, :] = jnp.dot(
                p[h * per_head:(h + 1) * per_head].astype(bf16),
                v_ref[:, h * NSA_HEAD_DIM:(h + 1) * NSA_HEAD_DIM], preferred_element_type=f32)

    n_cp = kc_ref.shape[0]
    s = lax.dot_general(q, kc_ref[...], nt, preferred_element_type=f32) * scale
    n_idx = lax.broadcasted_iota(jnp.int32, (rows, n_cp), 1)
    cmask = (n_idx * NSA_CMP_STRIDE + (NSA_CMP_BLOCK - 1) <= tq) & (n_idx < n_cmp)
    p = jnp.where(cmask, _nsa_softmax_rows(s, cmask), 0.0)
    heads_out(ocmp_ref, p, vc_ref)

    psum = jnp.concatenate(
        [sum(p[h * per_head + g * t_new:h * per_head + (g + 1) * t_new] for g in range(NSA_GQA))
         for h in range(NSA_N_KV)], axis=0)
    n_sp = sel_ref.shape[1]
    cols = NSA_N_KV * t_new
    jn = lax.broadcasted_iota(jnp.int32, (n_sp, n_cp), 0)
    nn = lax.broadcasted_iota(jnp.int32, (n_sp, n_cp), 1)
    r = NSA_SEL_BLOCK // NSA_CMP_STRIDE
    span = NSA_CMP_BLOCK // NSA_CMP_STRIDE
    pool = jnp.where((nn >= r * jn - (span - 1)) & (nn <= r * jn + (r - 1)), 1.0, 0.0).astype(bf16)
    imp = jnp.zeros((n_sp, cols), f32)
    rest = psum
    for _ in range(3):
        part = rest.astype(bf16)
        imp = imp + lax.dot_general(pool, part, nt, preferred_element_type=f32)
        rest = rest - part.astype(f32)
    jidx = lax.broadcasted_iota(jnp.int32, (n_sp, cols), 0)
    cur = (pos0 + (lax.broadcasted_iota(jnp.int32, (n_sp, cols), 1) & (t_new - 1))) >> NSA_SEL_SHIFT
    forced = (jidx == 0) | (jidx == cur) | (jidx == cur - 1)
    score = jnp.where(forced, FORCE_SCORE, jnp.where(jidx <= cur, imp, -FORCE_SCORE))
    score_ref[...] = jnp.where(jidx < n_sel, score, -2.0 * FORCE_SCORE)
    rank_ref[...] = jnp.zeros(rank_ref.shape, jnp.int32)

    def rank_body(k, carry):
        row = score_ref[pl.ds(k, 1), :]
        sc = score_ref[...]
        before = (row > sc) | ((row == sc) & (k < jidx))
        rank_ref[...] = rank_ref[...] + jnp.where(before, 1, 0)
        return carry

    lax.fori_loop(0, n_sel, rank_body, 0)
    sel_t = jnp.where((rank_ref[...] < n_top) & (jidx < n_sel), 1.0, 0.0).astype(bf16)
    ri = lax.broadcasted_iota(jnp.int32, (rows, cols), 0)
    ci = lax.broadcasted_iota(jnp.int32, (rows, cols), 1)
    same = (((ri >> _log2(per_head)) == (ci >> _log2(t_new)))
            & ((ri & (t_new - 1)) == (ci & (t_new - 1))))
    spread = jnp.where(same, 1.0, 0.0).astype(bf16)
    sel_ref[...] = lax.dot_general(spread, sel_t, nt, preferred_element_type=f32).astype(bf16)

    s = lax.dot_general(q, wk_ref[...], nt, preferred_element_type=f32) * scale
    kidx = lax.broadcasted_iota(jnp.int32, (rows, wk_ref.shape[0]), 1)
    diff = tq - (pos0 - w_buf + kidx)
    wmask = (diff >= 0) & (diff < NSA_WINDOW) & (kidx < n_win) & (pos0 - w_buf + kidx >= 0)
    heads_out(owin_ref, _nsa_softmax_rows(s, wmask), wv_ref)


def _nsa_decode_slc_kernel(pt_ref, q_ref, sel_ref, new_ref, *refs, pos0, t_new):
    del pt_ref
    pages = refs[:NSA_SLC_PAGES]
    o_ref, m_ref, l_ref, acc_ref = refs[NSA_SLC_PAGES:]
    f32, bf16 = jnp.float32, jnp.bfloat16
    nt = (((1,), (1,)), ((), ()))
    step = pl.program_id(1)
    rows = q_ref.shape[0]
    page = pages[0].shape[0]
    per_head = NSA_GQA * t_new
    n_sp = sel_ref.shape[1]
    q = q_ref[...]
    tq = pos0 + (lax.broadcasted_iota(jnp.int32, (rows, 1), 0) & (t_new - 1))
    row_head = lax.broadcasted_iota(jnp.int32, (rows, page), 0) >> _log2(per_head)
    lane = lax.broadcasted_iota(jnp.int32, (rows, page), 1)
    blocks_per_page = page // NSA_SEL_BLOCK

    def attend(kv, page_index):
        kp = kv[:, :NSA_KVW].astype(bf16)
        vp = kv[:, NSA_KVW:].astype(bf16)
        s = lax.dot_general(q, kp, nt, preferred_element_type=f32) * (NSA_HEAD_DIM ** -0.5)
        jrow = lax.broadcasted_iota(jnp.int32, (n_sp, page), 0)
        jcol = page_index * blocks_per_page + (lax.broadcasted_iota(jnp.int32, (n_sp, page), 1) >> NSA_SEL_SHIFT)
        expand = jnp.where(jrow == jcol, 1.0, 0.0).astype(bf16)
        chosen = jnp.dot(sel_ref[...], expand, preferred_element_type=f32) > 0.5
        s = jnp.where(chosen & (page_index * page + lane <= tq), s, NEG_INF)
        m_old = m_ref[...]
        m_new = jnp.maximum(m_old, jnp.max(s, axis=-1, keepdims=True))
        alpha = jnp.exp(m_old - m_new)
        p = jnp.exp(s - m_new)
        l_ref[...] = alpha * l_ref[...] + jnp.sum(p, axis=-1, keepdims=True)
        p_heads = jnp.concatenate([jnp.where(row_head == h, p, 0.0) for h in range(NSA_N_KV)], axis=1).astype(bf16)
        v_heads = jnp.concatenate([vp[:, h * NSA_HEAD_DIM:(h + 1) * NSA_HEAD_DIM] for h in range(NSA_N_KV)], axis=0)
        acc_ref[...] = alpha * acc_ref[...] + jnp.dot(p_heads, v_heads, preferred_element_type=f32)
        m_ref[...] = m_new

    @pl.when(step == 0)
    def _():
        m_ref[...] = jnp.full(m_ref.shape, NEG_INF, f32)
        l_ref[...] = jnp.zeros(l_ref.shape, f32)
        acc_ref[...] = jnp.zeros(acc_ref.shape, f32)
        attend(new_ref[...], pos0 // page)

    for k, pg in enumerate(pages):
        attend(pg[...], step * NSA_SLC_PAGES + k)

    @pl.when(step == pl.num_programs(1) - 1)
    def _():
        o_ref[...] = acc_ref[...] / l_ref[...]


def nsa_decode_attention(q, kv, cache, page_ids, win_buf, w_cmp1, w_cmp2, cmp_pe):
    b_, T, _ = q.shape
    n_pages = page_ids.shape[1]
    page = cache.shape[1]
    pos0 = n_pages * page
    w_buf = win_buf.shape[1]
    assert T & (T - 1) == 0 and T <= NSA_SEL_BLOCK and pos0 % NSA_SEL_BLOCK == 0 and page % NSA_SEL_BLOCK == 0
    assert n_pages % NSA_SLC_PAGES == 0
    bf16 = jnp.bfloat16
    lp = _round_up(pos0 + T, NSA_SEL_BLOCK)
    n_sel = lp // NSA_SEL_BLOCK
    n_sp = _round_up(n_sel, 128)
    kv6 = kv.reshape(b_, T, 6, NSA_N_KV, NSA_HEAD_DIM)
    kc, vc = nsa_decode_compress(cache, page_ids, kv6[:, :, 0:2], w_cmp1, w_cmp2, cmp_pe)
    n_cmp = kc.shape[1]
    n_cp = _round_up(n_cmp, 128)
    pad_c = ((0, 0), (0, n_cp - n_cmp), (0, 0))
    kc, vc = jnp.pad(kc, pad_c).astype(bf16), jnp.pad(vc, pad_c).astype(bf16)

    rows = NSA_N_HEADS * T
    q5 = jnp.transpose(q.reshape(b_, T, NSA_N_KV, NSA_GQA, NSA_HEAD_DIM), (0, 2, 3, 1, 4))
    q_blk = jnp.einsum('bhgtd,hk->bhgtkd', q5, jnp.eye(NSA_N_KV, dtype=q.dtype))
    q_blk = q_blk.reshape(b_, rows, NSA_KVW).astype(bf16)

    n_win = w_buf + T
    n_wp = _round_up(n_win, 128)
    wk = jnp.concatenate([win_buf, kv6[:, :, 4:6]], axis=1)
    wk = jnp.pad(wk, ((0, 0), (0, n_wp - n_win), (0, 0), (0, 0), (0, 0))).astype(bf16)
    wkk, wkv = wk[:, :, 0].reshape(b_, n_wp, NSA_KVW), wk[:, :, 1].reshape(b_, n_wp, NSA_KVW)

    per_b = lambda n, w: pl.BlockSpec((None, n, w), lambda b: (b, 0, 0))
    o_shape = jax.ShapeDtypeStruct((b_, rows, NSA_HEAD_DIM), jnp.float32)
    o_cmp, o_win, sel = pl.pallas_call(
        functools.partial(_nsa_decode_select_kernel, n_cmp=n_cmp, n_sel=n_sel, n_top=min(NSA_N_SELECT, n_sel),
                          n_win=n_win, w_buf=w_buf, pos0=pos0, t_new=T),
        grid=(b_,),
        in_specs=[per_b(rows, NSA_KVW), per_b(n_cp, NSA_KVW), per_b(n_cp, NSA_KVW),
                  per_b(n_wp, NSA_KVW), per_b(n_wp, NSA_KVW)],
        out_specs=[per_b(rows, NSA_HEAD_DIM), per_b(rows, NSA_HEAD_DIM), per_b(rows, n_sp)],
        out_shape=[o_shape, o_shape, jax.ShapeDtypeStruct((b_, rows, n_sp), bf16)],
        scratch_shapes=[pltpu.VMEM((n_sp, NSA_N_KV * T), jnp.float32),
                        pltpu.VMEM((n_sp, NSA_N_KV * T), jnp.int32)],
        compiler_params=pltpu.CompilerParams(
            dimension_semantics=("parallel",), vmem_limit_bytes=V7X_VMEM_LIMIT_BYTES),
        name="nsa_decode_select",
    )(q_blk, kc, vc, wkk, wkv)

    new_slc = jnp.pad(kv6[:, :, 2:4].reshape(b_, T, NSA_HALF_WIDTH), ((0, 0), (0, page - T), (0, 0)))

    def page_spec(k):
        return pl.BlockSpec((None, page, NSA_HALF_WIDTH), lambda b, s, pt: (pt[b, NSA_SLC_PAGES * s + k], 0, 1))

    bs = lambda n, w: pl.BlockSpec((None, n, w), lambda b, s, pt: (b, 0, 0))
    o_slc = pl.pallas_call(
        functools.partial(_nsa_decode_slc_kernel, pos0=pos0, t_new=T),
        grid_spec=pltpu.PrefetchScalarGridSpec(
            num_scalar_prefetch=1,
            grid=(b_, n_pages // NSA_SLC_PAGES),
            in_specs=[bs(rows, NSA_KVW), bs(rows, n_sp), bs(page, NSA_HALF_WIDTH)]
            + [page_spec(k) for k in range(NSA_SLC_PAGES)],
            out_specs=bs(rows, NSA_HEAD_DIM),
            scratch_shapes=[pltpu.VMEM((rows, 1), jnp.float32),
                            pltpu.VMEM((rows, 1), jnp.float32),
                            pltpu.VMEM((rows, NSA_HEAD_DIM), jnp.float32)]),
        out_shape=o_shape,
        compiler_params=pltpu.CompilerParams(
            dimension_semantics=("parallel", "arbitrary"), vmem_limit_bytes=V7X_VMEM_LIMIT_BYTES),
        name="nsa_decode_slc",
    )(page_ids, q_blk, sel, new_slc, *([cache] * NSA_SLC_PAGES))

    def token_major(o):
        o = o.reshape(b_, NSA_N_KV, NSA_GQA, T, NSA_HEAD_DIM)
        return jnp.transpose(o, (0, 3, 1, 2, 4)).reshape(b_, T, NSA_N_HEADS * NSA_HEAD_DIM)

    return token_major(o_cmp), token_major(o_slc), token_major(o_win)


def nsa_decode_mixer(x, cache, page_ids, win_buf, w_q, w_kv, w_gate, b_gate, w_cmp1, w_cmp2, cmp_pe, w_out):
    b_, T, _ = x.shape
    q = matmul(x, w_q)
    kv = matmul(x, w_kv)
    kv6 = kv.reshape(b_, T, 6, NSA_N_KV, NSA_HEAD_DIM)
    o_cmp, o_slc, o_win = nsa_decode_attention(q, kv, cache, page_ids, win_buf, w_cmp1, w_cmp2, cmp_pe)
    gate = jax.nn.sigmoid(matmul(x, w_gate) + b_gate).reshape(b_, T, NSA_N_HEADS, 3)

    def heads(t):
        return t.reshape(b_, T, NSA_N_HEADS, NSA_HEAD_DIM)

    o = gate[..., 0:1] * heads(o_cmp) + gate[..., 1:2] * heads(o_slc) + gate[..., 2:3] * heads(o_win)
    y = matmul(o.reshape(b_, T, NSA_N_HEADS * NSA_HEAD_DIM), w_out)
    win_new = jnp.concatenate([win_buf, kv6[:, :, 4:6]], axis=1)[:, -win_buf.shape[1]:]
    return y, kv6[:, :, :4], win_new


def nsa_mixer(x, past_rows, win_buf, pos0, w_q, w_kv, w_gate, b_gate, w_cmp1, w_cmp2, cmp_pe, w_out):
    b_, T, _ = x.shape
    if past_rows is None:
        assert win_buf is None and pos0 == 0 and T % NSA_SEL_BLOCK == 0
        return nsa_prompt_mixer(x, w_q, w_kv, w_gate, b_gate, w_cmp1, w_cmp2, cmp_pe, w_out)
    q = matmul(x, w_q).reshape(b_, T, NSA_N_KV, NSA_GQA, NSA_HEAD_DIM)
    kv = matmul(x, w_kv).reshape(b_, T, 6, NSA_N_KV, NSA_HEAD_DIM)
    rows = kv[:, :, :4]
    full = rows if past_rows is None else jnp.concatenate([past_rows, rows], axis=1)
    q_pos = pos0 + jnp.arange(T)
    o_cmp, o_slc = nsa_cmp_slc(q, full, q_pos, w_cmp1, w_cmp2, cmp_pe)
    win_rows = kv[:, :, 4:6]
    if win_buf is None:
        o_win = nsa_window_prompt(q, win_rows)
        win_new = win_rows[:, -min(NSA_WINDOW, T):]
    else:
        w_b = win_buf.shape[1]
        wk = jnp.concatenate([win_buf, win_rows], axis=1)
        k_pos = pos0 - w_b + jnp.arange(w_b + T)
        o_win = window_attend(q[:, None], wk[:, None, :, 0], wk[:, None, :, 1],
                              q_pos[None], k_pos[None])[:, 0]
        win_new = wk[:, -w_b:]
    gate = jax.nn.sigmoid(matmul(x, w_gate) + b_gate).reshape(b_, T, NSA_N_KV, NSA_GQA, 3)
    o = gate[..., 0:1] * o_cmp + gate[..., 1:2] * o_slc + gate[..., 2:3] * o_win
    return matmul(o.reshape(b_, T, NSA_N_HEADS * NSA_HEAD_DIM), w_out), rows, win_new


def conv_ffn(x, hist, w_up, conv_w, conv_b, w_down):
    a, g = jnp.split(matmul(x, w_up), 2, axis=-1)
    ah = jnp.concatenate([hist, a], axis=1)
    a = causal_dwconv(ah, conv_w, conv_b)
    return matmul(jax.nn.gelu(a) * g, w_down), ah[:, -(FFN_CONV_W - 1):]


def kernel(x_prompt, x_sample, cache_nsa, state_nsa_win, state_ssd, state_ssd_conv, state_mlstm_c,
           state_mlstm_n, state_mlstm_m, state_mlstm_conv, state_s5, state_ffn_conv, page_table,
           ln_g, ln_b, ffn_w_up, ffn_conv_w, ffn_conv_b, ffn_w_down,
           ssd_w_in, ssd_conv_w, ssd_conv_b, ssd_dt_bias, ssd_a_log, ssd_d, ssd_norm_g, ssd_w_out,
           mlstm_w_up, mlstm_conv_w, mlstm_conv_b, mlstm_w_q, mlstm_w_k, mlstm_w_v, mlstm_w_if,
           mlstm_b_if, mlstm_skip, mlstm_norm_g, mlstm_w_down,
           s5_a_re, s5_a_im, s5_log_dt, s5_b_re, s5_b_im, s5_c_re, s5_c_im, s5_d, s5_w_glu_a, s5_w_glu_b,
           nsa_w_q, nsa_w_kv, nsa_w_gate, nsa_b_gate, nsa_w_cmp1, nsa_w_cmp2, nsa_cmp_pe, nsa_w_out):

    def bf16_stack(w):
        return w.astype(jnp.bfloat16)

    ffn_w_up, ffn_w_down = bf16_stack(ffn_w_up), bf16_stack(ffn_w_down)
    ssd_w_in, ssd_w_out = bf16_stack(ssd_w_in), bf16_stack(ssd_w_out)
    mlstm_w_up, mlstm_w_down = bf16_stack(mlstm_w_up), bf16_stack(mlstm_w_down)
    mlstm_w_if = bf16_stack(mlstm_w_if).reshape(-1, MLSTM_D_INNER, 2 * MLSTM_N_HEADS)
    s5_w_glu_a, s5_w_glu_b = bf16_stack(s5_w_glu_a), bf16_stack(s5_w_glu_b)
    nsa_w_q, nsa_w_kv, nsa_w_out = bf16_stack(nsa_w_q), bf16_stack(nsa_w_kv), bf16_stack(nsa_w_out)
    nsa_w_gate = bf16_stack(nsa_w_gate)

    def trunk(x, sample):
        b_, T, _ = x.shape
        dt_ = x.dtype
        pos0 = PAST_LEN if sample else 0
        o_nsa, o_win, o_ssd, o_ssdc, o_mc, o_mn, o_mm, o_mconv, o_s5, o_ffn = ([] for _ in range(10))
        for i in range(DEPTH):
            kind, j = i % N_MIXERS, i // N_MIXERS
            if kind == 0:
                hist = state_ssd_conv[j] if sample else jnp.zeros((b_, SSD_CONV_W - 1, SSD_CONV_DIM), dt_)
                h0 = state_ssd[j] if sample else jnp.zeros((b_, SSD_N_HEADS, SSD_HEADDIM, SSD_D_STATE), dt_)
                y, hist_new, h_new = ssd_mixer(x, hist, h0, (ssd_w_in, j), ssd_conv_w[j], ssd_conv_b[j],
                                               ssd_dt_bias[j], ssd_a_log[j], ssd_d[j], ssd_norm_g[j],
                                               (ssd_w_out, j))
                o_ssd.append(h_new)
                o_ssdc.append(hist_new)
            elif kind == 1:
                hist = state_mlstm_conv[j] if sample else jnp.zeros((b_, MLSTM_CONV_W - 1, MLSTM_D_INNER), dt_)
                c0 = state_mlstm_c[j] if sample else jnp.zeros((b_, MLSTM_N_HEADS, MLSTM_HEAD_DIM, MLSTM_HEAD_DIM), dt_)
                n0 = state_mlstm_n[j] if sample else jnp.zeros((b_, MLSTM_N_HEADS, MLSTM_HEAD_DIM), dt_)
                m0 = state_mlstm_m[j] if sample else jnp.zeros((b_, MLSTM_N_HEADS), dt_)
                y, hist_new, c, n, m = mlstm_mixer(x, hist, c0, n0, m0, (mlstm_w_up, j), mlstm_conv_w[j],
                                                   mlstm_conv_b[j], mlstm_w_q[j], mlstm_w_k[j], mlstm_w_v[j],
                                                   [(mlstm_w_if, 3 * j + part) for part in range(3)],
                                                   mlstm_b_if[j], mlstm_skip[j],
                                                   mlstm_norm_g[j], (mlstm_w_down, j))
                o_mc.append(c)
                o_mn.append(n)
                o_mm.append(m)
                o_mconv.append(hist_new)
            elif kind == 2:
                h0 = state_s5[j] if sample else jnp.zeros((b_, S5_N_GROUPS, S5_STATE, 2), dt_)
                y, h_new = s5_mixer(x, h0, s5_a_re[j], s5_a_im[j], s5_log_dt[j], s5_b_re[j], s5_b_im[j],
                                    s5_c_re[j], s5_c_im[j], s5_d[j], (s5_w_glu_a, j), (s5_w_glu_b, j))
                o_s5.append(h_new)
            else:
                nsa_w = ((nsa_w_q, j), (nsa_w_kv, j), (nsa_w_gate, j), nsa_b_gate[j], nsa_w_cmp1[j], nsa_w_cmp2[j],
                         nsa_cmp_pe[j], (nsa_w_out, j))
                if sample:
                    n_pool, page = cache_nsa.shape[1:3]
                    assert pos0 == page_table.shape[1] * page
                    y, rows, win_new = nsa_decode_mixer(
                        x, cache_nsa.reshape(-1, page, NSA_ROW_WIDTH), page_table + j * n_pool,
                        state_nsa_win[j], *nsa_w)
                else:
                    y, rows, win_new = nsa_mixer(x, None, None, pos0, *nsa_w)
                o_nsa.append(rows)
                o_win.append(win_new)
            x = layer_norm(DEEPNORM_ALPHA * x + y, ln_g[i, 0], ln_b[i, 0])
            fhist = state_ffn_conv[i] if sample else jnp.zeros((b_, FFN_CONV_W - 1, FFN_DIM), dt_)
            y, fhist_new = conv_ffn(x, fhist, (ffn_w_up, i), ffn_conv_w[i], ffn_conv_b[i], (ffn_w_down, i))
            o_ffn.append(fhist_new)
            x = layer_norm(DEEPNORM_ALPHA * x + y, ln_g[i, 1], ln_b[i, 1])
        st = jnp.stack
        return (x, st(o_nsa), st(o_win), st(o_ssd), st(o_ssdc), st(o_mc), st(o_mn), st(o_mm),
                st(o_mconv), st(o_s5), st(o_ffn))

    (y_prompt, nsa_p, win_p, ssd_p, ssdc_p, mc_p, mn_p, mm_p, mconv_p, s5_p, ffn_p) = trunk(x_prompt, False)
    (y_sample, nsa_s, win_s, ssd_s, ssdc_s, mc_s, mn_s, mm_s, mconv_s, s5_s, ffn_s) = trunk(x_sample, True)
    return (y_prompt, y_sample, nsa_p, nsa_s, win_p, win_s, ssd_p, ssd_s, ssdc_p, ssdc_s, mc_p, mc_s,
            mn_p, mn_s, mm_p, mm_s, mconv_p, mconv_s, s5_p, s5_s, ffn_p, ffn_s)
```

```python
import functools
import math

import jax
import jax.numpy as jnp
from jax import lax
from jax.experimental import pallas as pl
from jax.experimental.pallas import tpu as pltpu

D_MODEL = 2048
DEPTH = 4
PAST_LEN = 16384
N_MIXERS = 4

DEEPNORM_ALPHA = (2.0 * DEPTH) ** 0.25
LN_EPS = 1e-5
RMS_EPS = 1e-5
NEG_INF = -1e30
FORCE_SCORE = 1e4

SSD_D_INNER = 2 * D_MODEL
SSD_HEADDIM = 64
SSD_N_HEADS = SSD_D_INNER // SSD_HEADDIM
SSD_N_GROUPS = 8
SSD_D_STATE = 128
SSD_CONV_W = 4
SSD_CHUNK = 256
SSD_CONV_DIM = SSD_D_INNER + 2 * SSD_N_GROUPS * SSD_D_STATE

MLSTM_D_INNER = 2 * D_MODEL
MLSTM_N_HEADS = 4
MLSTM_HEAD_DIM = MLSTM_D_INNER // MLSTM_N_HEADS
MLSTM_CONV_W = 4
MLSTM_CHUNK = 64

S5_GROUP = 16
S5_N_GROUPS = D_MODEL // S5_GROUP
S5_STATE = 64

NSA_N_HEADS = 16
NSA_N_KV = 4
NSA_HEAD_DIM = D_MODEL // NSA_N_HEADS
NSA_GQA = NSA_N_HEADS // NSA_N_KV
NSA_CMP_BLOCK = 32
NSA_CMP_STRIDE = 16
NSA_SEL_BLOCK = 64
NSA_N_SELECT = 16
NSA_WINDOW = 512
NSA_QBLOCK = 32
NSA_WBLOCK = 128

FFN_DIM = 5632
FFN_CONV_W = 3

V7X_VMEM_LIMIT_BYTES = 48 * 1024 * 1024


def _mm_kernel(x_ref, w_ref, o_ref):
    o_ref[...] = jnp.dot(x_ref[...], w_ref[...], preferred_element_type=jnp.float32)


def _pick(dim, target):
    if dim <= target:
        return dim
    t = target
    while dim % t:
        t //= 2
    return t


def _mm_tiles(M, K, N):
    tm = _pick(M, 1024)
    tn = N if N <= 512 else (1024 if K <= 2048 else 512)
    double_buffered = 2 * (tm * K * 2 + K * tn * 2 + tm * tn * 4)
    assert double_buffered <= V7X_VMEM_LIMIT_BYTES, (M, K, N)
    return tm, tn


def matmul(x, w):
    stack, s = w if isinstance(w, tuple) else (w[None], 0)
    _, K, N = stack.shape
    lead = x.shape[:-1]
    x2 = x.astype(jnp.bfloat16).reshape(-1, K)
    M = x2.shape[0]
    tm, tn = _mm_tiles(M, K, N)
    out = pl.pallas_call(
        _mm_kernel,
        grid=(M // tm, pl.cdiv(N, tn)),
        in_specs=[pl.BlockSpec((tm, K), lambda i, j: (i, 0)),
                  pl.BlockSpec((None, K, tn), lambda i, j: (s, 0, j))],
        out_specs=pl.BlockSpec((tm, tn), lambda i, j: (i, j)),
        out_shape=jax.ShapeDtypeStruct((M, N), jnp.float32),
        compiler_params=pltpu.CompilerParams(
            dimension_semantics=("parallel", "arbitrary"),
            vmem_limit_bytes=V7X_VMEM_LIMIT_BYTES),
        name="matmul",
    )(x2, stack.astype(jnp.bfloat16))
    return out.reshape(lead + (N,))


def layer_norm(x, g, b):
    mu = jnp.mean(x, axis=-1, keepdims=True)
    var = jnp.mean(jnp.square(x - mu), axis=-1, keepdims=True)
    return (x - mu) * lax.rsqrt(var + LN_EPS) * g + b


def group_rms_norm(y, g, n_groups):
    yg = y.reshape(y.shape[:-1] + (n_groups, -1))
    yg = yg * lax.rsqrt(jnp.mean(yg * yg, axis=-1, keepdims=True) + RMS_EPS)
    return yg.reshape(y.shape) * g


def head_layer_norm(h, g):
    mu = jnp.mean(h, axis=-1, keepdims=True)
    var = jnp.mean(jnp.square(h - mu), axis=-1, keepdims=True)
    hn = (h - mu) * lax.rsqrt(var + LN_EPS)
    return hn.reshape(h.shape[:2] + (-1,)) * g


def causal_dwconv(x_hist, w, b):
    width = w.shape[0]
    T = x_hist.shape[1] - (width - 1)
    out = b
    for k in range(width):
        out = out + w[k] * x_hist[:, k:k + T]
    return out


CONV_TIME_TILE = 256
CONV_CHAN_TILE = 512
CONV_HALO = 8


def _conv_act_kernel(cur_ref, prev_ref, hist_ref, w_ref, b_ref, *rest, width, act):
    t = pl.program_id(1)
    cur = cur_ref[...]
    rows = cur.shape[0]
    ext = jnp.concatenate([jnp.where(t == 0, hist_ref[...], prev_ref[...]), cur], axis=0)
    acc = b_ref[...]
    for k in range(width):
        back = width - 1 - k
        acc = acc + w_ref[k:k + 1, :] * ext[CONV_HALO - back:CONV_HALO - back + rows]
    if act == "silu":
        o_ref, = rest
        o_ref[...] = jax.nn.silu(acc).astype(o_ref.dtype)
    else:
        g_ref, o_ref = rest
        o_ref[...] = (jax.nn.gelu(acc) * g_ref[...]).astype(o_ref.dtype)


def conv_act(src, col0, chans, hist, w, b, act, gate_col0=None, out_dtype=jnp.float32):
    b_, T, _ = src.shape
    width = w.shape[0]
    tt = min(T, CONV_TIME_TILE)
    ct = CONV_CHAN_TILE
    assert T % tt == 0 and tt % CONV_HALO == 0 and chans % ct == 0 and col0 % ct == 0 and width <= CONV_HALO + 1
    hist8 = jnp.pad(hist, ((0, 0), (CONV_HALO - (width - 1), 0), (0, 0)))
    halo_blocks = tt // CONV_HALO
    cb0 = col0 // ct
    in_specs = [pl.BlockSpec((None, tt, ct), lambda b, t, c: (b, t, cb0 + c)),
                pl.BlockSpec((None, CONV_HALO, ct), lambda b, t, c: (b, jnp.maximum(t * halo_blocks - 1, 0), cb0 + c)),
                pl.BlockSpec((None, CONV_HALO, ct), lambda b, t, c: (b, 0, c)),
                pl.BlockSpec((width, ct), lambda b, t, c: (0, c)),
                pl.BlockSpec((1, ct), lambda b, t, c: (0, c))]
    args = [src, src, hist8, w, b.reshape(1, chans)]
    if act == "gelu_gate":
        assert gate_col0 % ct == 0
        gb0 = gate_col0 // ct
        in_specs.append(pl.BlockSpec((None, tt, ct), lambda b, t, c: (b, t, gb0 + c)))
        args.append(src)
    return pl.pallas_call(
        functools.partial(_conv_act_kernel, width=width, act=act),
        grid=(b_, T // tt, chans // ct),
        in_specs=in_specs,
        out_specs=pl.BlockSpec((None, tt, ct), lambda b, t, c: (b, t, c)),
        out_shape=jax.ShapeDtypeStruct((b_, T, chans), out_dtype),
        compiler_params=pltpu.CompilerParams(
            dimension_semantics=("parallel", "parallel", "parallel"),
            vmem_limit_bytes=V7X_VMEM_LIMIT_BYTES),
        name="conv_act",
    )(*args)


def _blockdiag_coefs(w):
    nb, bs, _ = w.shape
    shifts = jnp.stack([jnp.eye(bs, k=d, dtype=w.dtype) for d in range(-(bs - 1), bs)])
    return jnp.einsum('ncd,kcd->knd', w, shifts).reshape(2 * bs - 1, nb * bs)


def _mlstm_qkv_kernel(cur_ref, prev_ref, hist_ref, w_ref, b_ref, cq_ref, ck_ref, cv_ref,
                      xc_ref, q_ref, k_ref, v_ref, *, width, bs):
    t = pl.program_id(1)
    cur = cur_ref[...]
    rows, lanes = cur.shape
    ext = jnp.concatenate([jnp.where(t == 0, hist_ref[...], prev_ref[...]), cur], axis=0)
    acc = b_ref[...]
    for k in range(width):
        back = width - 1 - k
        acc = acc + w_ref[k:k + 1, :] * ext[CONV_HALO - back:CONV_HALO - back + rows]
    xc = jax.nn.silu(acc)
    xc_ref[...] = xc

    def project(x, coef_refs):
        outs = [jnp.zeros(x.shape, jnp.float32) for _ in coef_refs]
        for d in range(-(bs - 1), bs):
            moved = x if d == 0 else pltpu.roll(x, d % lanes, 1)
            row = bs - 1 + d
            outs = [o + c[row:row + 1, :] * moved for o, c in zip(outs, coef_refs)]
        return outs

    q, k = project(xc, (cq_ref, ck_ref))
    v, = project(cur, (cv_ref,))
    q_ref[...] = q.astype(q_ref.dtype)
    k_ref[...] = k.astype(k_ref.dtype)
    v_ref[...] = v.astype(v_ref.dtype)


def mlstm_qkv(up, hist, conv_w, conv_b, w_q, w_k, w_v):
    b_, T, _ = up.shape
    chans = MLSTM_D_INNER
    width = conv_w.shape[0]
    bs = w_q.shape[1]
    tt = min(T, CONV_TIME_TILE)
    ct = CONV_CHAN_TILE
    assert T % tt == 0 and tt % CONV_HALO == 0 and chans % ct == 0 and ct % bs == 0
    hist8 = jnp.pad(hist, ((0, 0), (CONV_HALO - (width - 1), 0), (0, 0)))
    halo_blocks = tt // CONV_HALO
    tile = pl.BlockSpec((None, tt, ct), lambda b, t, c: (b, t, c))
    coef = pl.BlockSpec((2 * bs - 1, ct), lambda b, t, c: (0, c))
    shape = lambda dt: jax.ShapeDtypeStruct((b_, T, chans), dt)
    return pl.pallas_call(
        functools.partial(_mlstm_qkv_kernel, width=width, bs=bs),
        grid=(b_, T // tt, chans // ct),
        in_specs=[tile,
                  pl.BlockSpec((None, CONV_HALO, ct), lambda b, t, c: (b, jnp.maximum(t * halo_blocks - 1, 0), c)),
                  pl.BlockSpec((None, CONV_HALO, ct), lambda b, t, c: (b, 0, c)),
                  pl.BlockSpec((width, ct), lambda b, t, c: (0, c)),
                  pl.BlockSpec((1, ct), lambda b, t, c: (0, c)),
                  coef, coef, coef],
        out_specs=[tile, tile, tile, tile],
        out_shape=[shape(jnp.float32), shape(jnp.bfloat16), shape(jnp.bfloat16), shape(jnp.bfloat16)],
        compiler_params=pltpu.CompilerParams(
            dimension_semantics=("parallel", "parallel", "parallel"),
            vmem_limit_bytes=V7X_VMEM_LIMIT_BYTES),
        name="mlstm_qkv",
    )(up, up, hist8, conv_w, conv_b.reshape(1, chans),
      _blockdiag_coefs(w_q), _blockdiag_coefs(w_k), _blockdiag_coefs(w_v))


def conv_tail(hist, src, col0, chans):
    keep = hist.shape[1]
    return jnp.concatenate([hist, src[:, -keep:, col0:col0 + chans]], axis=1)[:, -keep:]


def blockdiag(x, w):
    nb, bs, _ = w.shape
    y = jnp.einsum('btnc,ncd->btnd', x.reshape(x.shape[:2] + (nb, bs)), w)
    return y.reshape(x.shape[:2] + (nb * bs,))


def segsum(x):
    T = x.shape[-1]
    xr = jnp.broadcast_to(x[..., :, None], x.shape + (T,))
    strict = jnp.tril(jnp.ones((T, T), bool), -1)
    xs = jnp.cumsum(jnp.where(strict, xr, 0), axis=-2)
    return jnp.where(jnp.tril(jnp.ones((T, T), bool)), xs, -jnp.inf)


def ssd_scan(xs, dt, a, bm, cm, h0):
    b_, T = xs.shape[:2]
    cl = math.gcd(T, SSD_CHUNK)
    nc = T // cl

    def chunk(t):
        return t.reshape((b_, nc, cl) + t.shape[2:])

    xc, dtc, bc, cc = chunk(xs), chunk(dt), chunk(bm), chunk(cm)
    dt_t = jnp.moveaxis(dtc, 2, -1)
    da = dt_t * a[:, :, None]
    acs = jnp.cumsum(da, axis=-1)
    decay_in = jnp.exp(segsum(da))
    cb = jnp.einsum('bclgn,bcsgn->bcgls', cc, bc)
    w_diag = cb[:, :, :, None] * decay_in * dt_t[..., None, :]
    y_diag = jnp.einsum('bcgrls,bcsgrp->bclgrp', w_diag, xc)
    w_state = jnp.exp(acs[..., -1:] - acs) * dt_t
    states = jnp.einsum('bclgn,bcgrl,bclgrp->bcgrpn', bc, w_state, xc)
    states = jnp.concatenate([h0[:, None], states], axis=1)
    tot = jnp.pad(jnp.moveaxis(acs[..., -1], 1, -1), ((0, 0), (0, 0), (0, 0), (1, 0)))
    decay_chunk = jnp.exp(segsum(tot))
    new_states = jnp.einsum('bgrzc,bcgrpn->bzgrpn', decay_chunk, states)
    y_off = jnp.einsum('bclgn,bcgrpn,bcgrl->bclgrp', cc, new_states[:, :-1], jnp.exp(acs))
    return (y_diag + y_off).reshape(xs.shape), new_states[:, -1]


SSD_HEADS_PER_GROUP = SSD_N_HEADS // SSD_N_GROUPS
SSD_GROUP_WIDTH = SSD_HEADS_PER_GROUP * SSD_HEADDIM
SSD_HEAD_SHIFT = SSD_HEADDIM.bit_length() - 1
assert 1 << SSD_HEAD_SHIFT == SSD_HEADDIM and SSD_GROUP_WIDTH == SSD_D_INNER // SSD_N_GROUPS


def _ssd_kernel(x_ref, z_ref, cm_ref, bmt_ref, acs_ref, dt_ref, rows_ref, h0_ref, d_ref, g_ref,
                y_ref, h_out_ref, h_ref):
    f32, bf16 = jnp.float32, jnp.bfloat16
    ck = pl.program_id(2)
    L, W = x_ref.shape

    @pl.when(ck == 0)
    def _():
        h_ref[...] = h0_ref[...]

    x = x_ref[...]
    xb = x.astype(bf16)
    cmb = cm_ref[...].astype(bf16)
    bmt = bmt_ref[...]
    acs = acs_ref[...]
    cb = jnp.dot(cmb, bmt, preferred_element_type=f32)
    causal = (lax.broadcasted_iota(jnp.int32, (L, L), 0) >= lax.broadcasted_iota(jnp.int32, (L, L), 1))
    lane_head = lax.broadcasted_iota(jnp.int32, (L, W), 1) >> SSD_HEAD_SHIFT
    y = jnp.zeros((L, W), f32)
    for r in range(SSD_HEADS_PER_GROUP):
        acs_col = acs[:, r * SSD_HEADDIM:r * SSD_HEADDIM + 1]
        acs_row = rows_ref[r:r + 1, :]
        dt_row = rows_ref[SSD_HEADS_PER_GROUP + r:SSD_HEADS_PER_GROUP + r + 1, :]
        decay = jnp.exp(jnp.where(causal, acs_col - acs_row, -jnp.inf))
        w = (cb * decay * dt_row).astype(bf16)
        y = jnp.where(lane_head == r, jnp.dot(w, xb, preferred_element_type=f32), y)
    total = acs[L - 1:L, :]
    xw = (x * (jnp.exp(total - acs) * dt_ref[...])).astype(bf16)
    h_t = h_ref[...]
    y = y + jnp.dot(cmb, h_t.astype(bf16), preferred_element_type=f32) * jnp.exp(acs)
    h_ref[...] = jnp.exp(total) * h_t + jnp.dot(bmt, xw, preferred_element_type=f32)
    y = (y + d_ref[...] * x) * jax.nn.silu(z_ref[...])
    y_ref[...] = y * lax.rsqrt(jnp.mean(y * y, axis=-1, keepdims=True) + RMS_EPS) * g_ref[...]

    @pl.when(ck == pl.num_programs(2) - 1)
    def _():
        h_out_ref[...] = h_ref[...]


def ssd_cell(zx, xbc, dt, a, d_skip, norm_g, h0):
    b_, T, _ = xbc.shape
    G, R, P, N, W = SSD_N_GROUPS, SSD_HEADS_PER_GROUP, SSD_HEADDIM, SSD_D_STATE, SSD_GROUP_WIDTH
    L = math.gcd(T, SSD_CHUNK)
    nc = T // L
    acs = jnp.cumsum((dt * a).reshape(b_, nc, L, SSD_N_HEADS), axis=2)
    dtc = dt.reshape(b_, nc, L, SSD_N_HEADS)

    def rows(t):
        return jnp.transpose(t.reshape(b_, nc, L, G, R), (0, 3, 1, 4, 2))

    rowpack = jnp.concatenate([rows(acs), rows(dtc)], axis=3)
    acs_e = jnp.repeat(acs.reshape(b_, T, SSD_N_HEADS), P, axis=-1)
    dt_e = jnp.repeat(dt, P, axis=-1)
    bmt = jnp.transpose(xbc[..., SSD_D_INNER:SSD_D_INNER + G * N].astype(jnp.bfloat16).reshape(b_, T, G, N),
                        (0, 2, 3, 1))
    h0t = jnp.transpose(h0.reshape(b_, G, R, P, N), (0, 1, 4, 2, 3)).reshape(b_, G, N, W)
    chan = lambda b, g, c: (b, c, g)
    cm_block0 = (SSD_D_INNER + G * N) // N
    state_spec = pl.BlockSpec((None, None, N, W), lambda b, g, c: (b, g, 0, 0))
    row_spec = pl.BlockSpec((1, W), lambda b, g, c: (0, g))
    y, ht = pl.pallas_call(
        _ssd_kernel,
        grid=(b_, G, nc),
        in_specs=[pl.BlockSpec((None, L, W), chan),
                  pl.BlockSpec((None, L, W), chan),
                  pl.BlockSpec((None, L, N), lambda b, g, c: (b, c, cm_block0 + g)),
                  pl.BlockSpec((None, None, N, L), lambda b, g, c: (b, g, 0, c)),
                  pl.BlockSpec((None, L, W), chan),
                  pl.BlockSpec((None, L, W), chan),
                  pl.BlockSpec((None, None, None, 2 * R, L), lambda b, g, c: (b, g, c, 0, 0)),
                  state_spec, row_spec, row_spec],
        out_specs=[pl.BlockSpec((None, L, W), chan), state_spec],
        out_shape=[jax.ShapeDtypeStruct((b_, T, SSD_D_INNER), jnp.float32),
                   jax.ShapeDtypeStruct((b_, G, N, W), jnp.float32)],
        scratch_shapes=[pltpu.VMEM((N, W), jnp.float32)],
        compiler_params=pltpu.CompilerParams(
            dimension_semantics=("parallel", "parallel", "arbitrary"),
            vmem_limit_bytes=V7X_VMEM_LIMIT_BYTES),
        name="ssd_cell",
    )(xbc, zx, xbc, bmt, acs_e, dt_e, rowpack, h0t,
      jnp.repeat(d_skip, P).reshape(1, SSD_D_INNER), norm_g.reshape(1, SSD_D_INNER))
    h_new = jnp.transpose(ht.reshape(b_, G, N, R, P), (0, 1, 3, 4, 2)).reshape(b_, SSD_N_HEADS, P, N)
    return y, h_new


def ssd_mixer(x, conv_hist, h0, w_in, conv_w, conv_b, dt_bias, a_log, d_skip, norm_g, w_out):
    zx = matmul(x, w_in)
    xbc = conv_act(zx, SSD_D_INNER, SSD_CONV_DIM, conv_hist, conv_w, conv_b, "silu")
    dt = jax.nn.softplus(zx[..., SSD_D_INNER + SSD_CONV_DIM:] + dt_bias)
    y, h_new = ssd_cell(zx, xbc, dt, -jnp.exp(a_log), d_skip, norm_g, h0)
    return matmul(y, w_out), conv_tail(conv_hist, zx, SSD_D_INNER, SSD_CONV_DIM), h_new


def mlstm_chunked(q, k, v, i_pre, logf, c0, n0, m0):
    b_, T, H, _ = q.shape
    cl = math.gcd(T, MLSTM_CHUNK)
    nc = T // cl
    causal = jnp.tril(jnp.ones((cl, cl), bool))

    def chunk(t):
        return jnp.moveaxis(t.reshape((b_, nc, cl) + t.shape[2:]), 1, 0)

    def step(carry, inp):
        c, n, m = carry
        qc, kc, vc, ic, fc = inp
        bcum = jnp.cumsum(fc, axis=1)
        dmat = bcum[:, :, None, :] - bcum[:, None, :, :] + ic[:, None, :, :]
        dmat = jnp.where(causal[None, :, :, None], dmat, -jnp.inf)
        inter = m[:, None, :] + bcum
        m_t = jnp.maximum(inter, jnp.max(dmat, axis=2))
        w = jnp.exp(dmat - m_t[:, :, None, :])
        s = jnp.einsum('bthd,bshd->btsh', qc, kc) * w
        sc_inter = jnp.exp(inter - m_t)
        num = (jnp.einsum('btsh,bshe->bthe', s, vc)
               + sc_inter[..., None] * jnp.einsum('bthd,bhde->bthe', qc, c))
        den = jnp.sum(s, axis=2) + sc_inter * jnp.einsum('bthd,bhd->bth', qc, n)
        h = num / jnp.maximum(jnp.abs(den), jnp.exp(-m_t))[..., None]
        m_new = m_t[:, -1]
        decay_s = jnp.exp(bcum[:, -1:] - bcum + ic - m_new[:, None])
        sc_c = jnp.exp(m + bcum[:, -1] - m_new)
        c_new = sc_c[..., None, None] * c + jnp.einsum('bsh,bshd,bshe->bhde', decay_s, kc, vc)
        n_new = sc_c[..., None] * n + jnp.einsum('bsh,bshd->bhd', decay_s, kc)
        return (c_new, n_new, m_new), h

    (c, n, m), hs = lax.scan(step, (c0, n0, m0),
                             (chunk(q), chunk(k), chunk(v), chunk(i_pre), chunk(logf)))
    h = jnp.moveaxis(hs, 0, 1).reshape(b_, T, H, -1)
    return h, c, n, m


MLSTM_STEP = 256
MLSTM_NORM_LANES = 128


def _mlstm_kernel(q_ref, kt_ref, v_ref, acol_ref, grow_ref, c0_ref, n0_ref, m0_ref, g_ref,
                  h_ref, c_out_ref, n_out_ref, m_out_ref, c_ref, m_ref):
    f32, bf16 = jnp.float32, jnp.bfloat16
    step = pl.program_id(2)
    L, D = q_ref.shape

    @pl.when(step == 0)
    def _():
        c_ref[:, :D] = c0_ref[...]
        c_ref[:, D:] = n0_ref[...]
        m_ref[...] = m0_ref[...]

    q = q_ref[...]
    kt = kt_ref[...]
    a_col = acol_ref[...]
    g_row = grow_ref[...]
    m_prev = m_ref[:, 0:1]
    causal = (lax.broadcasted_iota(jnp.int32, (L, L), 0) >= lax.broadcasted_iota(jnp.int32, (L, L), 1))
    dmat = jnp.where(causal, a_col + g_row, -jnp.inf)
    inter = m_prev + a_col
    m_t = jnp.maximum(inter, jnp.max(dmat, axis=1, keepdims=True))
    s = jnp.dot(q, kt, preferred_element_type=f32) * jnp.exp(dmat - m_t)
    sc_inter = jnp.exp(inter - m_t)
    qc = jnp.dot(q, c_ref[...].astype(bf16), preferred_element_type=f32)
    num = jnp.dot(s.astype(bf16), v_ref[...], preferred_element_type=f32) + sc_inter * qc[:, :D]
    den = jnp.sum(s, axis=1, keepdims=True) + sc_inter * qc[:, D:D + 1]
    h = num / jnp.maximum(jnp.abs(den), jnp.exp(-m_t))
    mu = jnp.mean(h, axis=-1, keepdims=True)
    var = jnp.mean(jnp.square(h - mu), axis=-1, keepdims=True)
    h_ref[...] = (h - mu) * lax.rsqrt(var + LN_EPS) * g_ref[...]

    m_new = m_t[L - 1:L, :]
    total = a_col[L - 1:L, :]
    decay = jnp.exp(total + g_row - m_new)
    sc_c = jnp.exp(m_prev + total - m_new)
    ktd = (kt.astype(f32) * decay).astype(bf16)
    one_hot = jnp.where(lax.broadcasted_iota(jnp.int32, (L, MLSTM_NORM_LANES), 1) == 0, 1.0, 0.0).astype(bf16)
    v_ext = jnp.concatenate([v_ref[...], one_hot], axis=1)
    c_ref[...] = sc_c * c_ref[...] + jnp.dot(ktd, v_ext, preferred_element_type=f32)
    m_ref[...] = jnp.broadcast_to(m_new, m_ref.shape)

    @pl.when(step == pl.num_programs(2) - 1)
    def _():
        c_out_ref[...] = c_ref[:, :D]
        n_out_ref[...] = c_ref[:, D:]
        m_out_ref[...] = m_ref[...]


def mlstm_cell(q, k, v, i_pre, logf, c0, n0, m0, norm_g):
    b_, T, _ = q.shape
    H, D = MLSTM_N_HEADS, MLSTM_HEAD_DIM
    L = min(T, MLSTM_STEP)
    ns = T // L
    bf16 = jnp.bfloat16
    kt = jnp.swapaxes((k * (D ** -0.5)).astype(bf16).reshape(b_, T, H, D), 1, 3)
    kt = jnp.swapaxes(kt, 1, 2)
    bcum = jnp.cumsum(logf.reshape(b_, ns, L, H), axis=2)
    acol = jnp.moveaxis(bcum, 3, 1).reshape(b_, H, T, 1)
    grow = jnp.moveaxis(i_pre.reshape(b_, ns, L, H) - bcum, 3, 1).reshape(b_, H, ns, 1, L)
    n0e = jnp.pad(n0[..., None], ((0, 0), (0, 0), (0, 0), (0, MLSTM_NORM_LANES - 1)))
    m0e = jnp.broadcast_to(m0[:, :, None, None], (b_, H, 1, MLSTM_NORM_LANES))

    tok_spec = pl.BlockSpec((None, L, D), lambda b, h, s: (b, s, h))
    state = lambda w: pl.BlockSpec((None, None, D, w), lambda b, h, s: (b, h, 0, 0))
    m_spec = pl.BlockSpec((None, None, 1, MLSTM_NORM_LANES), lambda b, h, s: (b, h, 0, 0))
    hn, c, n, m = pl.pallas_call(
        _mlstm_kernel,
        grid=(b_, H, ns),
        in_specs=[tok_spec,
                  pl.BlockSpec((None, None, D, L), lambda b, h, s: (b, h, 0, s)),
                  tok_spec,
                  pl.BlockSpec((None, None, L, 1), lambda b, h, s: (b, h, s, 0)),
                  pl.BlockSpec((None, None, None, 1, L), lambda b, h, s: (b, h, s, 0, 0)),
                  state(D), state(MLSTM_NORM_LANES), m_spec,
                  pl.BlockSpec((1, D), lambda b, h, s: (0, h))],
        out_specs=[tok_spec, state(D), state(MLSTM_NORM_LANES), m_spec],
        out_shape=[jax.ShapeDtypeStruct((b_, T, H * D), jnp.float32),
                   jax.ShapeDtypeStruct((b_, H, D, D), jnp.float32),
                   jax.ShapeDtypeStruct((b_, H, D, MLSTM_NORM_LANES), jnp.float32),
                   jax.ShapeDtypeStruct((b_, H, 1, MLSTM_NORM_LANES), jnp.float32)],
        scratch_shapes=[pltpu.VMEM((D, D + MLSTM_NORM_LANES), jnp.float32),
                        pltpu.VMEM((1, MLSTM_NORM_LANES), jnp.float32)],
        compiler_params=pltpu.CompilerParams(
            dimension_semantics=("parallel", "parallel", "arbitrary"),
            vmem_limit_bytes=V7X_VMEM_LIMIT_BYTES),
        name="mlstm_cell",
    )(q.astype(bf16), kt, v.astype(bf16), acol, grow, c0, n0e, m0e, norm_g.reshape(1, H * D))
    return hn, c, n[..., 0], m[:, :, 0, 0]


def mlstm_mixer(x, conv_hist, c0, n0, m0, w_up, conv_w, conv_b, w_q, w_k, w_v, w_if, b_if,
                skip, norm_g, w_down):
    b_, T, _ = x.shape
    up = matmul(x, w_up)
    z = up[..., MLSTM_D_INNER:]
    xc, q, k, v = mlstm_qkv(up, conv_hist, conv_w, conv_b, w_q, w_k, w_v)
    gates = matmul(q, w_if[0]) + matmul(k, w_if[1]) + matmul(v, w_if[2]) + b_if
    i_pre, f_pre = gates[..., :MLSTM_N_HEADS], gates[..., MLSTM_N_HEADS:]
    h, c, n, m = mlstm_cell(q, k, v, i_pre, jax.nn.log_sigmoid(f_pre), c0, n0, m0, norm_g)
    h = (h + skip * xc) * jax.nn.silu(z)
    return matmul(h, w_down), conv_tail(conv_hist, up, 0, MLSTM_D_INNER), c, n, m


S5_D_STATE = S5_N_GROUPS * S5_STATE
S5_PACK = 8
S5_N_PACKS = S5_N_GROUPS // S5_PACK
S5_SCAN_ROWS = 8
S5_SCAN_LANES = 256
S5_TIME_TILE = 256


def _cmul(ar, ai, br, bi):
    return ar * br - ai * bi, ar * bi + ai * br


def _s5_kernel(x_ref, bre_ref, bim_ref, cre_ref, cim_ref, pw_ref, h0r_ref, h0i_ref, d_ref,
               g_ref, hr_out_ref, hi_out_ref, bur_ref, bui_ref, hr_ref, hi_ref):
    tt = pl.program_id(1)
    rows = x_ref.shape[0]
    pk_in = S5_PACK * S5_GROUP
    pk_st = S5_PACK * S5_STATE

    @pl.when(tt == 0)
    def _():
        hr_ref[...] = jnp.broadcast_to(h0r_ref[...], hr_ref.shape)
        hi_ref[...] = jnp.broadcast_to(h0i_ref[...], hi_ref.shape)

    for c in range(S5_N_PACKS):
        xc = x_ref[:, c * pk_in:(c + 1) * pk_in].astype(jnp.bfloat16)
        bur_ref[:, c * pk_st:(c + 1) * pk_st] = jnp.dot(xc, bre_ref[c], preferred_element_type=jnp.float32)
        bui_ref[:, c * pk_st:(c + 1) * pk_st] = jnp.dot(xc, bim_ref[c], preferred_element_type=jnp.float32)

    def col_body(cb, carry):
        cs = pl.ds(pl.multiple_of(cb * S5_SCAN_LANES, S5_SCAN_LANES), S5_SCAN_LANES)
        stages = [(pw_ref[2 * k, :, cs], pw_ref[2 * k + 1, :, cs], 1 << k) for k in range(3)]
        lr, li = pw_ref[6, :, cs], pw_ref[7, :, cs]

        def row_body(r, h):
            hr, hi = h
            rs = pl.ds(pl.multiple_of(r * S5_SCAN_ROWS, S5_SCAN_ROWS), S5_SCAN_ROWS)
            vr, vi = bur_ref[rs, cs], bui_ref[rs, cs]
            for mr, mi, s in stages:
                pr, pi = _cmul(mr, mi, pltpu.roll(vr, s, 0), pltpu.roll(vi, s, 0))
                vr, vi = vr + pr, vi + pi
            pr, pi = _cmul(lr, li, hr, hi)
            vr, vi = vr + pr, vi + pi
            bur_ref[rs, cs] = vr
            bui_ref[rs, cs] = vi
            last = S5_SCAN_ROWS - 1
            return (jnp.broadcast_to(vr[last:, :], vr.shape), jnp.broadcast_to(vi[last:, :], vi.shape))

        hr, hi = lax.fori_loop(0, rows // S5_SCAN_ROWS, row_body, (hr_ref[:, cs], hi_ref[:, cs]))
        hr_ref[:, cs] = hr
        hi_ref[:, cs] = hi
        return carry

    lax.fori_loop(0, S5_D_STATE // S5_SCAN_LANES, col_body, 0)

    for c in range(S5_N_PACKS):
        hr = bur_ref[:, c * pk_st:(c + 1) * pk_st].astype(jnp.bfloat16)
        hi = bui_ref[:, c * pk_st:(c + 1) * pk_st].astype(jnp.bfloat16)
        y = (jnp.dot(hr, cre_ref[c], preferred_element_type=jnp.float32)
             - jnp.dot(hi, cim_ref[c], preferred_element_type=jnp.float32))
        cols = slice(c * pk_in, (c + 1) * pk_in)
        g_ref[:, cols] = jax.nn.gelu(y + d_ref[:, cols] * x_ref[:, cols])

    @pl.when(tt == pl.num_programs(1) - 1)
    def _():
        hr_out_ref[...] = hr_ref[0:1, :]
        hi_out_ref[...] = hi_ref[0:1, :]


def _block_diag_packs(w):
    g, r, c = w.shape
    eye = jnp.eye(S5_PACK, dtype=w.dtype)
    wb = jnp.einsum('kgrc,gh->kgrhc', w.reshape(g // S5_PACK, S5_PACK, r, c), eye)
    return wb.reshape(g // S5_PACK, S5_PACK * r, S5_PACK * c).astype(jnp.bfloat16)


def s5_mixer(x, h0, a_re, a_im, log_dt, b_re, b_im, c_re, c_im, d_skip, w_glu_a, w_glu_b):
    b_, T, _ = x.shape
    step = jnp.exp(log_dt)[:, None]
    mag = jnp.exp(a_re * step)
    ab_re, ab_im = mag * jnp.cos(a_im * step), mag * jnp.sin(a_im * step)
    den = a_re * a_re + a_im * a_im
    nr, ni = ab_re - 1.0, ab_im
    f_re = (nr * a_re + ni * a_im) / den
    f_im = (ni * a_re - nr * a_im) / den
    bb_re = f_re[..., None] * b_re - f_im[..., None] * b_im
    bb_im = f_re[..., None] * b_im + f_im[..., None] * b_re
    bre = _block_diag_packs(jnp.swapaxes(bb_re, 1, 2))
    bim = _block_diag_packs(jnp.swapaxes(bb_im, 1, 2))
    cre = _block_diag_packs(jnp.swapaxes(c_re, 1, 2))
    cim = _block_diag_packs(jnp.swapaxes(c_im, 1, 2))
    l1 = (ab_re.reshape(-1), ab_im.reshape(-1))
    l2 = _cmul(*l1, *l1)
    l4 = _cmul(*l2, *l2)
    row = jnp.arange(S5_SCAN_ROWS)[:, None]
    pw = []
    for s, (pr, pi) in ((1, l1), (2, l2), (4, l4)):
        pw += [jnp.where(row >= s, pr[None, :], 0.0), jnp.where(row >= s, pi[None, :], 0.0)]
    acc = [l1]
    for _ in range(S5_SCAN_ROWS - 1):
        acc.append(_cmul(*acc[-1], *l1))
    pw += [jnp.stack([a[0] for a in acc]), jnp.stack([a[1] for a in acc])]
    pw = jnp.stack(pw)

    tt = min(T, S5_TIME_TILE)
    h0r = h0[..., 0].reshape(b_, 1, S5_D_STATE)
    h0i = h0[..., 1].reshape(b_, 1, S5_D_STATE)
    pk_in, pk_st = S5_PACK * S5_GROUP, S5_PACK * S5_STATE

    def const3(b, t):
        return (0, 0, 0)

    state_spec = pl.BlockSpec((None, 1, S5_D_STATE), lambda b, t: (b, 0, 0))
    g, hr, hi = pl.pallas_call(
        _s5_kernel,
        grid=(b_, T // tt),
        in_specs=[pl.BlockSpec((None, tt, D_MODEL), lambda b, t: (b, t, 0)),
                  pl.BlockSpec((S5_N_PACKS, pk_in, pk_st), const3),
                  pl.BlockSpec((S5_N_PACKS, pk_in, pk_st), const3),
                  pl.BlockSpec((S5_N_PACKS, pk_st, pk_in), const3),
                  pl.BlockSpec((S5_N_PACKS, pk_st, pk_in), const3),
                  pl.BlockSpec((8, S5_SCAN_ROWS, S5_D_STATE), const3),
                  state_spec, state_spec,
                  pl.BlockSpec((1, D_MODEL), lambda b, t: (0, 0))],
        out_specs=[pl.BlockSpec((None, tt, D_MODEL), lambda b, t: (b, t, 0)), state_spec, state_spec],
        out_shape=[jax.ShapeDtypeStruct((b_, T, D_MODEL), jnp.float32),
                   jax.ShapeDtypeStruct((b_, 1, S5_D_STATE), jnp.float32),
                   jax.ShapeDtypeStruct((b_, 1, S5_D_STATE), jnp.float32)],
        scratch_shapes=[pltpu.VMEM((tt, S5_D_STATE), jnp.float32),
                        pltpu.VMEM((tt, S5_D_STATE), jnp.float32),
                        pltpu.VMEM((S5_SCAN_ROWS, S5_D_STATE), jnp.float32),
                        pltpu.VMEM((S5_SCAN_ROWS, S5_D_STATE), jnp.float32)],
        compiler_params=pltpu.CompilerParams(
            dimension_semantics=("parallel", "arbitrary"),
            vmem_limit_bytes=V7X_VMEM_LIMIT_BYTES),
        name="s5_scan",
    )(x, bre, bim, cre, cim, pw, h0r, h0i, d_skip.reshape(1, D_MODEL))
    out = matmul(g, w_glu_a) * jax.nn.sigmoid(matmul(g, w_glu_b))
    h_new = jnp.stack([hr.reshape(b_, S5_N_GROUPS, S5_STATE), hi.reshape(b_, S5_N_GROUPS, S5_STATE)], axis=-1)
    return out, h_new


def nsa_compress(kv, w1, w2, pe):
    b_, L = kv.shape[:2]
    span = NSA_CMP_BLOCK // NSA_CMP_STRIDE
    n_str = L // NSA_CMP_STRIDE
    n_cmp = n_str - span + 1
    chunks = kv.reshape(b_, n_str, NSA_CMP_STRIDE, NSA_N_KV, NSA_HEAD_DIM)
    blocks = jnp.concatenate([chunks[:, s:s + n_cmp] for s in range(span)], axis=2) + pe[:, None, :]
    flat = jnp.moveaxis(blocks, 3, 2).reshape(b_, n_cmp, NSA_N_KV, NSA_CMP_BLOCK * NSA_HEAD_DIM)
    return jax.nn.gelu(flat @ w1) @ w2


def cmp_to_sel(imp, n_sel):
    r = NSA_SEL_BLOCK // NSA_CMP_STRIDE
    span = NSA_CMP_BLOCK // NSA_CMP_STRIDE
    pad = jnp.pad(imp, [(0, 0)] * (imp.ndim - 1) + [(span - 1, span - 1)])
    return sum(pad[..., s:s + r * (n_sel - 1) + 1:r] for s in range(r + span - 1))


def nsa_cmp_slc_block(qb, tq, kcmp, vcmp, ks_blk, vs_blk, n_top):
    scale = NSA_HEAD_DIM ** -0.5
    b_ = qb.shape[0]
    n_cmp = kcmp.shape[1]
    n_sel = ks_blk.shape[2]
    cmp_end = jnp.arange(n_cmp) * NSA_CMP_STRIDE + (NSA_CMP_BLOCK - 1)
    cmask = (cmp_end[None, :] <= tq[:, None])[None, :, None, None, :]
    s = jnp.einsum('bqhgd,bnhd->bqhgn', qb, kcmp) * scale
    p = jax.nn.softmax(jnp.where(cmask, s, NEG_INF), axis=-1)
    p = jnp.where(cmask, p, 0.0)
    o_cmp = jnp.einsum('bqhgn,bnhd->bqhgd', p, vcmp)
    imp = cmp_to_sel(jnp.sum(p, axis=3), n_sel)
    blk = jnp.arange(n_sel)[None, :]
    cur = (tq // NSA_SEL_BLOCK)[:, None]
    forced = (blk == 0) | (blk == cur) | (blk == cur - 1)
    valid = blk <= cur
    score = jnp.where(forced[None, :, None, :], FORCE_SCORE,
                      jnp.where(valid[None, :, None, :], imp, -FORCE_SCORE))
    _, idx = lax.top_k(score, n_top)
    bi = jnp.arange(b_)[:, None, None, None]
    hi = jnp.arange(NSA_N_KV)[None, None, :, None]
    kg = ks_blk[bi, hi, idx]
    vg = vs_blk[bi, hi, idx]
    kpos = idx[..., None] * NSA_SEL_BLOCK + jnp.arange(NSA_SEL_BLOCK)
    smask = (kpos <= tq[None, :, None, None, None])[:, :, :, None]
    s2 = jnp.einsum('bqhgd,bqhnsd->bqhgns', qb, kg) * scale
    s2 = jnp.where(smask, s2, NEG_INF)
    p2 = jax.nn.softmax(s2.reshape(s2.shape[:4] + (-1,)), axis=-1).reshape(s2.shape)
    o_slc = jnp.einsum('bqhgns,bqhnsd->bqhgd', p2, vg)
    return o_cmp, o_slc


def nsa_cmp_slc(q, rows, q_pos, w_cmp1, w_cmp2, cmp_pe):
    b_, L = rows.shape[:2]
    lp = -(-L // NSA_SEL_BLOCK) * NSA_SEL_BLOCK
    rows = jnp.pad(rows, ((0, 0), (0, lp - L), (0, 0), (0, 0), (0, 0)))
    kcmp = nsa_compress(rows[:, :, 0], w_cmp1[0], w_cmp2[0], cmp_pe[0])
    vcmp = nsa_compress(rows[:, :, 1], w_cmp1[1], w_cmp2[1], cmp_pe[1])
    n_sel = lp // NSA_SEL_BLOCK

    def sel_blocks(t):
        return jnp.moveaxis(t.reshape(b_, n_sel, NSA_SEL_BLOCK, NSA_N_KV, NSA_HEAD_DIM), 3, 1)

    ks_blk, vs_blk = sel_blocks(rows[:, :, 2]), sel_blocks(rows[:, :, 3])
    n_top = min(NSA_N_SELECT, n_sel)
    T = q.shape[1]
    qbs = math.gcd(T, NSA_QBLOCK)
    nqb = T // qbs
    qb = jnp.moveaxis(q.reshape((b_, nqb, qbs) + q.shape[2:]), 1, 0)
    pb = q_pos.reshape(nqb, qbs)
    o_cmp, o_slc = lax.map(
        lambda a: nsa_cmp_slc_block(a[0], a[1], kcmp, vcmp, ks_blk, vs_blk, n_top), (qb, pb))
    return (jnp.moveaxis(o_cmp, 0, 1).reshape(q.shape), jnp.moveaxis(o_slc, 0, 1).reshape(q.shape))


def window_attend(qb, kb, vb, q_pos, k_pos):
    s = jnp.einsum('bnqhgd,bnshd->bnqhgs', qb, kb) * (NSA_HEAD_DIM ** -0.5)
    diff = q_pos[:, :, None] - k_pos[:, None, :]
    mask = ((diff >= 0) & (diff < NSA_WINDOW) & (k_pos[:, None, :] >= 0))[None, :, :, None, None, :]
    p = jax.nn.softmax(jnp.where(mask, s, NEG_INF), axis=-1)
    return jnp.einsum('bnqhgs,bnshd->bnqhgd', p, vb)


def nsa_window_prompt(q, win_rows):
    b_, T = q.shape[:2]
    qbs = math.gcd(T, NSA_WBLOCK)
    nqb = T // qbs
    pad = jnp.pad(win_rows, ((0, 0), (NSA_WINDOW, 0), (0, 0), (0, 0), (0, 0)))
    idx = jnp.arange(nqb)[:, None] * qbs + jnp.arange(NSA_WINDOW + qbs)[None, :]
    kvb = pad[:, idx]
    q_pos = jnp.arange(T).reshape(nqb, qbs)
    o = window_attend(q.reshape((b_, nqb, qbs) + q.shape[2:]), kvb[:, :, :, 0], kvb[:, :, :, 1],
                      q_pos, idx - NSA_WINDOW)
    return o.reshape(q.shape)


NSA_TQ = NSA_WBLOCK
NSA_SLC_CHUNK = 256
NSA_WIN_CHUNK = 128
NSA_ROWS = NSA_GQA * NSA_TQ
NSA_SEL_SHIFT = NSA_SEL_BLOCK.bit_length() - 1
assert 1 << NSA_SEL_SHIFT == NSA_SEL_BLOCK


def _nsa_stream_softmax(q, k_ref, v_ref, c_lo, c_hi, chunk, mask_fn, m_ref, l_ref, acc_ref):
    scale = NSA_HEAD_DIM ** -0.5
    m_ref[...] = jnp.full(m_ref.shape, NEG_INF, jnp.float32)
    l_ref[...] = jnp.zeros(l_ref.shape, jnp.float32)
    acc_ref[...] = jnp.zeros(acc_ref.shape, jnp.float32)

    def body(c, carry):
        start = pl.multiple_of(c * chunk, chunk)
        k = k_ref[pl.ds(start, chunk), :]
        v = v_ref[pl.ds(start, chunk), :]
        s = lax.dot_general(q, k, (((1,), (1,)), ((), ())), preferred_element_type=jnp.float32) * scale
        s = jnp.where(mask_fn(start), s, NEG_INF)
        m_old = m_ref[...]
        m_new = jnp.maximum(m_old, jnp.max(s, axis=-1, keepdims=True))
        alpha = jnp.exp(m_old - m_new)
        p = jnp.exp(s - m_new)
        l_ref[...] = alpha * l_ref[...] + jnp.sum(p, axis=-1, keepdims=True)
        acc_ref[...] = alpha * acc_ref[...] + jnp.dot(p.astype(jnp.bfloat16), v,
                                                      preferred_element_type=jnp.float32)
        m_ref[...] = m_new
        return carry

    lax.fori_loop(c_lo, c_hi, body, 0)
    return acc_ref[...] / l_ref[...]


def _nsa_prompt_kernel(q_ref, kc_ref, vc_ref, ks_ref, vs_ref, kw_ref, vw_ref,
                       ocmp_ref, oslc_ref, owin_ref, selexp_ref, m_ref, l_ref, acc_ref,
                       *, n_cmp, n_top):
    f32, bf16 = jnp.float32, jnp.bfloat16
    i = pl.program_id(2)
    t0 = i * NSA_TQ
    n_cp = kc_ref.shape[0]
    t_len = selexp_ref.shape[1]
    n_sel = t_len // NSA_SEL_BLOCK
    nt = (((1,), (1,)), ((), ()))
    q = jnp.concatenate([q_ref[:, g * NSA_HEAD_DIM:(g + 1) * NSA_HEAD_DIM] for g in range(NSA_GQA)], axis=0)
    tq = t0 + (lax.broadcasted_iota(jnp.int32, (NSA_ROWS, 1), 0) & (NSA_TQ - 1))

    def store_heads(o_ref, o):
        for g in range(NSA_GQA):
            o_ref[:, g * NSA_HEAD_DIM:(g + 1) * NSA_HEAD_DIM] = o[g * NSA_TQ:(g + 1) * NSA_TQ]

    s = lax.dot_general(q, kc_ref[...], nt, preferred_element_type=f32) * (NSA_HEAD_DIM ** -0.5)
    n_idx = lax.broadcasted_iota(jnp.int32, (NSA_ROWS, n_cp), 1)
    cmask = (n_idx * NSA_CMP_STRIDE + (NSA_CMP_BLOCK - 1) <= tq) & (n_idx < n_cmp)
    s = jnp.where(cmask, s, NEG_INF)
    e = jnp.exp(s - jnp.max(s, axis=-1, keepdims=True))
    p = jnp.where(cmask, e / jnp.sum(e, axis=-1, keepdims=True), 0.0)
    store_heads(ocmp_ref, jnp.dot(p.astype(bf16), vc_ref[...], preferred_element_type=f32))

    psum = p[0:NSA_TQ]
    for g in range(1, NSA_GQA):
        psum = psum + p[g * NSA_TQ:(g + 1) * NSA_TQ]
    jn = lax.broadcasted_iota(jnp.int32, (n_sel, n_cp), 0)
    nn = lax.broadcasted_iota(jnp.int32, (n_sel, n_cp), 1)
    r = NSA_SEL_BLOCK // NSA_CMP_STRIDE
    span = NSA_CMP_BLOCK // NSA_CMP_STRIDE
    pool = jnp.where((nn >= r * jn - (span - 1)) & (nn <= r * jn + (r - 1)), 1.0, 0.0).astype(bf16)
    imp = jnp.zeros((n_sel, NSA_TQ), f32)
    rest = psum
    for _ in range(3):
        part = rest.astype(bf16)
        imp = imp + lax.dot_general(pool, part, nt, preferred_element_type=f32)
        rest = rest - part.astype(f32)

    jidx = lax.broadcasted_iota(jnp.int32, (n_sel, NSA_TQ), 0)
    cur = (t0 + lax.broadcasted_iota(jnp.int32, (n_sel, NSA_TQ), 1)) >> NSA_SEL_SHIFT
    forced = (jidx == 0) | (jidx == cur) | (jidx == cur - 1)
    score = jnp.where(forced, FORCE_SCORE, jnp.where(jidx <= cur, imp, -FORCE_SCORE))
    rank = jnp.zeros((n_sel, NSA_TQ), jnp.int32)
    for k in range(n_sel):
        row = score[k:k + 1, :]
        before = (row > score) | ((row == score) & (k < jidx))
        rank = rank + jnp.where(before, 1, 0)
    sel_t = jnp.where(rank < n_top, 1.0, 0.0).astype(bf16)
    eye = jnp.where(lax.broadcasted_iota(jnp.int32, (NSA_TQ, NSA_TQ), 0)
                    == lax.broadcasted_iota(jnp.int32, (NSA_TQ, NSA_TQ), 1), 1.0, 0.0).astype(bf16)
    sel = lax.dot_general(eye, sel_t, nt, preferred_element_type=f32).astype(bf16)
    expand = jnp.where((lax.broadcasted_iota(jnp.int32, (n_sel, t_len), 1) >> NSA_SEL_SHIFT)
                       == lax.broadcasted_iota(jnp.int32, (n_sel, t_len), 0), 1.0, 0.0).astype(bf16)
    selexp_ref[...] = jnp.dot(sel, expand, preferred_element_type=f32)

    def slc_mask(start):
        kpos = start + lax.broadcasted_iota(jnp.int32, (NSA_ROWS, NSA_SLC_CHUNK), 1)
        chosen = jnp.concatenate([selexp_ref[:, pl.ds(start, NSA_SLC_CHUNK)]] * NSA_GQA, axis=0)
        return (chosen > 0.5) & (kpos <= tq)

    store_heads(oslc_ref, _nsa_stream_softmax(
        q, ks_ref, vs_ref, 0, (t0 + NSA_TQ - 1) // NSA_SLC_CHUNK + 1, NSA_SLC_CHUNK, slc_mask,
        m_ref, l_ref, acc_ref))

    def win_mask(start):
        diff = tq - (start + lax.broadcasted_iota(jnp.int32, (NSA_ROWS, NSA_WIN_CHUNK), 1))
        return (diff >= 0) & (diff < NSA_WINDOW)

    store_heads(owin_ref, _nsa_stream_softmax(
        q, kw_ref, vw_ref, jnp.maximum(i - NSA_WINDOW // NSA_WIN_CHUNK, 0), i + 1, NSA_WIN_CHUNK,
        win_mask, m_ref, l_ref, acc_ref))


def nsa_prompt_attention(q, kv, kcmp, vcmp):
    b_, T, _ = q.shape
    assert T % NSA_SLC_CHUNK == 0 and (T // NSA_SEL_BLOCK) % 8 == 0
    n_cmp = kcmp.shape[1]
    n_cp = T // NSA_CMP_STRIDE
    n_sel = T // NSA_SEL_BLOCK
    bf16 = jnp.bfloat16

    def cmp_layout(t):
        return jnp.pad(jnp.moveaxis(t, 2, 1), ((0, 0), (0, 0), (0, n_cp - n_cmp), (0, 0))).astype(bf16)

    kvb = kv.astype(bf16)
    width = NSA_GQA * NSA_HEAD_DIM
    q_spec = pl.BlockSpec((None, NSA_TQ, width), lambda b, h, i: (b, i, h))
    cmp_spec = pl.BlockSpec((None, None, n_cp, NSA_HEAD_DIM), lambda b, h, i: (b, h, 0, 0))

    def kv_spec(comp):
        return pl.BlockSpec((None, T, NSA_HEAD_DIM), lambda b, h, i: (b, 0, comp * NSA_N_KV + h))

    out = jax.ShapeDtypeStruct((b_, T, NSA_N_HEADS * NSA_HEAD_DIM), jnp.float32)
    return pl.pallas_call(
        functools.partial(_nsa_prompt_kernel, n_cmp=n_cmp, n_top=min(NSA_N_SELECT, n_sel)),
        grid=(b_, NSA_N_KV, T // NSA_TQ),
        in_specs=[q_spec, cmp_spec, cmp_spec, kv_spec(2), kv_spec(3), kv_spec(4), kv_spec(5)],
        out_specs=[q_spec, q_spec, q_spec],
        out_shape=[out, out, out],
        scratch_shapes=[pltpu.VMEM((NSA_TQ, T), jnp.float32),
                        pltpu.VMEM((NSA_ROWS, 1), jnp.float32),
                        pltpu.VMEM((NSA_ROWS, 1), jnp.float32),
                        pltpu.VMEM((NSA_ROWS, NSA_HEAD_DIM), jnp.float32)],
        compiler_params=pltpu.CompilerParams(
            dimension_semantics=("parallel", "parallel", "arbitrary"),
            vmem_limit_bytes=V7X_VMEM_LIMIT_BYTES),
        name="nsa_prompt",
    )(q.astype(bf16), cmp_layout(kcmp), cmp_layout(vcmp), kvb, kvb, kvb, kvb)


def _nsa_stream_softmax_t(q_t, k_ref, vt_ref, c_lo, c_hi, chunk, mask_fn, m_ref, l_ref, acc_ref):
    scale = NSA_HEAD_DIM ** -0.5
    m_ref[...] = jnp.full(m_ref.shape, NEG_INF, jnp.float32)
    l_ref[...] = jnp.zeros(l_ref.shape, jnp.float32)
    acc_ref[...] = jnp.zeros(acc_ref.shape, jnp.float32)

    def body(c, carry):
        start = pl.multiple_of(c * chunk, chunk)
        s = jnp.dot(k_ref[pl.ds(start, chunk), :], q_t, preferred_element_type=jnp.float32) * scale
        s = jnp.where(mask_fn(start), s, NEG_INF)
        m_old = m_ref[...]
        m_new = jnp.maximum(m_old, jnp.max(s, axis=0, keepdims=True))
        alpha = jnp.exp(m_old - m_new)
        p = jnp.exp(s - m_new)
        l_ref[...] = alpha * l_ref[...] + jnp.sum(p, axis=0, keepdims=True)
        acc_ref[...] = alpha * acc_ref[...] + jnp.dot(vt_ref[:, pl.ds(start, chunk)], p.astype(jnp.bfloat16),
                                                      preferred_element_type=jnp.float32)
        m_ref[...] = m_new
        return carry

    lax.fori_loop(c_lo, c_hi, body, 0)
    return acc_ref[...] / l_ref[...]


def _nsa_prompt_t_kernel(qt_ref, kc_ref, vct_ref, ks_ref, vst_ref, kw_ref, vwt_ref,
                         ocmp_ref, oslc_ref, owin_ref, selexp_ref, m_ref, l_ref, acc_ref,
                         *, n_cmp, n_top):
    f32, bf16 = jnp.float32, jnp.bfloat16
    i = pl.program_id(2)
    t0 = i * NSA_TQ
    n_cp = kc_ref.shape[0]
    t_len = selexp_ref.shape[0]
    n_sel = t_len // NSA_SEL_BLOCK
    q_t = qt_ref[...]
    tq = t0 + (lax.broadcasted_iota(jnp.int32, (1, NSA_ROWS), 1) & (NSA_TQ - 1))

    def store_heads(o_ref, o_t):
        for g in range(NSA_GQA):
            o_ref[:, g * NSA_HEAD_DIM:(g + 1) * NSA_HEAD_DIM] = o_t[:, g * NSA_TQ:(g + 1) * NSA_TQ].T

    s = jnp.dot(kc_ref[...], q_t, preferred_element_type=f32) * (NSA_HEAD_DIM ** -0.5)
    n_idx = lax.broadcasted_iota(jnp.int32, (n_cp, NSA_ROWS), 0)
    cmask = (n_idx * NSA_CMP_STRIDE + (NSA_CMP_BLOCK - 1) <= tq) & (n_idx < n_cmp)
    s = jnp.where(cmask, s, NEG_INF)
    e = jnp.exp(s - jnp.max(s, axis=0, keepdims=True))
    p = jnp.where(cmask, e / jnp.sum(e, axis=0, keepdims=True), 0.0)
    store_heads(ocmp_ref, jnp.dot(vct_ref[...], p.astype(bf16), preferred_element_type=f32))

    psum = p[:, 0:NSA_TQ]
    for g in range(1, NSA_GQA):
        psum = psum + p[:, g * NSA_TQ:(g + 1) * NSA_TQ]
    jn = lax.broadcasted_iota(jnp.int32, (n_sel, n_cp), 0)
    nn = lax.broadcasted_iota(jnp.int32, (n_sel, n_cp), 1)
    r = NSA_SEL_BLOCK // NSA_CMP_STRIDE
    span = NSA_CMP_BLOCK // NSA_CMP_STRIDE
    pool = jnp.where((nn >= r * jn - (span - 1)) & (nn <= r * jn + (r - 1)), 1.0, 0.0).astype(bf16)
    imp = jnp.zeros((n_sel, NSA_TQ), f32)
    rest = psum
    for _ in range(3):
        part = rest.astype(bf16)
        imp = imp + jnp.dot(pool, part, preferred_element_type=f32)
        rest = rest - part.astype(f32)

    jidx = lax.broadcasted_iota(jnp.int32, (n_sel, NSA_TQ), 0)
    cur = (t0 + lax.broadcasted_iota(jnp.int32, (n_sel, NSA_TQ), 1)) >> NSA_SEL_SHIFT
    forced = (jidx == 0) | (jidx == cur) | (jidx == cur - 1)
    score = jnp.where(forced, FORCE_SCORE, jnp.where(jidx <= cur, imp, -FORCE_SCORE))
    rank = jnp.zeros((n_sel, NSA_TQ), jnp.int32)
    for k in range(n_sel):
        row = score[k:k + 1, :]
        before = (row > score) | ((row == score) & (k < jidx))
        rank = rank + jnp.where(before, 1, 0)
    sel_t = jnp.where(rank < n_top, 1.0, 0.0).astype(bf16)
    expand = jnp.where((lax.broadcasted_iota(jnp.int32, (t_len, n_sel), 0) >> NSA_SEL_SHIFT)
                       == lax.broadcasted_iota(jnp.int32, (t_len, n_sel), 1), 1.0, 0.0).astype(bf16)
    selexp_ref[...] = jnp.dot(expand, sel_t, preferred_element_type=f32)

    def slc_mask(start):
        kpos = start + lax.broadcasted_iota(jnp.int32, (NSA_SLC_CHUNK, NSA_ROWS), 0)
        chosen = jnp.concatenate([selexp_ref[pl.ds(start, NSA_SLC_CHUNK), :]] * NSA_GQA, axis=1)
        return (chosen > 0.5) & (kpos <= tq)

    store_heads(oslc_ref, _nsa_stream_softmax_t(
        q_t, ks_ref, vst_ref, 0, (t0 + NSA_TQ - 1) // NSA_SLC_CHUNK + 1, NSA_SLC_CHUNK, slc_mask,
        m_ref, l_ref, acc_ref))

    def win_mask(start):
        diff = tq - (start + lax.broadcasted_iota(jnp.int32, (NSA_WIN_CHUNK, NSA_ROWS), 0))
        return (diff >= 0) & (diff < NSA_WINDOW)

    store_heads(owin_ref, _nsa_stream_softmax_t(
        q_t, kw_ref, vwt_ref, jnp.maximum(i - NSA_WINDOW // NSA_WIN_CHUNK, 0), i + 1, NSA_WIN_CHUNK,
        win_mask, m_ref, l_ref, acc_ref))


def nsa_prompt_attention_t(q, kv, kcmp, vcmp):
    b_, T, _ = q.shape
    assert T % NSA_SLC_CHUNK == 0 and (T // NSA_SEL_BLOCK) % 8 == 0
    n_cmp = kcmp.shape[1]
    n_cp = T // NSA_CMP_STRIDE
    n_sel = T // NSA_SEL_BLOCK
    n_tiles = T // NSA_TQ
    bf16 = jnp.bfloat16
    pad_c = ((0, 0), (0, n_cp - n_cmp), (0, 0), (0, 0))
    kc = jnp.transpose(jnp.pad(kcmp, pad_c).astype(bf16), (0, 2, 1, 3))
    vct = jnp.transpose(jnp.pad(vcmp, pad_c).astype(bf16), (0, 2, 3, 1))
    kv6 = kv.astype(bf16).reshape(b_, T, 6, NSA_N_KV, NSA_HEAD_DIM)
    keys = lambda comp: jnp.transpose(kv6[:, :, comp], (0, 2, 1, 3))
    vals_t = lambda comp: jnp.transpose(kv6[:, :, comp], (0, 2, 3, 1))
    q_t = q.astype(bf16).reshape(b_, n_tiles, NSA_TQ, NSA_N_KV, NSA_GQA, NSA_HEAD_DIM)
    q_t = jnp.transpose(q_t, (0, 3, 1, 5, 4, 2)).reshape(b_, NSA_N_KV, n_tiles, NSA_HEAD_DIM, NSA_ROWS)

    per_head = lambda r, c: pl.BlockSpec((None, None, r, c), lambda b, h, i: (b, h, 0, 0))
    o_spec = pl.BlockSpec((None, NSA_TQ, NSA_GQA * NSA_HEAD_DIM), lambda b, h, i: (b, i, h))
    out = jax.ShapeDtypeStruct((b_, T, NSA_N_HEADS * NSA_HEAD_DIM), jnp.float32)
    return pl.pallas_call(
        functools.partial(_nsa_prompt_t_kernel, n_cmp=n_cmp, n_top=min(NSA_N_SELECT, n_sel)),
        grid=(b_, NSA_N_KV, n_tiles),
        in_specs=[pl.BlockSpec((None, None, None, NSA_HEAD_DIM, NSA_ROWS), lambda b, h, i: (b, h, i, 0, 0)),
                  per_head(n_cp, NSA_HEAD_DIM), per_head(NSA_HEAD_DIM, n_cp),
                  per_head(T, NSA_HEAD_DIM), per_head(NSA_HEAD_DIM, T),
                  per_head(T, NSA_HEAD_DIM), per_head(NSA_HEAD_DIM, T)],
        out_specs=[o_spec, o_spec, o_spec],
        out_shape=[out, out, out],
        scratch_shapes=[pltpu.VMEM((T, NSA_TQ), jnp.float32),
                        pltpu.VMEM((1, NSA_ROWS), jnp.float32),
                        pltpu.VMEM((1, NSA_ROWS), jnp.float32),
                        pltpu.VMEM((NSA_HEAD_DIM, NSA_ROWS), jnp.float32)],
        compiler_params=pltpu.CompilerParams(
            dimension_semantics=("parallel", "parallel", "arbitrary"),
            vmem_limit_bytes=V7X_VMEM_LIMIT_BYTES),
        name="nsa_prompt",
    )(q_t, kc, vct, keys(2), vals_t(3), keys(4), vals_t(5))


def nsa_prompt_mixer(x, w_q, w_kv, w_gate, b_gate, w_cmp1, w_cmp2, cmp_pe, w_out):
    b_, T, _ = x.shape
    q = matmul(x, w_q)
    kv = matmul(x, w_kv)
    kv6 = kv.reshape(b_, T, 6, NSA_N_KV, NSA_HEAD_DIM)
    kcmp = nsa_compress(kv6[:, :, 0], w_cmp1[0], w_cmp2[0], cmp_pe[0])
    vcmp = nsa_compress(kv6[:, :, 1], w_cmp1[1], w_cmp2[1], cmp_pe[1])
    o_cmp, o_slc, o_win = nsa_prompt_attention_t(q, kv, kcmp, vcmp)
    gate = jax.nn.sigmoid(matmul(x, w_gate) + b_gate).reshape(b_, T, NSA_N_HEADS, 3)

    def heads(t):
        return t.reshape(b_, T, NSA_N_HEADS, NSA_HEAD_DIM)

    o = gate[..., 0:1] * heads(o_cmp) + gate[..., 1:2] * heads(o_slc) + gate[..., 2:3] * heads(o_win)
    y = matmul(o.reshape(b_, T, NSA_N_HEADS * NSA_HEAD_DIM), w_out)
    return y, kv6[:, :, :4], kv6[:, :, 4:6][:, -min(NSA_WINDOW, T):]


NSA_ROW_SLABS = 4 * NSA_N_KV
NSA_HALF_SLABS = NSA_ROW_SLABS // 2
NSA_KVW = NSA_N_KV * NSA_HEAD_DIM
NSA_CMP_PAGES = 8
NSA_SLC_PAGES = 4


def _round_up(n, m):
    return -(-n // m) * m


def _log2(n):
    assert n > 0 and n & (n - 1) == 0
    return n.bit_length() - 1


def _nsa_compress_kernel(pt_ref, *refs):
    del pt_ref
    pages = refs[:NSA_CMP_PAGES]
    w1_ref, pe_ref, a_ref, b_ref = refs[NSA_CMP_PAGES:]
    page_rows = pages[0].shape[0]
    per_page = page_rows // NSA_CMP_STRIDE
    rows = NSA_CMP_PAGES * NSA_N_KV * per_page
    half = NSA_CMP_STRIDE * NSA_HEAD_DIM
    for comp in range(2):
        acc_a = jnp.zeros((rows, NSA_HEAD_DIM), jnp.float32)
        acc_b = jnp.zeros((rows, NSA_HEAD_DIM), jnp.float32)
        for j0 in range(0, NSA_CMP_STRIDE, 2):
            xa, xb = [], []
            for j in (j0, j0 + 1):
                x = jnp.concatenate(
                    [pg[pl.ds(j, per_page, stride=NSA_CMP_STRIDE), comp * NSA_N_KV + h, :]
                     for pg in pages for h in range(NSA_N_KV)], axis=0)
                xa.append((x + pe_ref[comp, j:j + 1, :]).astype(jnp.bfloat16))
                xb.append((x + pe_ref[comp, NSA_CMP_STRIDE + j:NSA_CMP_STRIDE + j + 1, :]).astype(jnp.bfloat16))
            lo = j0 * NSA_HEAD_DIM
            acc_a = acc_a + jnp.dot(jnp.concatenate(xa, axis=1), w1_ref[comp, lo:lo + 2 * NSA_HEAD_DIM, :],
                                    preferred_element_type=jnp.float32)
            acc_b = acc_b + jnp.dot(jnp.concatenate(xb, axis=1),
                                    w1_ref[comp, half + lo:half + lo + 2 * NSA_HEAD_DIM, :],
                                    preferred_element_type=jnp.float32)
        shape = (NSA_CMP_PAGES, NSA_N_KV, per_page, NSA_HEAD_DIM)
        a_ref[comp] = acc_a.reshape(shape)
        b_ref[comp] = acc_b.reshape(shape)


def nsa_decode_compress(cache, page_ids, new_rows, w_cmp1, w_cmp2, cmp_pe):
    b_, n_pages = page_ids.shape
    page = cache.shape[1]
    T = new_rows.shape[1]
    pos0 = n_pages * page
    lp = _round_up(pos0 + T, NSA_SEL_BLOCK)
    n_cmp = lp // NSA_CMP_STRIDE - (NSA_CMP_BLOCK // NSA_CMP_STRIDE - 1)
    per_page = page // NSA_CMP_STRIDE
    assert n_pages % NSA_CMP_PAGES == 0 and NSA_CMP_BLOCK == 2 * NSA_CMP_STRIDE
    w1 = w_cmp1.astype(jnp.bfloat16)

    def page_spec(k):
        return pl.BlockSpec((None, page, NSA_HALF_SLABS, NSA_HEAD_DIM),
                            lambda b, s, pt: (pt[b, NSA_CMP_PAGES * s + k], 0, 0, 0))

    ab_shape = jax.ShapeDtypeStruct((b_, 2, n_pages, NSA_N_KV, per_page, NSA_HEAD_DIM), jnp.float32)
    ab_spec = pl.BlockSpec((None, 2, NSA_CMP_PAGES, NSA_N_KV, per_page, NSA_HEAD_DIM),
                           lambda b, s, pt: (b, 0, s, 0, 0, 0))
    part_a, part_b = pl.pallas_call(
        _nsa_compress_kernel,
        grid_spec=pltpu.PrefetchScalarGridSpec(
            num_scalar_prefetch=1,
            grid=(b_, n_pages // NSA_CMP_PAGES),
            in_specs=[page_spec(k) for k in range(NSA_CMP_PAGES)]
            + [pl.BlockSpec(w1.shape, lambda b, s, pt: (0, 0, 0)),
               pl.BlockSpec(cmp_pe.shape, lambda b, s, pt: (0, 0, 0))],
            out_specs=[ab_spec, ab_spec]),
        out_shape=[ab_shape, ab_shape],
        compiler_params=pltpu.CompilerParams(
            dimension_semantics=("parallel", "arbitrary"),
            vmem_limit_bytes=V7X_VMEM_LIMIT_BYTES),
        name="nsa_compress_pages",
    )(page_ids, *([cache] * NSA_CMP_PAGES), w1, cmp_pe)

    def strides(t):
        return jnp.transpose(t, (0, 1, 3, 2, 4, 5)).reshape(b_, 2, NSA_N_KV, n_pages * per_page, NSA_HEAD_DIM)

    n_tail = (lp - pos0) // NSA_CMP_STRIDE
    tail = jnp.pad(new_rows, ((0, 0), (0, lp - pos0 - T), (0, 0), (0, 0), (0, 0)))
    tail = jnp.transpose(tail.reshape(b_, n_tail, NSA_CMP_STRIDE, 2, NSA_N_KV, NSA_HEAD_DIM), (0, 3, 4, 1, 2, 5))
    w1s = w_cmp1.reshape(2, 2, NSA_CMP_STRIDE, NSA_HEAD_DIM, -1)
    pes = cmp_pe.reshape(2, 2, NSA_CMP_STRIDE, NSA_HEAD_DIM)
    tail_a = jnp.einsum('bchsjd,cjdk->bchsk', tail + pes[None, :, 0, None, None], w1s[:, 0])
    tail_b = jnp.einsum('bchsjd,cjdk->bchsk', tail + pes[None, :, 1, None, None], w1s[:, 1])
    full_a = jnp.concatenate([strides(part_a), tail_a], axis=3)
    full_b = jnp.concatenate([strides(part_b), tail_b], axis=3)
    hidden = jax.nn.gelu(full_a[:, :, :, :n_cmp] + full_b[:, :, :, 1:n_cmp + 1])
    out = jnp.einsum('bchnk,ckd->bcnhd', hidden, w_cmp2).reshape(b_, 2, n_cmp, NSA_KVW)
    return out[:, 0], out[:, 1]


def _nsa_softmax_rows(s, mask):
    s = jnp.where(mask, s, NEG_INF)
    e = jnp.exp(s - jnp.max(s, axis=-1, keepdims=True))
    return e / jnp.sum(e, axis=-1, keepdims=True)


def _nsa_decode_select_kernel(q_ref, kc_ref, vc_ref, wk_ref, wv_ref, ocmp_ref, owin_ref, sel_ref,
                              score_ref, rank_ref, *, n_cmp, n_sel, n_top, n_win, w_buf, pos0, t_new):
    f32, bf16 = jnp.float32, jnp.bfloat16
    nt = (((1,), (1,)), ((), ()))
    scale = NSA_HEAD_DIM ** -0.5
    rows = q_ref.shape[0]
    per_head = NSA_GQA * t_new
    q = q_ref[...]
    tq = pos0 + (lax.broadcasted_iota(jnp.int32, (rows, 1), 0) & (t_new - 1))

    def heads_out(o_ref, p, v_ref):
        for h in range(NSA_N_KV):
            o_ref[h * per_head:(h + 1) * per_head, :] = jnp.dot(
                p[h * per_head:(h + 1) * per_head].astype(bf16),
                v_ref[:, h * NSA_HEAD_DIM:(h + 1) * NSA_HEAD_DIM], preferred_element_type=f32)

    n_cp = kc_ref.shape[0]
    s = lax.dot_general(q, kc_ref[...], nt, preferred_element_type=f32) * scale
    n_idx = lax.broadcasted_iota(jnp.int32, (rows, n_cp), 1)
    cmask = (n_idx * NSA_CMP_STRIDE + (NSA_CMP_BLOCK - 1) <= tq) & (n_idx < n_cmp)
    p = jnp.where(cmask, _nsa_softmax_rows(s, cmask), 0.0)
    heads_out(ocmp_ref, p, vc_ref)

    psum = jnp.concatenate(
        [sum(p[h * per_head + g * t_new:h * per_head + (g + 1) * t_new] for g in range(NSA_GQA))
         for h in range(NSA_N_KV)], axis=0)
    n_sp = sel_ref.shape[1]
    cols = NSA_N_KV * t_new
    jn = lax.broadcasted_iota(jnp.int32, (n_sp, n_cp), 0)
    nn = lax.broadcasted_iota(jnp.int32, (n_sp, n_cp), 1)
    r = NSA_SEL_BLOCK // NSA_CMP_STRIDE
    span = NSA_CMP_BLOCK // NSA_CMP_STRIDE
    pool = jnp.where((nn >= r * jn - (span - 1)) & (nn <= r * jn + (r - 1)), 1.0, 0.0).astype(bf16)
    imp = jnp.zeros((n_sp, cols), f32)
    rest = psum
    for _ in range(3):
        part = rest.astype(bf16)
        imp = imp + lax.dot_general(pool, part, nt, preferred_element_type=f32)
        rest = rest - part.astype(f32)
    jidx = lax.broadcasted_iota(jnp.int32, (n_sp, cols), 0)
    cur = (pos0 + (lax.broadcasted_iota(jnp.int32, (n_sp, cols), 1) & (t_new - 1))) >> NSA_SEL_SHIFT
    forced = (jidx == 0) | (jidx == cur) | (jidx == cur - 1)
    score = jnp.where(forced, FORCE_SCORE, jnp.where(jidx <= cur, imp, -FORCE_SCORE))
    score_ref[...] = jnp.where(jidx < n_sel, score, -2.0 * FORCE_SCORE)
    rank_ref[...] = jnp.zeros(rank_ref.shape, jnp.int32)

    def rank_body(k, carry):
        row = score_ref[pl.ds(k, 1), :]
        sc = score_ref[...]
        before = (row > sc) | ((row == sc) & (k < jidx))
        rank_ref[...] = rank_ref[...] + jnp.where(before, 1, 0)
        return carry

    lax.fori_loop(0, n_sel, rank_body, 0)
    sel_t = jnp.where((rank_ref[...] < n_top) & (jidx < n_sel), 1.0, 0.0).astype(bf16)
    ri = lax.broadcasted_iota(jnp.int32, (rows, cols), 0)
    ci = lax.broadcasted_iota(jnp.int32, (rows, cols), 1)
    same = (((ri >> _log2(per_head)) == (ci >> _log2(t_new)))
            & ((ri & (t_new - 1)) == (ci & (t_new - 1))))
    spread = jnp.where(same, 1.0, 0.0).astype(bf16)
    sel_ref[...] = lax.dot_general(spread, sel_t, nt, preferred_element_type=f32).astype(bf16)

    s = lax.dot_general(q, wk_ref[...], nt, preferred_element_type=f32) * scale
    kidx = lax.broadcasted_iota(jnp.int32, (rows, wk_ref.shape[0]), 1)
    diff = tq - (pos0 - w_buf + kidx)
    wmask = (diff >= 0) & (diff < NSA_WINDOW) & (kidx < n_win) & (pos0 - w_buf + kidx >= 0)
    heads_out(owin_ref, _nsa_softmax_rows(s, wmask), wv_ref)


def _nsa_decode_slc_kernel(pt_ref, q_ref, sel_ref, new_ref, *refs, pos0, t_new):
    del pt_ref
    pages = refs[:NSA_SLC_PAGES]
    o_ref, m_ref, l_ref, acc_ref = refs[NSA_SLC_PAGES:]
    f32, bf16 = jnp.float32, jnp.bfloat16
    nt = (((1,), (1,)), ((), ()))
    step = pl.program_id(1)
    rows = q_ref.shape[0]
    page = pages[0].shape[0]
    per_head = NSA_GQA * t_new
    n_sp = sel_ref.shape[1]
    q = q_ref[...]
    tq = pos0 + (lax.broadcasted_iota(jnp.int32, (rows, 1), 0) & (t_new - 1))
    row_head = lax.broadcasted_iota(jnp.int32, (rows, page), 0) >> _log2(per_head)
    lane = lax.broadcasted_iota(jnp.int32, (rows, page), 1)
    blocks_per_page = page // NSA_SEL_BLOCK

    def attend(pg_ref, page_index):
        slab = lambda c: pg_ref[:, c, :].astype(bf16)
        kp = jnp.concatenate([slab(h) for h in range(NSA_N_KV)], axis=1)
        v_heads = jnp.concatenate([slab(NSA_N_KV + h) for h in range(NSA_N_KV)], axis=0)
        s = lax.dot_general(q, kp, nt, preferred_element_type=f32) * (NSA_HEAD_DIM ** -0.5)
        jrow = lax.broadcasted_iota(jnp.int32, (n_sp, page), 0)
        jcol = page_index * blocks_per_page + (lax.broadcasted_iota(jnp.int32, (n_sp, page), 1) >> NSA_SEL_SHIFT)
        expand = jnp.where(jrow == jcol, 1.0, 0.0).astype(bf16)
        chosen = jnp.dot(sel_ref[...], expand, preferred_element_type=f32) > 0.5
        s = jnp.where(chosen & (page_index * page + lane <= tq), s, NEG_INF)
        m_old = m_ref[...]
        m_new = jnp.maximum(m_old, jnp.max(s, axis=-1, keepdims=True))
        alpha = jnp.exp(m_old - m_new)
        p = jnp.exp(s - m_new)
        l_ref[...] = alpha * l_ref[...] + jnp.sum(p, axis=-1, keepdims=True)
        p_heads = jnp.concatenate([jnp.where(row_head == h, p, 0.0) for h in range(NSA_N_KV)], axis=1).astype(bf16)
        acc_ref[...] = alpha * acc_ref[...] + jnp.dot(p_heads, v_heads, preferred_element_type=f32)
        m_ref[...] = m_new

    @pl.when(step == 0)
    def _():
        m_ref[...] = jnp.full(m_ref.shape, NEG_INF, f32)
        l_ref[...] = jnp.zeros(l_ref.shape, f32)
        acc_ref[...] = jnp.zeros(acc_ref.shape, f32)
        attend(new_ref, pos0 // page)

    for k, pg in enumerate(pages):
        attend(pg, step * NSA_SLC_PAGES + k)

    @pl.when(step == pl.num_programs(1) - 1)
    def _():
        o_ref[...] = acc_ref[...] / l_ref[...]


def nsa_decode_attention(q, kv, cache, page_ids, win_buf, w_cmp1, w_cmp2, cmp_pe):
    b_, T, _ = q.shape
    n_pages = page_ids.shape[1]
    page = cache.shape[1]
    pos0 = n_pages * page
    w_buf = win_buf.shape[1]
    assert T & (T - 1) == 0 and T <= NSA_SEL_BLOCK and pos0 % NSA_SEL_BLOCK == 0 and page % NSA_SEL_BLOCK == 0
    assert n_pages % NSA_SLC_PAGES == 0
    bf16 = jnp.bfloat16
    lp = _round_up(pos0 + T, NSA_SEL_BLOCK)
    n_sel = lp // NSA_SEL_BLOCK
    n_sp = _round_up(n_sel, 128)
    kv6 = kv.reshape(b_, T, 6, NSA_N_KV, NSA_HEAD_DIM)
    kc, vc = nsa_decode_compress(cache, page_ids, kv6[:, :, 0:2], w_cmp1, w_cmp2, cmp_pe)
    n_cmp = kc.shape[1]
    n_cp = _round_up(n_cmp, 128)
    pad_c = ((0, 0), (0, n_cp - n_cmp), (0, 0))
    kc, vc = jnp.pad(kc, pad_c).astype(bf16), jnp.pad(vc, pad_c).astype(bf16)

    rows = NSA_N_HEADS * T
    q5 = jnp.transpose(q.reshape(b_, T, NSA_N_KV, NSA_GQA, NSA_HEAD_DIM), (0, 2, 3, 1, 4))
    q_blk = jnp.einsum('bhgtd,hk->bhgtkd', q5, jnp.eye(NSA_N_KV, dtype=q.dtype))
    q_blk = q_blk.reshape(b_, rows, NSA_KVW).astype(bf16)

    n_win = w_buf + T
    n_wp = _round_up(n_win, 128)
    wk = jnp.concatenate([win_buf, kv6[:, :, 4:6]], axis=1)
    wk = jnp.pad(wk, ((0, 0), (0, n_wp - n_win), (0, 0), (0, 0), (0, 0))).astype(bf16)
    wkk, wkv = wk[:, :, 0].reshape(b_, n_wp, NSA_KVW), wk[:, :, 1].reshape(b_, n_wp, NSA_KVW)

    per_b = lambda n, w: pl.BlockSpec((None, n, w), lambda b: (b, 0, 0))
    o_shape = jax.ShapeDtypeStruct((b_, rows, NSA_HEAD_DIM), jnp.float32)
    o_cmp, o_win, sel = pl.pallas_call(
        functools.partial(_nsa_decode_select_kernel, n_cmp=n_cmp, n_sel=n_sel, n_top=min(NSA_N_SELECT, n_sel),
                          n_win=n_win, w_buf=w_buf, pos0=pos0, t_new=T),
        grid=(b_,),
        in_specs=[per_b(rows, NSA_KVW), per_b(n_cp, NSA_KVW), per_b(n_cp, NSA_KVW),
                  per_b(n_wp, NSA_KVW), per_b(n_wp, NSA_KVW)],
        out_specs=[per_b(rows, NSA_HEAD_DIM), per_b(rows, NSA_HEAD_DIM), per_b(rows, n_sp)],
        out_shape=[o_shape, o_shape, jax.ShapeDtypeStruct((b_, rows, n_sp), bf16)],
        scratch_shapes=[pltpu.VMEM((n_sp, NSA_N_KV * T), jnp.float32),
                        pltpu.VMEM((n_sp, NSA_N_KV * T), jnp.int32)],
        compiler_params=pltpu.CompilerParams(
            dimension_semantics=("parallel",), vmem_limit_bytes=V7X_VMEM_LIMIT_BYTES),
        name="nsa_decode_select",
    )(q_blk, kc, vc, wkk, wkv)

    new_slc = jnp.pad(kv6[:, :, 2:4].reshape(b_, T, NSA_HALF_SLABS, NSA_HEAD_DIM),
                      ((0, 0), (0, page - T), (0, 0), (0, 0)))
    half_page = (None, page, NSA_HALF_SLABS, NSA_HEAD_DIM)

    def page_spec(k):
        return pl.BlockSpec(half_page, lambda b, s, pt: (pt[b, NSA_SLC_PAGES * s + k], 0, 1, 0))

    bs = lambda n, w: pl.BlockSpec((None, n, w), lambda b, s, pt: (b, 0, 0))
    o_slc = pl.pallas_call(
        functools.partial(_nsa_decode_slc_kernel, pos0=pos0, t_new=T),
        grid_spec=pltpu.PrefetchScalarGridSpec(
            num_scalar_prefetch=1,
            grid=(b_, n_pages // NSA_SLC_PAGES),
            in_specs=[bs(rows, NSA_KVW), bs(rows, n_sp), pl.BlockSpec(half_page, lambda b, s, pt: (b, 0, 0, 0))]
            + [page_spec(k) for k in range(NSA_SLC_PAGES)],
            out_specs=bs(rows, NSA_HEAD_DIM),
            scratch_shapes=[pltpu.VMEM((rows, 1), jnp.float32),
                            pltpu.VMEM((rows, 1), jnp.float32),
                            pltpu.VMEM((rows, NSA_HEAD_DIM), jnp.float32)]),
        out_shape=o_shape,
        compiler_params=pltpu.CompilerParams(
            dimension_semantics=("parallel", "arbitrary"), vmem_limit_bytes=V7X_VMEM_LIMIT_BYTES),
        name="nsa_decode_slc",
    )(page_ids, q_blk, sel, new_slc, *([cache] * NSA_SLC_PAGES))

    def token_major(o):
        o = o.reshape(b_, NSA_N_KV, NSA_GQA, T, NSA_HEAD_DIM)
        return jnp.transpose(o, (0, 3, 1, 2, 4)).reshape(b_, T, NSA_N_HEADS * NSA_HEAD_DIM)

    return token_major(o_cmp), token_major(o_slc), token_major(o_win)


def nsa_decode_mixer(x, cache, page_ids, win_buf, w_q, w_kv, w_gate, b_gate, w_cmp1, w_cmp2, cmp_pe, w_out):
    b_, T, _ = x.shape
    q = matmul(x, w_q)
    kv = matmul(x, w_kv)
    kv6 = kv.reshape(b_, T, 6, NSA_N_KV, NSA_HEAD_DIM)
    o_cmp, o_slc, o_win = nsa_decode_attention(q, kv, cache, page_ids, win_buf, w_cmp1, w_cmp2, cmp_pe)
    gate = jax.nn.sigmoid(matmul(x, w_gate) + b_gate).reshape(b_, T, NSA_N_HEADS, 3)

    def heads(t):
        return t.reshape(b_, T, NSA_N_HEADS, NSA_HEAD_DIM)

    o = gate[..., 0:1] * heads(o_cmp) + gate[..., 1:2] * heads(o_slc) + gate[..., 2:3] * heads(o_win)
    y = matmul(o.reshape(b_, T, NSA_N_HEADS * NSA_HEAD_DIM), w_out)
    win_new = jnp.concatenate([win_buf, kv6[:, :, 4:6]], axis=1)[:, -win_buf.shape[1]:]
    return y, kv6[:, :, :4], win_new


def nsa_mixer(x, past_rows, win_buf, pos0, w_q, w_kv, w_gate, b_gate, w_cmp1, w_cmp2, cmp_pe, w_out):
    b_, T, _ = x.shape
    if past_rows is None:
        assert win_buf is None and pos0 == 0 and T % NSA_SEL_BLOCK == 0
        return nsa_prompt_mixer(x, w_q, w_kv, w_gate, b_gate, w_cmp1, w_cmp2, cmp_pe, w_out)
    q = matmul(x, w_q).reshape(b_, T, NSA_N_KV, NSA_GQA, NSA_HEAD_DIM)
    kv = matmul(x, w_kv).reshape(b_, T, 6, NSA_N_KV, NSA_HEAD_DIM)
    rows = kv[:, :, :4]
    full = rows if past_rows is None else jnp.concatenate([past_rows, rows], axis=1)
    q_pos = pos0 + jnp.arange(T)
    o_cmp, o_slc = nsa_cmp_slc(q, full, q_pos, w_cmp1, w_cmp2, cmp_pe)
    win_rows = kv[:, :, 4:6]
    if win_buf is None:
        o_win = nsa_window_prompt(q, win_rows)
        win_new = win_rows[:, -min(NSA_WINDOW, T):]
    else:
        w_b = win_buf.shape[1]
        wk = jnp.concatenate([win_buf, win_rows], axis=1)
        k_pos = pos0 - w_b + jnp.arange(w_b + T)
        o_win = window_attend(q[:, None], wk[:, None, :, 0], wk[:, None, :, 1],
                              q_pos[None], k_pos[None])[:, 0]
        win_new = wk[:, -w_b:]
    gate = jax.nn.sigmoid(matmul(x, w_gate) + b_gate).reshape(b_, T, NSA_N_KV, NSA_GQA, 3)
    o = gate[..., 0:1] * o_cmp + gate[..., 1:2] * o_slc + gate[..., 2:3] * o_win
    return matmul(o.reshape(b_, T, NSA_N_HEADS * NSA_HEAD_DIM), w_out), rows, win_new


def conv_ffn(x, hist, w_up, conv_w, conv_b, w_down):
    ag = matmul(x, w_up)
    h = conv_act(ag, 0, FFN_DIM, hist, conv_w, conv_b, "gelu_gate", gate_col0=FFN_DIM, out_dtype=jnp.bfloat16)
    return matmul(h, w_down), conv_tail(hist, ag, 0, FFN_DIM)


def kernel(x_prompt, x_sample, cache_nsa, state_nsa_win, state_ssd, state_ssd_conv, state_mlstm_c,
           state_mlstm_n, state_mlstm_m, state_mlstm_conv, state_s5, state_ffn_conv, page_table,
           ln_g, ln_b, ffn_w_up, ffn_conv_w, ffn_conv_b, ffn_w_down,
           ssd_w_in, ssd_conv_w, ssd_conv_b, ssd_dt_bias, ssd_a_log, ssd_d, ssd_norm_g, ssd_w_out,
           mlstm_w_up, mlstm_conv_w, mlstm_conv_b, mlstm_w_q, mlstm_w_k, mlstm_w_v, mlstm_w_if,
           mlstm_b_if, mlstm_skip, mlstm_norm_g, mlstm_w_down,
           s5_a_re, s5_a_im, s5_log_dt, s5_b_re, s5_b_im, s5_c_re, s5_c_im, s5_d, s5_w_glu_a, s5_w_glu_b,
           nsa_w_q, nsa_w_kv, nsa_w_gate, nsa_b_gate, nsa_w_cmp1, nsa_w_cmp2, nsa_cmp_pe, nsa_w_out):

    def bf16_stack(w):
        return w.astype(jnp.bfloat16)

    ffn_w_up, ffn_w_down = bf16_stack(ffn_w_up), bf16_stack(ffn_w_down)
    ssd_w_in, ssd_w_out = bf16_stack(ssd_w_in), bf16_stack(ssd_w_out)
    mlstm_w_up, mlstm_w_down = bf16_stack(mlstm_w_up), bf16_stack(mlstm_w_down)
    mlstm_w_if = bf16_stack(mlstm_w_if).reshape(-1, MLSTM_D_INNER, 2 * MLSTM_N_HEADS)
    s5_w_glu_a, s5_w_glu_b = bf16_stack(s5_w_glu_a), bf16_stack(s5_w_glu_b)
    nsa_w_q, nsa_w_kv, nsa_w_out = bf16_stack(nsa_w_q), bf16_stack(nsa_w_kv), bf16_stack(nsa_w_out)
    nsa_w_gate = bf16_stack(nsa_w_gate)

    def trunk(x, sample):
        b_, T, _ = x.shape
        dt_ = x.dtype
        pos0 = PAST_LEN if sample else 0
        o_nsa, o_win, o_ssd, o_ssdc, o_mc, o_mn, o_mm, o_mconv, o_s5, o_ffn = ([] for _ in range(10))
        for i in range(DEPTH):
            kind, j = i % N_MIXERS, i // N_MIXERS
            if kind == 0:
                hist = state_ssd_conv[j] if sample else jnp.zeros((b_, SSD_CONV_W - 1, SSD_CONV_DIM), dt_)
                h0 = state_ssd[j] if sample else jnp.zeros((b_, SSD_N_HEADS, SSD_HEADDIM, SSD_D_STATE), dt_)
                y, hist_new, h_new = ssd_mixer(x, hist, h0, (ssd_w_in, j), ssd_conv_w[j], ssd_conv_b[j],
                                               ssd_dt_bias[j], ssd_a_log[j], ssd_d[j], ssd_norm_g[j],
                                               (ssd_w_out, j))
                o_ssd.append(h_new)
                o_ssdc.append(hist_new)
            elif kind == 1:
                hist = state_mlstm_conv[j] if sample else jnp.zeros((b_, MLSTM_CONV_W - 1, MLSTM_D_INNER), dt_)
                c0 = state_mlstm_c[j] if sample else jnp.zeros((b_, MLSTM_N_HEADS, MLSTM_HEAD_DIM, MLSTM_HEAD_DIM), dt_)
                n0 = state_mlstm_n[j] if sample else jnp.zeros((b_, MLSTM_N_HEADS, MLSTM_HEAD_DIM), dt_)
                m0 = state_mlstm_m[j] if sample else jnp.zeros((b_, MLSTM_N_HEADS), dt_)
                y, hist_new, c, n, m = mlstm_mixer(x, hist, c0, n0, m0, (mlstm_w_up, j), mlstm_conv_w[j],
                                                   mlstm_conv_b[j], mlstm_w_q[j], mlstm_w_k[j], mlstm_w_v[j],
                                                   [(mlstm_w_if, 3 * j + part) for part in range(3)],
                                                   mlstm_b_if[j], mlstm_skip[j],
                                                   mlstm_norm_g[j], (mlstm_w_down, j))
                o_mc.append(c)
                o_mn.append(n)
                o_mm.append(m)
                o_mconv.append(hist_new)
            elif kind == 2:
                h0 = state_s5[j] if sample else jnp.zeros((b_, S5_N_GROUPS, S5_STATE, 2), dt_)
                y, h_new = s5_mixer(x, h0, s5_a_re[j], s5_a_im[j], s5_log_dt[j], s5_b_re[j], s5_b_im[j],
                                    s5_c_re[j], s5_c_im[j], s5_d[j], (s5_w_glu_a, j), (s5_w_glu_b, j))
                o_s5.append(h_new)
            else:
                nsa_w = ((nsa_w_q, j), (nsa_w_kv, j), (nsa_w_gate, j), nsa_b_gate[j], nsa_w_cmp1[j], nsa_w_cmp2[j],
                         nsa_cmp_pe[j], (nsa_w_out, j))
                if sample:
                    n_pool, page = cache_nsa.shape[1:3]
                    assert pos0 == page_table.shape[1] * page
                    y, rows, win_new = nsa_decode_mixer(
                        x, cache_nsa.reshape(-1, page, NSA_ROW_SLABS, NSA_HEAD_DIM), page_table + j * n_pool,
                        state_nsa_win[j], *nsa_w)
                else:
                    y, rows, win_new = nsa_mixer(x, None, None, pos0, *nsa_w)
                o_nsa.append(rows)
                o_win.append(win_new)
            x = layer_norm(DEEPNORM_ALPHA * x + y, ln_g[i, 0], ln_b[i, 0])
            fhist = state_ffn_conv[i] if sample else jnp.zeros((b_, FFN_CONV_W - 1, FFN_DIM), dt_)
            y, fhist_new = conv_ffn(x, fhist, (ffn_w_up, i), ffn_conv_w[i], ffn_conv_b[i], (ffn_w_down, i))
            o_ffn.append(fhist_new)
            x = layer_norm(DEEPNORM_ALPHA * x + y, ln_g[i, 1], ln_b[i, 1])
        st = jnp.stack
        return (x, st(o_nsa), st(o_win), st(o_ssd), st(o_ssdc), st(o_mc), st(o_mn), st(o_mm),
                st(o_mconv), st(o_s5), st(o_ffn))

    (y_prompt, nsa_p, win_p, ssd_p, ssdc_p, mc_p, mn_p, mm_p, mconv_p, s5_p, ffn_p) = trunk(x_prompt, False)
    (y_sample, nsa_s, win_s, ssd_s, ssdc_s, mc_s, mn_s, mm_s, mconv_s, s5_s, ffn_s) = trunk(x_sample, True)
    return (y_prompt, y_sample, nsa_p, nsa_s, win_p, win_s, ssd_p, ssd_s, ssdc_p, ssdc_s, mc_p, mc_s,
            mn_p, mn_s, mm_p, mm_s, mconv_p, mconv_s, s5_p, s5_s, ffn_p, ffn_s)
```

```python
import functools
import math

import jax
import jax.numpy as jnp
from jax import lax
from jax.experimental import pallas as pl
from jax.experimental.pallas import tpu as pltpu

D_MODEL = 2048
DEPTH = 4
PAST_LEN = 16384
N_MIXERS = 4

DEEPNORM_ALPHA = (2.0 * DEPTH) ** 0.25
LN_EPS = 1e-5
RMS_EPS = 1e-5
NEG_INF = -1e30
FORCE_SCORE = 1e4

SSD_D_INNER = 2 * D_MODEL
SSD_HEADDIM = 64
SSD_N_HEADS = SSD_D_INNER // SSD_HEADDIM
SSD_N_GROUPS = 8
SSD_D_STATE = 128
SSD_CONV_W = 4
SSD_CHUNK = 256
SSD_CONV_DIM = SSD_D_INNER + 2 * SSD_N_GROUPS * SSD_D_STATE

MLSTM_D_INNER = 2 * D_MODEL
MLSTM_N_HEADS = 4
MLSTM_HEAD_DIM = MLSTM_D_INNER // MLSTM_N_HEADS
MLSTM_CONV_W = 4
MLSTM_CHUNK = 64

S5_GROUP = 16
S5_N_GROUPS = D_MODEL // S5_GROUP
S5_STATE = 64

NSA_N_HEADS = 16
NSA_N_KV = 4
NSA_HEAD_DIM = D_MODEL // NSA_N_HEADS
NSA_GQA = NSA_N_HEADS // NSA_N_KV
NSA_CMP_BLOCK = 32
NSA_CMP_STRIDE = 16
NSA_SEL_BLOCK = 64
NSA_N_SELECT = 16
NSA_WINDOW = 512
NSA_QBLOCK = 32
NSA_WBLOCK = 128

FFN_DIM = 5632
FFN_CONV_W = 3

V7X_VMEM_LIMIT_BYTES = 48 * 1024 * 1024


def _mm_kernel(x_ref, w_ref, o_ref):
    o_ref[...] = jnp.dot(x_ref[...], w_ref[...], preferred_element_type=jnp.float32)


def _pick(dim, target):
    if dim <= target:
        return dim
    t = target
    while dim % t:
        t //= 2
    return t


def _mm_tiles(M, K, N):
    tm = _pick(M, 1024)
    tn = N if N <= 512 else (1024 if K <= 2048 else 512)
    double_buffered = 2 * (tm * K * 2 + K * tn * 2 + tm * tn * 4)
    assert double_buffered <= V7X_VMEM_LIMIT_BYTES, (M, K, N)
    return tm, tn


def matmul(x, w):
    stack, s = w if isinstance(w, tuple) else (w[None], 0)
    _, K, N = stack.shape
    lead = x.shape[:-1]
    x2 = x.astype(jnp.bfloat16).reshape(-1, K)
    M = x2.shape[0]
    tm, tn = _mm_tiles(M, K, N)
    out = pl.pallas_call(
        _mm_kernel,
        grid=(M // tm, pl.cdiv(N, tn)),
        in_specs=[pl.BlockSpec((tm, K), lambda i, j: (i, 0)),
                  pl.BlockSpec((None, K, tn), lambda i, j: (s, 0, j))],
        out_specs=pl.BlockSpec((tm, tn), lambda i, j: (i, j)),
        out_shape=jax.ShapeDtypeStruct((M, N), jnp.float32),
        compiler_params=pltpu.CompilerParams(
            dimension_semantics=("parallel", "arbitrary"),
            vmem_limit_bytes=V7X_VMEM_LIMIT_BYTES),
        name="matmul",
    )(x2, stack.astype(jnp.bfloat16))
    return out.reshape(lead + (N,))


def layer_norm(x, g, b):
    mu = jnp.mean(x, axis=-1, keepdims=True)
    var = jnp.mean(jnp.square(x - mu), axis=-1, keepdims=True)
    return (x - mu) * lax.rsqrt(var + LN_EPS) * g + b


def group_rms_norm(y, g, n_groups):
    yg = y.reshape(y.shape[:-1] + (n_groups, -1))
    yg = yg * lax.rsqrt(jnp.mean(yg * yg, axis=-1, keepdims=True) + RMS_EPS)
    return yg.reshape(y.shape) * g


def head_layer_norm(h, g):
    mu = jnp.mean(h, axis=-1, keepdims=True)
    var = jnp.mean(jnp.square(h - mu), axis=-1, keepdims=True)
    hn = (h - mu) * lax.rsqrt(var + LN_EPS)
    return hn.reshape(h.shape[:2] + (-1,)) * g


def causal_dwconv(x_hist, w, b):
    width = w.shape[0]
    T = x_hist.shape[1] - (width - 1)
    out = b
    for k in range(width):
        out = out + w[k] * x_hist[:, k:k + T]
    return out


CONV_TIME_TILE = 512
CONV_CHAN_TILE = 2048
QKV_CHAN_TILE = 512
CONV_HALO = 8
VREG_LANES = 128


def _causal_conv_tile(cur_ref, prev_ref, hist_ref, w_ref, b_ref, ext_ref, width):
    rows = cur_ref.shape[0]
    ext_ref[0:CONV_HALO, :] = jnp.where(pl.program_id(1) == 0, hist_ref[...], prev_ref[...])
    ext_ref[CONV_HALO:, :] = cur_ref[...]
    acc = b_ref[...]
    for k in range(width):
        acc = acc + w_ref[k:k + 1, :] * ext_ref[pl.ds(CONV_HALO - (width - 1 - k), rows), :]
    return acc


def _conv_act_kernel(cur_ref, prev_ref, hist_ref, w_ref, b_ref, *rest, width, act):
    acc = _causal_conv_tile(cur_ref, prev_ref, hist_ref, w_ref, b_ref, rest[-1], width)
    if act == "silu":
        o_ref = rest[0]
        o_ref[...] = jax.nn.silu(acc).astype(o_ref.dtype)
    else:
        g_ref, o_ref = rest[:2]
        o_ref[...] = (jax.nn.gelu(acc) * g_ref[...]).astype(o_ref.dtype)


def _conv_tiles(T, chans, offsets, max_lanes):
    lanes = 128
    units = math.gcd(chans // lanes, *[o // lanes for o in offsets])
    ct = lanes * max(d for d in range(1, units + 1) if units % d == 0 and lanes * d <= max_lanes)
    tt = min(T, CONV_TIME_TILE)
    assert T % tt == 0 and tt % CONV_HALO == 0 and chans % lanes == 0 and all(o % lanes == 0 for o in offsets)
    return tt, ct


def conv_act(src, col0, chans, hist, w, b, act, gate_col0=None, out_dtype=jnp.float32):
    b_, T, _ = src.shape
    width = w.shape[0]
    gated = act == "gelu_gate"
    tt, ct = _conv_tiles(T, chans, [col0, gate_col0] if gated else [col0], CONV_CHAN_TILE)
    assert width <= CONV_HALO + 1
    hist8 = jnp.pad(hist, ((0, 0), (CONV_HALO - (width - 1), 0), (0, 0)))
    halo_blocks = tt // CONV_HALO
    cb0 = col0 // ct
    in_specs = [pl.BlockSpec((None, tt, ct), lambda b, t, c: (b, t, cb0 + c)),
                pl.BlockSpec((None, CONV_HALO, ct), lambda b, t, c: (b, jnp.maximum(t * halo_blocks - 1, 0), cb0 + c)),
                pl.BlockSpec((None, CONV_HALO, ct), lambda b, t, c: (b, 0, c)),
                pl.BlockSpec((width, ct), lambda b, t, c: (0, c)),
                pl.BlockSpec((1, ct), lambda b, t, c: (0, c))]
    args = [src, src, hist8, w, b.reshape(1, chans)]
    if gated:
        gb0 = gate_col0 // ct
        in_specs.append(pl.BlockSpec((None, tt, ct), lambda b, t, c: (b, t, gb0 + c)))
        args.append(src)
    return pl.pallas_call(
        functools.partial(_conv_act_kernel, width=width, act=act),
        grid=(b_, T // tt, chans // ct),
        in_specs=in_specs,
        out_specs=pl.BlockSpec((None, tt, ct), lambda b, t, c: (b, t, c)),
        out_shape=jax.ShapeDtypeStruct((b_, T, chans), out_dtype),
        scratch_shapes=[pltpu.VMEM((CONV_HALO + tt, ct), jnp.float32)],
        compiler_params=pltpu.CompilerParams(
            dimension_semantics=("parallel", "parallel", "parallel"),
            vmem_limit_bytes=V7X_VMEM_LIMIT_BYTES),
        name="conv_act",
    )(*args)


def _blockdiag_coefs(w):
    nb, bs, _ = w.shape
    shifts = jnp.stack([jnp.eye(bs, k=d, dtype=w.dtype) for d in range(-(bs - 1), bs)])
    return jnp.einsum('ncd,kcd->knd', w, shifts).reshape(2 * bs - 1, nb * bs)


def _mlstm_qkv_kernel(cur_ref, prev_ref, hist_ref, w_ref, b_ref, cq_ref, ck_ref, cv_ref,
                      xc_ref, q_ref, k_ref, v_ref, ext_ref, *, width, bs):
    xc_ref[...] = jax.nn.silu(_causal_conv_tile(cur_ref, prev_ref, hist_ref, w_ref, b_ref, ext_ref, width))

    def project(x_ref, col, coef_refs, out_refs):
        x = x_ref[:, col]
        outs = [jnp.zeros(x.shape, jnp.float32) for _ in coef_refs]
        for d in range(-(bs - 1), bs):
            moved = x if d == 0 else pltpu.roll(x, d % VREG_LANES, 1)
            row = bs - 1 + d
            outs = [o + c[row:row + 1, col] * moved for o, c in zip(outs, coef_refs)]
        for o, o_ref in zip(outs, out_refs):
            o_ref[:, col] = o.astype(o_ref.dtype)

    for c0 in range(0, cur_ref.shape[1], VREG_LANES):
        col = slice(c0, c0 + VREG_LANES)
        project(xc_ref, col, (cq_ref, ck_ref), (q_ref, k_ref))
        project(cur_ref, col, (cv_ref,), (v_ref,))


def mlstm_qkv(up, hist, conv_w, conv_b, w_q, w_k, w_v):
    b_, T, _ = up.shape
    chans = MLSTM_D_INNER
    width = conv_w.shape[0]
    bs = w_q.shape[1]
    tt, ct = _conv_tiles(T, chans, [0], QKV_CHAN_TILE)
    assert VREG_LANES % bs == 0 and width <= CONV_HALO + 1
    hist8 = jnp.pad(hist, ((0, 0), (CONV_HALO - (width - 1), 0), (0, 0)))
    halo_blocks = tt // CONV_HALO
    tile = pl.BlockSpec((None, tt, ct), lambda b, t, c: (b, t, c))
    coef = pl.BlockSpec((2 * bs - 1, ct), lambda b, t, c: (0, c))
    shape = lambda dt: jax.ShapeDtypeStruct((b_, T, chans), dt)
    return pl.pallas_call(
        functools.partial(_mlstm_qkv_kernel, width=width, bs=bs),
        grid=(b_, T // tt, chans // ct),
        in_specs=[tile,
                  pl.BlockSpec((None, CONV_HALO, ct), lambda b, t, c: (b, jnp.maximum(t * halo_blocks - 1, 0), c)),
                  pl.BlockSpec((None, CONV_HALO, ct), lambda b, t, c: (b, 0, c)),
                  pl.BlockSpec((width, ct), lambda b, t, c: (0, c)),
                  pl.BlockSpec((1, ct), lambda b, t, c: (0, c)),
                  coef, coef, coef],
        out_specs=[tile, tile, tile, tile],
        out_shape=[shape(jnp.float32), shape(jnp.bfloat16), shape(jnp.bfloat16), shape(jnp.bfloat16)],
        scratch_shapes=[pltpu.VMEM((CONV_HALO + tt, ct), jnp.float32)],
        compiler_params=pltpu.CompilerParams(
            dimension_semantics=("parallel", "parallel", "parallel"),
            vmem_limit_bytes=V7X_VMEM_LIMIT_BYTES),
        name="mlstm_qkv",
    )(up, up, hist8, conv_w, conv_b.reshape(1, chans),
      _blockdiag_coefs(w_q), _blockdiag_coefs(w_k), _blockdiag_coefs(w_v))


def conv_tail(hist, src, col0, chans):
    keep = hist.shape[1]
    return jnp.concatenate([hist, src[:, -keep:, col0:col0 + chans]], axis=1)[:, -keep:]


def blockdiag(x, w):
    nb, bs, _ = w.shape
    y = jnp.einsum('btnc,ncd->btnd', x.reshape(x.shape[:2] + (nb, bs)), w)
    return y.reshape(x.shape[:2] + (nb * bs,))


def segsum(x):
    T = x.shape[-1]
    xr = jnp.broadcast_to(x[..., :, None], x.shape + (T,))
    strict = jnp.tril(jnp.ones((T, T), bool), -1)
    xs = jnp.cumsum(jnp.where(strict, xr, 0), axis=-2)
    return jnp.where(jnp.tril(jnp.ones((T, T), bool)), xs, -jnp.inf)


def ssd_scan(xs, dt, a, bm, cm, h0):
    b_, T = xs.shape[:2]
    cl = math.gcd(T, SSD_CHUNK)
    nc = T // cl

    def chunk(t):
        return t.reshape((b_, nc, cl) + t.shape[2:])

    xc, dtc, bc, cc = chunk(xs), chunk(dt), chunk(bm), chunk(cm)
    dt_t = jnp.moveaxis(dtc, 2, -1)
    da = dt_t * a[:, :, None]
    acs = jnp.cumsum(da, axis=-1)
    decay_in = jnp.exp(segsum(da))
    cb = jnp.einsum('bclgn,bcsgn->bcgls', cc, bc)
    w_diag = cb[:, :, :, None] * decay_in * dt_t[..., None, :]
    y_diag = jnp.einsum('bcgrls,bcsgrp->bclgrp', w_diag, xc)
    w_state = jnp.exp(acs[..., -1:] - acs) * dt_t
    states = jnp.einsum('bclgn,bcgrl,bclgrp->bcgrpn', bc, w_state, xc)
    states = jnp.concatenate([h0[:, None], states], axis=1)
    tot = jnp.pad(jnp.moveaxis(acs[..., -1], 1, -1), ((0, 0), (0, 0), (0, 0), (1, 0)))
    decay_chunk = jnp.exp(segsum(tot))
    new_states = jnp.einsum('bgrzc,bcgrpn->bzgrpn', decay_chunk, states)
    y_off = jnp.einsum('bclgn,bcgrpn,bcgrl->bclgrp', cc, new_states[:, :-1], jnp.exp(acs))
    return (y_diag + y_off).reshape(xs.shape), new_states[:, -1]


SSD_HEADS_PER_GROUP = SSD_N_HEADS // SSD_N_GROUPS
SSD_GROUP_WIDTH = SSD_HEADS_PER_GROUP * SSD_HEADDIM
SSD_HEAD_SHIFT = SSD_HEADDIM.bit_length() - 1
assert 1 << SSD_HEAD_SHIFT == SSD_HEADDIM and SSD_GROUP_WIDTH == SSD_D_INNER // SSD_N_GROUPS


def _ssd_kernel(x_ref, z_ref, cm_ref, bmt_ref, cols_ref, rows_ref, h0_ref, d_ref, g_ref,
                y_ref, h_out_ref, h_ref):
    f32, bf16 = jnp.float32, jnp.bfloat16
    ck = pl.program_id(2)
    L, W = x_ref.shape
    R = SSD_HEADS_PER_GROUP

    @pl.when(ck == 0)
    def _():
        h_ref[...] = h0_ref[...]

    def per_channel(c):
        spread = jnp.where((lax.broadcasted_iota(jnp.int32, (R, W), 1) >> SSD_HEAD_SHIFT)
                           == lax.broadcasted_iota(jnp.int32, (R, W), 0), 1.0, 0.0).astype(bf16)
        out = jnp.zeros((L, W), f32)
        for _ in range(3):
            part = c.astype(bf16)
            out = out + jnp.dot(part, spread, preferred_element_type=f32)
            c = c - part.astype(f32)
        return out

    x = x_ref[...]
    xb = x.astype(bf16)
    cmb = cm_ref[...].astype(bf16)
    bmt = bmt_ref[...]
    cols = cols_ref[...]
    acs = per_channel(cols[:, :R])
    cb = jnp.dot(cmb, bmt, preferred_element_type=f32)
    causal = (lax.broadcasted_iota(jnp.int32, (L, L), 0) >= lax.broadcasted_iota(jnp.int32, (L, L), 1))
    lane_head = lax.broadcasted_iota(jnp.int32, (L, W), 1) >> SSD_HEAD_SHIFT
    y = jnp.zeros((L, W), f32)
    for r in range(R):
        acs_col = cols[:, r:r + 1]
        acs_row = rows_ref[r:r + 1, :]
        dt_row = rows_ref[SSD_HEADS_PER_GROUP + r:SSD_HEADS_PER_GROUP + r + 1, :]
        decay = jnp.exp(jnp.where(causal, acs_col - acs_row, -jnp.inf))
        w = (cb * decay * dt_row).astype(bf16)
        y = jnp.where(lane_head == r, jnp.dot(w, xb, preferred_element_type=f32), y)
    total = acs[L - 1:L, :]
    xw = (x * (jnp.exp(total - acs) * per_channel(cols[:, R:]))).astype(bf16)
    h_t = h_ref[...]
    y = y + jnp.dot(cmb, h_t.astype(bf16), preferred_element_type=f32) * jnp.exp(acs)
    h_ref[...] = jnp.exp(total) * h_t + jnp.dot(bmt, xw, preferred_element_type=f32)
    y = (y + d_ref[...] * x) * jax.nn.silu(z_ref[...])
    y_ref[...] = y * lax.rsqrt(jnp.mean(y * y, axis=-1, keepdims=True) + RMS_EPS) * g_ref[...]

    @pl.when(ck == pl.num_programs(2) - 1)
    def _():
        h_out_ref[...] = h_ref[...]


def ssd_cell(zx, xbc, dt, a, d_skip, norm_g, h0):
    b_, T, _ = xbc.shape
    G, R, P, N, W = SSD_N_GROUPS, SSD_HEADS_PER_GROUP, SSD_HEADDIM, SSD_D_STATE, SSD_GROUP_WIDTH
    L = math.gcd(T, SSD_CHUNK)
    nc = T // L
    acs = jnp.cumsum((dt * a).reshape(b_, nc, L, SSD_N_HEADS), axis=2)
    dtc = dt.reshape(b_, nc, L, SSD_N_HEADS)

    def rows(t):
        return jnp.transpose(t.reshape(b_, nc, L, G, R), (0, 3, 1, 4, 2))

    def cols(t):
        return jnp.transpose(t.reshape(b_, T, G, R), (0, 2, 1, 3))

    rowpack = jnp.concatenate([rows(acs), rows(dtc)], axis=3)
    colpack = jnp.concatenate([cols(acs), cols(dtc)], axis=3)
    bmt = jnp.transpose(xbc[..., SSD_D_INNER:SSD_D_INNER + G * N].astype(jnp.bfloat16).reshape(b_, T, G, N),
                        (0, 2, 3, 1))
    h0t = jnp.transpose(h0.reshape(b_, G, R, P, N), (0, 1, 4, 2, 3)).reshape(b_, G, N, W)
    chan = lambda b, g, c: (b, c, g)
    cm_block0 = (SSD_D_INNER + G * N) // N
    state_spec = pl.BlockSpec((None, None, N, W), lambda b, g, c: (b, g, 0, 0))
    row_spec = pl.BlockSpec((1, W), lambda b, g, c: (0, g))
    y, ht = pl.pallas_call(
        _ssd_kernel,
        grid=(b_, G, nc),
        in_specs=[pl.BlockSpec((None, L, W), chan),
                  pl.BlockSpec((None, L, W), chan),
                  pl.BlockSpec((None, L, N), lambda b, g, c: (b, c, cm_block0 + g)),
                  pl.BlockSpec((None, None, N, L), lambda b, g, c: (b, g, 0, c)),
                  pl.BlockSpec((None, None, L, 2 * R), lambda b, g, c: (b, g, c, 0)),
                  pl.BlockSpec((None, None, None, 2 * R, L), lambda b, g, c: (b, g, c, 0, 0)),
                  state_spec, row_spec, row_spec],
        out_specs=[pl.BlockSpec((None, L, W), chan), state_spec],
        out_shape=[jax.ShapeDtypeStruct((b_, T, SSD_D_INNER), jnp.float32),
                   jax.ShapeDtypeStruct((b_, G, N, W), jnp.float32)],
        scratch_shapes=[pltpu.VMEM((N, W), jnp.float32)],
        compiler_params=pltpu.CompilerParams(
            dimension_semantics=("parallel", "parallel", "arbitrary"),
            vmem_limit_bytes=V7X_VMEM_LIMIT_BYTES),
        name="ssd_cell",
    )(xbc, zx, xbc, bmt, colpack, rowpack, h0t,
      jnp.repeat(d_skip, P).reshape(1, SSD_D_INNER), norm_g.reshape(1, SSD_D_INNER))
    h_new = jnp.transpose(ht.reshape(b_, G, N, R, P), (0, 1, 3, 4, 2)).reshape(b_, SSD_N_HEADS, P, N)
    return y, h_new


def ssd_mixer(x, conv_hist, h0, w_in, conv_w, conv_b, dt_bias, a_log, d_skip, norm_g, w_out):
    zx = matmul(x, w_in)
    xbc = conv_act(zx, SSD_D_INNER, SSD_CONV_DIM, conv_hist, conv_w, conv_b, "silu")
    dt = jax.nn.softplus(zx[..., SSD_D_INNER + SSD_CONV_DIM:] + dt_bias)
    y, h_new = ssd_cell(zx, xbc, dt, -jnp.exp(a_log), d_skip, norm_g, h0)
    return matmul(y, w_out), conv_tail(conv_hist, zx, SSD_D_INNER, SSD_CONV_DIM), h_new


def mlstm_chunked(q, k, v, i_pre, logf, c0, n0, m0):
    b_, T, H, _ = q.shape
    cl = math.gcd(T, MLSTM_CHUNK)
    nc = T // cl
    causal = jnp.tril(jnp.ones((cl, cl), bool))

    def chunk(t):
        return jnp.moveaxis(t.reshape((b_, nc, cl) + t.shape[2:]), 1, 0)

    def step(carry, inp):
        c, n, m = carry
        qc, kc, vc, ic, fc = inp
        bcum = jnp.cumsum(fc, axis=1)
        dmat = bcum[:, :, None, :] - bcum[:, None, :, :] + ic[:, None, :, :]
        dmat = jnp.where(causal[None, :, :, None], dmat, -jnp.inf)
        inter = m[:, None, :] + bcum
        m_t = jnp.maximum(inter, jnp.max(dmat, axis=2))
        w = jnp.exp(dmat - m_t[:, :, None, :])
        s = jnp.einsum('bthd,bshd->btsh', qc, kc) * w
        sc_inter = jnp.exp(inter - m_t)
        num = (jnp.einsum('btsh,bshe->bthe', s, vc)
               + sc_inter[..., None] * jnp.einsum('bthd,bhde->bthe', qc, c))
        den = jnp.sum(s, axis=2) + sc_inter * jnp.einsum('bthd,bhd->bth', qc, n)
        h = num / jnp.maximum(jnp.abs(den), jnp.exp(-m_t))[..., None]
        m_new = m_t[:, -1]
        decay_s = jnp.exp(bcum[:, -1:] - bcum + ic - m_new[:, None])
        sc_c = jnp.exp(m + bcum[:, -1] - m_new)
        c_new = sc_c[..., None, None] * c + jnp.einsum('bsh,bshd,bshe->bhde', decay_s, kc, vc)
        n_new = sc_c[..., None] * n + jnp.einsum('bsh,bshd->bhd', decay_s, kc)
        return (c_new, n_new, m_new), h

    (c, n, m), hs = lax.scan(step, (c0, n0, m0),
                             (chunk(q), chunk(k), chunk(v), chunk(i_pre), chunk(logf)))
    h = jnp.moveaxis(hs, 0, 1).reshape(b_, T, H, -1)
    return h, c, n, m


MLSTM_STEP = 256
MLSTM_NORM_LANES = 128


def _mlstm_kernel(q_ref, kt_ref, v_ref, acol_ref, grow_ref, c0_ref, n0_ref, m0_ref, g_ref,
                  h_ref, c_out_ref, n_out_ref, m_out_ref, c_ref, m_ref):
    f32, bf16 = jnp.float32, jnp.bfloat16
    step = pl.program_id(2)
    L, D = q_ref.shape

    @pl.when(step == 0)
    def _():
        c_ref[:, :D] = c0_ref[...]
        c_ref[:, D:] = n0_ref[...]
        m_ref[...] = m0_ref[...]

    q = q_ref[...]
    kt = kt_ref[...]
    a_col = acol_ref[...]
    g_row = grow_ref[...]
    m_prev = m_ref[:, 0:1]
    causal = (lax.broadcasted_iota(jnp.int32, (L, L), 0) >= lax.broadcasted_iota(jnp.int32, (L, L), 1))
    dmat = jnp.where(causal, a_col + g_row, -jnp.inf)
    inter = m_prev + a_col
    m_t = jnp.maximum(inter, jnp.max(dmat, axis=1, keepdims=True))
    s = jnp.dot(q, kt, preferred_element_type=f32) * jnp.exp(dmat - m_t)
    sc_inter = jnp.exp(inter - m_t)
    qc = jnp.dot(q, c_ref[...].astype(bf16), preferred_element_type=f32)
    num = jnp.dot(s.astype(bf16), v_ref[...], preferred_element_type=f32) + sc_inter * qc[:, :D]
    den = jnp.sum(s, axis=1, keepdims=True) + sc_inter * qc[:, D:D + 1]
    h = num / jnp.maximum(jnp.abs(den), jnp.exp(-m_t))
    mu = jnp.mean(h, axis=-1, keepdims=True)
    var = jnp.mean(jnp.square(h - mu), axis=-1, keepdims=True)
    h_ref[...] = (h - mu) * lax.rsqrt(var + LN_EPS) * g_ref[...]

    m_new = m_t[L - 1:L, :]
    total = a_col[L - 1:L, :]
    decay = jnp.exp(total + g_row - m_new)
    sc_c = jnp.exp(m_prev + total - m_new)
    ktd = (kt.astype(f32) * decay).astype(bf16)
    one_hot = jnp.where(lax.broadcasted_iota(jnp.int32, (L, MLSTM_NORM_LANES), 1) == 0, 1.0, 0.0).astype(bf16)
    v_ext = jnp.concatenate([v_ref[...], one_hot], axis=1)
    c_ref[...] = sc_c * c_ref[...] + jnp.dot(ktd, v_ext, preferred_element_type=f32)
    m_ref[...] = jnp.broadcast_to(m_new, m_ref.shape)

    @pl.when(step == pl.num_programs(2) - 1)
    def _():
        c_out_ref[...] = c_ref[:, :D]
        n_out_ref[...] = c_ref[:, D:]
        m_out_ref[...] = m_ref[...]


def mlstm_cell(q, k, v, i_pre, logf, c0, n0, m0, norm_g):
    b_, T, _ = q.shape
    H, D = MLSTM_N_HEADS, MLSTM_HEAD_DIM
    L = min(T, MLSTM_STEP)
    ns = T // L
    bf16 = jnp.bfloat16
    kt = jnp.swapaxes((k * (D ** -0.5)).astype(bf16).reshape(b_, T, H, D), 1, 3)
    kt = jnp.swapaxes(kt, 1, 2)
    bcum = jnp.cumsum(logf.reshape(b_, ns, L, H), axis=2)
    acol = jnp.moveaxis(bcum, 3, 1).reshape(b_, H, T, 1)
    grow = jnp.moveaxis(i_pre.reshape(b_, ns, L, H) - bcum, 3, 1).reshape(b_, H, ns, 1, L)
    n0e = jnp.pad(n0[..., None], ((0, 0), (0, 0), (0, 0), (0, MLSTM_NORM_LANES - 1)))
    m0e = jnp.broadcast_to(m0[:, :, None, None], (b_, H, 1, MLSTM_NORM_LANES))

    tok_spec = pl.BlockSpec((None, L, D), lambda b, h, s: (b, s, h))
    state = lambda w: pl.BlockSpec((None, None, D, w), lambda b, h, s: (b, h, 0, 0))
    m_spec = pl.BlockSpec((None, None, 1, MLSTM_NORM_LANES), lambda b, h, s: (b, h, 0, 0))
    hn, c, n, m = pl.pallas_call(
        _mlstm_kernel,
        grid=(b_, H, ns),
        in_specs=[tok_spec,
                  pl.BlockSpec((None, None, D, L), lambda b, h, s: (b, h, 0, s)),
                  tok_spec,
                  pl.BlockSpec((None, None, L, 1), lambda b, h, s: (b, h, s, 0)),
                  pl.BlockSpec((None, None, None, 1, L), lambda b, h, s: (b, h, s, 0, 0)),
                  state(D), state(MLSTM_NORM_LANES), m_spec,
                  pl.BlockSpec((1, D), lambda b, h, s: (0, h))],
        out_specs=[tok_spec, state(D), state(MLSTM_NORM_LANES), m_spec],
        out_shape=[jax.ShapeDtypeStruct((b_, T, H * D), jnp.float32),
                   jax.ShapeDtypeStruct((b_, H, D, D), jnp.float32),
                   jax.ShapeDtypeStruct((b_, H, D, MLSTM_NORM_LANES), jnp.float32),
                   jax.ShapeDtypeStruct((b_, H, 1, MLSTM_NORM_LANES), jnp.float32)],
        scratch_shapes=[pltpu.VMEM((D, D + MLSTM_NORM_LANES), jnp.float32),
                        pltpu.VMEM((1, MLSTM_NORM_LANES), jnp.float32)],
        compiler_params=pltpu.CompilerParams(
            dimension_semantics=("parallel", "parallel", "arbitrary"),
            vmem_limit_bytes=V7X_VMEM_LIMIT_BYTES),
        name="mlstm_cell",
    )(q.astype(bf16), kt, v.astype(bf16), acol, grow, c0, n0e, m0e, norm_g.reshape(1, H * D))
    return hn, c, n[..., 0], m[:, :, 0, 0]


def mlstm_mixer(x, conv_hist, c0, n0, m0, w_up, conv_w, conv_b, w_q, w_k, w_v, w_if, b_if,
                skip, norm_g, w_down):
    b_, T, _ = x.shape
    up = matmul(x, w_up)
    z = up[..., MLSTM_D_INNER:]
    xc, q, k, v = mlstm_qkv(up, conv_hist, conv_w, conv_b, w_q, w_k, w_v)
    gates = matmul(q, w_if[0]) + matmul(k, w_if[1]) + matmul(v, w_if[2]) + b_if
    i_pre, f_pre = gates[..., :MLSTM_N_HEADS], gates[..., MLSTM_N_HEADS:]
    h, c, n, m = mlstm_cell(q, k, v, i_pre, jax.nn.log_sigmoid(f_pre), c0, n0, m0, norm_g)
    h = (h + skip * xc) * jax.nn.silu(z)
    return matmul(h, w_down), conv_tail(conv_hist, up, 0, MLSTM_D_INNER), c, n, m


S5_D_STATE = S5_N_GROUPS * S5_STATE
S5_PACK = 8
S5_N_PACKS = S5_N_GROUPS // S5_PACK
S5_SCAN_ROWS = 8
S5_SCAN_LANES = 256
S5_TIME_TILE = 256
S5_SCAN_UNROLL = 4


def _cmul(ar, ai, br, bi):
    return ar * br - ai * bi, ar * bi + ai * br


def _s5_kernel(x_ref, bre_ref, bim_ref, cre_ref, cim_ref, pw_ref, h0r_ref, h0i_ref, d_ref,
               g_ref, hr_out_ref, hi_out_ref, bur_ref, bui_ref, hr_ref, hi_ref):
    tt = pl.program_id(1)
    rows = x_ref.shape[0]
    pk_in = S5_PACK * S5_GROUP
    pk_st = S5_PACK * S5_STATE

    @pl.when(tt == 0)
    def _():
        hr_ref[...] = jnp.broadcast_to(h0r_ref[...], hr_ref.shape)
        hi_ref[...] = jnp.broadcast_to(h0i_ref[...], hi_ref.shape)

    for c in range(S5_N_PACKS):
        xc = x_ref[:, c * pk_in:(c + 1) * pk_in].astype(jnp.bfloat16)
        bur_ref[:, c * pk_st:(c + 1) * pk_st] = jnp.dot(xc, bre_ref[c], preferred_element_type=jnp.float32)
        bui_ref[:, c * pk_st:(c + 1) * pk_st] = jnp.dot(xc, bim_ref[c], preferred_element_type=jnp.float32)

    def col_body(cb, carry):
        cs = pl.ds(pl.multiple_of(cb * S5_SCAN_LANES, S5_SCAN_LANES), S5_SCAN_LANES)
        stages = [(pw_ref[2 * k, :, cs], pw_ref[2 * k + 1, :, cs], 1 << k) for k in range(3)]
        lr, li = pw_ref[6, :, cs], pw_ref[7, :, cs]

        def row_body(r, h):
            hr, hi = h
            rs = pl.ds(pl.multiple_of(r * S5_SCAN_ROWS, S5_SCAN_ROWS), S5_SCAN_ROWS)
            vr, vi = bur_ref[rs, cs], bui_ref[rs, cs]
            for mr, mi, s in stages:
                pr, pi = _cmul(mr, mi, pltpu.roll(vr, s, 0), pltpu.roll(vi, s, 0))
                vr, vi = vr + pr, vi + pi
            pr, pi = _cmul(lr, li, hr, hi)
            vr, vi = vr + pr, vi + pi
            bur_ref[rs, cs] = vr
            bui_ref[rs, cs] = vi
            last = S5_SCAN_ROWS - 1
            return (jnp.broadcast_to(vr[last:, :], vr.shape), jnp.broadcast_to(vi[last:, :], vi.shape))

        hr, hi = lax.fori_loop(0, rows // S5_SCAN_ROWS, row_body, (hr_ref[:, cs], hi_ref[:, cs]),
                               unroll=min(S5_SCAN_UNROLL, rows // S5_SCAN_ROWS))
        hr_ref[:, cs] = hr
        hi_ref[:, cs] = hi
        return carry

    lax.fori_loop(0, S5_D_STATE // S5_SCAN_LANES, col_body, 0)

    for c in range(S5_N_PACKS):
        hr = bur_ref[:, c * pk_st:(c + 1) * pk_st].astype(jnp.bfloat16)
        hi = bui_ref[:, c * pk_st:(c + 1) * pk_st].astype(jnp.bfloat16)
        y = (jnp.dot(hr, cre_ref[c], preferred_element_type=jnp.float32)
             - jnp.dot(hi, cim_ref[c], preferred_element_type=jnp.float32))
        cols = slice(c * pk_in, (c + 1) * pk_in)
        g_ref[:, cols] = jax.nn.gelu(y + d_ref[:, cols] * x_ref[:, cols])

    @pl.when(tt == pl.num_programs(1) - 1)
    def _():
        hr_out_ref[...] = hr_ref[0:1, :]
        hi_out_ref[...] = hi_ref[0:1, :]


def _block_diag_packs(w):
    g, r, c = w.shape
    eye = jnp.eye(S5_PACK, dtype=w.dtype)
    wb = jnp.einsum('kgrc,gh->kgrhc', w.reshape(g // S5_PACK, S5_PACK, r, c), eye)
    return wb.reshape(g // S5_PACK, S5_PACK * r, S5_PACK * c).astype(jnp.bfloat16)


def s5_mixer(x, h0, a_re, a_im, log_dt, b_re, b_im, c_re, c_im, d_skip, w_glu_a, w_glu_b):
    b_, T, _ = x.shape
    step = jnp.exp(log_dt)[:, None]
    mag = jnp.exp(a_re * step)
    ab_re, ab_im = mag * jnp.cos(a_im * step), mag * jnp.sin(a_im * step)
    den = a_re * a_re + a_im * a_im
    nr, ni = ab_re - 1.0, ab_im
    f_re = (nr * a_re + ni * a_im) / den
    f_im = (ni * a_re - nr * a_im) / den
    bb_re = f_re[..., None] * b_re - f_im[..., None] * b_im
    bb_im = f_re[..., None] * b_im + f_im[..., None] * b_re
    bre = _block_diag_packs(jnp.swapaxes(bb_re, 1, 2))
    bim = _block_diag_packs(jnp.swapaxes(bb_im, 1, 2))
    cre = _block_diag_packs(jnp.swapaxes(c_re, 1, 2))
    cim = _block_diag_packs(jnp.swapaxes(c_im, 1, 2))
    l1 = (ab_re.reshape(-1), ab_im.reshape(-1))
    l2 = _cmul(*l1, *l1)
    l4 = _cmul(*l2, *l2)
    row = jnp.arange(S5_SCAN_ROWS)[:, None]
    pw = []
    for s, (pr, pi) in ((1, l1), (2, l2), (4, l4)):
        pw += [jnp.where(row >= s, pr[None, :], 0.0), jnp.where(row >= s, pi[None, :], 0.0)]
    acc = [l1]
    for _ in range(S5_SCAN_ROWS - 1):
        acc.append(_cmul(*acc[-1], *l1))
    pw += [jnp.stack([a[0] for a in acc]), jnp.stack([a[1] for a in acc])]
    pw = jnp.stack(pw)

    tt = min(T, S5_TIME_TILE)
    h0r = h0[..., 0].reshape(b_, 1, S5_D_STATE)
    h0i = h0[..., 1].reshape(b_, 1, S5_D_STATE)
    pk_in, pk_st = S5_PACK * S5_GROUP, S5_PACK * S5_STATE

    def const3(b, t):
        return (0, 0, 0)

    state_spec = pl.BlockSpec((None, 1, S5_D_STATE), lambda b, t: (b, 0, 0))
    g, hr, hi = pl.pallas_call(
        _s5_kernel,
        grid=(b_, T // tt),
        in_specs=[pl.BlockSpec((None, tt, D_MODEL), lambda b, t: (b, t, 0)),
                  pl.BlockSpec((S5_N_PACKS, pk_in, pk_st), const3),
                  pl.BlockSpec((S5_N_PACKS, pk_in, pk_st), const3),
                  pl.BlockSpec((S5_N_PACKS, pk_st, pk_in), const3),
                  pl.BlockSpec((S5_N_PACKS, pk_st, pk_in), const3),
                  pl.BlockSpec((8, S5_SCAN_ROWS, S5_D_STATE), const3),
                  state_spec, state_spec,
                  pl.BlockSpec((1, D_MODEL), lambda b, t: (0, 0))],
        out_specs=[pl.BlockSpec((None, tt, D_MODEL), lambda b, t: (b, t, 0)), state_spec, state_spec],
        out_shape=[jax.ShapeDtypeStruct((b_, T, D_MODEL), jnp.float32),
                   jax.ShapeDtypeStruct((b_, 1, S5_D_STATE), jnp.float32),
                   jax.ShapeDtypeStruct((b_, 1, S5_D_STATE), jnp.float32)],
        scratch_shapes=[pltpu.VMEM((tt, S5_D_STATE), jnp.float32),
                        pltpu.VMEM((tt, S5_D_STATE), jnp.float32),
                        pltpu.VMEM((S5_SCAN_ROWS, S5_D_STATE), jnp.float32),
                        pltpu.VMEM((S5_SCAN_ROWS, S5_D_STATE), jnp.float32)],
        compiler_params=pltpu.CompilerParams(
            dimension_semantics=("parallel", "arbitrary"),
            vmem_limit_bytes=V7X_VMEM_LIMIT_BYTES),
        name="s5_scan",
    )(x, bre, bim, cre, cim, pw, h0r, h0i, d_skip.reshape(1, D_MODEL))
    out = matmul(g, w_glu_a) * jax.nn.sigmoid(matmul(g, w_glu_b))
    h_new = jnp.stack([hr.reshape(b_, S5_N_GROUPS, S5_STATE), hi.reshape(b_, S5_N_GROUPS, S5_STATE)], axis=-1)
    return out, h_new


def nsa_compress(kv, w1, w2, pe):
    b_, L = kv.shape[:2]
    span = NSA_CMP_BLOCK // NSA_CMP_STRIDE
    n_str = L // NSA_CMP_STRIDE
    n_cmp = n_str - span + 1
    chunks = kv.reshape(b_, n_str, NSA_CMP_STRIDE, NSA_N_KV, NSA_HEAD_DIM)
    blocks = jnp.concatenate([chunks[:, s:s + n_cmp] for s in range(span)], axis=2) + pe[:, None, :]
    flat = jnp.moveaxis(blocks, 3, 2).reshape(b_, n_cmp, NSA_N_KV, NSA_CMP_BLOCK * NSA_HEAD_DIM)
    return jax.nn.gelu(flat @ w1) @ w2


def cmp_to_sel(imp, n_sel):
    r = NSA_SEL_BLOCK // NSA_CMP_STRIDE
    span = NSA_CMP_BLOCK // NSA_CMP_STRIDE
    pad = jnp.pad(imp, [(0, 0)] * (imp.ndim - 1) + [(span - 1, span - 1)])
    return sum(pad[..., s:s + r * (n_sel - 1) + 1:r] for s in range(r + span - 1))


def nsa_cmp_slc_block(qb, tq, kcmp, vcmp, ks_blk, vs_blk, n_top):
    scale = NSA_HEAD_DIM ** -0.5
    b_ = qb.shape[0]
    n_cmp = kcmp.shape[1]
    n_sel = ks_blk.shape[2]
    cmp_end = jnp.arange(n_cmp) * NSA_CMP_STRIDE + (NSA_CMP_BLOCK - 1)
    cmask = (cmp_end[None, :] <= tq[:, None])[None, :, None, None, :]
    s = jnp.einsum('bqhgd,bnhd->bqhgn', qb, kcmp) * scale
    p = jax.nn.softmax(jnp.where(cmask, s, NEG_INF), axis=-1)
    p = jnp.where(cmask, p, 0.0)
    o_cmp = jnp.einsum('bqhgn,bnhd->bqhgd', p, vcmp)
    imp = cmp_to_sel(jnp.sum(p, axis=3), n_sel)
    blk = jnp.arange(n_sel)[None, :]
    cur = (tq // NSA_SEL_BLOCK)[:, None]
    forced = (blk == 0) | (blk == cur) | (blk == cur - 1)
    valid = blk <= cur
    score = jnp.where(forced[None, :, None, :], FORCE_SCORE,
                      jnp.where(valid[None, :, None, :], imp, -FORCE_SCORE))
    _, idx = lax.top_k(score, n_top)
    bi = jnp.arange(b_)[:, None, None, None]
    hi = jnp.arange(NSA_N_KV)[None, None, :, None]
    kg = ks_blk[bi, hi, idx]
    vg = vs_blk[bi, hi, idx]
    kpos = idx[..., None] * NSA_SEL_BLOCK + jnp.arange(NSA_SEL_BLOCK)
    smask = (kpos <= tq[None, :, None, None, None])[:, :, :, None]
    s2 = jnp.einsum('bqhgd,bqhnsd->bqhgns', qb, kg) * scale
    s2 = jnp.where(smask, s2, NEG_INF)
    p2 = jax.nn.softmax(s2.reshape(s2.shape[:4] + (-1,)), axis=-1).reshape(s2.shape)
    o_slc = jnp.einsum('bqhgns,bqhnsd->bqhgd', p2, vg)
    return o_cmp, o_slc


def nsa_cmp_slc(q, rows, q_pos, w_cmp1, w_cmp2, cmp_pe):
    b_, L = rows.shape[:2]
    lp = -(-L // NSA_SEL_BLOCK) * NSA_SEL_BLOCK
    rows = jnp.pad(rows, ((0, 0), (0, lp - L), (0, 0), (0, 0), (0, 0)))
    kcmp = nsa_compress(rows[:, :, 0], w_cmp1[0], w_cmp2[0], cmp_pe[0])
    vcmp = nsa_compress(rows[:, :, 1], w_cmp1[1], w_cmp2[1], cmp_pe[1])
    n_sel = lp // NSA_SEL_BLOCK

    def sel_blocks(t):
        return jnp.moveaxis(t.reshape(b_, n_sel, NSA_SEL_BLOCK, NSA_N_KV, NSA_HEAD_DIM), 3, 1)

    ks_blk, vs_blk = sel_blocks(rows[:, :, 2]), sel_blocks(rows[:, :, 3])
    n_top = min(NSA_N_SELECT, n_sel)
    T = q.shape[1]
    qbs = math.gcd(T, NSA_QBLOCK)
    nqb = T // qbs
    qb = jnp.moveaxis(q.reshape((b_, nqb, qbs) + q.shape[2:]), 1, 0)
    pb = q_pos.reshape(nqb, qbs)
    o_cmp, o_slc = lax.map(
        lambda a: nsa_cmp_slc_block(a[0], a[1], kcmp, vcmp, ks_blk, vs_blk, n_top), (qb, pb))
    return (jnp.moveaxis(o_cmp, 0, 1).reshape(q.shape), jnp.moveaxis(o_slc, 0, 1).reshape(q.shape))


def window_attend(qb, kb, vb, q_pos, k_pos):
    s = jnp.einsum('bnqhgd,bnshd->bnqhgs', qb, kb) * (NSA_HEAD_DIM ** -0.5)
    diff = q_pos[:, :, None] - k_pos[:, None, :]
    mask = ((diff >= 0) & (diff < NSA_WINDOW) & (k_pos[:, None, :] >= 0))[None, :, :, None, None, :]
    p = jax.nn.softmax(jnp.where(mask, s, NEG_INF), axis=-1)
    return jnp.einsum('bnqhgs,bnshd->bnqhgd', p, vb)


def nsa_window_prompt(q, win_rows):
    b_, T = q.shape[:2]
    qbs = math.gcd(T, NSA_WBLOCK)
    nqb = T // qbs
    pad = jnp.pad(win_rows, ((0, 0), (NSA_WINDOW, 0), (0, 0), (0, 0), (0, 0)))
    idx = jnp.arange(nqb)[:, None] * qbs + jnp.arange(NSA_WINDOW + qbs)[None, :]
    kvb = pad[:, idx]
    q_pos = jnp.arange(T).reshape(nqb, qbs)
    o = window_attend(q.reshape((b_, nqb, qbs) + q.shape[2:]), kvb[:, :, :, 0], kvb[:, :, :, 1],
                      q_pos, idx - NSA_WINDOW)
    return o.reshape(q.shape)


NSA_TQ = NSA_WBLOCK
NSA_SLC_CHUNK = 512
NSA_WIN_CHUNK = 256
NSA_ROWS = NSA_GQA * NSA_TQ
NSA_SEL_SHIFT = NSA_SEL_BLOCK.bit_length() - 1
assert 1 << NSA_SEL_SHIFT == NSA_SEL_BLOCK


def _nsa_stream_softmax(q, k_ref, v_ref, c_lo, c_hi, chunk, mask_fn, m_ref, l_ref, acc_ref):
    scale = NSA_HEAD_DIM ** -0.5
    m_ref[...] = jnp.full(m_ref.shape, NEG_INF, jnp.float32)
    l_ref[...] = jnp.zeros(l_ref.shape, jnp.float32)
    acc_ref[...] = jnp.zeros(acc_ref.shape, jnp.float32)

    def body(c, carry):
        start = pl.multiple_of(c * chunk, chunk)
        k = k_ref[pl.ds(start, chunk), :]
        v = v_ref[pl.ds(start, chunk), :]
        s = lax.dot_general(q, k, (((1,), (1,)), ((), ())), preferred_element_type=jnp.float32) * scale
        s = jnp.where(mask_fn(start), s, NEG_INF)
        m_old = m_ref[...]
        m_new = jnp.maximum(m_old, jnp.max(s, axis=-1, keepdims=True))
        alpha = jnp.exp(m_old - m_new)
        p = jnp.exp(s - m_new)
        l_ref[...] = alpha * l_ref[...] + jnp.sum(p, axis=-1, keepdims=True)
        acc_ref[...] = alpha * acc_ref[...] + jnp.dot(p.astype(jnp.bfloat16), v,
                                                      preferred_element_type=jnp.float32)
        m_ref[...] = m_new
        return carry

    lax.fori_loop(c_lo, c_hi, body, 0)
    return acc_ref[...] / l_ref[...]


def _nsa_prompt_kernel(q_ref, kc_ref, vc_ref, ks_ref, vs_ref, kw_ref, vw_ref,
                       ocmp_ref, oslc_ref, owin_ref, selexp_ref, m_ref, l_ref, acc_ref,
                       *, n_cmp, n_top):
    f32, bf16 = jnp.float32, jnp.bfloat16
    i = pl.program_id(2)
    t0 = i * NSA_TQ
    n_cp = kc_ref.shape[0]
    t_len = selexp_ref.shape[1]
    n_sel = t_len // NSA_SEL_BLOCK
    nt = (((1,), (1,)), ((), ()))
    q = jnp.concatenate([q_ref[:, g * NSA_HEAD_DIM:(g + 1) * NSA_HEAD_DIM] for g in range(NSA_GQA)], axis=0)
    tq = t0 + (lax.broadcasted_iota(jnp.int32, (NSA_ROWS, 1), 0) & (NSA_TQ - 1))

    def store_heads(o_ref, o):
        for g in range(NSA_GQA):
            o_ref[:, g * NSA_HEAD_DIM:(g + 1) * NSA_HEAD_DIM] = o[g * NSA_TQ:(g + 1) * NSA_TQ]

    s = lax.dot_general(q, kc_ref[...], nt, preferred_element_type=f32) * (NSA_HEAD_DIM ** -0.5)
    n_idx = lax.broadcasted_iota(jnp.int32, (NSA_ROWS, n_cp), 1)
    cmask = (n_idx * NSA_CMP_STRIDE + (NSA_CMP_BLOCK - 1) <= tq) & (n_idx < n_cmp)
    s = jnp.where(cmask, s, NEG_INF)
    e = jnp.exp(s - jnp.max(s, axis=-1, keepdims=True))
    p = jnp.where(cmask, e / jnp.sum(e, axis=-1, keepdims=True), 0.0)
    store_heads(ocmp_ref, jnp.dot(p.astype(bf16), vc_ref[...], preferred_element_type=f32))

    psum = p[0:NSA_TQ]
    for g in range(1, NSA_GQA):
        psum = psum + p[g * NSA_TQ:(g + 1) * NSA_TQ]
    jn = lax.broadcasted_iota(jnp.int32, (n_sel, n_cp), 0)
    nn = lax.broadcasted_iota(jnp.int32, (n_sel, n_cp), 1)
    r = NSA_SEL_BLOCK // NSA_CMP_STRIDE
    span = NSA_CMP_BLOCK // NSA_CMP_STRIDE
    pool = jnp.where((nn >= r * jn - (span - 1)) & (nn <= r * jn + (r - 1)), 1.0, 0.0).astype(bf16)
    imp = jnp.zeros((n_sel, NSA_TQ), f32)
    rest = psum
    for _ in range(3):
        part = rest.astype(bf16)
        imp = imp + lax.dot_general(pool, part, nt, preferred_element_type=f32)
        rest = rest - part.astype(f32)

    jidx = lax.broadcasted_iota(jnp.int32, (n_sel, NSA_TQ), 0)
    cur = (t0 + lax.broadcasted_iota(jnp.int32, (n_sel, NSA_TQ), 1)) >> NSA_SEL_SHIFT
    forced = (jidx == 0) | (jidx == cur) | (jidx == cur - 1)
    score = jnp.where(forced, FORCE_SCORE, jnp.where(jidx <= cur, imp, -FORCE_SCORE))
    rank = jnp.zeros((n_sel, NSA_TQ), jnp.int32)
    for k in range(n_sel):
        row = score[k:k + 1, :]
        before = (row > score) | ((row == score) & (k < jidx))
        rank = rank + jnp.where(before, 1, 0)
    sel_t = jnp.where(rank < n_top, 1.0, 0.0).astype(bf16)
    eye = jnp.where(lax.broadcasted_iota(jnp.int32, (NSA_TQ, NSA_TQ), 0)
                    == lax.broadcasted_iota(jnp.int32, (NSA_TQ, NSA_TQ), 1), 1.0, 0.0).astype(bf16)
    sel = lax.dot_general(eye, sel_t, nt, preferred_element_type=f32).astype(bf16)
    expand = jnp.where((lax.broadcasted_iota(jnp.int32, (n_sel, t_len), 1) >> NSA_SEL_SHIFT)
                       == lax.broadcasted_iota(jnp.int32, (n_sel, t_len), 0), 1.0, 0.0).astype(bf16)
    selexp_ref[...] = jnp.dot(sel, expand, preferred_element_type=f32)

    def slc_mask(start):
        kpos = start + lax.broadcasted_iota(jnp.int32, (NSA_ROWS, NSA_SLC_CHUNK), 1)
        chosen = jnp.concatenate([selexp_ref[:, pl.ds(start, NSA_SLC_CHUNK)]] * NSA_GQA, axis=0)
        return (chosen > 0.5) & (kpos <= tq)

    store_heads(oslc_ref, _nsa_stream_softmax(
        q, ks_ref, vs_ref, 0, (t0 + NSA_TQ - 1) // NSA_SLC_CHUNK + 1, NSA_SLC_CHUNK, slc_mask,
        m_ref, l_ref, acc_ref))

    def win_mask(start):
        diff = tq - (start + lax.broadcasted_iota(jnp.int32, (NSA_ROWS, NSA_WIN_CHUNK), 1))
        return (diff >= 0) & (diff < NSA_WINDOW)

    store_heads(owin_ref, _nsa_stream_softmax(
        q, kw_ref, vw_ref, jnp.maximum(i - NSA_WINDOW // NSA_WIN_CHUNK, 0), i + 1, NSA_WIN_CHUNK,
        win_mask, m_ref, l_ref, acc_ref))


def nsa_prompt_attention(q, kv, kcmp, vcmp):
    b_, T, _ = q.shape
    assert T % NSA_SLC_CHUNK == 0 and (T // NSA_SEL_BLOCK) % 8 == 0
    n_cmp = kcmp.shape[1]
    n_cp = T // NSA_CMP_STRIDE
    n_sel = T // NSA_SEL_BLOCK
    bf16 = jnp.bfloat16

    def cmp_layout(t):
        return jnp.pad(jnp.moveaxis(t, 2, 1), ((0, 0), (0, 0), (0, n_cp - n_cmp), (0, 0))).astype(bf16)

    kvb = kv.astype(bf16)
    width = NSA_GQA * NSA_HEAD_DIM
    q_spec = pl.BlockSpec((None, NSA_TQ, width), lambda b, h, i: (b, i, h))
    cmp_spec = pl.BlockSpec((None, None, n_cp, NSA_HEAD_DIM), lambda b, h, i: (b, h, 0, 0))

    def kv_spec(comp):
        return pl.BlockSpec((None, T, NSA_HEAD_DIM), lambda b, h, i: (b, 0, comp * NSA_N_KV + h))

    out = jax.ShapeDtypeStruct((b_, T, NSA_N_HEADS * NSA_HEAD_DIM), jnp.float32)
    return pl.pallas_call(
        functools.partial(_nsa_prompt_kernel, n_cmp=n_cmp, n_top=min(NSA_N_SELECT, n_sel)),
        grid=(b_, NSA_N_KV, T // NSA_TQ),
        in_specs=[q_spec, cmp_spec, cmp_spec, kv_spec(2), kv_spec(3), kv_spec(4), kv_spec(5)],
        out_specs=[q_spec, q_spec, q_spec],
        out_shape=[out, out, out],
        scratch_shapes=[pltpu.VMEM((NSA_TQ, T), jnp.float32),
                        pltpu.VMEM((NSA_ROWS, 1), jnp.float32),
                        pltpu.VMEM((NSA_ROWS, 1), jnp.float32),
                        pltpu.VMEM((NSA_ROWS, NSA_HEAD_DIM), jnp.float32)],
        compiler_params=pltpu.CompilerParams(
            dimension_semantics=("parallel", "parallel", "arbitrary"),
            vmem_limit_bytes=V7X_VMEM_LIMIT_BYTES),
        name="nsa_prompt",
    )(q.astype(bf16), cmp_layout(kcmp), cmp_layout(vcmp), kvb, kvb, kvb, kvb)


def _nsa_stream_softmax_t(q_t, k_ref, vt_ref, c_lo, c_hi, chunk, mask_fn, m_ref, l_ref, acc_ref):
    scale = NSA_HEAD_DIM ** -0.5
    m_ref[...] = jnp.full(m_ref.shape, NEG_INF, jnp.float32)
    l_ref[...] = jnp.zeros(l_ref.shape, jnp.float32)
    acc_ref[...] = jnp.zeros(acc_ref.shape, jnp.float32)

    def body(c, carry):
        start = pl.multiple_of(c * chunk, chunk)
        s = jnp.dot(k_ref[pl.ds(start, chunk), :], q_t, preferred_element_type=jnp.float32) * scale
        s = jnp.where(mask_fn(start), s, NEG_INF)
        m_old = m_ref[...]
        m_new = jnp.maximum(m_old, jnp.max(s, axis=0, keepdims=True))
        alpha = jnp.exp(m_old - m_new)
        p = jnp.exp(s - m_new)
        l_ref[...] = alpha * l_ref[...] + jnp.sum(p, axis=0, keepdims=True)
        acc_ref[...] = alpha * acc_ref[...] + jnp.dot(vt_ref[:, pl.ds(start, chunk)], p.astype(jnp.bfloat16),
                                                      preferred_element_type=jnp.float32)
        m_ref[...] = m_new
        return carry

    lax.fori_loop(c_lo, c_hi, body, 0)
    return acc_ref[...] / l_ref[...]


def _nsa_prompt_t_kernel(qt_ref, kc_ref, vct_ref, ks_ref, vst_ref, kw_ref, vwt_ref,
                         ocmp_ref, oslc_ref, owin_ref, sel_ref, m_ref, l_ref, acc_ref,
                         *, n_cmp, n_top):
    f32, bf16 = jnp.float32, jnp.bfloat16
    i = pl.program_id(2)
    t0 = i * NSA_TQ
    n_cp = kc_ref.shape[0]
    n_sel = sel_ref.shape[0]
    q_t = qt_ref[...]
    tq = t0 + (lax.broadcasted_iota(jnp.int32, (1, NSA_ROWS), 1) & (NSA_TQ - 1))

    def store_heads(o_ref, o_t):
        for g in range(NSA_GQA):
            o_ref[:, g * NSA_HEAD_DIM:(g + 1) * NSA_HEAD_DIM] = o_t[:, g * NSA_TQ:(g + 1) * NSA_TQ].T

    s = jnp.dot(kc_ref[...], q_t, preferred_element_type=f32) * (NSA_HEAD_DIM ** -0.5)
    n_idx = lax.broadcasted_iota(jnp.int32, (n_cp, NSA_ROWS), 0)
    cmask = (n_idx * NSA_CMP_STRIDE + (NSA_CMP_BLOCK - 1) <= tq) & (n_idx < n_cmp)
    s = jnp.where(cmask, s, NEG_INF)
    e = jnp.exp(s - jnp.max(s, axis=0, keepdims=True))
    p = jnp.where(cmask, e / jnp.sum(e, axis=0, keepdims=True), 0.0)
    store_heads(ocmp_ref, jnp.dot(vct_ref[...], p.astype(bf16), preferred_element_type=f32))

    psum = p[:, 0:NSA_TQ]
    for g in range(1, NSA_GQA):
        psum = psum + p[:, g * NSA_TQ:(g + 1) * NSA_TQ]
    jn = lax.broadcasted_iota(jnp.int32, (n_sel, n_cp), 0)
    nn = lax.broadcasted_iota(jnp.int32, (n_sel, n_cp), 1)
    r = NSA_SEL_BLOCK // NSA_CMP_STRIDE
    span = NSA_CMP_BLOCK // NSA_CMP_STRIDE
    pool = jnp.where((nn >= r * jn - (span - 1)) & (nn <= r * jn + (r - 1)), 1.0, 0.0).astype(bf16)
    imp = jnp.zeros((n_sel, NSA_TQ), f32)
    rest = psum
    for _ in range(3):
        part = rest.astype(bf16)
        imp = imp + jnp.dot(pool, part, preferred_element_type=f32)
        rest = rest - part.astype(f32)

    jidx = lax.broadcasted_iota(jnp.int32, (n_sel, NSA_TQ), 0)
    cur = (t0 + lax.broadcasted_iota(jnp.int32, (n_sel, NSA_TQ), 1)) >> NSA_SEL_SHIFT
    forced = (jidx == 0) | (jidx == cur) | (jidx == cur - 1)
    score = jnp.where(forced, FORCE_SCORE, jnp.where(jidx <= cur, imp, -FORCE_SCORE))
    rank = jnp.zeros((n_sel, NSA_TQ), jnp.int32)
    for k in range(n_sel):
        row = score[k:k + 1, :]
        before = (row > score) | ((row == score) & (k < jidx))
        rank = rank + jnp.where(before, 1, 0)
    sel_ref[...] = jnp.where(rank < n_top, 1.0, 0.0)

    def slc_mask(start):
        kpos = start + lax.broadcasted_iota(jnp.int32, (NSA_SLC_CHUNK, NSA_ROWS), 0)
        first = start >> NSA_SEL_SHIFT
        chosen = jnp.concatenate(
            [jnp.broadcast_to(sel_ref[pl.ds(first + j, 1), :], (NSA_SEL_BLOCK, NSA_TQ))
             for j in range(NSA_SLC_CHUNK // NSA_SEL_BLOCK)], axis=0)
        chosen = jnp.concatenate([chosen] * NSA_GQA, axis=1)
        return (chosen > 0.5) & (kpos <= tq)

    store_heads(oslc_ref, _nsa_stream_softmax_t(
        q_t, ks_ref, vst_ref, 0, (t0 + NSA_TQ - 1) // NSA_SLC_CHUNK + 1, NSA_SLC_CHUNK, slc_mask,
        m_ref, l_ref, acc_ref))

    def win_mask(start):
        diff = tq - (start + lax.broadcasted_iota(jnp.int32, (NSA_WIN_CHUNK, NSA_ROWS), 0))
        return (diff >= 0) & (diff < NSA_WINDOW)

    store_heads(owin_ref, _nsa_stream_softmax_t(
        q_t, kw_ref, vwt_ref, jnp.maximum(t0 - NSA_WINDOW, 0) // NSA_WIN_CHUNK,
        (t0 + NSA_TQ - 1) // NSA_WIN_CHUNK + 1, NSA_WIN_CHUNK, win_mask, m_ref, l_ref, acc_ref))


def nsa_prompt_attention_t(q, kv, kcmp, vcmp):
    b_, T, _ = q.shape
    assert T % NSA_SLC_CHUNK == 0 and (T // NSA_SEL_BLOCK) % 8 == 0
    n_cmp = kcmp.shape[1]
    n_cp = T // NSA_CMP_STRIDE
    n_sel = T // NSA_SEL_BLOCK
    n_tiles = T // NSA_TQ
    bf16 = jnp.bfloat16
    pad_c = ((0, 0), (0, n_cp - n_cmp), (0, 0), (0, 0))
    kc = jnp.transpose(jnp.pad(kcmp, pad_c).astype(bf16), (0, 2, 1, 3))
    vct = jnp.transpose(jnp.pad(vcmp, pad_c).astype(bf16), (0, 2, 3, 1))
    kvb = kv.astype(bf16)
    kv6 = kvb.reshape(b_, T, 6, NSA_N_KV, NSA_HEAD_DIM)
    keys = lambda comp: pl.BlockSpec((None, T, NSA_HEAD_DIM), lambda b, h, i: (b, 0, comp * NSA_N_KV + h))
    vals_t = lambda comp: jnp.transpose(kv6[:, :, comp], (0, 2, 3, 1))
    q_t = q.astype(bf16).reshape(b_, n_tiles, NSA_TQ, NSA_N_KV, NSA_GQA, NSA_HEAD_DIM)
    q_t = jnp.transpose(q_t, (0, 3, 1, 5, 4, 2)).reshape(b_, NSA_N_KV, n_tiles, NSA_HEAD_DIM, NSA_ROWS)

    per_head = lambda r, c: pl.BlockSpec((None, None, r, c), lambda b, h, i: (b, h, 0, 0))
    o_spec = pl.BlockSpec((None, NSA_TQ, NSA_GQA * NSA_HEAD_DIM), lambda b, h, i: (b, i, h))
    out = jax.ShapeDtypeStruct((b_, T, NSA_N_HEADS * NSA_HEAD_DIM), jnp.float32)
    return pl.pallas_call(
        functools.partial(_nsa_prompt_t_kernel, n_cmp=n_cmp, n_top=min(NSA_N_SELECT, n_sel)),
        grid=(b_, NSA_N_KV, n_tiles),
        in_specs=[pl.BlockSpec((None, None, None, NSA_HEAD_DIM, NSA_ROWS), lambda b, h, i: (b, h, i, 0, 0)),
                  per_head(n_cp, NSA_HEAD_DIM), per_head(NSA_HEAD_DIM, n_cp),
                  keys(2), per_head(NSA_HEAD_DIM, T), keys(4), per_head(NSA_HEAD_DIM, T)],
        out_specs=[o_spec, o_spec, o_spec],
        out_shape=[out, out, out],
        scratch_shapes=[pltpu.VMEM((n_sel, NSA_TQ), jnp.float32),
                        pltpu.VMEM((1, NSA_ROWS), jnp.float32),
                        pltpu.VMEM((1, NSA_ROWS), jnp.float32),
                        pltpu.VMEM((NSA_HEAD_DIM, NSA_ROWS), jnp.float32)],
        compiler_params=pltpu.CompilerParams(
            dimension_semantics=("parallel", "parallel", "arbitrary"),
            vmem_limit_bytes=V7X_VMEM_LIMIT_BYTES),
        name="nsa_prompt",
    )(q_t, kc, vct, kvb, vals_t(3), kvb, vals_t(5))


def nsa_prompt_mixer(x, w_q, w_kv, w_gate, b_gate, w_cmp1, w_cmp2, cmp_pe, w_out):
    b_, T, _ = x.shape
    q = matmul(x, w_q)
    kv = matmul(x, w_kv)
    kv6 = kv.reshape(b_, T, 6, NSA_N_KV, NSA_HEAD_DIM)
    kcmp = nsa_compress(kv6[:, :, 0], w_cmp1[0], w_cmp2[0], cmp_pe[0])
    vcmp = nsa_compress(kv6[:, :, 1], w_cmp1[1], w_cmp2[1], cmp_pe[1])
    o_cmp, o_slc, o_win = nsa_prompt_attention_t(q, kv, kcmp, vcmp)
    gate = jax.nn.sigmoid(matmul(x, w_gate) + b_gate).reshape(b_, T, NSA_N_HEADS, 3)

    def heads(t):
        return t.reshape(b_, T, NSA_N_HEADS, NSA_HEAD_DIM)

    o = gate[..., 0:1] * heads(o_cmp) + gate[..., 1:2] * heads(o_slc) + gate[..., 2:3] * heads(o_win)
    y = matmul(o.reshape(b_, T, NSA_N_HEADS * NSA_HEAD_DIM), w_out)
    return y, kv6[:, :, :4], kv6[:, :, 4:6][:, -min(NSA_WINDOW, T):]


NSA_ROW_SLABS = 4 * NSA_N_KV
NSA_HALF_SLABS = NSA_ROW_SLABS // 2
NSA_KVW = NSA_N_KV * NSA_HEAD_DIM
NSA_CMP_PAGES = 8
NSA_SLC_PAGES = 4


def _round_up(n, m):
    return -(-n // m) * m


def _log2(n):
    assert n > 0 and n & (n - 1) == 0
    return n.bit_length() - 1


def _nsa_compress_kernel(pt_ref, *refs):
    del pt_ref
    pages = refs[:NSA_CMP_PAGES]
    w1_ref, pe_ref, a_ref, b_ref = refs[NSA_CMP_PAGES:]
    page_rows = pages[0].shape[0]
    per_page = page_rows // NSA_CMP_STRIDE
    rows = NSA_CMP_PAGES * NSA_N_KV * per_page
    half = NSA_CMP_STRIDE * NSA_HEAD_DIM
    slabs = [jnp.swapaxes(pg[...], 0, 1) for pg in pages]
    for comp in range(2):
        acc_a = jnp.zeros((rows, NSA_HEAD_DIM), jnp.float32)
        acc_b = jnp.zeros((rows, NSA_HEAD_DIM), jnp.float32)
        by_row = [jnp.swapaxes(s[comp * NSA_N_KV + h].reshape(per_page, NSA_CMP_STRIDE, NSA_HEAD_DIM), 0, 1)
                  for s in slabs for h in range(NSA_N_KV)]
        for j0 in range(0, NSA_CMP_STRIDE, 2):
            xa, xb = [], []
            for j in (j0, j0 + 1):
                x = jnp.concatenate([t[j] for t in by_row], axis=0)
                xa.append((x + pe_ref[comp, j:j + 1, :]).astype(jnp.bfloat16))
                xb.append((x + pe_ref[comp, NSA_CMP_STRIDE + j:NSA_CMP_STRIDE + j + 1, :]).astype(jnp.bfloat16))
            lo = j0 * NSA_HEAD_DIM
            acc_a = acc_a + jnp.dot(jnp.concatenate(xa, axis=1), w1_ref[comp, lo:lo + 2 * NSA_HEAD_DIM, :],
                                    preferred_element_type=jnp.float32)
            acc_b = acc_b + jnp.dot(jnp.concatenate(xb, axis=1),
                                    w1_ref[comp, half + lo:half + lo + 2 * NSA_HEAD_DIM, :],
                                    preferred_element_type=jnp.float32)
        shape = (NSA_CMP_PAGES, NSA_N_KV, per_page, NSA_HEAD_DIM)
        a_ref[comp] = acc_a.reshape(shape)
        b_ref[comp] = acc_b.reshape(shape)


def nsa_decode_compress(cache, page_ids, new_rows, w_cmp1, w_cmp2, cmp_pe):
    b_, n_pages = page_ids.shape
    page = cache.shape[1]
    T = new_rows.shape[1]
    pos0 = n_pages * page
    lp = _round_up(pos0 + T, NSA_SEL_BLOCK)
    n_cmp = lp // NSA_CMP_STRIDE - (NSA_CMP_BLOCK // NSA_CMP_STRIDE - 1)
    per_page = page // NSA_CMP_STRIDE
    assert n_pages % NSA_CMP_PAGES == 0 and NSA_CMP_BLOCK == 2 * NSA_CMP_STRIDE
    w1 = w_cmp1.astype(jnp.bfloat16)

    def page_spec(k):
        return pl.BlockSpec((None, page, NSA_HALF_SLABS, NSA_HEAD_DIM),
                            lambda b, s, pt: (pt[b, NSA_CMP_PAGES * s + k], 0, 0, 0))

    ab_shape = jax.ShapeDtypeStruct((b_, 2, n_pages, NSA_N_KV, per_page, NSA_HEAD_DIM), jnp.float32)
    ab_spec = pl.BlockSpec((None, 2, NSA_CMP_PAGES, NSA_N_KV, per_page, NSA_HEAD_DIM),
                           lambda b, s, pt: (b, 0, s, 0, 0, 0))
    part_a, part_b = pl.pallas_call(
        _nsa_compress_kernel,
        grid_spec=pltpu.PrefetchScalarGridSpec(
            num_scalar_prefetch=1,
            grid=(b_, n_pages // NSA_CMP_PAGES),
            in_specs=[page_spec(k) for k in range(NSA_CMP_PAGES)]
            + [pl.BlockSpec(w1.shape, lambda b, s, pt: (0, 0, 0)),
               pl.BlockSpec(cmp_pe.shape, lambda b, s, pt: (0, 0, 0))],
            out_specs=[ab_spec, ab_spec]),
        out_shape=[ab_shape, ab_shape],
        compiler_params=pltpu.CompilerParams(
            dimension_semantics=("parallel", "arbitrary"),
            vmem_limit_bytes=V7X_VMEM_LIMIT_BYTES),
        name="nsa_compress_pages",
    )(page_ids, *([cache] * NSA_CMP_PAGES), w1, cmp_pe)

    def strides(t):
        return jnp.transpose(t, (0, 1, 3, 2, 4, 5)).reshape(b_, 2, NSA_N_KV, n_pages * per_page, NSA_HEAD_DIM)

    n_tail = (lp - pos0) // NSA_CMP_STRIDE
    tail = jnp.pad(new_rows, ((0, 0), (0, lp - pos0 - T), (0, 0), (0, 0), (0, 0)))
    tail = jnp.transpose(tail.reshape(b_, n_tail, NSA_CMP_STRIDE, 2, NSA_N_KV, NSA_HEAD_DIM), (0, 3, 4, 1, 2, 5))
    w1s = w_cmp1.reshape(2, 2, NSA_CMP_STRIDE, NSA_HEAD_DIM, -1)
    pes = cmp_pe.reshape(2, 2, NSA_CMP_STRIDE, NSA_HEAD_DIM)
    tail_a = jnp.einsum('bchsjd,cjdk->bchsk', tail + pes[None, :, 0, None, None], w1s[:, 0])
    tail_b = jnp.einsum('bchsjd,cjdk->bchsk', tail + pes[None, :, 1, None, None], w1s[:, 1])
    full_a = jnp.concatenate([strides(part_a), tail_a], axis=3)
    full_b = jnp.concatenate([strides(part_b), tail_b], axis=3)
    hidden = jax.nn.gelu(full_a[:, :, :, :n_cmp] + full_b[:, :, :, 1:n_cmp + 1])
    out = jnp.einsum('bchnk,ckd->bcnhd', hidden, w_cmp2).reshape(b_, 2, n_cmp, NSA_KVW)
    return out[:, 0], out[:, 1]


def _nsa_softmax_rows(s, mask):
    s = jnp.where(mask, s, NEG_INF)
    e = jnp.exp(s - jnp.max(s, axis=-1, keepdims=True))
    return e / jnp.sum(e, axis=-1, keepdims=True)


def _nsa_decode_select_kernel(q_ref, kc_ref, vc_ref, wk_ref, wv_ref, ocmp_ref, owin_ref, sel_ref,
                              score_ref, rank_ref, *, n_cmp, n_sel, n_top, n_win, w_buf, pos0, t_new):
    f32, bf16 = jnp.float32, jnp.bfloat16
    nt = (((1,), (1,)), ((), ()))
    scale = NSA_HEAD_DIM ** -0.5
    rows = q_ref.shape[0]
    per_head = NSA_GQA * t_new
    q = q_ref[...]
    tq = pos0 + (lax.broadcasted_iota(jnp.int32, (rows, 1), 0) & (t_new - 1))

    def heads_out(o_ref, p, v_ref):
        for h in range(NSA_N_KV):
            o_ref[h * per_head:(h + 1) * per_head, :] = jnp.dot(
                p[h * per_head:(h + 1) * per_head].astype(bf16),
                v_ref[:, h * NSA_HEAD_DIM:(h + 1) * NSA_HEAD_DIM], preferred_element_type=f32)

    n_cp = kc_ref.shape[0]
    s = lax.dot_general(q, kc_ref[...], nt, preferred_element_type=f32) * scale
    n_idx = lax.broadcasted_iota(jnp.int32, (rows, n_cp), 1)
    cmask = (n_idx * NSA_CMP_STRIDE + (NSA_CMP_BLOCK - 1) <= tq) & (n_idx < n_cmp)
    p = jnp.where(cmask, _nsa_softmax_rows(s, cmask), 0.0)
    heads_out(ocmp_ref, p, vc_ref)

    psum = jnp.concatenate(
        [sum(p[h * per_head + g * t_new:h * per_head + (g + 1) * t_new] for g in range(NSA_GQA))
         for h in range(NSA_N_KV)], axis=0)
    n_sp = sel_ref.shape[1]
    cols = NSA_N_KV * t_new
    jn = lax.broadcasted_iota(jnp.int32, (n_sp, n_cp), 0)
    nn = lax.broadcasted_iota(jnp.int32, (n_sp, n_cp), 1)
    r = NSA_SEL_BLOCK // NSA_CMP_STRIDE
    span = NSA_CMP_BLOCK // NSA_CMP_STRIDE
    pool = jnp.where((nn >= r * jn - (span - 1)) & (nn <= r * jn + (r - 1)), 1.0, 0.0).astype(bf16)
    imp = jnp.zeros((n_sp, cols), f32)
    rest = psum
    for _ in range(3):
        part = rest.astype(bf16)
        imp = imp + lax.dot_general(pool, part, nt, preferred_element_type=f32)
        rest = rest - part.astype(f32)
    jidx = lax.broadcasted_iota(jnp.int32, (n_sp, cols), 0)
    cur = (pos0 + (lax.broadcasted_iota(jnp.int32, (n_sp, cols), 1) & (t_new - 1))) >> NSA_SEL_SHIFT
    forced = (jidx == 0) | (jidx == cur) | (jidx == cur - 1)
    score = jnp.where(forced, FORCE_SCORE, jnp.where(jidx <= cur, imp, -FORCE_SCORE))
    score_ref[...] = jnp.where(jidx < n_sel, score, -2.0 * FORCE_SCORE)
    rank_ref[...] = jnp.zeros(rank_ref.shape, jnp.int32)

    def rank_body(k, carry):
        row = score_ref[pl.ds(k, 1), :]
        sc = score_ref[...]
        before = (row > sc) | ((row == sc) & (k < jidx))
        rank_ref[...] = rank_ref[...] + jnp.where(before, 1, 0)
        return carry

    lax.fori_loop(0, n_sel, rank_body, 0)
    sel_t = jnp.where((rank_ref[...] < n_top) & (jidx < n_sel), 1.0, 0.0).astype(bf16)
    ri = lax.broadcasted_iota(jnp.int32, (rows, cols), 0)
    ci = lax.broadcasted_iota(jnp.int32, (rows, cols), 1)
    same = (((ri >> _log2(per_head)) == (ci >> _log2(t_new)))
            & ((ri & (t_new - 1)) == (ci & (t_new - 1))))
    spread = jnp.where(same, 1.0, 0.0).astype(bf16)
    sel_ref[...] = lax.dot_general(spread, sel_t, nt, preferred_element_type=f32).astype(bf16)

    s = lax.dot_general(q, wk_ref[...], nt, preferred_element_type=f32) * scale
    kidx = lax.broadcasted_iota(jnp.int32, (rows, wk_ref.shape[0]), 1)
    diff = tq - (pos0 - w_buf + kidx)
    wmask = (diff >= 0) & (diff < NSA_WINDOW) & (kidx < n_win) & (pos0 - w_buf + kidx >= 0)
    heads_out(owin_ref, _nsa_softmax_rows(s, wmask), wv_ref)


def _nsa_decode_slc_kernel(pt_ref, q_ref, sel_ref, new_ref, *refs, pos0, t_new):
    del pt_ref
    pages = refs[:NSA_SLC_PAGES]
    o_ref, m_ref, l_ref, acc_ref = refs[NSA_SLC_PAGES:]
    f32, bf16 = jnp.float32, jnp.bfloat16
    nt = (((1,), (1,)), ((), ()))
    step = pl.program_id(1)
    rows = q_ref.shape[0]
    page = pages[0].shape[0]
    per_head = NSA_GQA * t_new
    n_sp = sel_ref.shape[1]
    q = q_ref[...]
    tq = pos0 + (lax.broadcasted_iota(jnp.int32, (rows, 1), 0) & (t_new - 1))
    row_head = lax.broadcasted_iota(jnp.int32, (rows, page), 0) >> _log2(per_head)
    lane = lax.broadcasted_iota(jnp.int32, (rows, page), 1)
    blocks_per_page = page // NSA_SEL_BLOCK

    def attend(pg_ref, page_index):
        slabs = jnp.swapaxes(pg_ref[...], 0, 1)
        slab = lambda c: slabs[c].astype(bf16)
        kp = jnp.concatenate([slab(h) for h in range(NSA_N_KV)], axis=1)
        v_heads = jnp.concatenate([slab(NSA_N_KV + h) for h in range(NSA_N_KV)], axis=0)
        s = lax.dot_general(q, kp, nt, preferred_element_type=f32) * (NSA_HEAD_DIM ** -0.5)
        jrow = lax.broadcasted_iota(jnp.int32, (n_sp, page), 0)
        jcol = page_index * blocks_per_page + (lax.broadcasted_iota(jnp.int32, (n_sp, page), 1) >> NSA_SEL_SHIFT)
        expand = jnp.where(jrow == jcol, 1.0, 0.0).astype(bf16)
        chosen = jnp.dot(sel_ref[...], expand, preferred_element_type=f32) > 0.5
        s = jnp.where(chosen & (page_index * page + lane <= tq), s, NEG_INF)
        m_old = m_ref[...]
        m_new = jnp.maximum(m_old, jnp.max(s, axis=-1, keepdims=True))
        alpha = jnp.exp(m_old - m_new)
        p = jnp.exp(s - m_new)
        l_ref[...] = alpha * l_ref[...] + jnp.sum(p, axis=-1, keepdims=True)
        p_heads = jnp.concatenate([jnp.where(row_head == h, p, 0.0) for h in range(NSA_N_KV)], axis=1).astype(bf16)
        acc_ref[...] = alpha * acc_ref[...] + jnp.dot(p_heads, v_heads, preferred_element_type=f32)
        m_ref[...] = m_new

    @pl.when(step == 0)
    def _():
        m_ref[...] = jnp.full(m_ref.shape, NEG_INF, f32)
        l_ref[...] = jnp.zeros(l_ref.shape, f32)
        acc_ref[...] = jnp.zeros(acc_ref.shape, f32)
        attend(new_ref, pos0 // page)

    for k, pg in enumerate(pages):
        attend(pg, step * NSA_SLC_PAGES + k)

    @pl.when(step == pl.num_programs(1) - 1)
    def _():
        o_ref[...] = acc_ref[...] / l_ref[...]


def nsa_decode_attention(q, kv, cache, page_ids, win_buf, w_cmp1, w_cmp2, cmp_pe):
    b_, T, _ = q.shape
    n_pages = page_ids.shape[1]
    page = cache.shape[1]
    pos0 = n_pages * page
    w_buf = win_buf.shape[1]
    assert T & (T - 1) == 0 and T <= NSA_SEL_BLOCK and pos0 % NSA_SEL_BLOCK == 0 and page % NSA_SEL_BLOCK == 0
    assert n_pages % NSA_SLC_PAGES == 0
    bf16 = jnp.bfloat16
    lp = _round_up(pos0 + T, NSA_SEL_BLOCK)
    n_sel = lp // NSA_SEL_BLOCK
    n_sp = _round_up(n_sel, 128)
    kv6 = kv.reshape(b_, T, 6, NSA_N_KV, NSA_HEAD_DIM)
    kc, vc = nsa_decode_compress(cache, page_ids, kv6[:, :, 0:2], w_cmp1, w_cmp2, cmp_pe)
    n_cmp = kc.shape[1]
    n_cp = _round_up(n_cmp, 128)
    pad_c = ((0, 0), (0, n_cp - n_cmp), (0, 0))
    kc, vc = jnp.pad(kc, pad_c).astype(bf16), jnp.pad(vc, pad_c).astype(bf16)

    rows = NSA_N_HEADS * T
    q5 = jnp.transpose(q.reshape(b_, T, NSA_N_KV, NSA_GQA, NSA_HEAD_DIM), (0, 2, 3, 1, 4))
    q_blk = jnp.einsum('bhgtd,hk->bhgtkd', q5, jnp.eye(NSA_N_KV, dtype=q.dtype))
    q_blk = q_blk.reshape(b_, rows, NSA_KVW).astype(bf16)

    n_win = w_buf + T
    n_wp = _round_up(n_win, 128)
    wk = jnp.concatenate([win_buf, kv6[:, :, 4:6]], axis=1)
    wk = jnp.pad(wk, ((0, 0), (0, n_wp - n_win), (0, 0), (0, 0), (0, 0))).astype(bf16)
    wkk, wkv = wk[:, :, 0].reshape(b_, n_wp, NSA_KVW), wk[:, :, 1].reshape(b_, n_wp, NSA_KVW)

    per_b = lambda n, w: pl.BlockSpec((None, n, w), lambda b: (b, 0, 0))
    o_shape = jax.ShapeDtypeStruct((b_, rows, NSA_HEAD_DIM), jnp.float32)
    o_cmp, o_win, sel = pl.pallas_call(
        functools.partial(_nsa_decode_select_kernel, n_cmp=n_cmp, n_sel=n_sel, n_top=min(NSA_N_SELECT, n_sel),
                          n_win=n_win, w_buf=w_buf, pos0=pos0, t_new=T),
        grid=(b_,),
        in_specs=[per_b(rows, NSA_KVW), per_b(n_cp, NSA_KVW), per_b(n_cp, NSA_KVW),
                  per_b(n_wp, NSA_KVW), per_b(n_wp, NSA_KVW)],
        out_specs=[per_b(rows, NSA_HEAD_DIM), per_b(rows, NSA_HEAD_DIM), per_b(rows, n_sp)],
        out_shape=[o_shape, o_shape, jax.ShapeDtypeStruct((b_, rows, n_sp), bf16)],
        scratch_shapes=[pltpu.VMEM((n_sp, NSA_N_KV * T), jnp.float32),
                        pltpu.VMEM((n_sp, NSA_N_KV * T), jnp.int32)],
        compiler_params=pltpu.CompilerParams(
            dimension_semantics=("parallel",), vmem_limit_bytes=V7X_VMEM_LIMIT_BYTES),
        name="nsa_decode_select",
    )(q_blk, kc, vc, wkk, wkv)

    new_slc = jnp.pad(kv6[:, :, 2:4].reshape(b_, T, NSA_HALF_SLABS, NSA_HEAD_DIM),
                      ((0, 0), (0, page - T), (0, 0), (0, 0)))
    half_page = (None, page, NSA_HALF_SLABS, NSA_HEAD_DIM)

    def page_spec(k):
        return pl.BlockSpec(half_page, lambda b, s, pt: (pt[b, NSA_SLC_PAGES * s + k], 0, 1, 0))

    bs = lambda n, w: pl.BlockSpec((None, n, w), lambda b, s, pt: (b, 0, 0))
    o_slc = pl.pallas_call(
        functools.partial(_nsa_decode_slc_kernel, pos0=pos0, t_new=T),
        grid_spec=pltpu.PrefetchScalarGridSpec(
            num_scalar_prefetch=1,
            grid=(b_, n_pages // NSA_SLC_PAGES),
            in_specs=[bs(rows, NSA_KVW), bs(rows, n_sp), pl.BlockSpec(half_page, lambda b, s, pt: (b, 0, 0, 0))]
            + [page_spec(k) for k in range(NSA_SLC_PAGES)],
            out_specs=bs(rows, NSA_HEAD_DIM),
            scratch_shapes=[pltpu.VMEM((rows, 1), jnp.float32),
                            pltpu.VMEM((rows, 1), jnp.float32),
                            pltpu.VMEM((rows, NSA_HEAD_DIM), jnp.float32)]),
        out_shape=o_shape,
        compiler_params=pltpu.CompilerParams(
            dimension_semantics=("parallel", "arbitrary"), vmem_limit_bytes=V7X_VMEM_LIMIT_BYTES),
        name="nsa_decode_slc",
    )(page_ids, q_blk, sel, new_slc, *([cache] * NSA_SLC_PAGES))

    def token_major(o):
        o = o.reshape(b_, NSA_N_KV, NSA_GQA, T, NSA_HEAD_DIM)
        return jnp.transpose(o, (0, 3, 1, 2, 4)).reshape(b_, T, NSA_N_HEADS * NSA_HEAD_DIM)

    return token_major(o_cmp), token_major(o_slc), token_major(o_win)


def nsa_decode_mixer(x, cache, page_ids, win_buf, w_q, w_kv, w_gate, b_gate, w_cmp1, w_cmp2, cmp_pe, w_out):
    b_, T, _ = x.shape
    q = matmul(x, w_q)
    kv = matmul(x, w_kv)
    kv6 = kv.reshape(b_, T, 6, NSA_N_KV, NSA_HEAD_DIM)
    o_cmp, o_slc, o_win = nsa_decode_attention(q, kv, cache, page_ids, win_buf, w_cmp1, w_cmp2, cmp_pe)
    gate = jax.nn.sigmoid(matmul(x, w_gate) + b_gate).reshape(b_, T, NSA_N_HEADS, 3)

    def heads(t):
        return t.reshape(b_, T, NSA_N_HEADS, NSA_HEAD_DIM)

    o = gate[..., 0:1] * heads(o_cmp) + gate[..., 1:2] * heads(o_slc) + gate[..., 2:3] * heads(o_win)
    y = matmul(o.reshape(b_, T, NSA_N_HEADS * NSA_HEAD_DIM), w_out)
    win_new = jnp.concatenate([win_buf, kv6[:, :, 4:6]], axis=1)[:, -win_buf.shape[1]:]
    return y, kv6[:, :, :4], win_new


def nsa_mixer(x, past_rows, win_buf, pos0, w_q, w_kv, w_gate, b_gate, w_cmp1, w_cmp2, cmp_pe, w_out):
    b_, T, _ = x.shape
    if past_rows is None:
        assert win_buf is None and pos0 == 0 and T % NSA_SEL_BLOCK == 0
        return nsa_prompt_mixer(x, w_q, w_kv, w_gate, b_gate, w_cmp1, w_cmp2, cmp_pe, w_out)
    q = matmul(x, w_q).reshape(b_, T, NSA_N_KV, NSA_GQA, NSA_HEAD_DIM)
    kv = matmul(x, w_kv).reshape(b_, T, 6, NSA_N_KV, NSA_HEAD_DIM)
    rows = kv[:, :, :4]
    full = rows if past_rows is None else jnp.concatenate([past_rows, rows], axis=1)
    q_pos = pos0 + jnp.arange(T)
    o_cmp, o_slc = nsa_cmp_slc(q, full, q_pos, w_cmp1, w_cmp2, cmp_pe)
    win_rows = kv[:, :, 4:6]
    if win_buf is None:
        o_win = nsa_window_prompt(q, win_rows)
        win_new = win_rows[:, -min(NSA_WINDOW, T):]
    else:
        w_b = win_buf.shape[1]
        wk = jnp.concatenate([win_buf, win_rows], axis=1)
        k_pos = pos0 - w_b + jnp.arange(w_b + T)
        o_win = window_attend(q[:, None], wk[:, None, :, 0], wk[:, None, :, 1],
                              q_pos[None], k_pos[None])[:, 0]
        win_new = wk[:, -w_b:]
    gate = jax.nn.sigmoid(matmul(x, w_gate) + b_gate).reshape(b_, T, NSA_N_KV, NSA_GQA, 3)
    o = gate[..., 0:1] * o_cmp + gate[..., 1:2] * o_slc + gate[..., 2:3] * o_win
    return matmul(o.reshape(b_, T, NSA_N_HEADS * NSA_HEAD_DIM), w_out), rows, win_new


def conv_ffn(x, hist, w_up, conv_w, conv_b, w_down):
    ag = matmul(x, w_up)
    h = conv_act(ag, 0, FFN_DIM, hist, conv_w, conv_b, "gelu_gate", gate_col0=FFN_DIM, out_dtype=jnp.bfloat16)
    return matmul(h, w_down), conv_tail(hist, ag, 0, FFN_DIM)


def kernel(x_prompt, x_sample, cache_nsa, state_nsa_win, state_ssd, state_ssd_conv, state_mlstm_c,
           state_mlstm_n, state_mlstm_m, state_mlstm_conv, state_s5, state_ffn_conv, page_table,
           ln_g, ln_b, ffn_w_up, ffn_conv_w, ffn_conv_b, ffn_w_down,
           ssd_w_in, ssd_conv_w, ssd_conv_b, ssd_dt_bias, ssd_a_log, ssd_d, ssd_norm_g, ssd_w_out,
           mlstm_w_up, mlstm_conv_w, mlstm_conv_b, mlstm_w_q, mlstm_w_k, mlstm_w_v, mlstm_w_if,
           mlstm_b_if, mlstm_skip, mlstm_norm_g, mlstm_w_down,
           s5_a_re, s5_a_im, s5_log_dt, s5_b_re, s5_b_im, s5_c_re, s5_c_im, s5_d, s5_w_glu_a, s5_w_glu_b,
           nsa_w_q, nsa_w_kv, nsa_w_gate, nsa_b_gate, nsa_w_cmp1, nsa_w_cmp2, nsa_cmp_pe, nsa_w_out):

    def bf16_stack(w):
        return w.astype(jnp.bfloat16)

    ffn_w_up, ffn_w_down = bf16_stack(ffn_w_up), bf16_stack(ffn_w_down)
    ssd_w_in, ssd_w_out = bf16_stack(ssd_w_in), bf16_stack(ssd_w_out)
    mlstm_w_up, mlstm_w_down = bf16_stack(mlstm_w_up), bf16_stack(mlstm_w_down)
    mlstm_w_if = bf16_stack(mlstm_w_if).reshape(-1, MLSTM_D_INNER, 2 * MLSTM_N_HEADS)
    s5_w_glu_a, s5_w_glu_b = bf16_stack(s5_w_glu_a), bf16_stack(s5_w_glu_b)
    nsa_w_q, nsa_w_kv, nsa_w_out = bf16_stack(nsa_w_q), bf16_stack(nsa_w_kv), bf16_stack(nsa_w_out)
    nsa_w_gate = bf16_stack(nsa_w_gate)

    def trunk(x, sample):
        b_, T, _ = x.shape
        dt_ = x.dtype
        pos0 = PAST_LEN if sample else 0
        o_nsa, o_win, o_ssd, o_ssdc, o_mc, o_mn, o_mm, o_mconv, o_s5, o_ffn = ([] for _ in range(10))
        for i in range(DEPTH):
            kind, j = i % N_MIXERS, i // N_MIXERS
            if kind == 0:
                hist = state_ssd_conv[j] if sample else jnp.zeros((b_, SSD_CONV_W - 1, SSD_CONV_DIM), dt_)
                h0 = state_ssd[j] if sample else jnp.zeros((b_, SSD_N_HEADS, SSD_HEADDIM, SSD_D_STATE), dt_)
                y, hist_new, h_new = ssd_mixer(x, hist, h0, (ssd_w_in, j), ssd_conv_w[j], ssd_conv_b[j],
                                               ssd_dt_bias[j], ssd_a_log[j], ssd_d[j], ssd_norm_g[j],
                                               (ssd_w_out, j))
                o_ssd.append(h_new)
                o_ssdc.append(hist_new)
            elif kind == 1:
                hist = state_mlstm_conv[j] if sample else jnp.zeros((b_, MLSTM_CONV_W - 1, MLSTM_D_INNER), dt_)
                c0 = state_mlstm_c[j] if sample else jnp.zeros((b_, MLSTM_N_HEADS, MLSTM_HEAD_DIM, MLSTM_HEAD_DIM), dt_)
                n0 = state_mlstm_n[j] if sample else jnp.zeros((b_, MLSTM_N_HEADS, MLSTM_HEAD_DIM), dt_)
                m0 = state_mlstm_m[j] if sample else jnp.zeros((b_, MLSTM_N_HEADS), dt_)
                y, hist_new, c, n, m = mlstm_mixer(x, hist, c0, n0, m0, (mlstm_w_up, j), mlstm_conv_w[j],
                                                   mlstm_conv_b[j], mlstm_w_q[j], mlstm_w_k[j], mlstm_w_v[j],
                                                   [(mlstm_w_if, 3 * j + part) for part in range(3)],
                                                   mlstm_b_if[j], mlstm_skip[j],
                                                   mlstm_norm_g[j], (mlstm_w_down, j))
                o_mc.append(c)
                o_mn.append(n)
                o_mm.append(m)
                o_mconv.append(hist_new)
            elif kind == 2:
                h0 = state_s5[j] if sample else jnp.zeros((b_, S5_N_GROUPS, S5_STATE, 2), dt_)
                y, h_new = s5_mixer(x, h0, s5_a_re[j], s5_a_im[j], s5_log_dt[j], s5_b_re[j], s5_b_im[j],
                                    s5_c_re[j], s5_c_im[j], s5_d[j], (s5_w_glu_a, j), (s5_w_glu_b, j))
                o_s5.append(h_new)
            else:
                nsa_w = ((nsa_w_q, j), (nsa_w_kv, j), (nsa_w_gate, j), nsa_b_gate[j], nsa_w_cmp1[j], nsa_w_cmp2[j],
                         nsa_cmp_pe[j], (nsa_w_out, j))
                if sample:
                    n_pool, page = cache_nsa.shape[1:3]
                    assert pos0 == page_table.shape[1] * page
                    y, rows, win_new = nsa_decode_mixer(
                        x, cache_nsa.reshape(-1, page, NSA_ROW_SLABS, NSA_HEAD_DIM), page_table + j * n_pool,
                        state_nsa_win[j], *nsa_w)
                else:
                    y, rows, win_new = nsa_mixer(x, None, None, pos0, *nsa_w)
                o_nsa.append(rows)
                o_win.append(win_new)
            x = layer_norm(DEEPNORM_ALPHA * x + y, ln_g[i, 0], ln_b[i, 0])
            fhist = state_ffn_conv[i] if sample else jnp.zeros((b_, FFN_CONV_W - 1, FFN_DIM), dt_)
            y, fhist_new = conv_ffn(x, fhist, (ffn_w_up, i), ffn_conv_w[i], ffn_conv_b[i], (ffn_w_down, i))
            o_ffn.append(fhist_new)
            x = layer_norm(DEEPNORM_ALPHA * x + y, ln_g[i, 1], ln_b[i, 1])
        st = jnp.stack
        return (x, st(o_nsa), st(o_win), st(o_ssd), st(o_ssdc), st(o_mc), st(o_mn), st(o_mm),
                st(o_mconv), st(o_s5), st(o_ffn))

    (y_prompt, nsa_p, win_p, ssd_p, ssdc_p, mc_p, mn_p, mm_p, mconv_p, s5_p, ffn_p) = trunk(x_prompt, False)
    (y_sample, nsa_s, win_s, ssd_s, ssdc_s, mc_s, mn_s, mm_s, mconv_s, s5_s, ffn_s) = trunk(x_sample, True)
    return (y_prompt, y_sample, nsa_p, nsa_s, win_p, win_s, ssd_p, ssd_s, ssdc_p, ssdc_s, mc_p, mc_s,
            mn_p, mn_s, mm_p, mm_s, mconv_p, mconv_s, s5_p, s5_s, ffn_p, ffn_s)
```

```python
import functools
import math
from typing import NamedTuple

import jax
import jax.numpy as jnp
from jax import lax
from jax.experimental import pallas as pl
from jax.experimental.pallas import tpu as pltpu

D_MODEL = 2048
DEPTH = 4
PAST_LEN = 16384
N_MIXERS = 4

DEEPNORM_ALPHA = (2.0 * DEPTH) ** 0.25
LN_EPS = 1e-5
RMS_EPS = 1e-5
NEG_INF = -1e30
FORCE_SCORE = 1e4

SSD_D_INNER = 2 * D_MODEL
SSD_HEADDIM = 64
SSD_N_HEADS = SSD_D_INNER // SSD_HEADDIM
SSD_N_GROUPS = 8
SSD_D_STATE = 128
SSD_CONV_W = 4
SSD_CHUNK = 256
SSD_CONV_DIM = SSD_D_INNER + 2 * SSD_N_GROUPS * SSD_D_STATE

MLSTM_D_INNER = 2 * D_MODEL
MLSTM_N_HEADS = 4
MLSTM_HEAD_DIM = MLSTM_D_INNER // MLSTM_N_HEADS
MLSTM_CONV_W = 4
MLSTM_CHUNK = 64

S5_GROUP = 16
S5_N_GROUPS = D_MODEL // S5_GROUP
S5_STATE = 64

NSA_N_HEADS = 16
NSA_N_KV = 4
NSA_HEAD_DIM = D_MODEL // NSA_N_HEADS
NSA_GQA = NSA_N_HEADS // NSA_N_KV
NSA_CMP_BLOCK = 32
NSA_CMP_STRIDE = 16
NSA_SEL_BLOCK = 64
NSA_N_SELECT = 16
NSA_WINDOW = 512
NSA_QBLOCK = 32
NSA_WBLOCK = 128

FFN_DIM = 5632
FFN_CONV_W = 3

V7X_VMEM_LIMIT_BYTES = 48 * 1024 * 1024


def _mm_kernel(x_ref, w_ref, o_ref):
    o_ref[...] = jnp.dot(x_ref[...], w_ref[...], preferred_element_type=jnp.float32)


def _pick(dim, target):
    if dim <= target:
        return dim
    t = target
    while dim % t:
        t //= 2
    return t


def _mm_tiles(M, K, N):
    tm = _pick(M, 1024)
    tn = N if N <= 512 else (1024 if K <= 2048 else 512)
    double_buffered = 2 * (tm * K * 2 + K * tn * 2 + tm * tn * 4)
    assert double_buffered <= V7X_VMEM_LIMIT_BYTES, (M, K, N)
    return tm, tn


def matmul(x, w):
    stack, s = w if isinstance(w, tuple) else (w[None], 0)
    _, K, N = stack.shape
    lead = x.shape[:-1]
    x2 = x.astype(jnp.bfloat16).reshape(-1, K)
    M = x2.shape[0]
    tm, tn = _mm_tiles(M, K, N)
    out = pl.pallas_call(
        _mm_kernel,
        grid=(M // tm, pl.cdiv(N, tn)),
        in_specs=[pl.BlockSpec((tm, K), lambda i, j: (i, 0)),
                  pl.BlockSpec((None, K, tn), lambda i, j: (s, 0, j))],
        out_specs=pl.BlockSpec((tm, tn), lambda i, j: (i, j)),
        out_shape=jax.ShapeDtypeStruct((M, N), jnp.float32),
        compiler_params=pltpu.CompilerParams(
            dimension_semantics=("parallel", "arbitrary"),
            vmem_limit_bytes=V7X_VMEM_LIMIT_BYTES),
        name="matmul",
    )(x2, stack.astype(jnp.bfloat16))
    return out.reshape(lead + (N,))


class Proj(NamedTuple):
    h: jax.Array
    w: object


LN_ROW_TILE = 512
LN_K_TILE = 512


def _mm_res_ln_kernel(h_ref, w_ref, x_ref, g_ref, b_ref, o_ref, ob_ref, acc_ref):
    k = pl.program_id(1)

    @pl.when(k == 0)
    def _():
        acc_ref[...] = jnp.zeros_like(acc_ref)

    acc_ref[...] += jnp.dot(h_ref[...], w_ref[...], preferred_element_type=jnp.float32)

    @pl.when(k == pl.num_programs(1) - 1)
    def _():
        z = DEEPNORM_ALPHA * x_ref[...] + acc_ref[...]
        mu = jnp.mean(z, axis=-1, keepdims=True)
        var = jnp.mean(jnp.square(z - mu), axis=-1, keepdims=True)
        out = (z - mu) * lax.rsqrt(var + LN_EPS) * g_ref[...] + b_ref[...]
        o_ref[...] = out
        ob_ref[...] = out.astype(ob_ref.dtype)


def matmul_residual_ln(h, w, x, g, b):
    stack, s = w if isinstance(w, tuple) else (w[None], 0)
    _, K, N = stack.shape
    lead = x.shape[:-1]
    h2 = h.astype(jnp.bfloat16).reshape(-1, K)
    x2 = x.reshape(-1, N)
    M = x2.shape[0]
    tm, tk = _pick(M, LN_ROW_TILE), _pick(K, LN_K_TILE)
    rows = pl.BlockSpec((tm, N), lambda i, k: (i, 0))
    vec = pl.BlockSpec((1, N), lambda i, k: (0, 0))
    out, out_b = pl.pallas_call(
        _mm_res_ln_kernel,
        grid=(M // tm, K // tk),
        in_specs=[pl.BlockSpec((tm, tk), lambda i, k: (i, k)),
                  pl.BlockSpec((None, tk, N), lambda i, k: (s, k, 0)),
                  rows, vec, vec],
        out_specs=[rows, rows],
        out_shape=[jax.ShapeDtypeStruct((M, N), jnp.float32), jax.ShapeDtypeStruct((M, N), jnp.bfloat16)],
        scratch_shapes=[pltpu.VMEM((tm, N), jnp.float32)],
        compiler_params=pltpu.CompilerParams(
            dimension_semantics=("parallel", "arbitrary"),
            vmem_limit_bytes=V7X_VMEM_LIMIT_BYTES),
        name="matmul_residual_ln",
    )(h2, stack.astype(jnp.bfloat16), x2, g.reshape(1, N), b.reshape(1, N))
    return out.reshape(lead + (N,)), out_b.reshape(lead + (N,))


def layer_norm(x, g, b):
    mu = jnp.mean(x, axis=-1, keepdims=True)
    var = jnp.mean(jnp.square(x - mu), axis=-1, keepdims=True)
    return (x - mu) * lax.rsqrt(var + LN_EPS) * g + b


def group_rms_norm(y, g, n_groups):
    yg = y.reshape(y.shape[:-1] + (n_groups, -1))
    yg = yg * lax.rsqrt(jnp.mean(yg * yg, axis=-1, keepdims=True) + RMS_EPS)
    return yg.reshape(y.shape) * g


def head_layer_norm(h, g):
    mu = jnp.mean(h, axis=-1, keepdims=True)
    var = jnp.mean(jnp.square(h - mu), axis=-1, keepdims=True)
    hn = (h - mu) * lax.rsqrt(var + LN_EPS)
    return hn.reshape(h.shape[:2] + (-1,)) * g


def causal_dwconv(x_hist, w, b):
    width = w.shape[0]
    T = x_hist.shape[1] - (width - 1)
    out = b
    for k in range(width):
        out = out + w[k] * x_hist[:, k:k + T]
    return out


CONV_TIME_TILE = 512
CONV_CHAN_TILE = 2048
QKV_CHAN_TILE = 512
CONV_HALO = 8
VREG_LANES = 128


def _causal_conv_tile(cur_ref, prev_ref, hist_ref, w_ref, b_ref, ext_ref, width):
    rows = cur_ref.shape[0]
    ext_ref[0:CONV_HALO, :] = jnp.where(pl.program_id(1) == 0, hist_ref[...], prev_ref[...])
    ext_ref[CONV_HALO:, :] = cur_ref[...]
    acc = b_ref[...]
    for k in range(width):
        acc = acc + w_ref[k:k + 1, :] * ext_ref[pl.ds(CONV_HALO - (width - 1 - k), rows), :]
    return acc


def _conv_act_kernel(cur_ref, prev_ref, hist_ref, w_ref, b_ref, *rest, width, act):
    acc = _causal_conv_tile(cur_ref, prev_ref, hist_ref, w_ref, b_ref, rest[-1], width)
    if act == "silu":
        o_ref = rest[0]
        o_ref[...] = jax.nn.silu(acc).astype(o_ref.dtype)
    else:
        g_ref, o_ref = rest[:2]
        o_ref[...] = (jax.nn.gelu(acc) * g_ref[...]).astype(o_ref.dtype)


def _conv_tiles(T, chans, offsets, max_lanes):
    lanes = 128
    units = math.gcd(chans // lanes, *[o // lanes for o in offsets])
    ct = lanes * max(d for d in range(1, units + 1) if units % d == 0 and lanes * d <= max_lanes)
    tt = min(T, CONV_TIME_TILE)
    assert T % tt == 0 and tt % CONV_HALO == 0 and chans % lanes == 0 and all(o % lanes == 0 for o in offsets)
    return tt, ct


def conv_act(src, col0, chans, hist, w, b, act, gate_col0=None, out_dtype=jnp.float32):
    b_, T, _ = src.shape
    width = w.shape[0]
    gated = act == "gelu_gate"
    tt, ct = _conv_tiles(T, chans, [col0, gate_col0] if gated else [col0], CONV_CHAN_TILE)
    assert width <= CONV_HALO + 1
    hist8 = jnp.pad(hist, ((0, 0), (CONV_HALO - (width - 1), 0), (0, 0)))
    halo_blocks = tt // CONV_HALO
    cb0 = col0 // ct
    in_specs = [pl.BlockSpec((None, tt, ct), lambda b, t, c: (b, t, cb0 + c)),
                pl.BlockSpec((None, CONV_HALO, ct), lambda b, t, c: (b, jnp.maximum(t * halo_blocks - 1, 0), cb0 + c)),
                pl.BlockSpec((None, CONV_HALO, ct), lambda b, t, c: (b, 0, c)),
                pl.BlockSpec((width, ct), lambda b, t, c: (0, c)),
                pl.BlockSpec((1, ct), lambda b, t, c: (0, c))]
    args = [src, src, hist8, w, b.reshape(1, chans)]
    if gated:
        gb0 = gate_col0 // ct
        in_specs.append(pl.BlockSpec((None, tt, ct), lambda b, t, c: (b, t, gb0 + c)))
        args.append(src)
    return pl.pallas_call(
        functools.partial(_conv_act_kernel, width=width, act=act),
        grid=(b_, T // tt, chans // ct),
        in_specs=in_specs,
        out_specs=pl.BlockSpec((None, tt, ct), lambda b, t, c: (b, t, c)),
        out_shape=jax.ShapeDtypeStruct((b_, T, chans), out_dtype),
        scratch_shapes=[pltpu.VMEM((CONV_HALO + tt, ct), jnp.float32)],
        compiler_params=pltpu.CompilerParams(
            dimension_semantics=("parallel", "parallel", "parallel"),
            vmem_limit_bytes=V7X_VMEM_LIMIT_BYTES),
        name="conv_act",
    )(*args)


def _blockdiag_coefs(w):
    nb, bs, _ = w.shape
    shifts = jnp.stack([jnp.eye(bs, k=d, dtype=w.dtype) for d in range(-(bs - 1), bs)])
    return jnp.einsum('ncd,kcd->knd', w, shifts).reshape(2 * bs - 1, nb * bs)


def _mlstm_qkv_kernel(cur_ref, prev_ref, hist_ref, w_ref, b_ref, cq_ref, ck_ref, cv_ref,
                      xc_ref, q_ref, k_ref, v_ref, ext_ref, *, width, bs):
    xc_ref[...] = jax.nn.silu(_causal_conv_tile(cur_ref, prev_ref, hist_ref, w_ref, b_ref, ext_ref, width))

    def project(x_ref, col, coef_refs, out_refs):
        x = x_ref[:, col]
        outs = [jnp.zeros(x.shape, jnp.float32) for _ in coef_refs]
        for d in range(-(bs - 1), bs):
            moved = x if d == 0 else pltpu.roll(x, d % VREG_LANES, 1)
            row = bs - 1 + d
            outs = [o + c[row:row + 1, col] * moved for o, c in zip(outs, coef_refs)]
        for o, o_ref in zip(outs, out_refs):
            o_ref[:, col] = o.astype(o_ref.dtype)

    for c0 in range(0, cur_ref.shape[1], VREG_LANES):
        col = slice(c0, c0 + VREG_LANES)
        project(xc_ref, col, (cq_ref, ck_ref), (q_ref, k_ref))
        project(cur_ref, col, (cv_ref,), (v_ref,))


def mlstm_qkv(up, hist, conv_w, conv_b, w_q, w_k, w_v):
    b_, T, _ = up.shape
    chans = MLSTM_D_INNER
    width = conv_w.shape[0]
    bs = w_q.shape[1]
    tt, ct = _conv_tiles(T, chans, [0], QKV_CHAN_TILE)
    assert VREG_LANES % bs == 0 and width <= CONV_HALO + 1
    hist8 = jnp.pad(hist, ((0, 0), (CONV_HALO - (width - 1), 0), (0, 0)))
    halo_blocks = tt // CONV_HALO
    tile = pl.BlockSpec((None, tt, ct), lambda b, t, c: (b, t, c))
    coef = pl.BlockSpec((2 * bs - 1, ct), lambda b, t, c: (0, c))
    shape = lambda dt: jax.ShapeDtypeStruct((b_, T, chans), dt)
    return pl.pallas_call(
        functools.partial(_mlstm_qkv_kernel, width=width, bs=bs),
        grid=(b_, T // tt, chans // ct),
        in_specs=[tile,
                  pl.BlockSpec((None, CONV_HALO, ct), lambda b, t, c: (b, jnp.maximum(t * halo_blocks - 1, 0), c)),
                  pl.BlockSpec((None, CONV_HALO, ct), lambda b, t, c: (b, 0, c)),
                  pl.BlockSpec((width, ct), lambda b, t, c: (0, c)),
                  pl.BlockSpec((1, ct), lambda b, t, c: (0, c)),
                  coef, coef, coef],
        out_specs=[tile, tile, tile, tile],
        out_shape=[shape(jnp.float32), shape(jnp.bfloat16), shape(jnp.bfloat16), shape(jnp.bfloat16)],
        scratch_shapes=[pltpu.VMEM((CONV_HALO + tt, ct), jnp.float32)],
        compiler_params=pltpu.CompilerParams(
            dimension_semantics=("parallel", "parallel", "parallel"),
            vmem_limit_bytes=V7X_VMEM_LIMIT_BYTES),
        name="mlstm_qkv",
    )(up, up, hist8, conv_w, conv_b.reshape(1, chans),
      _blockdiag_coefs(w_q), _blockdiag_coefs(w_k), _blockdiag_coefs(w_v))


def conv_tail(hist, src, col0, chans):
    keep = hist.shape[1]
    return jnp.concatenate([hist, src[:, -keep:, col0:col0 + chans]], axis=1)[:, -keep:]


def blockdiag(x, w):
    nb, bs, _ = w.shape
    y = jnp.einsum('btnc,ncd->btnd', x.reshape(x.shape[:2] + (nb, bs)), w)
    return y.reshape(x.shape[:2] + (nb * bs,))


def segsum(x):
    T = x.shape[-1]
    xr = jnp.broadcast_to(x[..., :, None], x.shape + (T,))
    strict = jnp.tril(jnp.ones((T, T), bool), -1)
    xs = jnp.cumsum(jnp.where(strict, xr, 0), axis=-2)
    return jnp.where(jnp.tril(jnp.ones((T, T), bool)), xs, -jnp.inf)


def ssd_scan(xs, dt, a, bm, cm, h0):
    b_, T = xs.shape[:2]
    cl = math.gcd(T, SSD_CHUNK)
    nc = T // cl

    def chunk(t):
        return t.reshape((b_, nc, cl) + t.shape[2:])

    xc, dtc, bc, cc = chunk(xs), chunk(dt), chunk(bm), chunk(cm)
    dt_t = jnp.moveaxis(dtc, 2, -1)
    da = dt_t * a[:, :, None]
    acs = jnp.cumsum(da, axis=-1)
    decay_in = jnp.exp(segsum(da))
    cb = jnp.einsum('bclgn,bcsgn->bcgls', cc, bc)
    w_diag = cb[:, :, :, None] * decay_in * dt_t[..., None, :]
    y_diag = jnp.einsum('bcgrls,bcsgrp->bclgrp', w_diag, xc)
    w_state = jnp.exp(acs[..., -1:] - acs) * dt_t
    states = jnp.einsum('bclgn,bcgrl,bclgrp->bcgrpn', bc, w_state, xc)
    states = jnp.concatenate([h0[:, None], states], axis=1)
    tot = jnp.pad(jnp.moveaxis(acs[..., -1], 1, -1), ((0, 0), (0, 0), (0, 0), (1, 0)))
    decay_chunk = jnp.exp(segsum(tot))
    new_states = jnp.einsum('bgrzc,bcgrpn->bzgrpn', decay_chunk, states)
    y_off = jnp.einsum('bclgn,bcgrpn,bcgrl->bclgrp', cc, new_states[:, :-1], jnp.exp(acs))
    return (y_diag + y_off).reshape(xs.shape), new_states[:, -1]


SSD_HEADS_PER_GROUP = SSD_N_HEADS // SSD_N_GROUPS
SSD_GROUP_WIDTH = SSD_HEADS_PER_GROUP * SSD_HEADDIM
SSD_HEAD_SHIFT = SSD_HEADDIM.bit_length() - 1
assert 1 << SSD_HEAD_SHIFT == SSD_HEADDIM and SSD_GROUP_WIDTH == SSD_D_INNER // SSD_N_GROUPS


def _ssd_kernel(x_ref, z_ref, cm_ref, bmt_ref, cols_ref, rows_ref, h0_ref, d_ref, g_ref,
                y_ref, h_out_ref, h_ref):
    f32, bf16 = jnp.float32, jnp.bfloat16
    ck = pl.program_id(2)
    L, W = x_ref.shape
    R = SSD_HEADS_PER_GROUP

    @pl.when(ck == 0)
    def _():
        h_ref[...] = h0_ref[...]

    def per_channel(c):
        spread = jnp.where((lax.broadcasted_iota(jnp.int32, (R, W), 1) >> SSD_HEAD_SHIFT)
                           == lax.broadcasted_iota(jnp.int32, (R, W), 0), 1.0, 0.0).astype(bf16)
        out = jnp.zeros((L, W), f32)
        for _ in range(3):
            part = c.astype(bf16)
            out = out + jnp.dot(part, spread, preferred_element_type=f32)
            c = c - part.astype(f32)
        return out

    x = x_ref[...]
    xb = x.astype(bf16)
    cmb = cm_ref[...].astype(bf16)
    bmt = bmt_ref[...]
    cols = cols_ref[...]
    acs = per_channel(cols[:, :R])
    cb = jnp.dot(cmb, bmt, preferred_element_type=f32)
    causal = (lax.broadcasted_iota(jnp.int32, (L, L), 0) >= lax.broadcasted_iota(jnp.int32, (L, L), 1))
    lane_head = lax.broadcasted_iota(jnp.int32, (L, W), 1) >> SSD_HEAD_SHIFT
    y = jnp.zeros((L, W), f32)
    for r in range(R):
        acs_col = cols[:, r:r + 1]
        acs_row = rows_ref[r:r + 1, :]
        dt_row = rows_ref[SSD_HEADS_PER_GROUP + r:SSD_HEADS_PER_GROUP + r + 1, :]
        decay = jnp.exp(jnp.where(causal, acs_col - acs_row, -jnp.inf))
        w = (cb * decay * dt_row).astype(bf16)
        y = jnp.where(lane_head == r, jnp.dot(w, xb, preferred_element_type=f32), y)
    total = acs[L - 1:L, :]
    xw = (x * (jnp.exp(total - acs) * per_channel(cols[:, R:]))).astype(bf16)
    h_t = h_ref[...]
    y = y + jnp.dot(cmb, h_t.astype(bf16), preferred_element_type=f32) * jnp.exp(acs)
    h_ref[...] = jnp.exp(total) * h_t + jnp.dot(bmt, xw, preferred_element_type=f32)
    y = (y + d_ref[...] * x) * jax.nn.silu(z_ref[...])
    y_ref[...] = y * lax.rsqrt(jnp.mean(y * y, axis=-1, keepdims=True) + RMS_EPS) * g_ref[...]

    @pl.when(ck == pl.num_programs(2) - 1)
    def _():
        h_out_ref[...] = h_ref[...]


def ssd_cell(zx, xbc, dt, a, d_skip, norm_g, h0):
    b_, T, _ = xbc.shape
    G, R, P, N, W = SSD_N_GROUPS, SSD_HEADS_PER_GROUP, SSD_HEADDIM, SSD_D_STATE, SSD_GROUP_WIDTH
    L = math.gcd(T, SSD_CHUNK)
    nc = T // L
    acs = jnp.cumsum((dt * a).reshape(b_, nc, L, SSD_N_HEADS), axis=2)
    dtc = dt.reshape(b_, nc, L, SSD_N_HEADS)

    def rows(t):
        return jnp.transpose(t.reshape(b_, nc, L, G, R), (0, 3, 1, 4, 2))

    def cols(t):
        return jnp.transpose(t.reshape(b_, T, G, R), (0, 2, 1, 3))

    rowpack = jnp.concatenate([rows(acs), rows(dtc)], axis=3)
    colpack = jnp.concatenate([cols(acs), cols(dtc)], axis=3)
    bmt = jnp.transpose(xbc[..., SSD_D_INNER:SSD_D_INNER + G * N].astype(jnp.bfloat16).reshape(b_, T, G, N),
                        (0, 2, 3, 1))
    h0t = jnp.transpose(h0.reshape(b_, G, R, P, N), (0, 1, 4, 2, 3)).reshape(b_, G, N, W)
    chan = lambda b, g, c: (b, c, g)
    cm_block0 = (SSD_D_INNER + G * N) // N
    state_spec = pl.BlockSpec((None, None, N, W), lambda b, g, c: (b, g, 0, 0))
    row_spec = pl.BlockSpec((1, W), lambda b, g, c: (0, g))
    y, ht = pl.pallas_call(
        _ssd_kernel,
        grid=(b_, G, nc),
        in_specs=[pl.BlockSpec((None, L, W), chan),
                  pl.BlockSpec((None, L, W), chan),
                  pl.BlockSpec((None, L, N), lambda b, g, c: (b, c, cm_block0 + g)),
                  pl.BlockSpec((None, None, N, L), lambda b, g, c: (b, g, 0, c)),
                  pl.BlockSpec((None, None, L, 2 * R), lambda b, g, c: (b, g, c, 0)),
                  pl.BlockSpec((None, None, None, 2 * R, L), lambda b, g, c: (b, g, c, 0, 0)),
                  state_spec, row_spec, row_spec],
        out_specs=[pl.BlockSpec((None, L, W), chan), state_spec],
        out_shape=[jax.ShapeDtypeStruct((b_, T, SSD_D_INNER), jnp.float32),
                   jax.ShapeDtypeStruct((b_, G, N, W), jnp.float32)],
        scratch_shapes=[pltpu.VMEM((N, W), jnp.float32)],
        compiler_params=pltpu.CompilerParams(
            dimension_semantics=("parallel", "parallel", "arbitrary"),
            vmem_limit_bytes=V7X_VMEM_LIMIT_BYTES),
        name="ssd_cell",
    )(xbc, zx, xbc, bmt, colpack, rowpack, h0t,
      jnp.repeat(d_skip, P).reshape(1, SSD_D_INNER), norm_g.reshape(1, SSD_D_INNER))
    h_new = jnp.transpose(ht.reshape(b_, G, N, R, P), (0, 1, 3, 4, 2)).reshape(b_, SSD_N_HEADS, P, N)
    return y, h_new


def ssd_mixer(x, conv_hist, h0, w_in, conv_w, conv_b, dt_bias, a_log, d_skip, norm_g, w_out):
    zx = matmul(x, w_in)
    xbc = conv_act(zx, SSD_D_INNER, SSD_CONV_DIM, conv_hist, conv_w, conv_b, "silu")
    dt = jax.nn.softplus(zx[..., SSD_D_INNER + SSD_CONV_DIM:] + dt_bias)
    y, h_new = ssd_cell(zx, xbc, dt, -jnp.exp(a_log), d_skip, norm_g, h0)
    return Proj(y, w_out), conv_tail(conv_hist, zx, SSD_D_INNER, SSD_CONV_DIM), h_new


def mlstm_chunked(q, k, v, i_pre, logf, c0, n0, m0):
    b_, T, H, _ = q.shape
    cl = math.gcd(T, MLSTM_CHUNK)
    nc = T // cl
    causal = jnp.tril(jnp.ones((cl, cl), bool))

    def chunk(t):
        return jnp.moveaxis(t.reshape((b_, nc, cl) + t.shape[2:]), 1, 0)

    def step(carry, inp):
        c, n, m = carry
        qc, kc, vc, ic, fc = inp
        bcum = jnp.cumsum(fc, axis=1)
        dmat = bcum[:, :, None, :] - bcum[:, None, :, :] + ic[:, None, :, :]
        dmat = jnp.where(causal[None, :, :, None], dmat, -jnp.inf)
        inter = m[:, None, :] + bcum
        m_t = jnp.maximum(inter, jnp.max(dmat, axis=2))
        w = jnp.exp(dmat - m_t[:, :, None, :])
        s = jnp.einsum('bthd,bshd->btsh', qc, kc) * w
        sc_inter = jnp.exp(inter - m_t)
        num = (jnp.einsum('btsh,bshe->bthe', s, vc)
               + sc_inter[..., None] * jnp.einsum('bthd,bhde->bthe', qc, c))
        den = jnp.sum(s, axis=2) + sc_inter * jnp.einsum('bthd,bhd->bth', qc, n)
        h = num / jnp.maximum(jnp.abs(den), jnp.exp(-m_t))[..., None]
        m_new = m_t[:, -1]
        decay_s = jnp.exp(bcum[:, -1:] - bcum + ic - m_new[:, None])
        sc_c = jnp.exp(m + bcum[:, -1] - m_new)
        c_new = sc_c[..., None, None] * c + jnp.einsum('bsh,bshd,bshe->bhde', decay_s, kc, vc)
        n_new = sc_c[..., None] * n + jnp.einsum('bsh,bshd->bhd', decay_s, kc)
        return (c_new, n_new, m_new), h

    (c, n, m), hs = lax.scan(step, (c0, n0, m0),
                             (chunk(q), chunk(k), chunk(v), chunk(i_pre), chunk(logf)))
    h = jnp.moveaxis(hs, 0, 1).reshape(b_, T, H, -1)
    return h, c, n, m


MLSTM_STEP = 256
MLSTM_NORM_LANES = 128


def _mlstm_kernel(q_ref, kt_ref, v_ref, acol_ref, grow_ref, c0_ref, n0_ref, m0_ref, g_ref,
                  xc_ref, z_ref, skip_ref, h_ref, c_out_ref, n_out_ref, m_out_ref, c_ref, m_ref):
    f32, bf16 = jnp.float32, jnp.bfloat16
    step = pl.program_id(2)
    L, D = q_ref.shape

    @pl.when(step == 0)
    def _():
        c_ref[:, :D] = c0_ref[...]
        c_ref[:, D:] = n0_ref[...]
        m_ref[...] = m0_ref[...]

    q = q_ref[...]
    kt = kt_ref[...]
    a_col = acol_ref[...]
    g_row = grow_ref[...]
    m_prev = m_ref[:, 0:1]
    causal = (lax.broadcasted_iota(jnp.int32, (L, L), 0) >= lax.broadcasted_iota(jnp.int32, (L, L), 1))
    dmat = jnp.where(causal, a_col + g_row, -jnp.inf)
    inter = m_prev + a_col
    m_t = jnp.maximum(inter, jnp.max(dmat, axis=1, keepdims=True))
    s = jnp.dot(q, kt, preferred_element_type=f32) * jnp.exp(dmat - m_t)
    sc_inter = jnp.exp(inter - m_t)
    qc = jnp.dot(q, c_ref[...].astype(bf16), preferred_element_type=f32)
    num = jnp.dot(s.astype(bf16), v_ref[...], preferred_element_type=f32) + sc_inter * qc[:, :D]
    den = jnp.sum(s, axis=1, keepdims=True) + sc_inter * qc[:, D:D + 1]
    h = num / jnp.maximum(jnp.abs(den), jnp.exp(-m_t))
    mu = jnp.mean(h, axis=-1, keepdims=True)
    var = jnp.mean(jnp.square(h - mu), axis=-1, keepdims=True)
    hn = (h - mu) * lax.rsqrt(var + LN_EPS) * g_ref[...]
    h_ref[...] = ((hn + skip_ref[...] * xc_ref[...]) * jax.nn.silu(z_ref[...])).astype(h_ref.dtype)

    m_new = m_t[L - 1:L, :]
    total = a_col[L - 1:L, :]
    decay = jnp.exp(total + g_row - m_new)
    sc_c = jnp.exp(m_prev + total - m_new)
    ktd = (kt.astype(f32) * decay).astype(bf16)
    one_hot = jnp.where(lax.broadcasted_iota(jnp.int32, (L, MLSTM_NORM_LANES), 1) == 0, 1.0, 0.0).astype(bf16)
    v_ext = jnp.concatenate([v_ref[...], one_hot], axis=1)
    c_ref[...] = sc_c * c_ref[...] + jnp.dot(ktd, v_ext, preferred_element_type=f32)
    m_ref[...] = jnp.broadcast_to(m_new, m_ref.shape)

    @pl.when(step == pl.num_programs(2) - 1)
    def _():
        c_out_ref[...] = c_ref[:, :D]
        n_out_ref[...] = c_ref[:, D:]
        m_out_ref[...] = m_ref[...]


def mlstm_cell(q, k, v, i_pre, logf, c0, n0, m0, norm_g, xc, up, skip):
    b_, T, _ = q.shape
    H, D = MLSTM_N_HEADS, MLSTM_HEAD_DIM
    L = min(T, MLSTM_STEP)
    ns = T // L
    bf16 = jnp.bfloat16
    kt = jnp.swapaxes((k * (D ** -0.5)).astype(bf16).reshape(b_, T, H, D), 1, 3)
    kt = jnp.swapaxes(kt, 1, 2)
    bcum = jnp.cumsum(logf.reshape(b_, ns, L, H), axis=2)
    acol = jnp.moveaxis(bcum, 3, 1).reshape(b_, H, T, 1)
    grow = jnp.moveaxis(i_pre.reshape(b_, ns, L, H) - bcum, 3, 1).reshape(b_, H, ns, 1, L)
    n0e = jnp.pad(n0[..., None], ((0, 0), (0, 0), (0, 0), (0, MLSTM_NORM_LANES - 1)))
    m0e = jnp.broadcast_to(m0[:, :, None, None], (b_, H, 1, MLSTM_NORM_LANES))

    tok_spec = pl.BlockSpec((None, L, D), lambda b, h, s: (b, s, h))
    state = lambda w: pl.BlockSpec((None, None, D, w), lambda b, h, s: (b, h, 0, 0))
    m_spec = pl.BlockSpec((None, None, 1, MLSTM_NORM_LANES), lambda b, h, s: (b, h, 0, 0))
    hn, c, n, m = pl.pallas_call(
        _mlstm_kernel,
        grid=(b_, H, ns),
        in_specs=[tok_spec,
                  pl.BlockSpec((None, None, D, L), lambda b, h, s: (b, h, 0, s)),
                  tok_spec,
                  pl.BlockSpec((None, None, L, 1), lambda b, h, s: (b, h, s, 0)),
                  pl.BlockSpec((None, None, None, 1, L), lambda b, h, s: (b, h, s, 0, 0)),
                  state(D), state(MLSTM_NORM_LANES), m_spec,
                  pl.BlockSpec((1, D), lambda b, h, s: (0, h)),
                  tok_spec,
                  pl.BlockSpec((None, L, D), lambda b, h, s: (b, s, H + h)),
                  pl.BlockSpec((1, D), lambda b, h, s: (0, h))],
        out_specs=[tok_spec, state(D), state(MLSTM_NORM_LANES), m_spec],
        out_shape=[jax.ShapeDtypeStruct((b_, T, H * D), bf16),
                   jax.ShapeDtypeStruct((b_, H, D, D), jnp.float32),
                   jax.ShapeDtypeStruct((b_, H, D, MLSTM_NORM_LANES), jnp.float32),
                   jax.ShapeDtypeStruct((b_, H, 1, MLSTM_NORM_LANES), jnp.float32)],
        scratch_shapes=[pltpu.VMEM((D, D + MLSTM_NORM_LANES), jnp.float32),
                        pltpu.VMEM((1, MLSTM_NORM_LANES), jnp.float32)],
        compiler_params=pltpu.CompilerParams(
            dimension_semantics=("parallel", "parallel", "arbitrary"),
            vmem_limit_bytes=V7X_VMEM_LIMIT_BYTES),
        name="mlstm_cell",
    )(q.astype(bf16), kt, v.astype(bf16), acol, grow, c0, n0e, m0e, norm_g.reshape(1, H * D),
      xc, up, skip.reshape(1, H * D))
    return hn, c, n[..., 0], m[:, :, 0, 0]


def mlstm_mixer(x, conv_hist, c0, n0, m0, w_up, conv_w, conv_b, w_q, w_k, w_v, w_if, b_if,
                skip, norm_g, w_down):
    b_, T, _ = x.shape
    up = matmul(x, w_up)
    xc, q, k, v = mlstm_qkv(up, conv_hist, conv_w, conv_b, w_q, w_k, w_v)
    gates = matmul(q, w_if[0]) + matmul(k, w_if[1]) + matmul(v, w_if[2]) + b_if
    i_pre, f_pre = gates[..., :MLSTM_N_HEADS], gates[..., MLSTM_N_HEADS:]
    h, c, n, m = mlstm_cell(q, k, v, i_pre, jax.nn.log_sigmoid(f_pre), c0, n0, m0, norm_g, xc, up, skip)
    return Proj(h, w_down), conv_tail(conv_hist, up, 0, MLSTM_D_INNER), c, n, m


S5_D_STATE = S5_N_GROUPS * S5_STATE
S5_PACK = 8
S5_N_PACKS = S5_N_GROUPS // S5_PACK
S5_SCAN_ROWS = 8
S5_SCAN_LANES = 256
S5_TIME_TILE = 256
S5_SCAN_UNROLL = 4


def _cmul(ar, ai, br, bi):
    return ar * br - ai * bi, ar * bi + ai * br


def _s5_kernel(x_ref, bre_ref, bim_ref, cre_ref, cim_ref, pw_ref, h0r_ref, h0i_ref, d_ref,
               g_ref, hr_out_ref, hi_out_ref, bur_ref, bui_ref, hr_ref, hi_ref):
    tt = pl.program_id(1)
    rows = x_ref.shape[0]
    pk_in = S5_PACK * S5_GROUP
    pk_st = S5_PACK * S5_STATE

    @pl.when(tt == 0)
    def _():
        hr_ref[...] = jnp.broadcast_to(h0r_ref[...], hr_ref.shape)
        hi_ref[...] = jnp.broadcast_to(h0i_ref[...], hi_ref.shape)

    for c in range(S5_N_PACKS):
        xc = x_ref[:, c * pk_in:(c + 1) * pk_in].astype(jnp.bfloat16)
        bur_ref[:, c * pk_st:(c + 1) * pk_st] = jnp.dot(xc, bre_ref[c], preferred_element_type=jnp.float32)
        bui_ref[:, c * pk_st:(c + 1) * pk_st] = jnp.dot(xc, bim_ref[c], preferred_element_type=jnp.float32)

    def col_body(cb, carry):
        cs = pl.ds(pl.multiple_of(cb * S5_SCAN_LANES, S5_SCAN_LANES), S5_SCAN_LANES)
        stages = [(pw_ref[2 * k, :, cs], pw_ref[2 * k + 1, :, cs], 1 << k) for k in range(3)]
        lr, li = pw_ref[6, :, cs], pw_ref[7, :, cs]

        def row_body(r, h):
            hr, hi = h
            rs = pl.ds(pl.multiple_of(r * S5_SCAN_ROWS, S5_SCAN_ROWS), S5_SCAN_ROWS)
            vr, vi = bur_ref[rs, cs], bui_ref[rs, cs]
            for mr, mi, s in stages:
                pr, pi = _cmul(mr, mi, pltpu.roll(vr, s, 0), pltpu.roll(vi, s, 0))
                vr, vi = vr + pr, vi + pi
            pr, pi = _cmul(lr, li, hr, hi)
            vr, vi = vr + pr, vi + pi
            bur_ref[rs, cs] = vr
            bui_ref[rs, cs] = vi
            last = S5_SCAN_ROWS - 1
            return (jnp.broadcast_to(vr[last:, :], vr.shape), jnp.broadcast_to(vi[last:, :], vi.shape))

        hr, hi = lax.fori_loop(0, rows // S5_SCAN_ROWS, row_body, (hr_ref[:, cs], hi_ref[:, cs]),
                               unroll=min(S5_SCAN_UNROLL, rows // S5_SCAN_ROWS))
        hr_ref[:, cs] = hr
        hi_ref[:, cs] = hi
        return carry

    lax.fori_loop(0, S5_D_STATE // S5_SCAN_LANES, col_body, 0)

    for c in range(S5_N_PACKS):
        hr = bur_ref[:, c * pk_st:(c + 1) * pk_st].astype(jnp.bfloat16)
        hi = bui_ref[:, c * pk_st:(c + 1) * pk_st].astype(jnp.bfloat16)
        y = (jnp.dot(hr, cre_ref[c], preferred_element_type=jnp.float32)
             - jnp.dot(hi, cim_ref[c], preferred_element_type=jnp.float32))
        cols = slice(c * pk_in, (c + 1) * pk_in)
        g_ref[:, cols] = jax.nn.gelu(y + d_ref[:, cols] * x_ref[:, cols])

    @pl.when(tt == pl.num_programs(1) - 1)
    def _():
        hr_out_ref[...] = hr_ref[0:1, :]
        hi_out_ref[...] = hi_ref[0:1, :]


def _block_diag_packs(w):
    g, r, c = w.shape
    eye = jnp.eye(S5_PACK, dtype=w.dtype)
    wb = jnp.einsum('kgrc,gh->kgrhc', w.reshape(g // S5_PACK, S5_PACK, r, c), eye)
    return wb.reshape(g // S5_PACK, S5_PACK * r, S5_PACK * c).astype(jnp.bfloat16)


def s5_mixer(x, h0, a_re, a_im, log_dt, b_re, b_im, c_re, c_im, d_skip, w_glu_a, w_glu_b):
    b_, T, _ = x.shape
    step = jnp.exp(log_dt)[:, None]
    mag = jnp.exp(a_re * step)
    ab_re, ab_im = mag * jnp.cos(a_im * step), mag * jnp.sin(a_im * step)
    den = a_re * a_re + a_im * a_im
    nr, ni = ab_re - 1.0, ab_im
    f_re = (nr * a_re + ni * a_im) / den
    f_im = (ni * a_re - nr * a_im) / den
    bb_re = f_re[..., None] * b_re - f_im[..., None] * b_im
    bb_im = f_re[..., None] * b_im + f_im[..., None] * b_re
    bre = _block_diag_packs(jnp.swapaxes(bb_re, 1, 2))
    bim = _block_diag_packs(jnp.swapaxes(bb_im, 1, 2))
    cre = _block_diag_packs(jnp.swapaxes(c_re, 1, 2))
    cim = _block_diag_packs(jnp.swapaxes(c_im, 1, 2))
    l1 = (ab_re.reshape(-1), ab_im.reshape(-1))
    l2 = _cmul(*l1, *l1)
    l4 = _cmul(*l2, *l2)
    row = jnp.arange(S5_SCAN_ROWS)[:, None]
    pw = []
    for s, (pr, pi) in ((1, l1), (2, l2), (4, l4)):
        pw += [jnp.where(row >= s, pr[None, :], 0.0), jnp.where(row >= s, pi[None, :], 0.0)]
    acc = [l1]
    for _ in range(S5_SCAN_ROWS - 1):
        acc.append(_cmul(*acc[-1], *l1))
    pw += [jnp.stack([a[0] for a in acc]), jnp.stack([a[1] for a in acc])]
    pw = jnp.stack(pw)

    tt = min(T, S5_TIME_TILE)
    h0r = h0[..., 0].reshape(b_, 1, S5_D_STATE)
    h0i = h0[..., 1].reshape(b_, 1, S5_D_STATE)
    pk_in, pk_st = S5_PACK * S5_GROUP, S5_PACK * S5_STATE

    def const3(b, t):
        return (0, 0, 0)

    state_spec = pl.BlockSpec((None, 1, S5_D_STATE), lambda b, t: (b, 0, 0))
    g, hr, hi = pl.pallas_call(
        _s5_kernel,
        grid=(b_, T // tt),
        in_specs=[pl.BlockSpec((None, tt, D_MODEL), lambda b, t: (b, t, 0)),
                  pl.BlockSpec((S5_N_PACKS, pk_in, pk_st), const3),
                  pl.BlockSpec((S5_N_PACKS, pk_in, pk_st), const3),
                  pl.BlockSpec((S5_N_PACKS, pk_st, pk_in), const3),
                  pl.BlockSpec((S5_N_PACKS, pk_st, pk_in), const3),
                  pl.BlockSpec((8, S5_SCAN_ROWS, S5_D_STATE), const3),
                  state_spec, state_spec,
                  pl.BlockSpec((1, D_MODEL), lambda b, t: (0, 0))],
        out_specs=[pl.BlockSpec((None, tt, D_MODEL), lambda b, t: (b, t, 0)), state_spec, state_spec],
        out_shape=[jax.ShapeDtypeStruct((b_, T, D_MODEL), jnp.float32),
                   jax.ShapeDtypeStruct((b_, 1, S5_D_STATE), jnp.float32),
                   jax.ShapeDtypeStruct((b_, 1, S5_D_STATE), jnp.float32)],
        scratch_shapes=[pltpu.VMEM((tt, S5_D_STATE), jnp.float32),
                        pltpu.VMEM((tt, S5_D_STATE), jnp.float32),
                        pltpu.VMEM((S5_SCAN_ROWS, S5_D_STATE), jnp.float32),
                        pltpu.VMEM((S5_SCAN_ROWS, S5_D_STATE), jnp.float32)],
        compiler_params=pltpu.CompilerParams(
            dimension_semantics=("parallel", "arbitrary"),
            vmem_limit_bytes=V7X_VMEM_LIMIT_BYTES),
        name="s5_scan",
    )(x, bre, bim, cre, cim, pw, h0r, h0i, d_skip.reshape(1, D_MODEL))
    out = matmul(g, w_glu_a) * jax.nn.sigmoid(matmul(g, w_glu_b))
    h_new = jnp.stack([hr.reshape(b_, S5_N_GROUPS, S5_STATE), hi.reshape(b_, S5_N_GROUPS, S5_STATE)], axis=-1)
    return out, h_new


def nsa_compress(kv, w1, w2, pe):
    b_, L = kv.shape[:2]
    span = NSA_CMP_BLOCK // NSA_CMP_STRIDE
    n_str = L // NSA_CMP_STRIDE
    n_cmp = n_str - span + 1
    chunks = kv.reshape(b_, n_str, NSA_CMP_STRIDE, NSA_N_KV, NSA_HEAD_DIM)
    blocks = jnp.concatenate([chunks[:, s:s + n_cmp] for s in range(span)], axis=2) + pe[:, None, :]
    flat = jnp.moveaxis(blocks, 3, 2).reshape(b_, n_cmp, NSA_N_KV, NSA_CMP_BLOCK * NSA_HEAD_DIM)
    return jax.nn.gelu(flat @ w1) @ w2


def cmp_to_sel(imp, n_sel):
    r = NSA_SEL_BLOCK // NSA_CMP_STRIDE
    span = NSA_CMP_BLOCK // NSA_CMP_STRIDE
    pad = jnp.pad(imp, [(0, 0)] * (imp.ndim - 1) + [(span - 1, span - 1)])
    return sum(pad[..., s:s + r * (n_sel - 1) + 1:r] for s in range(r + span - 1))


def nsa_cmp_slc_block(qb, tq, kcmp, vcmp, ks_blk, vs_blk, n_top):
    scale = NSA_HEAD_DIM ** -0.5
    b_ = qb.shape[0]
    n_cmp = kcmp.shape[1]
    n_sel = ks_blk.shape[2]
    cmp_end = jnp.arange(n_cmp) * NSA_CMP_STRIDE + (NSA_CMP_BLOCK - 1)
    cmask = (cmp_end[None, :] <= tq[:, None])[None, :, None, None, :]
    s = jnp.einsum('bqhgd,bnhd->bqhgn', qb, kcmp) * scale
    p = jax.nn.softmax(jnp.where(cmask, s, NEG_INF), axis=-1)
    p = jnp.where(cmask, p, 0.0)
    o_cmp = jnp.einsum('bqhgn,bnhd->bqhgd', p, vcmp)
    imp = cmp_to_sel(jnp.sum(p, axis=3), n_sel)
    blk = jnp.arange(n_sel)[None, :]
    cur = (tq // NSA_SEL_BLOCK)[:, None]
    forced = (blk == 0) | (blk == cur) | (blk == cur - 1)
    valid = blk <= cur
    score = jnp.where(forced[None, :, None, :], FORCE_SCORE,
                      jnp.where(valid[None, :, None, :], imp, -FORCE_SCORE))
    _, idx = lax.top_k(score, n_top)
    bi = jnp.arange(b_)[:, None, None, None]
    hi = jnp.arange(NSA_N_KV)[None, None, :, None]
    kg = ks_blk[bi, hi, idx]
    vg = vs_blk[bi, hi, idx]
    kpos = idx[..., None] * NSA_SEL_BLOCK + jnp.arange(NSA_SEL_BLOCK)
    smask = (kpos <= tq[None, :, None, None, None])[:, :, :, None]
    s2 = jnp.einsum('bqhgd,bqhnsd->bqhgns', qb, kg) * scale
    s2 = jnp.where(smask, s2, NEG_INF)
    p2 = jax.nn.softmax(s2.reshape(s2.shape[:4] + (-1,)), axis=-1).reshape(s2.shape)
    o_slc = jnp.einsum('bqhgns,bqhnsd->bqhgd', p2, vg)
    return o_cmp, o_slc


def nsa_cmp_slc(q, rows, q_pos, w_cmp1, w_cmp2, cmp_pe):
    b_, L = rows.shape[:2]
    lp = -(-L // NSA_SEL_BLOCK) * NSA_SEL_BLOCK
    rows = jnp.pad(rows, ((0, 0), (0, lp - L), (0, 0), (0, 0), (0, 0)))
    kcmp = nsa_compress(rows[:, :, 0], w_cmp1[0], w_cmp2[0], cmp_pe[0])
    vcmp = nsa_compress(rows[:, :, 1], w_cmp1[1], w_cmp2[1], cmp_pe[1])
    n_sel = lp // NSA_SEL_BLOCK

    def sel_blocks(t):
        return jnp.moveaxis(t.reshape(b_, n_sel, NSA_SEL_BLOCK, NSA_N_KV, NSA_HEAD_DIM), 3, 1)

    ks_blk, vs_blk = sel_blocks(rows[:, :, 2]), sel_blocks(rows[:, :, 3])
    n_top = min(NSA_N_SELECT, n_sel)
    T = q.shape[1]
    qbs = math.gcd(T, NSA_QBLOCK)
    nqb = T // qbs
    qb = jnp.moveaxis(q.reshape((b_, nqb, qbs) + q.shape[2:]), 1, 0)
    pb = q_pos.reshape(nqb, qbs)
    o_cmp, o_slc = lax.map(
        lambda a: nsa_cmp_slc_block(a[0], a[1], kcmp, vcmp, ks_blk, vs_blk, n_top), (qb, pb))
    return (jnp.moveaxis(o_cmp, 0, 1).reshape(q.shape), jnp.moveaxis(o_slc, 0, 1).reshape(q.shape))


def window_attend(qb, kb, vb, q_pos, k_pos):
    s = jnp.einsum('bnqhgd,bnshd->bnqhgs', qb, kb) * (NSA_HEAD_DIM ** -0.5)
    diff = q_pos[:, :, None] - k_pos[:, None, :]
    mask = ((diff >= 0) & (diff < NSA_WINDOW) & (k_pos[:, None, :] >= 0))[None, :, :, None, None, :]
    p = jax.nn.softmax(jnp.where(mask, s, NEG_INF), axis=-1)
    return jnp.einsum('bnqhgs,bnshd->bnqhgd', p, vb)


def nsa_window_prompt(q, win_rows):
    b_, T = q.shape[:2]
    qbs = math.gcd(T, NSA_WBLOCK)
    nqb = T // qbs
    pad = jnp.pad(win_rows, ((0, 0), (NSA_WINDOW, 0), (0, 0), (0, 0), (0, 0)))
    idx = jnp.arange(nqb)[:, None] * qbs + jnp.arange(NSA_WINDOW + qbs)[None, :]
    kvb = pad[:, idx]
    q_pos = jnp.arange(T).reshape(nqb, qbs)
    o = window_attend(q.reshape((b_, nqb, qbs) + q.shape[2:]), kvb[:, :, :, 0], kvb[:, :, :, 1],
                      q_pos, idx - NSA_WINDOW)
    return o.reshape(q.shape)


NSA_TQ = NSA_WBLOCK
NSA_SLC_CHUNK = 512
NSA_WIN_CHUNK = 256
NSA_ROWS = NSA_GQA * NSA_TQ
NSA_SEL_SHIFT = NSA_SEL_BLOCK.bit_length() - 1
assert 1 << NSA_SEL_SHIFT == NSA_SEL_BLOCK


def _nsa_stream_softmax(q, k_ref, v_ref, c_lo, c_hi, chunk, mask_fn, m_ref, l_ref, acc_ref):
    scale = NSA_HEAD_DIM ** -0.5
    m_ref[...] = jnp.full(m_ref.shape, NEG_INF, jnp.float32)
    l_ref[...] = jnp.zeros(l_ref.shape, jnp.float32)
    acc_ref[...] = jnp.zeros(acc_ref.shape, jnp.float32)

    def body(c, carry):
        start = pl.multiple_of(c * chunk, chunk)
        k = k_ref[pl.ds(start, chunk), :]
        v = v_ref[pl.ds(start, chunk), :]
        s = lax.dot_general(q, k, (((1,), (1,)), ((), ())), preferred_element_type=jnp.float32) * scale
        s = jnp.where(mask_fn(start), s, NEG_INF)
        m_old = m_ref[...]
        m_new = jnp.maximum(m_old, jnp.max(s, axis=-1, keepdims=True))
        alpha = jnp.exp(m_old - m_new)
        p = jnp.exp(s - m_new)
        l_ref[...] = alpha * l_ref[...] + jnp.sum(p, axis=-1, keepdims=True)
        acc_ref[...] = alpha * acc_ref[...] + jnp.dot(p.astype(jnp.bfloat16), v,
                                                      preferred_element_type=jnp.float32)
        m_ref[...] = m_new
        return carry

    lax.fori_loop(c_lo, c_hi, body, 0)
    return acc_ref[...] / l_ref[...]


def _nsa_prompt_kernel(q_ref, kc_ref, vc_ref, ks_ref, vs_ref, kw_ref, vw_ref,
                       ocmp_ref, oslc_ref, owin_ref, selexp_ref, m_ref, l_ref, acc_ref,
                       *, n_cmp, n_top):
    f32, bf16 = jnp.float32, jnp.bfloat16
    i = pl.program_id(2)
    t0 = i * NSA_TQ
    n_cp = kc_ref.shape[0]
    t_len = selexp_ref.shape[1]
    n_sel = t_len // NSA_SEL_BLOCK
    nt = (((1,), (1,)), ((), ()))
    q = jnp.concatenate([q_ref[:, g * NSA_HEAD_DIM:(g + 1) * NSA_HEAD_DIM] for g in range(NSA_GQA)], axis=0)
    tq = t0 + (lax.broadcasted_iota(jnp.int32, (NSA_ROWS, 1), 0) & (NSA_TQ - 1))

    def store_heads(o_ref, o):
        for g in range(NSA_GQA):
            o_ref[:, g * NSA_HEAD_DIM:(g + 1) * NSA_HEAD_DIM] = o[g * NSA_TQ:(g + 1) * NSA_TQ]

    s = lax.dot_general(q, kc_ref[...], nt, preferred_element_type=f32) * (NSA_HEAD_DIM ** -0.5)
    n_idx = lax.broadcasted_iota(jnp.int32, (NSA_ROWS, n_cp), 1)
    cmask = (n_idx * NSA_CMP_STRIDE + (NSA_CMP_BLOCK - 1) <= tq) & (n_idx < n_cmp)
    s = jnp.where(cmask, s, NEG_INF)
    e = jnp.exp(s - jnp.max(s, axis=-1, keepdims=True))
    p = jnp.where(cmask, e / jnp.sum(e, axis=-1, keepdims=True), 0.0)
    store_heads(ocmp_ref, jnp.dot(p.astype(bf16), vc_ref[...], preferred_element_type=f32))

    psum = p[0:NSA_TQ]
    for g in range(1, NSA_GQA):
        psum = psum + p[g * NSA_TQ:(g + 1) * NSA_TQ]
    jn = lax.broadcasted_iota(jnp.int32, (n_sel, n_cp), 0)
    nn = lax.broadcasted_iota(jnp.int32, (n_sel, n_cp), 1)
    r = NSA_SEL_BLOCK // NSA_CMP_STRIDE
    span = NSA_CMP_BLOCK // NSA_CMP_STRIDE
    pool = jnp.where((nn >= r * jn - (span - 1)) & (nn <= r * jn + (r - 1)), 1.0, 0.0).astype(bf16)
    imp = jnp.zeros((n_sel, NSA_TQ), f32)
    rest = psum
    for _ in range(3):
        part = rest.astype(bf16)
        imp = imp + lax.dot_general(pool, part, nt, preferred_element_type=f32)
        rest = rest - part.astype(f32)

    jidx = lax.broadcasted_iota(jnp.int32, (n_sel, NSA_TQ), 0)
    cur = (t0 + lax.broadcasted_iota(jnp.int32, (n_sel, NSA_TQ), 1)) >> NSA_SEL_SHIFT
    forced = (jidx == 0) | (jidx == cur) | (jidx == cur - 1)
    score = jnp.where(forced, FORCE_SCORE, jnp.where(jidx <= cur, imp, -FORCE_SCORE))
    rank = jnp.zeros((n_sel, NSA_TQ), jnp.int32)
    for k in range(n_sel):
        row = score[k:k + 1, :]
        before = (row > score) | ((row == score) & (k < jidx))
        rank = rank + jnp.where(before, 1, 0)
    sel_t = jnp.where(rank < n_top, 1.0, 0.0).astype(bf16)
    eye = jnp.where(lax.broadcasted_iota(jnp.int32, (NSA_TQ, NSA_TQ), 0)
                    == lax.broadcasted_iota(jnp.int32, (NSA_TQ, NSA_TQ), 1), 1.0, 0.0).astype(bf16)
    sel = lax.dot_general(eye, sel_t, nt, preferred_element_type=f32).astype(bf16)
    expand = jnp.where((lax.broadcasted_iota(jnp.int32, (n_sel, t_len), 1) >> NSA_SEL_SHIFT)
                       == lax.broadcasted_iota(jnp.int32, (n_sel, t_len), 0), 1.0, 0.0).astype(bf16)
    selexp_ref[...] = jnp.dot(sel, expand, preferred_element_type=f32)

    def slc_mask(start):
        kpos = start + lax.broadcasted_iota(jnp.int32, (NSA_ROWS, NSA_SLC_CHUNK), 1)
        chosen = jnp.concatenate([selexp_ref[:, pl.ds(start, NSA_SLC_CHUNK)]] * NSA_GQA, axis=0)
        return (chosen > 0.5) & (kpos <= tq)

    store_heads(oslc_ref, _nsa_stream_softmax(
        q, ks_ref, vs_ref, 0, (t0 + NSA_TQ - 1) // NSA_SLC_CHUNK + 1, NSA_SLC_CHUNK, slc_mask,
        m_ref, l_ref, acc_ref))

    def win_mask(start):
        diff = tq - (start + lax.broadcasted_iota(jnp.int32, (NSA_ROWS, NSA_WIN_CHUNK), 1))
        return (diff >= 0) & (diff < NSA_WINDOW)

    store_heads(owin_ref, _nsa_stream_softmax(
        q, kw_ref, vw_ref, jnp.maximum(i - NSA_WINDOW // NSA_WIN_CHUNK, 0), i + 1, NSA_WIN_CHUNK,
        win_mask, m_ref, l_ref, acc_ref))


def nsa_prompt_attention(q, kv, kcmp, vcmp):
    b_, T, _ = q.shape
    assert T % NSA_SLC_CHUNK == 0 and (T // NSA_SEL_BLOCK) % 8 == 0
    n_cmp = kcmp.shape[1]
    n_cp = T // NSA_CMP_STRIDE
    n_sel = T // NSA_SEL_BLOCK
    bf16 = jnp.bfloat16

    def cmp_layout(t):
        return jnp.pad(jnp.moveaxis(t, 2, 1), ((0, 0), (0, 0), (0, n_cp - n_cmp), (0, 0))).astype(bf16)

    kvb = kv.astype(bf16)
    width = NSA_GQA * NSA_HEAD_DIM
    q_spec = pl.BlockSpec((None, NSA_TQ, width), lambda b, h, i: (b, i, h))
    cmp_spec = pl.BlockSpec((None, None, n_cp, NSA_HEAD_DIM), lambda b, h, i: (b, h, 0, 0))

    def kv_spec(comp):
        return pl.BlockSpec((None, T, NSA_HEAD_DIM), lambda b, h, i: (b, 0, comp * NSA_N_KV + h))

    out = jax.ShapeDtypeStruct((b_, T, NSA_N_HEADS * NSA_HEAD_DIM), jnp.float32)
    return pl.pallas_call(
        functools.partial(_nsa_prompt_kernel, n_cmp=n_cmp, n_top=min(NSA_N_SELECT, n_sel)),
        grid=(b_, NSA_N_KV, T // NSA_TQ),
        in_specs=[q_spec, cmp_spec, cmp_spec, kv_spec(2), kv_spec(3), kv_spec(4), kv_spec(5)],
        out_specs=[q_spec, q_spec, q_spec],
        out_shape=[out, out, out],
        scratch_shapes=[pltpu.VMEM((NSA_TQ, T), jnp.float32),
                        pltpu.VMEM((NSA_ROWS, 1), jnp.float32),
                        pltpu.VMEM((NSA_ROWS, 1), jnp.float32),
                        pltpu.VMEM((NSA_ROWS, NSA_HEAD_DIM), jnp.float32)],
        compiler_params=pltpu.CompilerParams(
            dimension_semantics=("parallel", "parallel", "arbitrary"),
            vmem_limit_bytes=V7X_VMEM_LIMIT_BYTES),
        name="nsa_prompt",
    )(q.astype(bf16), cmp_layout(kcmp), cmp_layout(vcmp), kvb, kvb, kvb, kvb)


def _nsa_stream_softmax_t(q_t, k_ref, vt_ref, c_lo, c_hi, chunk, mask_fn, m_ref, l_ref, acc_ref):
    scale = NSA_HEAD_DIM ** -0.5
    m_ref[...] = jnp.full(m_ref.shape, NEG_INF, jnp.float32)
    l_ref[...] = jnp.zeros(l_ref.shape, jnp.float32)
    acc_ref[...] = jnp.zeros(acc_ref.shape, jnp.float32)

    def body(c, carry):
        start = pl.multiple_of(c * chunk, chunk)
        s = jnp.dot(k_ref[pl.ds(start, chunk), :], q_t, preferred_element_type=jnp.float32) * scale
        s = jnp.where(mask_fn(start), s, NEG_INF)
        m_old = m_ref[...]
        m_new = jnp.maximum(m_old, jnp.max(s, axis=0, keepdims=True))
        alpha = jnp.exp(m_old - m_new)
        p = jnp.exp(s - m_new)
        l_ref[...] = alpha * l_ref[...] + jnp.sum(p, axis=0, keepdims=True)
        acc_ref[...] = alpha * acc_ref[...] + jnp.dot(vt_ref[:, pl.ds(start, chunk)], p.astype(jnp.bfloat16),
                                                      preferred_element_type=jnp.float32)
        m_ref[...] = m_new
        return carry

    lax.fori_loop(c_lo, c_hi, body, 0)
    return acc_ref[...] / l_ref[...]


def _nsa_prompt_t_kernel(qt_ref, gate_ref, kc_ref, vct_ref, ks_ref, vst_ref, kw_ref, vwt_ref,
                         o_ref, sel_ref, m_ref, l_ref, acc_ref, mix_ref, *, n_cmp, n_top):
    f32, bf16 = jnp.float32, jnp.bfloat16
    i = pl.program_id(2)
    t0 = i * NSA_TQ
    n_cp = kc_ref.shape[0]
    n_sel = sel_ref.shape[0]
    q_t = qt_ref[...]
    tq = t0 + (lax.broadcasted_iota(jnp.int32, (1, NSA_ROWS), 1) & (NSA_TQ - 1))

    s = jnp.dot(kc_ref[...], q_t, preferred_element_type=f32) * (NSA_HEAD_DIM ** -0.5)
    n_idx = lax.broadcasted_iota(jnp.int32, (n_cp, NSA_ROWS), 0)
    cmask = (n_idx * NSA_CMP_STRIDE + (NSA_CMP_BLOCK - 1) <= tq) & (n_idx < n_cmp)
    s = jnp.where(cmask, s, NEG_INF)
    e = jnp.exp(s - jnp.max(s, axis=0, keepdims=True))
    p = jnp.where(cmask, e / jnp.sum(e, axis=0, keepdims=True), 0.0)
    mix_ref[...] = gate_ref[0:1, :] * jnp.dot(vct_ref[...], p.astype(bf16), preferred_element_type=f32)

    psum = p[:, 0:NSA_TQ]
    for g in range(1, NSA_GQA):
        psum = psum + p[:, g * NSA_TQ:(g + 1) * NSA_TQ]
    jn = lax.broadcasted_iota(jnp.int32, (n_sel, n_cp), 0)
    nn = lax.broadcasted_iota(jnp.int32, (n_sel, n_cp), 1)
    r = NSA_SEL_BLOCK // NSA_CMP_STRIDE
    span = NSA_CMP_BLOCK // NSA_CMP_STRIDE
    pool = jnp.where((nn >= r * jn - (span - 1)) & (nn <= r * jn + (r - 1)), 1.0, 0.0).astype(bf16)
    imp = jnp.zeros((n_sel, NSA_TQ), f32)
    rest = psum
    for _ in range(3):
        part = rest.astype(bf16)
        imp = imp + jnp.dot(pool, part, preferred_element_type=f32)
        rest = rest - part.astype(f32)

    jidx = lax.broadcasted_iota(jnp.int32, (n_sel, NSA_TQ), 0)
    cur = (t0 + lax.broadcasted_iota(jnp.int32, (n_sel, NSA_TQ), 1)) >> NSA_SEL_SHIFT
    forced = (jidx == 0) | (jidx == cur) | (jidx == cur - 1)
    score = jnp.where(forced, FORCE_SCORE, jnp.where(jidx <= cur, imp, -FORCE_SCORE))
    rank = jnp.zeros((n_sel, NSA_TQ), jnp.int32)
    for k in range(n_sel):
        row = score[k:k + 1, :]
        before = (row > score) | ((row == score) & (k < jidx))
        rank = rank + jnp.where(before, 1, 0)
    sel_ref[...] = jnp.where(rank < n_top, 1.0, 0.0)

    def slc_mask(start):
        kpos = start + lax.broadcasted_iota(jnp.int32, (NSA_SLC_CHUNK, NSA_ROWS), 0)
        first = start >> NSA_SEL_SHIFT
        chosen = jnp.concatenate(
            [jnp.broadcast_to(sel_ref[pl.ds(first + j, 1), :], (NSA_SEL_BLOCK, NSA_TQ))
             for j in range(NSA_SLC_CHUNK // NSA_SEL_BLOCK)], axis=0)
        chosen = jnp.concatenate([chosen] * NSA_GQA, axis=1)
        return (chosen > 0.5) & (kpos <= tq)

    mix_ref[...] += gate_ref[1:2, :] * _nsa_stream_softmax_t(
        q_t, ks_ref, vst_ref, 0, (t0 + NSA_TQ - 1) // NSA_SLC_CHUNK + 1, NSA_SLC_CHUNK, slc_mask,
        m_ref, l_ref, acc_ref)

    def win_mask(start):
        diff = tq - (start + lax.broadcasted_iota(jnp.int32, (NSA_WIN_CHUNK, NSA_ROWS), 0))
        return (diff >= 0) & (diff < NSA_WINDOW)

    mixed = mix_ref[...] + gate_ref[2:3, :] * _nsa_stream_softmax_t(
        q_t, kw_ref, vwt_ref, jnp.maximum(t0 - NSA_WINDOW, 0) // NSA_WIN_CHUNK,
        (t0 + NSA_TQ - 1) // NSA_WIN_CHUNK + 1, NSA_WIN_CHUNK, win_mask, m_ref, l_ref, acc_ref)
    for g in range(NSA_GQA):
        o_ref[:, g * NSA_HEAD_DIM:(g + 1) * NSA_HEAD_DIM] = mixed[:, g * NSA_TQ:(g + 1) * NSA_TQ].T.astype(o_ref.dtype)


def nsa_prompt_attention_t(q, kv, kcmp, vcmp, gate):
    b_, T, _ = q.shape
    assert T % NSA_SLC_CHUNK == 0 and (T // NSA_SEL_BLOCK) % 8 == 0
    n_cmp = kcmp.shape[1]
    n_cp = T // NSA_CMP_STRIDE
    n_sel = T // NSA_SEL_BLOCK
    n_tiles = T // NSA_TQ
    bf16 = jnp.bfloat16
    pad_c = ((0, 0), (0, n_cp - n_cmp), (0, 0), (0, 0))
    kc = jnp.transpose(jnp.pad(kcmp, pad_c).astype(bf16), (0, 2, 1, 3))
    vct = jnp.transpose(jnp.pad(vcmp, pad_c).astype(bf16), (0, 2, 3, 1))
    kvb = kv.astype(bf16)
    keys = lambda comp: pl.BlockSpec((None, T, NSA_HEAD_DIM), lambda b, h, i: (b, 0, comp * NSA_N_KV + h))

    def vals_t(comp):
        v = kvb[:, :, comp * NSA_KVW:(comp + 1) * NSA_KVW].reshape(b_, T, NSA_N_KV, NSA_HEAD_DIM)
        return jnp.transpose(v, (0, 2, 3, 1))

    q_t = q.astype(bf16).reshape(b_, n_tiles, NSA_TQ, NSA_N_KV, NSA_GQA, NSA_HEAD_DIM)
    q_t = jnp.transpose(q_t, (0, 3, 1, 5, 4, 2)).reshape(b_, NSA_N_KV, n_tiles, NSA_HEAD_DIM, NSA_ROWS)
    gate_t = gate.reshape(b_, n_tiles, NSA_TQ, NSA_N_KV, NSA_GQA, 3)
    gate_t = jnp.transpose(gate_t, (0, 3, 1, 5, 4, 2)).reshape(b_, NSA_N_KV, n_tiles, 3, NSA_ROWS)

    per_head = lambda r, c: pl.BlockSpec((None, None, r, c), lambda b, h, i: (b, h, 0, 0))
    per_tile = lambda r: pl.BlockSpec((None, None, None, r, NSA_ROWS), lambda b, h, i: (b, h, i, 0, 0))
    return pl.pallas_call(
        functools.partial(_nsa_prompt_t_kernel, n_cmp=n_cmp, n_top=min(NSA_N_SELECT, n_sel)),
        grid=(b_, NSA_N_KV, n_tiles),
        in_specs=[per_tile(NSA_HEAD_DIM), per_tile(3),
                  per_head(n_cp, NSA_HEAD_DIM), per_head(NSA_HEAD_DIM, n_cp),
                  keys(2), per_head(NSA_HEAD_DIM, T), keys(4), per_head(NSA_HEAD_DIM, T)],
        out_specs=pl.BlockSpec((None, NSA_TQ, NSA_GQA * NSA_HEAD_DIM), lambda b, h, i: (b, i, h)),
        out_shape=jax.ShapeDtypeStruct((b_, T, NSA_N_HEADS * NSA_HEAD_DIM), bf16),
        scratch_shapes=[pltpu.VMEM((n_sel, NSA_TQ), jnp.float32),
                        pltpu.VMEM((1, NSA_ROWS), jnp.float32),
                        pltpu.VMEM((1, NSA_ROWS), jnp.float32),
                        pltpu.VMEM((NSA_HEAD_DIM, NSA_ROWS), jnp.float32),
                        pltpu.VMEM((NSA_HEAD_DIM, NSA_ROWS), jnp.float32)],
        compiler_params=pltpu.CompilerParams(
            dimension_semantics=("parallel", "parallel", "arbitrary"),
            vmem_limit_bytes=V7X_VMEM_LIMIT_BYTES),
        name="nsa_prompt",
    )(q_t, gate_t, kc, vct, kvb, vals_t(3), kvb, vals_t(5))


def nsa_prompt_mixer(x, w_q, w_kv, w_gate, b_gate, w_cmp1, w_cmp2, cmp_pe, w_out):
    b_, T, _ = x.shape
    q = matmul(x, w_q)
    kv = matmul(x, w_kv)
    comp = lambda c: kv[:, :, c * NSA_KVW:(c + 1) * NSA_KVW].reshape(b_, T, NSA_N_KV, NSA_HEAD_DIM)
    kcmp = nsa_compress(comp(0), w_cmp1[0], w_cmp2[0], cmp_pe[0])
    vcmp = nsa_compress(comp(1), w_cmp1[1], w_cmp2[1], cmp_pe[1])
    gate = jax.nn.sigmoid(matmul(x, w_gate) + b_gate)
    y = Proj(nsa_prompt_attention_t(q, kv, kcmp, vcmp, gate), w_out)
    rows = kv[:, :, :4 * NSA_KVW].reshape(b_, T, 4, NSA_N_KV, NSA_HEAD_DIM)
    keep = min(NSA_WINDOW, T)
    win_new = kv[:, T - keep:, 4 * NSA_KVW:].reshape(b_, keep, 2, NSA_N_KV, NSA_HEAD_DIM)
    return y, rows, win_new


NSA_ROW_SLABS = 4 * NSA_N_KV
NSA_HALF_SLABS = NSA_ROW_SLABS // 2
NSA_KVW = NSA_N_KV * NSA_HEAD_DIM
NSA_CMP_PAGES = 8
NSA_SLC_PAGES = 4


def _round_up(n, m):
    return -(-n // m) * m


def _log2(n):
    assert n > 0 and n & (n - 1) == 0
    return n.bit_length() - 1


def _nsa_compress_kernel(pt_ref, *refs):
    del pt_ref
    pages = refs[:NSA_CMP_PAGES]
    w1_ref, pe_ref, a_ref, b_ref = refs[NSA_CMP_PAGES:]
    page_rows = pages[0].shape[0]
    per_page = page_rows // NSA_CMP_STRIDE
    rows = NSA_CMP_PAGES * NSA_N_KV * per_page
    half = NSA_CMP_STRIDE * NSA_HEAD_DIM
    slabs = [jnp.swapaxes(pg[...], 0, 1) for pg in pages]
    for comp in range(2):
        acc_a = jnp.zeros((rows, NSA_HEAD_DIM), jnp.float32)
        acc_b = jnp.zeros((rows, NSA_HEAD_DIM), jnp.float32)
        by_row = [jnp.swapaxes(s[comp * NSA_N_KV + h].reshape(per_page, NSA_CMP_STRIDE, NSA_HEAD_DIM), 0, 1)
                  for s in slabs for h in range(NSA_N_KV)]
        for j0 in range(0, NSA_CMP_STRIDE, 2):
            xa, xb = [], []
            for j in (j0, j0 + 1):
                x = jnp.concatenate([t[j] for t in by_row], axis=0)
                xa.append((x + pe_ref[comp, j:j + 1, :]).astype(jnp.bfloat16))
                xb.append((x + pe_ref[comp, NSA_CMP_STRIDE + j:NSA_CMP_STRIDE + j + 1, :]).astype(jnp.bfloat16))
            lo = j0 * NSA_HEAD_DIM
            acc_a = acc_a + jnp.dot(jnp.concatenate(xa, axis=1), w1_ref[comp, lo:lo + 2 * NSA_HEAD_DIM, :],
                                    preferred_element_type=jnp.float32)
            acc_b = acc_b + jnp.dot(jnp.concatenate(xb, axis=1),
                                    w1_ref[comp, half + lo:half + lo + 2 * NSA_HEAD_DIM, :],
                                    preferred_element_type=jnp.float32)
        shape = (NSA_CMP_PAGES, NSA_N_KV, per_page, NSA_HEAD_DIM)
        a_ref[comp] = acc_a.reshape(shape)
        b_ref[comp] = acc_b.reshape(shape)


def nsa_decode_compress(cache, page_ids, new_rows, w_cmp1, w_cmp2, cmp_pe):
    b_, n_pages = page_ids.shape
    page = cache.shape[1]
    T = new_rows.shape[1]
    pos0 = n_pages * page
    lp = _round_up(pos0 + T, NSA_SEL_BLOCK)
    n_cmp = lp // NSA_CMP_STRIDE - (NSA_CMP_BLOCK // NSA_CMP_STRIDE - 1)
    per_page = page // NSA_CMP_STRIDE
    assert n_pages % NSA_CMP_PAGES == 0 and NSA_CMP_BLOCK == 2 * NSA_CMP_STRIDE
    w1 = w_cmp1.astype(jnp.bfloat16)

    def page_spec(k):
        return pl.BlockSpec((None, page, NSA_HALF_SLABS, NSA_HEAD_DIM),
                            lambda b, s, pt: (pt[b, NSA_CMP_PAGES * s + k], 0, 0, 0))

    ab_shape = jax.ShapeDtypeStruct((b_, 2, n_pages, NSA_N_KV, per_page, NSA_HEAD_DIM), jnp.float32)
    ab_spec = pl.BlockSpec((None, 2, NSA_CMP_PAGES, NSA_N_KV, per_page, NSA_HEAD_DIM),
                           lambda b, s, pt: (b, 0, s, 0, 0, 0))
    part_a, part_b = pl.pallas_call(
        _nsa_compress_kernel,
        grid_spec=pltpu.PrefetchScalarGridSpec(
            num_scalar_prefetch=1,
            grid=(b_, n_pages // NSA_CMP_PAGES),
            in_specs=[page_spec(k) for k in range(NSA_CMP_PAGES)]
            + [pl.BlockSpec(w1.shape, lambda b, s, pt: (0, 0, 0)),
               pl.BlockSpec(cmp_pe.shape, lambda b, s, pt: (0, 0, 0))],
            out_specs=[ab_spec, ab_spec]),
        out_shape=[ab_shape, ab_shape],
        compiler_params=pltpu.CompilerParams(
            dimension_semantics=("parallel", "arbitrary"),
            vmem_limit_bytes=V7X_VMEM_LIMIT_BYTES),
        name="nsa_compress_pages",
    )(page_ids, *([cache] * NSA_CMP_PAGES), w1, cmp_pe)

    def strides(t):
        return jnp.transpose(t, (0, 1, 3, 2, 4, 5)).reshape(b_, 2, NSA_N_KV, n_pages * per_page, NSA_HEAD_DIM)

    n_tail = (lp - pos0) // NSA_CMP_STRIDE
    tail = jnp.pad(new_rows, ((0, 0), (0, lp - pos0 - T), (0, 0), (0, 0), (0, 0)))
    tail = jnp.transpose(tail.reshape(b_, n_tail, NSA_CMP_STRIDE, 2, NSA_N_KV, NSA_HEAD_DIM), (0, 3, 4, 1, 2, 5))
    w1s = w_cmp1.reshape(2, 2, NSA_CMP_STRIDE, NSA_HEAD_DIM, -1)
    pes = cmp_pe.reshape(2, 2, NSA_CMP_STRIDE, NSA_HEAD_DIM)
    tail_a = jnp.einsum('bchsjd,cjdk->bchsk', tail + pes[None, :, 0, None, None], w1s[:, 0])
    tail_b = jnp.einsum('bchsjd,cjdk->bchsk', tail + pes[None, :, 1, None, None], w1s[:, 1])
    full_a = jnp.concatenate([strides(part_a), tail_a], axis=3)
    full_b = jnp.concatenate([strides(part_b), tail_b], axis=3)
    hidden = jax.nn.gelu(full_a[:, :, :, :n_cmp] + full_b[:, :, :, 1:n_cmp + 1])
    out = jnp.einsum('bchnk,ckd->bcnhd', hidden, w_cmp2).reshape(b_, 2, n_cmp, NSA_KVW)
    return out[:, 0], out[:, 1]


def _nsa_softmax_rows(s, mask):
    s = jnp.where(mask, s, NEG_INF)
    e = jnp.exp(s - jnp.max(s, axis=-1, keepdims=True))
    return e / jnp.sum(e, axis=-1, keepdims=True)


def _nsa_decode_select_kernel(q_ref, kc_ref, vc_ref, wk_ref, wv_ref, ocmp_ref, owin_ref, sel_ref,
                              score_ref, rank_ref, *, n_cmp, n_sel, n_top, n_win, w_buf, pos0, t_new):
    f32, bf16 = jnp.float32, jnp.bfloat16
    nt = (((1,), (1,)), ((), ()))
    scale = NSA_HEAD_DIM ** -0.5
    rows = q_ref.shape[0]
    per_head = NSA_GQA * t_new
    q = q_ref[...]
    tq = pos0 + (lax.broadcasted_iota(jnp.int32, (rows, 1), 0) & (t_new - 1))

    def heads_out(o_ref, p, v_ref):
        for h in range(NSA_N_KV):
            o_ref[h * per_head:(h + 1) * per_head, :] = jnp.dot(
                p[h * per_head:(h + 1) * per_head].astype(bf16),
                v_ref[:, h * NSA_HEAD_DIM:(h + 1) * NSA_HEAD_DIM], preferred_element_type=f32)

    n_cp = kc_ref.shape[0]
    s = lax.dot_general(q, kc_ref[...], nt, preferred_element_type=f32) * scale
    n_idx = lax.broadcasted_iota(jnp.int32, (rows, n_cp), 1)
    cmask = (n_idx * NSA_CMP_STRIDE + (NSA_CMP_BLOCK - 1) <= tq) & (n_idx < n_cmp)
    p = jnp.where(cmask, _nsa_softmax_rows(s, cmask), 0.0)
    heads_out(ocmp_ref, p, vc_ref)

    psum = jnp.concatenate(
        [sum(p[h * per_head + g * t_new:h * per_head + (g + 1) * t_new] for g in range(NSA_GQA))
         for h in range(NSA_N_KV)], axis=0)
    n_sp = sel_ref.shape[1]
    cols = NSA_N_KV * t_new
    jn = lax.broadcasted_iota(jnp.int32, (n_sp, n_cp), 0)
    nn = lax.broadcasted_iota(jnp.int32, (n_sp, n_cp), 1)
    r = NSA_SEL_BLOCK // NSA_CMP_STRIDE
    span = NSA_CMP_BLOCK // NSA_CMP_STRIDE
    pool = jnp.where((nn >= r * jn - (span - 1)) & (nn <= r * jn + (r - 1)), 1.0, 0.0).astype(bf16)
    imp = jnp.zeros((n_sp, cols), f32)
    rest = psum
    for _ in range(3):
        part = rest.astype(bf16)
        imp = imp + lax.dot_general(pool, part, nt, preferred_element_type=f32)
        rest = rest - part.astype(f32)
    jidx = lax.broadcasted_iota(jnp.int32, (n_sp, cols), 0)
    cur = (pos0 + (lax.broadcasted_iota(jnp.int32, (n_sp, cols), 1) & (t_new - 1))) >> NSA_SEL_SHIFT
    forced = (jidx == 0) | (jidx == cur) | (jidx == cur - 1)
    score = jnp.where(forced, FORCE_SCORE, jnp.where(jidx <= cur, imp, -FORCE_SCORE))
    score_ref[...] = jnp.where(jidx < n_sel, score, -2.0 * FORCE_SCORE)
    rank_ref[...] = jnp.zeros(rank_ref.shape, jnp.int32)

    def rank_body(k, carry):
        row = score_ref[pl.ds(k, 1), :]
        sc = score_ref[...]
        before = (row > sc) | ((row == sc) & (k < jidx))
        rank_ref[...] = rank_ref[...] + jnp.where(before, 1, 0)
        return carry

    lax.fori_loop(0, n_sel, rank_body, 0)
    sel_t = jnp.where((rank_ref[...] < n_top) & (jidx < n_sel), 1.0, 0.0).astype(bf16)
    ri = lax.broadcasted_iota(jnp.int32, (rows, cols), 0)
    ci = lax.broadcasted_iota(jnp.int32, (rows, cols), 1)
    same = (((ri >> _log2(per_head)) == (ci >> _log2(t_new)))
            & ((ri & (t_new - 1)) == (ci & (t_new - 1))))
    spread = jnp.where(same, 1.0, 0.0).astype(bf16)
    sel_ref[...] = lax.dot_general(spread, sel_t, nt, preferred_element_type=f32).astype(bf16)

    s = lax.dot_general(q, wk_ref[...], nt, preferred_element_type=f32) * scale
    kidx = lax.broadcasted_iota(jnp.int32, (rows, wk_ref.shape[0]), 1)
    diff = tq - (pos0 - w_buf + kidx)
    wmask = (diff >= 0) & (diff < NSA_WINDOW) & (kidx < n_win) & (pos0 - w_buf + kidx >= 0)
    heads_out(owin_ref, _nsa_softmax_rows(s, wmask), wv_ref)


def _nsa_decode_slc_kernel(pt_ref, q_ref, sel_ref, new_ref, *refs, pos0, t_new):
    del pt_ref
    pages = refs[:NSA_SLC_PAGES]
    o_ref, m_ref, l_ref, acc_ref = refs[NSA_SLC_PAGES:]
    f32, bf16 = jnp.float32, jnp.bfloat16
    nt = (((1,), (1,)), ((), ()))
    step = pl.program_id(1)
    rows = q_ref.shape[0]
    page = pages[0].shape[0]
    per_head = NSA_GQA * t_new
    n_sp = sel_ref.shape[1]
    q = q_ref[...]
    tq = pos0 + (lax.broadcasted_iota(jnp.int32, (rows, 1), 0) & (t_new - 1))
    row_head = lax.broadcasted_iota(jnp.int32, (rows, page), 0) >> _log2(per_head)
    lane = lax.broadcasted_iota(jnp.int32, (rows, page), 1)
    blocks_per_page = page // NSA_SEL_BLOCK

    def attend(pg_ref, page_index):
        slabs = jnp.swapaxes(pg_ref[...], 0, 1)
        slab = lambda c: slabs[c].astype(bf16)
        kp = jnp.concatenate([slab(h) for h in range(NSA_N_KV)], axis=1)
        v_heads = jnp.concatenate([slab(NSA_N_KV + h) for h in range(NSA_N_KV)], axis=0)
        s = lax.dot_general(q, kp, nt, preferred_element_type=f32) * (NSA_HEAD_DIM ** -0.5)
        jrow = lax.broadcasted_iota(jnp.int32, (n_sp, page), 0)
        jcol = page_index * blocks_per_page + (lax.broadcasted_iota(jnp.int32, (n_sp, page), 1) >> NSA_SEL_SHIFT)
        expand = jnp.where(jrow == jcol, 1.0, 0.0).astype(bf16)
        chosen = jnp.dot(sel_ref[...], expand, preferred_element_type=f32) > 0.5
        s = jnp.where(chosen & (page_index * page + lane <= tq), s, NEG_INF)
        m_old = m_ref[...]
        m_new = jnp.maximum(m_old, jnp.max(s, axis=-1, keepdims=True))
        alpha = jnp.exp(m_old - m_new)
        p = jnp.exp(s - m_new)
        l_ref[...] = alpha * l_ref[...] + jnp.sum(p, axis=-1, keepdims=True)
        p_heads = jnp.concatenate([jnp.where(row_head == h, p, 0.0) for h in range(NSA_N_KV)], axis=1).astype(bf16)
        acc_ref[...] = alpha * acc_ref[...] + jnp.dot(p_heads, v_heads, preferred_element_type=f32)
        m_ref[...] = m_new

    @pl.when(step == 0)
    def _():
        m_ref[...] = jnp.full(m_ref.shape, NEG_INF, f32)
        l_ref[...] = jnp.zeros(l_ref.shape, f32)
        acc_ref[...] = jnp.zeros(acc_ref.shape, f32)
        attend(new_ref, pos0 // page)

    for k, pg in enumerate(pages):
        attend(pg, step * NSA_SLC_PAGES + k)

    @pl.when(step == pl.num_programs(1) - 1)
    def _():
        o_ref[...] = acc_ref[...] / l_ref[...]


def nsa_decode_attention(q, kv, cache, page_ids, win_buf, w_cmp1, w_cmp2, cmp_pe):
    b_, T, _ = q.shape
    n_pages = page_ids.shape[1]
    page = cache.shape[1]
    pos0 = n_pages * page
    w_buf = win_buf.shape[1]
    assert T & (T - 1) == 0 and T <= NSA_SEL_BLOCK and pos0 % NSA_SEL_BLOCK == 0 and page % NSA_SEL_BLOCK == 0
    assert n_pages % NSA_SLC_PAGES == 0
    bf16 = jnp.bfloat16
    lp = _round_up(pos0 + T, NSA_SEL_BLOCK)
    n_sel = lp // NSA_SEL_BLOCK
    n_sp = _round_up(n_sel, 128)
    kv6 = kv.reshape(b_, T, 6, NSA_N_KV, NSA_HEAD_DIM)
    kc, vc = nsa_decode_compress(cache, page_ids, kv6[:, :, 0:2], w_cmp1, w_cmp2, cmp_pe)
    n_cmp = kc.shape[1]
    n_cp = _round_up(n_cmp, 128)
    pad_c = ((0, 0), (0, n_cp - n_cmp), (0, 0))
    kc, vc = jnp.pad(kc, pad_c).astype(bf16), jnp.pad(vc, pad_c).astype(bf16)

    rows = NSA_N_HEADS * T
    q5 = jnp.transpose(q.reshape(b_, T, NSA_N_KV, NSA_GQA, NSA_HEAD_DIM), (0, 2, 3, 1, 4))
    q_blk = jnp.einsum('bhgtd,hk->bhgtkd', q5, jnp.eye(NSA_N_KV, dtype=q.dtype))
    q_blk = q_blk.reshape(b_, rows, NSA_KVW).astype(bf16)

    n_win = w_buf + T
    n_wp = _round_up(n_win, 128)
    wk = jnp.concatenate([win_buf, kv6[:, :, 4:6]], axis=1)
    wk = jnp.pad(wk, ((0, 0), (0, n_wp - n_win), (0, 0), (0, 0), (0, 0))).astype(bf16)
    wkk, wkv = wk[:, :, 0].reshape(b_, n_wp, NSA_KVW), wk[:, :, 1].reshape(b_, n_wp, NSA_KVW)

    per_b = lambda n, w: pl.BlockSpec((None, n, w), lambda b: (b, 0, 0))
    o_shape = jax.ShapeDtypeStruct((b_, rows, NSA_HEAD_DIM), jnp.float32)
    o_cmp, o_win, sel = pl.pallas_call(
        functools.partial(_nsa_decode_select_kernel, n_cmp=n_cmp, n_sel=n_sel, n_top=min(NSA_N_SELECT, n_sel),
                          n_win=n_win, w_buf=w_buf, pos0=pos0, t_new=T),
        grid=(b_,),
        in_specs=[per_b(rows, NSA_KVW), per_b(n_cp, NSA_KVW), per_b(n_cp, NSA_KVW),
                  per_b(n_wp, NSA_KVW), per_b(n_wp, NSA_KVW)],
        out_specs=[per_b(rows, NSA_HEAD_DIM), per_b(rows, NSA_HEAD_DIM), per_b(rows, n_sp)],
        out_shape=[o_shape, o_shape, jax.ShapeDtypeStruct((b_, rows, n_sp), bf16)],
        scratch_shapes=[pltpu.VMEM((n_sp, NSA_N_KV * T), jnp.float32),
                        pltpu.VMEM((n_sp, NSA_N_KV * T), jnp.int32)],
        compiler_params=pltpu.CompilerParams(
            dimension_semantics=("parallel",), vmem_limit_bytes=V7X_VMEM_LIMIT_BYTES),
        name="nsa_decode_select",
    )(q_blk, kc, vc, wkk, wkv)

    new_slc = jnp.pad(kv6[:, :, 2:4].reshape(b_, T, NSA_HALF_SLABS, NSA_HEAD_DIM),
                      ((0, 0), (0, page - T), (0, 0), (0, 0)))
    half_page = (None, page, NSA_HALF_SLABS, NSA_HEAD_DIM)

    def page_spec(k):
        return pl.BlockSpec(half_page, lambda b, s, pt: (pt[b, NSA_SLC_PAGES * s + k], 0, 1, 0))

    bs = lambda n, w: pl.BlockSpec((None, n, w), lambda b, s, pt: (b, 0, 0))
    o_slc = pl.pallas_call(
        functools.partial(_nsa_decode_slc_kernel, pos0=pos0, t_new=T),
        grid_spec=pltpu.PrefetchScalarGridSpec(
            num_scalar_prefetch=1,
            grid=(b_, n_pages // NSA_SLC_PAGES),
            in_specs=[bs(rows, NSA_KVW), bs(rows, n_sp), pl.BlockSpec(half_page, lambda b, s, pt: (b, 0, 0, 0))]
            + [page_spec(k) for k in range(NSA_SLC_PAGES)],
            out_specs=bs(rows, NSA_HEAD_DIM),
            scratch_shapes=[pltpu.VMEM((rows, 1), jnp.float32),
                            pltpu.VMEM((rows, 1), jnp.float32),
                            pltpu.VMEM((rows, NSA_HEAD_DIM), jnp.float32)]),
        out_shape=o_shape,
        compiler_params=pltpu.CompilerParams(
            dimension_semantics=("parallel", "arbitrary"), vmem_limit_bytes=V7X_VMEM_LIMIT_BYTES),
        name="nsa_decode_slc",
    )(page_ids, q_blk, sel, new_slc, *([cache] * NSA_SLC_PAGES))

    def token_major(o):
        o = o.reshape(b_, NSA_N_KV, NSA_GQA, T, NSA_HEAD_DIM)
        return jnp.transpose(o, (0, 3, 1, 2, 4)).reshape(b_, T, NSA_N_HEADS * NSA_HEAD_DIM)

    return token_major(o_cmp), token_major(o_slc), token_major(o_win)


def nsa_decode_mixer(x, cache, page_ids, win_buf, w_q, w_kv, w_gate, b_gate, w_cmp1, w_cmp2, cmp_pe, w_out):
    b_, T, _ = x.shape
    q = matmul(x, w_q)
    kv = matmul(x, w_kv)
    kv6 = kv.reshape(b_, T, 6, NSA_N_KV, NSA_HEAD_DIM)
    o_cmp, o_slc, o_win = nsa_decode_attention(q, kv, cache, page_ids, win_buf, w_cmp1, w_cmp2, cmp_pe)
    gate = jax.nn.sigmoid(matmul(x, w_gate) + b_gate).reshape(b_, T, NSA_N_HEADS, 3)

    def heads(t):
        return t.reshape(b_, T, NSA_N_HEADS, NSA_HEAD_DIM)

    o = gate[..., 0:1] * heads(o_cmp) + gate[..., 1:2] * heads(o_slc) + gate[..., 2:3] * heads(o_win)
    y = Proj(o.reshape(b_, T, NSA_N_HEADS * NSA_HEAD_DIM), w_out)
    win_new = jnp.concatenate([win_buf, kv6[:, :, 4:6]], axis=1)[:, -win_buf.shape[1]:]
    return y, kv6[:, :, :4], win_new


def nsa_mixer(x, past_rows, win_buf, pos0, w_q, w_kv, w_gate, b_gate, w_cmp1, w_cmp2, cmp_pe, w_out):
    b_, T, _ = x.shape
    if past_rows is None:
        assert win_buf is None and pos0 == 0 and T % NSA_SEL_BLOCK == 0
        return nsa_prompt_mixer(x, w_q, w_kv, w_gate, b_gate, w_cmp1, w_cmp2, cmp_pe, w_out)
    q = matmul(x, w_q).reshape(b_, T, NSA_N_KV, NSA_GQA, NSA_HEAD_DIM)
    kv = matmul(x, w_kv).reshape(b_, T, 6, NSA_N_KV, NSA_HEAD_DIM)
    rows = kv[:, :, :4]
    full = rows if past_rows is None else jnp.concatenate([past_rows, rows], axis=1)
    q_pos = pos0 + jnp.arange(T)
    o_cmp, o_slc = nsa_cmp_slc(q, full, q_pos, w_cmp1, w_cmp2, cmp_pe)
    win_rows = kv[:, :, 4:6]
    if win_buf is None:
        o_win = nsa_window_prompt(q, win_rows)
        win_new = win_rows[:, -min(NSA_WINDOW, T):]
    else:
        w_b = win_buf.shape[1]
        wk = jnp.concatenate([win_buf, win_rows], axis=1)
        k_pos = pos0 - w_b + jnp.arange(w_b + T)
        o_win = window_attend(q[:, None], wk[:, None, :, 0], wk[:, None, :, 1],
                              q_pos[None], k_pos[None])[:, 0]
        win_new = wk[:, -w_b:]
    gate = jax.nn.sigmoid(matmul(x, w_gate) + b_gate).reshape(b_, T, NSA_N_KV, NSA_GQA, 3)
    o = gate[..., 0:1] * o_cmp + gate[..., 1:2] * o_slc + gate[..., 2:3] * o_win
    return matmul(o.reshape(b_, T, NSA_N_HEADS * NSA_HEAD_DIM), w_out), rows, win_new


FFN_ROW_TILE = 1024
FFN_COL_TILE = 512
FFN_X_HALO = 16


def _ffn_up_kernel(x_ref, xh_ref, hist_ref, wa_ref, wg_ref, cw_ref, cb_ref, h_ref, tail_ref, ext_ref,
                   *, width, tiles_per_seq):
    f32 = jnp.float32
    rows = x_ref.shape[0]
    x = x_ref[...]
    a = jnp.dot(x, wa_ref[...], preferred_element_type=f32)
    g = jnp.dot(x, wg_ref[...], preferred_element_type=f32)
    before = jnp.dot(xh_ref[...], wa_ref[...], preferred_element_type=f32)[FFN_X_HALO - CONV_HALO:]
    starts_seq = pl.program_id(1) % tiles_per_seq == 0
    ext_ref[0:CONV_HALO, :] = jnp.where(starts_seq, hist_ref[...], before)
    ext_ref[CONV_HALO:, :] = a
    acc = cb_ref[...]
    for k in range(width):
        acc = acc + cw_ref[k:k + 1, :] * ext_ref[pl.ds(CONV_HALO - (width - 1 - k), rows), :]
    h_ref[...] = (jax.nn.gelu(acc) * g).astype(h_ref.dtype)
    tail_ref[...] = a[rows - CONV_HALO:]


def ffn_up_fused(x, hist, w_up, conv_w, conv_b):
    b_, T, K = x.shape
    stack, s = w_up
    width = conv_w.shape[0]
    tm, tn = FFN_ROW_TILE, FFN_COL_TILE
    assert T % tm == 0 and FFN_DIM % tn == 0 and width <= CONV_HALO + 1 and stack.shape[2] == 2 * FFN_DIM
    nj = FFN_DIM // tn
    tiles_per_seq = T // tm
    halo_blocks = tm // FFN_X_HALO
    x2 = x.astype(jnp.bfloat16).reshape(b_ * T, K)
    hist8 = jnp.pad(hist, ((0, 0), (CONV_HALO - (width - 1), 0), (0, 0)))
    per_seq = pl.BlockSpec((None, CONV_HALO, tn), lambda j, i: (i // tiles_per_seq, 0, j))
    h, tail = pl.pallas_call(
        functools.partial(_ffn_up_kernel, width=width, tiles_per_seq=tiles_per_seq),
        grid=(nj, b_ * tiles_per_seq),
        in_specs=[pl.BlockSpec((tm, K), lambda j, i: (i, 0)),
                  pl.BlockSpec((FFN_X_HALO, K), lambda j, i: (jnp.maximum(i * halo_blocks - 1, 0), 0)),
                  per_seq,
                  pl.BlockSpec((None, K, tn), lambda j, i: (s, 0, j)),
                  pl.BlockSpec((None, K, tn), lambda j, i: (s, 0, nj + j)),
                  pl.BlockSpec((width, tn), lambda j, i: (0, j)),
                  pl.BlockSpec((1, tn), lambda j, i: (0, j))],
        out_specs=[pl.BlockSpec((tm, tn), lambda j, i: (i, j)), per_seq],
        out_shape=[jax.ShapeDtypeStruct((b_ * T, FFN_DIM), jnp.bfloat16),
                   jax.ShapeDtypeStruct((b_, CONV_HALO, FFN_DIM), jnp.float32)],
        scratch_shapes=[pltpu.VMEM((CONV_HALO + tm, tn), jnp.float32)],
        compiler_params=pltpu.CompilerParams(
            dimension_semantics=("parallel", "arbitrary"),
            vmem_limit_bytes=V7X_VMEM_LIMIT_BYTES),
        name="ffn_up",
    )(x2, x2, hist8, stack, stack, conv_w, conv_b.reshape(1, FFN_DIM))
    return h.reshape(b_, T, FFN_DIM), tail


def conv_ffn(x, hist, w_up, conv_w, conv_b, w_down):
    keep = hist.shape[1]
    if x.shape[1] % FFN_ROW_TILE == 0:
        h, tail = ffn_up_fused(x, hist, w_up, conv_w, conv_b)
        return Proj(h, w_down), tail[:, -keep:]
    ag = matmul(x, w_up)
    h = conv_act(ag, 0, FFN_DIM, hist, conv_w, conv_b, "gelu_gate", gate_col0=FFN_DIM, out_dtype=jnp.bfloat16)
    return Proj(h, w_down), conv_tail(hist, ag, 0, FFN_DIM)


def kernel(x_prompt, x_sample, cache_nsa, state_nsa_win, state_ssd, state_ssd_conv, state_mlstm_c,
           state_mlstm_n, state_mlstm_m, state_mlstm_conv, state_s5, state_ffn_conv, page_table,
           ln_g, ln_b, ffn_w_up, ffn_conv_w, ffn_conv_b, ffn_w_down,
           ssd_w_in, ssd_conv_w, ssd_conv_b, ssd_dt_bias, ssd_a_log, ssd_d, ssd_norm_g, ssd_w_out,
           mlstm_w_up, mlstm_conv_w, mlstm_conv_b, mlstm_w_q, mlstm_w_k, mlstm_w_v, mlstm_w_if,
           mlstm_b_if, mlstm_skip, mlstm_norm_g, mlstm_w_down,
           s5_a_re, s5_a_im, s5_log_dt, s5_b_re, s5_b_im, s5_c_re, s5_c_im, s5_d, s5_w_glu_a, s5_w_glu_b,
           nsa_w_q, nsa_w_kv, nsa_w_gate, nsa_b_gate, nsa_w_cmp1, nsa_w_cmp2, nsa_cmp_pe, nsa_w_out):

    def bf16_stack(w):
        return w.astype(jnp.bfloat16)

    ffn_w_up, ffn_w_down = bf16_stack(ffn_w_up), bf16_stack(ffn_w_down)
    ssd_w_in, ssd_w_out = bf16_stack(ssd_w_in), bf16_stack(ssd_w_out)
    mlstm_w_up, mlstm_w_down = bf16_stack(mlstm_w_up), bf16_stack(mlstm_w_down)
    mlstm_w_if = bf16_stack(mlstm_w_if).reshape(-1, MLSTM_D_INNER, 2 * MLSTM_N_HEADS)
    s5_w_glu_a, s5_w_glu_b = bf16_stack(s5_w_glu_a), bf16_stack(s5_w_glu_b)
    nsa_w_q, nsa_w_kv, nsa_w_out = bf16_stack(nsa_w_q), bf16_stack(nsa_w_kv), bf16_stack(nsa_w_out)
    nsa_w_gate = bf16_stack(nsa_w_gate)

    def trunk(x, sample):
        b_, T, _ = x.shape
        dt_ = x.dtype
        pos0 = PAST_LEN if sample else 0
        o_nsa, o_win, o_ssd, o_ssdc, o_mc, o_mn, o_mm, o_mconv, o_s5, o_ffn = ([] for _ in range(10))
        xb = x.astype(jnp.bfloat16)

        def residual_norm(x, y, g, b):
            if isinstance(y, Proj):
                return matmul_residual_ln(y.h, y.w, x, g, b)
            out = layer_norm(DEEPNORM_ALPHA * x + y, g, b)
            return out, out.astype(jnp.bfloat16)
        for i in range(DEPTH):
            kind, j = i % N_MIXERS, i // N_MIXERS
            if kind == 0:
                hist = state_ssd_conv[j] if sample else jnp.zeros((b_, SSD_CONV_W - 1, SSD_CONV_DIM), dt_)
                h0 = state_ssd[j] if sample else jnp.zeros((b_, SSD_N_HEADS, SSD_HEADDIM, SSD_D_STATE), dt_)
                y, hist_new, h_new = ssd_mixer(xb, hist, h0, (ssd_w_in, j), ssd_conv_w[j], ssd_conv_b[j],
                                               ssd_dt_bias[j], ssd_a_log[j], ssd_d[j], ssd_norm_g[j],
                                               (ssd_w_out, j))
                o_ssd.append(h_new)
                o_ssdc.append(hist_new)
            elif kind == 1:
                hist = state_mlstm_conv[j] if sample else jnp.zeros((b_, MLSTM_CONV_W - 1, MLSTM_D_INNER), dt_)
                c0 = state_mlstm_c[j] if sample else jnp.zeros((b_, MLSTM_N_HEADS, MLSTM_HEAD_DIM, MLSTM_HEAD_DIM), dt_)
                n0 = state_mlstm_n[j] if sample else jnp.zeros((b_, MLSTM_N_HEADS, MLSTM_HEAD_DIM), dt_)
                m0 = state_mlstm_m[j] if sample else jnp.zeros((b_, MLSTM_N_HEADS), dt_)
                y, hist_new, c, n, m = mlstm_mixer(xb, hist, c0, n0, m0, (mlstm_w_up, j), mlstm_conv_w[j],
                                                   mlstm_conv_b[j], mlstm_w_q[j], mlstm_w_k[j], mlstm_w_v[j],
                                                   [(mlstm_w_if, 3 * j + part) for part in range(3)],
                                                   mlstm_b_if[j], mlstm_skip[j],
                                                   mlstm_norm_g[j], (mlstm_w_down, j))
                o_mc.append(c)
                o_mn.append(n)
                o_mm.append(m)
                o_mconv.append(hist_new)
            elif kind == 2:
                h0 = state_s5[j] if sample else jnp.zeros((b_, S5_N_GROUPS, S5_STATE, 2), dt_)
                y, h_new = s5_mixer(x, h0, s5_a_re[j], s5_a_im[j], s5_log_dt[j], s5_b_re[j], s5_b_im[j],
                                    s5_c_re[j], s5_c_im[j], s5_d[j], (s5_w_glu_a, j), (s5_w_glu_b, j))
                o_s5.append(h_new)
            else:
                nsa_w = ((nsa_w_q, j), (nsa_w_kv, j), (nsa_w_gate, j), nsa_b_gate[j], nsa_w_cmp1[j], nsa_w_cmp2[j],
                         nsa_cmp_pe[j], (nsa_w_out, j))
                if sample:
                    n_pool, page = cache_nsa.shape[1:3]
                    assert pos0 == page_table.shape[1] * page
                    y, rows, win_new = nsa_decode_mixer(
                        xb, cache_nsa.reshape(-1, page, NSA_ROW_SLABS, NSA_HEAD_DIM), page_table + j * n_pool,
                        state_nsa_win[j], *nsa_w)
                else:
                    y, rows, win_new = nsa_mixer(xb, None, None, pos0, *nsa_w)
                o_nsa.append(rows)
                o_win.append(win_new)
            x, xb = residual_norm(x, y, ln_g[i, 0], ln_b[i, 0])
            fhist = state_ffn_conv[i] if sample else jnp.zeros((b_, FFN_CONV_W - 1, FFN_DIM), dt_)
            y, fhist_new = conv_ffn(xb, fhist, (ffn_w_up, i), ffn_conv_w[i], ffn_conv_b[i], (ffn_w_down, i))
            o_ffn.append(fhist_new)
            x, xb = residual_norm(x, y, ln_g[i, 1], ln_b[i, 1])
        st = jnp.stack
        return (x, st(o_nsa), st(o_win), st(o_ssd), st(o_ssdc), st(o_mc), st(o_mn), st(o_mm),
                st(o_mconv), st(o_s5), st(o_ffn))

    (y_prompt, nsa_p, win_p, ssd_p, ssdc_p, mc_p, mn_p, mm_p, mconv_p, s5_p, ffn_p) = trunk(x_prompt, False)
    (y_sample, nsa_s, win_s, ssd_s, ssdc_s, mc_s, mn_s, mm_s, mconv_s, s5_s, ffn_s) = trunk(x_sample, True)
    return (y_prompt, y_sample, nsa_p, nsa_s, win_p, win_s, ssd_p, ssd_s, ssdc_p, ssdc_s, mc_p, mc_s,
            mn_p, mn_s, mm_p, mm_s, mconv_p, mconv_s, s5_p, s5_s, ffn_p, ffn_s)
```

```python
import functools
import math
from typing import NamedTuple

import jax
import jax.numpy as jnp
from jax import lax
from jax.experimental import pallas as pl
from jax.experimental.pallas import tpu as pltpu

D_MODEL = 2048
DEPTH = 4
PAST_LEN = 16384
N_MIXERS = 4

DEEPNORM_ALPHA = (2.0 * DEPTH) ** 0.25
LN_EPS = 1e-5
RMS_EPS = 1e-5
NEG_INF = -1e30
FORCE_SCORE = 1e4

SSD_D_INNER = 2 * D_MODEL
SSD_HEADDIM = 64
SSD_N_HEADS = SSD_D_INNER // SSD_HEADDIM
SSD_N_GROUPS = 8
SSD_D_STATE = 128
SSD_CONV_W = 4
SSD_CHUNK = 256
SSD_CONV_DIM = SSD_D_INNER + 2 * SSD_N_GROUPS * SSD_D_STATE

MLSTM_D_INNER = 2 * D_MODEL
MLSTM_N_HEADS = 4
MLSTM_HEAD_DIM = MLSTM_D_INNER // MLSTM_N_HEADS
MLSTM_CONV_W = 4
MLSTM_CHUNK = 64

S5_GROUP = 16
S5_N_GROUPS = D_MODEL // S5_GROUP
S5_STATE = 64

NSA_N_HEADS = 16
NSA_N_KV = 4
NSA_HEAD_DIM = D_MODEL // NSA_N_HEADS
NSA_GQA = NSA_N_HEADS // NSA_N_KV
NSA_CMP_BLOCK = 32
NSA_CMP_STRIDE = 16
NSA_SEL_BLOCK = 64
NSA_N_SELECT = 16
NSA_WINDOW = 512
NSA_QBLOCK = 32
NSA_WBLOCK = 128

FFN_DIM = 5632
FFN_CONV_W = 3

V7X_VMEM_LIMIT_BYTES = 48 * 1024 * 1024


def _mm_kernel(x_ref, w_ref, o_ref):
    o_ref[...] = jnp.dot(x_ref[...], w_ref[...], preferred_element_type=jnp.float32)


def _pick(dim, target):
    if dim <= target:
        return dim
    t = target
    while dim % t:
        t //= 2
    return t


def _mm_tiles(M, K, N):
    tm = _pick(M, 1024)
    tn = N if N <= 512 else (1024 if K <= 2048 else 512)
    double_buffered = 2 * (tm * K * 2 + K * tn * 2 + tm * tn * 4)
    assert double_buffered <= V7X_VMEM_LIMIT_BYTES, (M, K, N)
    return tm, tn


def matmul(x, w):
    stack, s = w if isinstance(w, tuple) else (w[None], 0)
    _, K, N = stack.shape
    lead = x.shape[:-1]
    x2 = x.astype(jnp.bfloat16).reshape(-1, K)
    M = x2.shape[0]
    tm, tn = _mm_tiles(M, K, N)
    out = pl.pallas_call(
        _mm_kernel,
        grid=(M // tm, pl.cdiv(N, tn)),
        in_specs=[pl.BlockSpec((tm, K), lambda i, j: (i, 0)),
                  pl.BlockSpec((None, K, tn), lambda i, j: (s, 0, j))],
        out_specs=pl.BlockSpec((tm, tn), lambda i, j: (i, j)),
        out_shape=jax.ShapeDtypeStruct((M, N), jnp.float32),
        compiler_params=pltpu.CompilerParams(
            dimension_semantics=("parallel", "arbitrary"),
            vmem_limit_bytes=V7X_VMEM_LIMIT_BYTES),
        name="matmul",
    )(x2, stack.astype(jnp.bfloat16))
    return out.reshape(lead + (N,))


class Proj(NamedTuple):
    h: jax.Array
    w: object


LN_ROW_TILE = 512
LN_K_TILE = 1408


def _mm_res_ln_kernel(h_ref, w_ref, x_ref, g_ref, b_ref, o_ref, ob_ref, acc_ref):
    k = pl.program_id(1)

    @pl.when(k == 0)
    def _():
        acc_ref[...] = jnp.zeros_like(acc_ref)

    acc_ref[...] += jnp.dot(h_ref[...], w_ref[...], preferred_element_type=jnp.float32)

    @pl.when(k == pl.num_programs(1) - 1)
    def _():
        z = DEEPNORM_ALPHA * x_ref[...] + acc_ref[...]
        mu = jnp.mean(z, axis=-1, keepdims=True)
        var = jnp.mean(jnp.square(z - mu), axis=-1, keepdims=True)
        out = (z - mu) * lax.rsqrt(var + LN_EPS) * g_ref[...] + b_ref[...]
        o_ref[...] = out
        ob_ref[...] = out.astype(ob_ref.dtype)


def matmul_residual_ln(h, w, x, g, b):
    stack, s = w if isinstance(w, tuple) else (w[None], 0)
    _, K, N = stack.shape
    lead = x.shape[:-1]
    h2 = h.astype(jnp.bfloat16).reshape(-1, K)
    x2 = x.reshape(-1, N)
    M = x2.shape[0]
    tm = _pick(M, LN_ROW_TILE)
    tk = VREG_LANES * max(d for d in range(1, K // VREG_LANES + 1)
                          if (K // VREG_LANES) % d == 0 and VREG_LANES * d <= LN_K_TILE)
    rows = pl.BlockSpec((tm, N), lambda i, k: (i, 0))
    vec = pl.BlockSpec((1, N), lambda i, k: (0, 0))
    out, out_b = pl.pallas_call(
        _mm_res_ln_kernel,
        grid=(M // tm, K // tk),
        in_specs=[pl.BlockSpec((tm, tk), lambda i, k: (i, k)),
                  pl.BlockSpec((None, tk, N), lambda i, k: (s, k, 0)),
                  rows, vec, vec],
        out_specs=[rows, rows],
        out_shape=[jax.ShapeDtypeStruct((M, N), jnp.float32), jax.ShapeDtypeStruct((M, N), jnp.bfloat16)],
        scratch_shapes=[pltpu.VMEM((tm, N), jnp.float32)],
        compiler_params=pltpu.CompilerParams(
            dimension_semantics=("parallel", "arbitrary"),
            vmem_limit_bytes=V7X_VMEM_LIMIT_BYTES),
        name="matmul_residual_ln",
    )(h2, stack.astype(jnp.bfloat16), x2, g.reshape(1, N), b.reshape(1, N))
    return out.reshape(lead + (N,)), out_b.reshape(lead + (N,))


def layer_norm(x, g, b):
    mu = jnp.mean(x, axis=-1, keepdims=True)
    var = jnp.mean(jnp.square(x - mu), axis=-1, keepdims=True)
    return (x - mu) * lax.rsqrt(var + LN_EPS) * g + b


CONV_TIME_TILE = 512
CONV_CHAN_TILE = 2048
QKV_CHAN_TILE = 512
CONV_HALO = 8
VREG_LANES = 128


def _causal_conv_tile(cur_ref, prev_ref, hist_ref, w_ref, b_ref, ext_ref, width):
    rows = cur_ref.shape[0]
    ext_ref[0:CONV_HALO, :] = jnp.where(pl.program_id(1) == 0, hist_ref[...], prev_ref[...])
    ext_ref[CONV_HALO:, :] = cur_ref[...]
    acc = b_ref[...]
    for k in range(width):
        acc = acc + w_ref[k:k + 1, :] * ext_ref[pl.ds(CONV_HALO - (width - 1 - k), rows), :]
    return acc


def _conv_act_kernel(cur_ref, prev_ref, hist_ref, w_ref, b_ref, *rest, width, act):
    acc = _causal_conv_tile(cur_ref, prev_ref, hist_ref, w_ref, b_ref, rest[-1], width)
    if act == "silu":
        o_ref = rest[0]
        o_ref[...] = jax.nn.silu(acc).astype(o_ref.dtype)
    else:
        g_ref, o_ref = rest[:2]
        o_ref[...] = (jax.nn.gelu(acc) * g_ref[...]).astype(o_ref.dtype)


def _conv_tiles(T, chans, offsets, max_lanes):
    lanes = 128
    units = math.gcd(chans // lanes, *[o // lanes for o in offsets])
    ct = lanes * max(d for d in range(1, units + 1) if units % d == 0 and lanes * d <= max_lanes)
    tt = min(T, CONV_TIME_TILE)
    assert T % tt == 0 and tt % CONV_HALO == 0 and chans % lanes == 0 and all(o % lanes == 0 for o in offsets)
    return tt, ct


def conv_act(src, col0, chans, hist, w, b, act, gate_col0=None, out_dtype=jnp.float32):
    b_, T, _ = src.shape
    width = w.shape[0]
    gated = act == "gelu_gate"
    tt, ct = _conv_tiles(T, chans, [col0, gate_col0] if gated else [col0], CONV_CHAN_TILE)
    assert width <= CONV_HALO + 1
    hist8 = jnp.pad(hist, ((0, 0), (CONV_HALO - (width - 1), 0), (0, 0)))
    halo_blocks = tt // CONV_HALO
    cb0 = col0 // ct
    in_specs = [pl.BlockSpec((None, tt, ct), lambda b, t, c: (b, t, cb0 + c)),
                pl.BlockSpec((None, CONV_HALO, ct), lambda b, t, c: (b, jnp.maximum(t * halo_blocks - 1, 0), cb0 + c)),
                pl.BlockSpec((None, CONV_HALO, ct), lambda b, t, c: (b, 0, c)),
                pl.BlockSpec((width, ct), lambda b, t, c: (0, c)),
                pl.BlockSpec((1, ct), lambda b, t, c: (0, c))]
    args = [src, src, hist8, w, b.reshape(1, chans)]
    if gated:
        gb0 = gate_col0 // ct
        in_specs.append(pl.BlockSpec((None, tt, ct), lambda b, t, c: (b, t, gb0 + c)))
        args.append(src)
    return pl.pallas_call(
        functools.partial(_conv_act_kernel, width=width, act=act),
        grid=(b_, T // tt, chans // ct),
        in_specs=in_specs,
        out_specs=pl.BlockSpec((None, tt, ct), lambda b, t, c: (b, t, c)),
        out_shape=jax.ShapeDtypeStruct((b_, T, chans), out_dtype),
        scratch_shapes=[pltpu.VMEM((CONV_HALO + tt, ct), jnp.float32)],
        compiler_params=pltpu.CompilerParams(
            dimension_semantics=("parallel", "parallel", "parallel"),
            vmem_limit_bytes=V7X_VMEM_LIMIT_BYTES),
        name="conv_act",
    )(*args)


def _blockdiag_coefs(w):
    nb, bs, _ = w.shape
    shifts = jnp.stack([jnp.eye(bs, k=d, dtype=w.dtype) for d in range(-(bs - 1), bs)])
    return jnp.einsum('ncd,kcd->knd', w, shifts).reshape(2 * bs - 1, nb * bs)


def _mlstm_qkv_kernel(cur_ref, prev_ref, hist_ref, w_ref, b_ref, cq_ref, ck_ref, cv_ref,
                      xc_ref, q_ref, k_ref, v_ref, ext_ref, *, width, bs):
    xc_ref[...] = jax.nn.silu(_causal_conv_tile(cur_ref, prev_ref, hist_ref, w_ref, b_ref, ext_ref, width))

    def project(x_ref, col, coef_refs, out_refs):
        x = x_ref[:, col]
        outs = [jnp.zeros(x.shape, jnp.float32) for _ in coef_refs]
        for d in range(-(bs - 1), bs):
            moved = x if d == 0 else pltpu.roll(x, d % VREG_LANES, 1)
            row = bs - 1 + d
            outs = [o + c[row:row + 1, col] * moved for o, c in zip(outs, coef_refs)]
        for o, o_ref in zip(outs, out_refs):
            o_ref[:, col] = o.astype(o_ref.dtype)

    for c0 in range(0, cur_ref.shape[1], VREG_LANES):
        col = slice(c0, c0 + VREG_LANES)
        project(xc_ref, col, (cq_ref, ck_ref), (q_ref, k_ref))
        project(cur_ref, col, (cv_ref,), (v_ref,))


def mlstm_qkv(up, hist, conv_w, conv_b, w_q, w_k, w_v):
    b_, T, _ = up.shape
    chans = MLSTM_D_INNER
    width = conv_w.shape[0]
    bs = w_q.shape[1]
    tt, ct = _conv_tiles(T, chans, [0], QKV_CHAN_TILE)
    assert VREG_LANES % bs == 0 and width <= CONV_HALO + 1
    hist8 = jnp.pad(hist, ((0, 0), (CONV_HALO - (width - 1), 0), (0, 0)))
    halo_blocks = tt // CONV_HALO
    tile = pl.BlockSpec((None, tt, ct), lambda b, t, c: (b, t, c))
    coef = pl.BlockSpec((2 * bs - 1, ct), lambda b, t, c: (0, c))
    shape = lambda dt: jax.ShapeDtypeStruct((b_, T, chans), dt)
    return pl.pallas_call(
        functools.partial(_mlstm_qkv_kernel, width=width, bs=bs),
        grid=(b_, T // tt, chans // ct),
        in_specs=[tile,
                  pl.BlockSpec((None, CONV_HALO, ct), lambda b, t, c: (b, jnp.maximum(t * halo_blocks - 1, 0), c)),
                  pl.BlockSpec((None, CONV_HALO, ct), lambda b, t, c: (b, 0, c)),
                  pl.BlockSpec((width, ct), lambda b, t, c: (0, c)),
                  pl.BlockSpec((1, ct), lambda b, t, c: (0, c)),
                  coef, coef, coef],
        out_specs=[tile, tile, tile, tile],
        out_shape=[shape(jnp.float32), shape(jnp.bfloat16), shape(jnp.bfloat16), shape(jnp.bfloat16)],
        scratch_shapes=[pltpu.VMEM((CONV_HALO + tt, ct), jnp.float32)],
        compiler_params=pltpu.CompilerParams(
            dimension_semantics=("parallel", "parallel", "parallel"),
            vmem_limit_bytes=V7X_VMEM_LIMIT_BYTES),
        name="mlstm_qkv",
    )(up, up, hist8, conv_w, conv_b.reshape(1, chans),
      _blockdiag_coefs(w_q), _blockdiag_coefs(w_k), _blockdiag_coefs(w_v))


def conv_tail(hist, src, col0, chans):
    keep = hist.shape[1]
    return jnp.concatenate([hist, src[:, -keep:, col0:col0 + chans]], axis=1)[:, -keep:]


SSD_HEADS_PER_GROUP = SSD_N_HEADS // SSD_N_GROUPS
SSD_GROUP_WIDTH = SSD_HEADS_PER_GROUP * SSD_HEADDIM
SSD_HEAD_SHIFT = SSD_HEADDIM.bit_length() - 1
assert 1 << SSD_HEAD_SHIFT == SSD_HEADDIM and SSD_GROUP_WIDTH == SSD_D_INNER // SSD_N_GROUPS


def _ssd_kernel(x_ref, z_ref, cm_ref, bmt_ref, cols_ref, rows_ref, h0_ref, d_ref, g_ref,
                y_ref, h_out_ref, h_ref):
    f32, bf16 = jnp.float32, jnp.bfloat16
    ck = pl.program_id(2)
    L, W = x_ref.shape
    R = SSD_HEADS_PER_GROUP

    @pl.when(ck == 0)
    def _():
        h_ref[...] = h0_ref[...]

    def per_channel(c):
        spread = jnp.where((lax.broadcasted_iota(jnp.int32, (R, W), 1) >> SSD_HEAD_SHIFT)
                           == lax.broadcasted_iota(jnp.int32, (R, W), 0), 1.0, 0.0).astype(bf16)
        out = jnp.zeros((L, W), f32)
        for _ in range(3):
            part = c.astype(bf16)
            out = out + jnp.dot(part, spread, preferred_element_type=f32)
            c = c - part.astype(f32)
        return out

    x = x_ref[...]
    xb = x.astype(bf16)
    cmb = cm_ref[...].astype(bf16)
    bmt = bmt_ref[...]
    cols = cols_ref[...]
    acs = per_channel(cols[:, :R])
    cb = jnp.dot(cmb, bmt, preferred_element_type=f32)
    causal = (lax.broadcasted_iota(jnp.int32, (L, L), 0) >= lax.broadcasted_iota(jnp.int32, (L, L), 1))
    lane_head = lax.broadcasted_iota(jnp.int32, (L, W), 1) >> SSD_HEAD_SHIFT
    y = jnp.zeros((L, W), f32)
    for r in range(R):
        acs_col = cols[:, r:r + 1]
        acs_row = rows_ref[r:r + 1, :]
        dt_row = rows_ref[SSD_HEADS_PER_GROUP + r:SSD_HEADS_PER_GROUP + r + 1, :]
        decay = jnp.exp(jnp.where(causal, acs_col - acs_row, -jnp.inf))
        w = (cb * decay * dt_row).astype(bf16)
        y = jnp.where(lane_head == r, jnp.dot(w, xb, preferred_element_type=f32), y)
    total = acs[L - 1:L, :]
    xw = (x * (jnp.exp(total - acs) * per_channel(cols[:, R:]))).astype(bf16)
    h_t = h_ref[...]
    y = y + jnp.dot(cmb, h_t.astype(bf16), preferred_element_type=f32) * jnp.exp(acs)
    h_ref[...] = jnp.exp(total) * h_t + jnp.dot(bmt, xw, preferred_element_type=f32)
    y = (y + d_ref[...] * x) * jax.nn.silu(z_ref[...])
    y_ref[...] = y * lax.rsqrt(jnp.mean(y * y, axis=-1, keepdims=True) + RMS_EPS) * g_ref[...]

    @pl.when(ck == pl.num_programs(2) - 1)
    def _():
        h_out_ref[...] = h_ref[...]


def ssd_cell(zx, xbc, x0, dt, a, d_skip, norm_g, h0):
    b_, T, _ = xbc.shape
    G, R, P, N, W = SSD_N_GROUPS, SSD_HEADS_PER_GROUP, SSD_HEADDIM, SSD_D_STATE, SSD_GROUP_WIDTH
    L = math.gcd(T, SSD_CHUNK)
    nc = T // L
    acs = jnp.cumsum((dt * a).reshape(b_, nc, L, SSD_N_HEADS), axis=2)
    dtc = dt.reshape(b_, nc, L, SSD_N_HEADS)

    def rows(t):
        return jnp.transpose(t.reshape(b_, nc, L, G, R), (0, 3, 1, 4, 2))

    def cols(t):
        return jnp.transpose(t.reshape(b_, T, G, R), (0, 2, 1, 3))

    rowpack = jnp.concatenate([rows(acs), rows(dtc)], axis=3)
    colpack = jnp.concatenate([cols(acs), cols(dtc)], axis=3)
    b0 = x0 + SSD_D_INNER
    bmt = jnp.transpose(xbc[..., b0:b0 + G * N].astype(jnp.bfloat16).reshape(b_, T, G, N), (0, 2, 3, 1))
    h0t = jnp.transpose(h0.reshape(b_, G, R, P, N), (0, 1, 4, 2, 3)).reshape(b_, G, N, W)
    assert x0 % W == 0
    chan = lambda b, g, c: (b, c, g)
    x_chan = lambda b, g, c: (b, c, x0 // W + g)
    cm_block0 = (b0 + G * N) // N
    state_spec = pl.BlockSpec((None, None, N, W), lambda b, g, c: (b, g, 0, 0))
    row_spec = pl.BlockSpec((1, W), lambda b, g, c: (0, g))
    y, ht = pl.pallas_call(
        _ssd_kernel,
        grid=(b_, G, nc),
        in_specs=[pl.BlockSpec((None, L, W), x_chan),
                  pl.BlockSpec((None, L, W), chan),
                  pl.BlockSpec((None, L, N), lambda b, g, c: (b, c, cm_block0 + g)),
                  pl.BlockSpec((None, None, N, L), lambda b, g, c: (b, g, 0, c)),
                  pl.BlockSpec((None, None, L, 2 * R), lambda b, g, c: (b, g, c, 0)),
                  pl.BlockSpec((None, None, None, 2 * R, L), lambda b, g, c: (b, g, c, 0, 0)),
                  state_spec, row_spec, row_spec],
        out_specs=[pl.BlockSpec((None, L, W), chan), state_spec],
        out_shape=[jax.ShapeDtypeStruct((b_, T, SSD_D_INNER), jnp.float32),
                   jax.ShapeDtypeStruct((b_, G, N, W), jnp.float32)],
        scratch_shapes=[pltpu.VMEM((N, W), jnp.float32)],
        compiler_params=pltpu.CompilerParams(
            dimension_semantics=("parallel", "parallel", "arbitrary"),
            vmem_limit_bytes=V7X_VMEM_LIMIT_BYTES),
        name="ssd_cell",
    )(xbc, zx, xbc, bmt, colpack, rowpack, h0t,
      jnp.repeat(d_skip, P).reshape(1, SSD_D_INNER), norm_g.reshape(1, SSD_D_INNER))
    h_new = jnp.transpose(ht.reshape(b_, G, N, R, P), (0, 1, 3, 4, 2)).reshape(b_, SSD_N_HEADS, P, N)
    return y, h_new


SSD_IN_ROW_TILE = 1024
SSD_IN_COL_TILE = 1024


def _ssd_in_kernel(x_ref, xh_ref, hist_ref, w_ref, cw_ref, cb_ref, o_ref, tail_ref, ext_ref,
                   *, width, tiles_per_seq, conv_lo, conv_hi):
    f32 = jnp.float32
    j = pl.program_id(0)
    rows = x_ref.shape[0]
    a = jnp.dot(x_ref[...], w_ref[...], preferred_element_type=f32)
    is_conv = (j >= conv_lo) & (j < conv_hi)

    @pl.when(is_conv)
    def _():
        before = jnp.dot(xh_ref[...], w_ref[...], preferred_element_type=f32)[FFN_X_HALO - CONV_HALO:]
        starts_seq = pl.program_id(1) % tiles_per_seq == 0
        ext_ref[0:CONV_HALO, :] = jnp.where(starts_seq, hist_ref[...], before)
        ext_ref[CONV_HALO:, :] = a
        acc = cb_ref[...]
        for k in range(width):
            acc = acc + cw_ref[k:k + 1, :] * ext_ref[pl.ds(CONV_HALO - (width - 1 - k), rows), :]
        o_ref[...] = jax.nn.silu(acc)
        tail_ref[...] = a[rows - CONV_HALO:]

    @pl.when(jnp.logical_not(is_conv))
    def _():
        o_ref[...] = a
        tail_ref[...] = jnp.zeros(tail_ref.shape, f32)


def ssd_in_fused(x, hist, w_in, conv_w, conv_b):
    b_, T, K = x.shape
    stack, s = w_in
    n_out = stack.shape[2]
    width = conv_w.shape[0]
    tm, tn = SSD_IN_ROW_TILE, SSD_IN_COL_TILE
    assert T % tm == 0 and SSD_D_INNER % tn == 0 and SSD_CONV_DIM % tn == 0 and width <= CONV_HALO + 1
    conv_lo, n_conv = SSD_D_INNER // tn, SSD_CONV_DIM // tn
    tiles_per_seq = T // tm
    halo_blocks = tm // FFN_X_HALO
    x2 = x.astype(jnp.bfloat16).reshape(b_ * T, K)
    hist8 = jnp.pad(hist, ((0, 0), (CONV_HALO - (width - 1), 0), (0, 0)))
    conv_col = lambda j: jnp.clip(j - conv_lo, 0, n_conv - 1)
    n_col_tiles = pl.cdiv(n_out, tn)
    zxa, tail = pl.pallas_call(
        functools.partial(_ssd_in_kernel, width=width, tiles_per_seq=tiles_per_seq,
                          conv_lo=conv_lo, conv_hi=conv_lo + n_conv),
        grid=(n_col_tiles, b_ * tiles_per_seq),
        in_specs=[pl.BlockSpec((tm, K), lambda j, i: (i, 0)),
                  pl.BlockSpec((FFN_X_HALO, K), lambda j, i: (jnp.maximum(i * halo_blocks - 1, 0), 0)),
                  pl.BlockSpec((None, CONV_HALO, tn), lambda j, i: (i // tiles_per_seq, 0, conv_col(j))),
                  pl.BlockSpec((None, K, tn), lambda j, i: (s, 0, j)),
                  pl.BlockSpec((width, tn), lambda j, i: (0, conv_col(j))),
                  pl.BlockSpec((1, tn), lambda j, i: (0, conv_col(j)))],
        out_specs=[pl.BlockSpec((tm, tn), lambda j, i: (i, j)),
                   pl.BlockSpec((None, CONV_HALO, tn), lambda j, i: (i // tiles_per_seq, 0, j))],
        out_shape=[jax.ShapeDtypeStruct((b_ * T, n_out), jnp.float32),
                   jax.ShapeDtypeStruct((b_, CONV_HALO, n_col_tiles * tn), jnp.float32)],
        scratch_shapes=[pltpu.VMEM((CONV_HALO + tm, tn), jnp.float32)],
        compiler_params=pltpu.CompilerParams(
            dimension_semantics=("parallel", "arbitrary"),
            vmem_limit_bytes=V7X_VMEM_LIMIT_BYTES),
        name="ssd_in",
    )(x2, x2, hist8, stack, conv_w, conv_b.reshape(1, SSD_CONV_DIM))
    return zxa.reshape(b_, T, n_out), tail[:, :, SSD_D_INNER:SSD_D_INNER + SSD_CONV_DIM]


def ssd_mixer(x, conv_hist, h0, w_in, conv_w, conv_b, dt_bias, a_log, d_skip, norm_g, w_out):
    keep = conv_hist.shape[1]
    if x.shape[1] % SSD_IN_ROW_TILE == 0:
        zx, tail = ssd_in_fused(x, conv_hist, w_in, conv_w, conv_b)
        xbc, x0, hist_new = zx, SSD_D_INNER, tail[:, -keep:]
    else:
        zx = matmul(x, w_in)
        xbc, x0 = conv_act(zx, SSD_D_INNER, SSD_CONV_DIM, conv_hist, conv_w, conv_b, "silu"), 0
        hist_new = conv_tail(conv_hist, zx, SSD_D_INNER, SSD_CONV_DIM)
    dt = jax.nn.softplus(zx[..., SSD_D_INNER + SSD_CONV_DIM:] + dt_bias)
    y, h_new = ssd_cell(zx, xbc, x0, dt, -jnp.exp(a_log), d_skip, norm_g, h0)
    return Proj(y, w_out), hist_new, h_new


MLSTM_STEP = 256
MLSTM_NORM_LANES = 128


def _mlstm_kernel(q_ref, kt_ref, v_ref, acol_ref, grow_ref, c0_ref, n0_ref, m0_ref, g_ref,
                  xc_ref, z_ref, skip_ref, h_ref, c_out_ref, n_out_ref, m_out_ref, c_ref, m_ref):
    f32, bf16 = jnp.float32, jnp.bfloat16
    step = pl.program_id(2)
    L, D = q_ref.shape

    @pl.when(step == 0)
    def _():
        c_ref[:, :D] = c0_ref[...]
        c_ref[:, D:] = n0_ref[...]
        m_ref[...] = m0_ref[...]

    q = q_ref[...]
    kt = kt_ref[...]
    a_col = acol_ref[...]
    g_row = grow_ref[...]
    m_prev = m_ref[:, 0:1]
    causal = (lax.broadcasted_iota(jnp.int32, (L, L), 0) >= lax.broadcasted_iota(jnp.int32, (L, L), 1))
    dmat = jnp.where(causal, a_col + g_row, -jnp.inf)
    inter = m_prev + a_col
    m_t = jnp.maximum(inter, jnp.max(dmat, axis=1, keepdims=True))
    s = jnp.dot(q, kt, preferred_element_type=f32) * jnp.exp(dmat - m_t)
    sc_inter = jnp.exp(inter - m_t)
    qc = jnp.dot(q, c_ref[...].astype(bf16), preferred_element_type=f32)
    num = jnp.dot(s.astype(bf16), v_ref[...], preferred_element_type=f32) + sc_inter * qc[:, :D]
    den = jnp.sum(s, axis=1, keepdims=True) + sc_inter * qc[:, D:D + 1]
    h = num / jnp.maximum(jnp.abs(den), jnp.exp(-m_t))
    mu = jnp.mean(h, axis=-1, keepdims=True)
    var = jnp.mean(jnp.square(h - mu), axis=-1, keepdims=True)
    hn = (h - mu) * lax.rsqrt(var + LN_EPS) * g_ref[...]
    h_ref[...] = ((hn + skip_ref[...] * xc_ref[...]) * jax.nn.silu(z_ref[...])).astype(h_ref.dtype)

    m_new = m_t[L - 1:L, :]
    total = a_col[L - 1:L, :]
    decay = jnp.exp(total + g_row - m_new)
    sc_c = jnp.exp(m_prev + total - m_new)
    ktd = (kt.astype(f32) * decay).astype(bf16)
    one_hot = jnp.where(lax.broadcasted_iota(jnp.int32, (L, MLSTM_NORM_LANES), 1) == 0, 1.0, 0.0).astype(bf16)
    v_ext = jnp.concatenate([v_ref[...], one_hot], axis=1)
    c_ref[...] = sc_c * c_ref[...] + jnp.dot(ktd, v_ext, preferred_element_type=f32)
    m_ref[...] = jnp.broadcast_to(m_new, m_ref.shape)

    @pl.when(step == pl.num_programs(2) - 1)
    def _():
        c_out_ref[...] = c_ref[:, :D]
        n_out_ref[...] = c_ref[:, D:]
        m_out_ref[...] = m_ref[...]


def mlstm_cell(q, k, v, i_pre, logf, c0, n0, m0, norm_g, xc, up, skip):
    b_, T, _ = q.shape
    H, D = MLSTM_N_HEADS, MLSTM_HEAD_DIM
    L = min(T, MLSTM_STEP)
    ns = T // L
    bf16 = jnp.bfloat16
    kt = jnp.swapaxes((k * (D ** -0.5)).astype(bf16).reshape(b_, T, H, D), 1, 3)
    kt = jnp.swapaxes(kt, 1, 2)
    bcum = jnp.cumsum(logf.reshape(b_, ns, L, H), axis=2)
    acol = jnp.moveaxis(bcum, 3, 1).reshape(b_, H, T, 1)
    grow = jnp.moveaxis(i_pre.reshape(b_, ns, L, H) - bcum, 3, 1).reshape(b_, H, ns, 1, L)
    n0e = jnp.pad(n0[..., None], ((0, 0), (0, 0), (0, 0), (0, MLSTM_NORM_LANES - 1)))
    m0e = jnp.broadcast_to(m0[:, :, None, None], (b_, H, 1, MLSTM_NORM_LANES))

    tok_spec = pl.BlockSpec((None, L, D), lambda b, h, s: (b, s, h))
    state = lambda w: pl.BlockSpec((None, None, D, w), lambda b, h, s: (b, h, 0, 0))
    m_spec = pl.BlockSpec((None, None, 1, MLSTM_NORM_LANES), lambda b, h, s: (b, h, 0, 0))
    hn, c, n, m = pl.pallas_call(
        _mlstm_kernel,
        grid=(b_, H, ns),
        in_specs=[tok_spec,
                  pl.BlockSpec((None, None, D, L), lambda b, h, s: (b, h, 0, s)),
                  tok_spec,
                  pl.BlockSpec((None, None, L, 1), lambda b, h, s: (b, h, s, 0)),
                  pl.BlockSpec((None, None, None, 1, L), lambda b, h, s: (b, h, s, 0, 0)),
                  state(D), state(MLSTM_NORM_LANES), m_spec,
                  pl.BlockSpec((1, D), lambda b, h, s: (0, h)),
                  tok_spec,
                  pl.BlockSpec((None, L, D), lambda b, h, s: (b, s, H + h)),
                  pl.BlockSpec((1, D), lambda b, h, s: (0, h))],
        out_specs=[tok_spec, state(D), state(MLSTM_NORM_LANES), m_spec],
        out_shape=[jax.ShapeDtypeStruct((b_, T, H * D), bf16),
                   jax.ShapeDtypeStruct((b_, H, D, D), jnp.float32),
                   jax.ShapeDtypeStruct((b_, H, D, MLSTM_NORM_LANES), jnp.float32),
                   jax.ShapeDtypeStruct((b_, H, 1, MLSTM_NORM_LANES), jnp.float32)],
        scratch_shapes=[pltpu.VMEM((D, D + MLSTM_NORM_LANES), jnp.float32),
                        pltpu.VMEM((1, MLSTM_NORM_LANES), jnp.float32)],
        compiler_params=pltpu.CompilerParams(
            dimension_semantics=("parallel", "parallel", "arbitrary"),
            vmem_limit_bytes=V7X_VMEM_LIMIT_BYTES),
        name="mlstm_cell",
    )(q.astype(bf16), kt, v.astype(bf16), acol, grow, c0, n0e, m0e, norm_g.reshape(1, H * D),
      xc, up, skip.reshape(1, H * D))
    return hn, c, n[..., 0], m[:, :, 0, 0]


def mlstm_mixer(x, conv_hist, c0, n0, m0, w_up, conv_w, conv_b, w_q, w_k, w_v, w_if, b_if,
                skip, norm_g, w_down):
    b_, T, _ = x.shape
    up = matmul(x, w_up)
    xc, q, k, v = mlstm_qkv(up, conv_hist, conv_w, conv_b, w_q, w_k, w_v)
    gates = matmul(q, w_if[0]) + matmul(k, w_if[1]) + matmul(v, w_if[2]) + b_if
    i_pre, f_pre = gates[..., :MLSTM_N_HEADS], gates[..., MLSTM_N_HEADS:]
    h, c, n, m = mlstm_cell(q, k, v, i_pre, jax.nn.log_sigmoid(f_pre), c0, n0, m0, norm_g, xc, up, skip)
    return Proj(h, w_down), conv_tail(conv_hist, up, 0, MLSTM_D_INNER), c, n, m


S5_D_STATE = S5_N_GROUPS * S5_STATE
S5_PACK = 8
S5_N_PACKS = S5_N_GROUPS // S5_PACK
S5_SCAN_ROWS = 8
S5_SCAN_LANES = 256
S5_TIME_TILE = 256
S5_SCAN_UNROLL = 4


def _cmul(ar, ai, br, bi):
    return ar * br - ai * bi, ar * bi + ai * br


def _s5_kernel(x_ref, bre_ref, bim_ref, cre_ref, cim_ref, pw_ref, h0r_ref, h0i_ref, d_ref,
               g_ref, hr_out_ref, hi_out_ref, bur_ref, bui_ref, hr_ref, hi_ref):
    tt = pl.program_id(1)
    rows = x_ref.shape[0]
    pk_in = S5_PACK * S5_GROUP
    pk_st = S5_PACK * S5_STATE

    @pl.when(tt == 0)
    def _():
        hr_ref[...] = jnp.broadcast_to(h0r_ref[...], hr_ref.shape)
        hi_ref[...] = jnp.broadcast_to(h0i_ref[...], hi_ref.shape)

    for c in range(S5_N_PACKS):
        xc = x_ref[:, c * pk_in:(c + 1) * pk_in].astype(jnp.bfloat16)
        bur_ref[:, c * pk_st:(c + 1) * pk_st] = jnp.dot(xc, bre_ref[c], preferred_element_type=jnp.float32)
        bui_ref[:, c * pk_st:(c + 1) * pk_st] = jnp.dot(xc, bim_ref[c], preferred_element_type=jnp.float32)

    def col_body(cb, carry):
        cs = pl.ds(pl.multiple_of(cb * S5_SCAN_LANES, S5_SCAN_LANES), S5_SCAN_LANES)
        stages = [(pw_ref[2 * k, :, cs], pw_ref[2 * k + 1, :, cs], 1 << k) for k in range(3)]
        lr, li = pw_ref[6, :, cs], pw_ref[7, :, cs]

        def row_body(r, h):
            hr, hi = h
            rs = pl.ds(pl.multiple_of(r * S5_SCAN_ROWS, S5_SCAN_ROWS), S5_SCAN_ROWS)
            vr, vi = bur_ref[rs, cs], bui_ref[rs, cs]
            for mr, mi, s in stages:
                pr, pi = _cmul(mr, mi, pltpu.roll(vr, s, 0), pltpu.roll(vi, s, 0))
                vr, vi = vr + pr, vi + pi
            pr, pi = _cmul(lr, li, hr, hi)
            vr, vi = vr + pr, vi + pi
            bur_ref[rs, cs] = vr
            bui_ref[rs, cs] = vi
            last = S5_SCAN_ROWS - 1
            return (jnp.broadcast_to(vr[last:, :], vr.shape), jnp.broadcast_to(vi[last:, :], vi.shape))

        hr, hi = lax.fori_loop(0, rows // S5_SCAN_ROWS, row_body, (hr_ref[:, cs], hi_ref[:, cs]),
                               unroll=min(S5_SCAN_UNROLL, rows // S5_SCAN_ROWS))
        hr_ref[:, cs] = hr
        hi_ref[:, cs] = hi
        return carry

    lax.fori_loop(0, S5_D_STATE // S5_SCAN_LANES, col_body, 0)

    for c in range(S5_N_PACKS):
        hr = bur_ref[:, c * pk_st:(c + 1) * pk_st].astype(jnp.bfloat16)
        hi = bui_ref[:, c * pk_st:(c + 1) * pk_st].astype(jnp.bfloat16)
        y = (jnp.dot(hr, cre_ref[c], preferred_element_type=jnp.float32)
             - jnp.dot(hi, cim_ref[c], preferred_element_type=jnp.float32))
        cols = slice(c * pk_in, (c + 1) * pk_in)
        g_ref[:, cols] = jax.nn.gelu(y + d_ref[:, cols] * x_ref[:, cols])

    @pl.when(tt == pl.num_programs(1) - 1)
    def _():
        hr_out_ref[...] = hr_ref[0:1, :]
        hi_out_ref[...] = hi_ref[0:1, :]


def _block_diag_packs(w):
    g, r, c = w.shape
    eye = jnp.eye(S5_PACK, dtype=w.dtype)
    wb = jnp.einsum('kgrc,gh->kgrhc', w.reshape(g // S5_PACK, S5_PACK, r, c), eye)
    return wb.reshape(g // S5_PACK, S5_PACK * r, S5_PACK * c).astype(jnp.bfloat16)


def s5_mixer(x, h0, a_re, a_im, log_dt, b_re, b_im, c_re, c_im, d_skip, w_glu_a, w_glu_b):
    b_, T, _ = x.shape
    step = jnp.exp(log_dt)[:, None]
    mag = jnp.exp(a_re * step)
    ab_re, ab_im = mag * jnp.cos(a_im * step), mag * jnp.sin(a_im * step)
    den = a_re * a_re + a_im * a_im
    nr, ni = ab_re - 1.0, ab_im
    f_re = (nr * a_re + ni * a_im) / den
    f_im = (ni * a_re - nr * a_im) / den
    bb_re = f_re[..., None] * b_re - f_im[..., None] * b_im
    bb_im = f_re[..., None] * b_im + f_im[..., None] * b_re
    bre = _block_diag_packs(jnp.swapaxes(bb_re, 1, 2))
    bim = _block_diag_packs(jnp.swapaxes(bb_im, 1, 2))
    cre = _block_diag_packs(jnp.swapaxes(c_re, 1, 2))
    cim = _block_diag_packs(jnp.swapaxes(c_im, 1, 2))
    l1 = (ab_re.reshape(-1), ab_im.reshape(-1))
    l2 = _cmul(*l1, *l1)
    l4 = _cmul(*l2, *l2)
    row = jnp.arange(S5_SCAN_ROWS)[:, None]
    pw = []
    for s, (pr, pi) in ((1, l1), (2, l2), (4, l4)):
        pw += [jnp.where(row >= s, pr[None, :], 0.0), jnp.where(row >= s, pi[None, :], 0.0)]
    acc = [l1]
    for _ in range(S5_SCAN_ROWS - 1):
        acc.append(_cmul(*acc[-1], *l1))
    pw += [jnp.stack([a[0] for a in acc]), jnp.stack([a[1] for a in acc])]
    pw = jnp.stack(pw)

    tt = min(T, S5_TIME_TILE)
    h0r = h0[..., 0].reshape(b_, 1, S5_D_STATE)
    h0i = h0[..., 1].reshape(b_, 1, S5_D_STATE)
    pk_in, pk_st = S5_PACK * S5_GROUP, S5_PACK * S5_STATE

    def const3(b, t):
        return (0, 0, 0)

    state_spec = pl.BlockSpec((None, 1, S5_D_STATE), lambda b, t: (b, 0, 0))
    g, hr, hi = pl.pallas_call(
        _s5_kernel,
        grid=(b_, T // tt),
        in_specs=[pl.BlockSpec((None, tt, D_MODEL), lambda b, t: (b, t, 0)),
                  pl.BlockSpec((S5_N_PACKS, pk_in, pk_st), const3),
                  pl.BlockSpec((S5_N_PACKS, pk_in, pk_st), const3),
                  pl.BlockSpec((S5_N_PACKS, pk_st, pk_in), const3),
                  pl.BlockSpec((S5_N_PACKS, pk_st, pk_in), const3),
                  pl.BlockSpec((8, S5_SCAN_ROWS, S5_D_STATE), const3),
                  state_spec, state_spec,
                  pl.BlockSpec((1, D_MODEL), lambda b, t: (0, 0))],
        out_specs=[pl.BlockSpec((None, tt, D_MODEL), lambda b, t: (b, t, 0)), state_spec, state_spec],
        out_shape=[jax.ShapeDtypeStruct((b_, T, D_MODEL), jnp.float32),
                   jax.ShapeDtypeStruct((b_, 1, S5_D_STATE), jnp.float32),
                   jax.ShapeDtypeStruct((b_, 1, S5_D_STATE), jnp.float32)],
        scratch_shapes=[pltpu.VMEM((tt, S5_D_STATE), jnp.float32),
                        pltpu.VMEM((tt, S5_D_STATE), jnp.float32),
                        pltpu.VMEM((S5_SCAN_ROWS, S5_D_STATE), jnp.float32),
                        pltpu.VMEM((S5_SCAN_ROWS, S5_D_STATE), jnp.float32)],
        compiler_params=pltpu.CompilerParams(
            dimension_semantics=("parallel", "arbitrary"),
            vmem_limit_bytes=V7X_VMEM_LIMIT_BYTES),
        name="s5_scan",
    )(x, bre, bim, cre, cim, pw, h0r, h0i, d_skip.reshape(1, D_MODEL))
    out = matmul(g, w_glu_a) * jax.nn.sigmoid(matmul(g, w_glu_b))
    h_new = jnp.stack([hr.reshape(b_, S5_N_GROUPS, S5_STATE), hi.reshape(b_, S5_N_GROUPS, S5_STATE)], axis=-1)
    return out, h_new


def nsa_compress(kv, w1, w2, pe):
    b_, L = kv.shape[:2]
    span = NSA_CMP_BLOCK // NSA_CMP_STRIDE
    n_str = L // NSA_CMP_STRIDE
    n_cmp = n_str - span + 1
    chunks = kv.reshape(b_, n_str, NSA_CMP_STRIDE, NSA_N_KV, NSA_HEAD_DIM)
    blocks = jnp.concatenate([chunks[:, s:s + n_cmp] for s in range(span)], axis=2) + pe[:, None, :]
    flat = jnp.moveaxis(blocks, 3, 2).reshape(b_, n_cmp, NSA_N_KV, NSA_CMP_BLOCK * NSA_HEAD_DIM)
    return jax.nn.gelu(flat @ w1) @ w2


NSA_TQ = NSA_WBLOCK
NSA_SLC_CHUNK = 512
NSA_WIN_CHUNK = 256
NSA_ROWS = NSA_GQA * NSA_TQ
NSA_SEL_SHIFT = NSA_SEL_BLOCK.bit_length() - 1
assert 1 << NSA_SEL_SHIFT == NSA_SEL_BLOCK


def _nsa_stream_softmax_t(q_t, k_ref, vt_ref, c_lo, c_hi, chunk, mask_fn, m_ref, l_ref, acc_ref):
    scale = NSA_HEAD_DIM ** -0.5
    m_ref[...] = jnp.full(m_ref.shape, NEG_INF, jnp.float32)
    l_ref[...] = jnp.zeros(l_ref.shape, jnp.float32)
    acc_ref[...] = jnp.zeros(acc_ref.shape, jnp.float32)

    def body(c, carry):
        start = pl.multiple_of(c * chunk, chunk)
        s = jnp.dot(k_ref[pl.ds(start, chunk), :], q_t, preferred_element_type=jnp.float32) * scale
        s = jnp.where(mask_fn(start), s, NEG_INF)
        m_old = m_ref[...]
        m_new = jnp.maximum(m_old, jnp.max(s, axis=0, keepdims=True))
        alpha = jnp.exp(m_old - m_new)
        p = jnp.exp(s - m_new)
        l_ref[...] = alpha * l_ref[...] + jnp.sum(p, axis=0, keepdims=True)
        acc_ref[...] = alpha * acc_ref[...] + jnp.dot(vt_ref[:, pl.ds(start, chunk)], p.astype(jnp.bfloat16),
                                                      preferred_element_type=jnp.float32)
        m_ref[...] = m_new
        return carry

    lax.fori_loop(c_lo, c_hi, body, 0)
    return acc_ref[...] / l_ref[...]


def _nsa_prompt_t_kernel(qt_ref, gate_ref, kc_ref, vct_ref, ks_ref, vst_ref, kw_ref, vwt_ref,
                         o_ref, sel_ref, m_ref, l_ref, acc_ref, mix_ref, *, n_cmp, n_top):
    f32, bf16 = jnp.float32, jnp.bfloat16
    i = pl.program_id(2)
    t0 = i * NSA_TQ
    n_cp = kc_ref.shape[0]
    n_sel = sel_ref.shape[0]
    q_t = qt_ref[...]
    tq = t0 + (lax.broadcasted_iota(jnp.int32, (1, NSA_ROWS), 1) & (NSA_TQ - 1))

    s = jnp.dot(kc_ref[...], q_t, preferred_element_type=f32) * (NSA_HEAD_DIM ** -0.5)
    n_idx = lax.broadcasted_iota(jnp.int32, (n_cp, NSA_ROWS), 0)
    cmask = (n_idx * NSA_CMP_STRIDE + (NSA_CMP_BLOCK - 1) <= tq) & (n_idx < n_cmp)
    s = jnp.where(cmask, s, NEG_INF)
    e = jnp.exp(s - jnp.max(s, axis=0, keepdims=True))
    p = jnp.where(cmask, e / jnp.sum(e, axis=0, keepdims=True), 0.0)
    mix_ref[...] = gate_ref[0:1, :] * jnp.dot(vct_ref[...], p.astype(bf16), preferred_element_type=f32)

    psum = p[:, 0:NSA_TQ]
    for g in range(1, NSA_GQA):
        psum = psum + p[:, g * NSA_TQ:(g + 1) * NSA_TQ]
    jn = lax.broadcasted_iota(jnp.int32, (n_sel, n_cp), 0)
    nn = lax.broadcasted_iota(jnp.int32, (n_sel, n_cp), 1)
    r = NSA_SEL_BLOCK // NSA_CMP_STRIDE
    span = NSA_CMP_BLOCK // NSA_CMP_STRIDE
    pool = jnp.where((nn >= r * jn - (span - 1)) & (nn <= r * jn + (r - 1)), 1.0, 0.0).astype(bf16)
    imp = jnp.zeros((n_sel, NSA_TQ), f32)
    rest = psum
    for _ in range(3):
        part = rest.astype(bf16)
        imp = imp + jnp.dot(pool, part, preferred_element_type=f32)
        rest = rest - part.astype(f32)

    jidx = lax.broadcasted_iota(jnp.int32, (n_sel, NSA_TQ), 0)
    cur = (t0 + lax.broadcasted_iota(jnp.int32, (n_sel, NSA_TQ), 1)) >> NSA_SEL_SHIFT
    forced = (jidx == 0) | (jidx == cur) | (jidx == cur - 1)
    score = jnp.where(forced, FORCE_SCORE, jnp.where(jidx <= cur, imp, -FORCE_SCORE))
    rank = jnp.zeros((n_sel, NSA_TQ), jnp.int32)
    for k in range(n_sel):
        row = score[k:k + 1, :]
        before = (row > score) | ((row == score) & (k < jidx))
        rank = rank + jnp.where(before, 1, 0)
    sel_ref[...] = jnp.where(rank < n_top, 1.0, 0.0)

    def slc_mask(start):
        kpos = start + lax.broadcasted_iota(jnp.int32, (NSA_SLC_CHUNK, NSA_ROWS), 0)
        first = start >> NSA_SEL_SHIFT
        chosen = jnp.concatenate(
            [jnp.broadcast_to(sel_ref[pl.ds(first + j, 1), :], (NSA_SEL_BLOCK, NSA_TQ))
             for j in range(NSA_SLC_CHUNK // NSA_SEL_BLOCK)], axis=0)
        chosen = jnp.concatenate([chosen] * NSA_GQA, axis=1)
        return (chosen > 0.5) & (kpos <= tq)

    mix_ref[...] += gate_ref[1:2, :] * _nsa_stream_softmax_t(
        q_t, ks_ref, vst_ref, 0, (t0 + NSA_TQ - 1) // NSA_SLC_CHUNK + 1, NSA_SLC_CHUNK, slc_mask,
        m_ref, l_ref, acc_ref)

    def win_mask(start):
        diff = tq - (start + lax.broadcasted_iota(jnp.int32, (NSA_WIN_CHUNK, NSA_ROWS), 0))
        return (diff >= 0) & (diff < NSA_WINDOW)

    mixed = mix_ref[...] + gate_ref[2:3, :] * _nsa_stream_softmax_t(
        q_t, kw_ref, vwt_ref, jnp.maximum(t0 - NSA_WINDOW, 0) // NSA_WIN_CHUNK,
        (t0 + NSA_TQ - 1) // NSA_WIN_CHUNK + 1, NSA_WIN_CHUNK, win_mask, m_ref, l_ref, acc_ref)
    for g in range(NSA_GQA):
        o_ref[:, g * NSA_HEAD_DIM:(g + 1) * NSA_HEAD_DIM] = mixed[:, g * NSA_TQ:(g + 1) * NSA_TQ].T.astype(o_ref.dtype)


def nsa_prompt_attention_t(q, kv, kcmp, vcmp, gate):
    b_, T, _ = q.shape
    assert T % NSA_SLC_CHUNK == 0 and (T // NSA_SEL_BLOCK) % 8 == 0
    n_cmp = kcmp.shape[1]
    n_cp = T // NSA_CMP_STRIDE
    n_sel = T // NSA_SEL_BLOCK
    n_tiles = T // NSA_TQ
    bf16 = jnp.bfloat16
    pad_c = ((0, 0), (0, n_cp - n_cmp), (0, 0), (0, 0))
    kc = jnp.transpose(jnp.pad(kcmp, pad_c).astype(bf16), (0, 2, 1, 3))
    vct = jnp.transpose(jnp.pad(vcmp, pad_c).astype(bf16), (0, 2, 3, 1))
    kvb = kv.astype(bf16)
    keys = lambda comp: pl.BlockSpec((None, T, NSA_HEAD_DIM), lambda b, h, i: (b, 0, comp * NSA_N_KV + h))

    def vals_t(comp):
        v = kvb[:, :, comp * NSA_KVW:(comp + 1) * NSA_KVW].reshape(b_, T, NSA_N_KV, NSA_HEAD_DIM)
        return jnp.transpose(v, (0, 2, 3, 1))

    q_t = q.astype(bf16).reshape(b_, n_tiles, NSA_TQ, NSA_N_KV, NSA_GQA, NSA_HEAD_DIM)
    q_t = jnp.transpose(q_t, (0, 3, 1, 5, 4, 2)).reshape(b_, NSA_N_KV, n_tiles, NSA_HEAD_DIM, NSA_ROWS)
    gate_t = gate.reshape(b_, n_tiles, NSA_TQ, NSA_N_KV, NSA_GQA, 3)
    gate_t = jnp.transpose(gate_t, (0, 3, 1, 5, 4, 2)).reshape(b_, NSA_N_KV, n_tiles, 3, NSA_ROWS)

    per_head = lambda r, c: pl.BlockSpec((None, None, r, c), lambda b, h, i: (b, h, 0, 0))
    per_tile = lambda r: pl.BlockSpec((None, None, None, r, NSA_ROWS), lambda b, h, i: (b, h, i, 0, 0))
    return pl.pallas_call(
        functools.partial(_nsa_prompt_t_kernel, n_cmp=n_cmp, n_top=min(NSA_N_SELECT, n_sel)),
        grid=(b_, NSA_N_KV, n_tiles),
        in_specs=[per_tile(NSA_HEAD_DIM), per_tile(3),
                  per_head(n_cp, NSA_HEAD_DIM), per_head(NSA_HEAD_DIM, n_cp),
                  keys(2), per_head(NSA_HEAD_DIM, T), keys(4), per_head(NSA_HEAD_DIM, T)],
        out_specs=pl.BlockSpec((None, NSA_TQ, NSA_GQA * NSA_HEAD_DIM), lambda b, h, i: (b, i, h)),
        out_shape=jax.ShapeDtypeStruct((b_, T, NSA_N_HEADS * NSA_HEAD_DIM), bf16),
        scratch_shapes=[pltpu.VMEM((n_sel, NSA_TQ), jnp.float32),
                        pltpu.VMEM((1, NSA_ROWS), jnp.float32),
                        pltpu.VMEM((1, NSA_ROWS), jnp.float32),
                        pltpu.VMEM((NSA_HEAD_DIM, NSA_ROWS), jnp.float32),
                        pltpu.VMEM((NSA_HEAD_DIM, NSA_ROWS), jnp.float32)],
        compiler_params=pltpu.CompilerParams(
            dimension_semantics=("parallel", "parallel", "arbitrary"),
            vmem_limit_bytes=V7X_VMEM_LIMIT_BYTES),
        name="nsa_prompt",
    )(q_t, gate_t, kc, vct, kvb, vals_t(3), kvb, vals_t(5))


def nsa_prompt_mixer(x, w_q, w_kv, w_gate, b_gate, w_cmp1, w_cmp2, cmp_pe, w_out):
    b_, T, _ = x.shape
    q = matmul(x, w_q)
    kv = matmul(x, w_kv)
    comp = lambda c: kv[:, :, c * NSA_KVW:(c + 1) * NSA_KVW].reshape(b_, T, NSA_N_KV, NSA_HEAD_DIM)
    kcmp = nsa_compress(comp(0), w_cmp1[0], w_cmp2[0], cmp_pe[0])
    vcmp = nsa_compress(comp(1), w_cmp1[1], w_cmp2[1], cmp_pe[1])
    gate = jax.nn.sigmoid(matmul(x, w_gate) + b_gate)
    y = Proj(nsa_prompt_attention_t(q, kv, kcmp, vcmp, gate), w_out)
    rows = kv[:, :, :4 * NSA_KVW].reshape(b_, T, 4, NSA_N_KV, NSA_HEAD_DIM)
    keep = min(NSA_WINDOW, T)
    win_new = kv[:, T - keep:, 4 * NSA_KVW:].reshape(b_, keep, 2, NSA_N_KV, NSA_HEAD_DIM)
    return y, rows, win_new


NSA_ROW_SLABS = 4 * NSA_N_KV
NSA_HALF_SLABS = NSA_ROW_SLABS // 2
NSA_KVW = NSA_N_KV * NSA_HEAD_DIM
NSA_CMP_PAGES = 8
NSA_SLC_PAGES = 4


def _round_up(n, m):
    return -(-n // m) * m


def _log2(n):
    assert n > 0 and n & (n - 1) == 0
    return n.bit_length() - 1


def _nsa_compress_kernel(pt_ref, *refs):
    del pt_ref
    pages = refs[:NSA_CMP_PAGES]
    w1_ref, pe_ref, a_ref, b_ref = refs[NSA_CMP_PAGES:]
    page_rows = pages[0].shape[0]
    per_page = page_rows // NSA_CMP_STRIDE
    rows = NSA_CMP_PAGES * NSA_N_KV * per_page
    half = NSA_CMP_STRIDE * NSA_HEAD_DIM
    slabs = [jnp.swapaxes(pg[...], 0, 1) for pg in pages]
    for comp in range(2):
        acc_a = jnp.zeros((rows, NSA_HEAD_DIM), jnp.float32)
        acc_b = jnp.zeros((rows, NSA_HEAD_DIM), jnp.float32)
        by_row = [jnp.swapaxes(s[comp * NSA_N_KV + h].reshape(per_page, NSA_CMP_STRIDE, NSA_HEAD_DIM), 0, 1)
                  for s in slabs for h in range(NSA_N_KV)]
        for j0 in range(0, NSA_CMP_STRIDE, 2):
            xa, xb = [], []
            for j in (j0, j0 + 1):
                x = jnp.concatenate([t[j] for t in by_row], axis=0)
                xa.append((x + pe_ref[comp, j:j + 1, :]).astype(jnp.bfloat16))
                xb.append((x + pe_ref[comp, NSA_CMP_STRIDE + j:NSA_CMP_STRIDE + j + 1, :]).astype(jnp.bfloat16))
            lo = j0 * NSA_HEAD_DIM
            acc_a = acc_a + jnp.dot(jnp.concatenate(xa, axis=1), w1_ref[comp, lo:lo + 2 * NSA_HEAD_DIM, :],
                                    preferred_element_type=jnp.float32)
            acc_b = acc_b + jnp.dot(jnp.concatenate(xb, axis=1),
                                    w1_ref[comp, half + lo:half + lo + 2 * NSA_HEAD_DIM, :],
                                    preferred_element_type=jnp.float32)
        shape = (NSA_CMP_PAGES, NSA_N_KV, per_page, NSA_HEAD_DIM)
        a_ref[comp] = acc_a.reshape(shape)
        b_ref[comp] = acc_b.reshape(shape)


def nsa_decode_compress(cache, page_ids, new_rows, w_cmp1, w_cmp2, cmp_pe):
    b_, n_pages = page_ids.shape
    page = cache.shape[1]
    T = new_rows.shape[1]
    pos0 = n_pages * page
    lp = _round_up(pos0 + T, NSA_SEL_BLOCK)
    n_cmp = lp // NSA_CMP_STRIDE - (NSA_CMP_BLOCK // NSA_CMP_STRIDE - 1)
    per_page = page // NSA_CMP_STRIDE
    assert n_pages % NSA_CMP_PAGES == 0 and NSA_CMP_BLOCK == 2 * NSA_CMP_STRIDE
    w1 = w_cmp1.astype(jnp.bfloat16)

    def page_spec(k):
        return pl.BlockSpec((None, page, NSA_HALF_SLABS, NSA_HEAD_DIM),
                            lambda b, s, pt: (pt[b, NSA_CMP_PAGES * s + k], 0, 0, 0))

    ab_shape = jax.ShapeDtypeStruct((b_, 2, n_pages, NSA_N_KV, per_page, NSA_HEAD_DIM), jnp.float32)
    ab_spec = pl.BlockSpec((None, 2, NSA_CMP_PAGES, NSA_N_KV, per_page, NSA_HEAD_DIM),
                           lambda b, s, pt: (b, 0, s, 0, 0, 0))
    part_a, part_b = pl.pallas_call(
        _nsa_compress_kernel,
        grid_spec=pltpu.PrefetchScalarGridSpec(
            num_scalar_prefetch=1,
            grid=(b_, n_pages // NSA_CMP_PAGES),
            in_specs=[page_spec(k) for k in range(NSA_CMP_PAGES)]
            + [pl.BlockSpec(w1.shape, lambda b, s, pt: (0, 0, 0)),
               pl.BlockSpec(cmp_pe.shape, lambda b, s, pt: (0, 0, 0))],
            out_specs=[ab_spec, ab_spec]),
        out_shape=[ab_shape, ab_shape],
        compiler_params=pltpu.CompilerParams(
            dimension_semantics=("parallel", "arbitrary"),
            vmem_limit_bytes=V7X_VMEM_LIMIT_BYTES),
        name="nsa_compress_pages",
    )(page_ids, *([cache] * NSA_CMP_PAGES), w1, cmp_pe)

    def strides(t):
        return jnp.transpose(t, (0, 1, 3, 2, 4, 5)).reshape(b_, 2, NSA_N_KV, n_pages * per_page, NSA_HEAD_DIM)

    n_tail = (lp - pos0) // NSA_CMP_STRIDE
    tail = jnp.pad(new_rows, ((0, 0), (0, lp - pos0 - T), (0, 0), (0, 0), (0, 0)))
    tail = jnp.transpose(tail.reshape(b_, n_tail, NSA_CMP_STRIDE, 2, NSA_N_KV, NSA_HEAD_DIM), (0, 3, 4, 1, 2, 5))
    w1s = w_cmp1.reshape(2, 2, NSA_CMP_STRIDE, NSA_HEAD_DIM, -1)
    pes = cmp_pe.reshape(2, 2, NSA_CMP_STRIDE, NSA_HEAD_DIM)
    tail_a = jnp.einsum('bchsjd,cjdk->bchsk', tail + pes[None, :, 0, None, None], w1s[:, 0])
    tail_b = jnp.einsum('bchsjd,cjdk->bchsk', tail + pes[None, :, 1, None, None], w1s[:, 1])
    full_a = jnp.concatenate([strides(part_a), tail_a], axis=3)
    full_b = jnp.concatenate([strides(part_b), tail_b], axis=3)
    hidden = jax.nn.gelu(full_a[:, :, :, :n_cmp] + full_b[:, :, :, 1:n_cmp + 1])
    out = jnp.einsum('bchnk,ckd->bcnhd', hidden, w_cmp2).reshape(b_, 2, n_cmp, NSA_KVW)
    return out[:, 0], out[:, 1]


def _nsa_softmax_rows(s, mask):
    s = jnp.where(mask, s, NEG_INF)
    e = jnp.exp(s - jnp.max(s, axis=-1, keepdims=True))
    return e / jnp.sum(e, axis=-1, keepdims=True)


def _nsa_decode_select_kernel(q_ref, kc_ref, vc_ref, wk_ref, wv_ref, ocmp_ref, owin_ref, sel_ref,
                              score_ref, rank_ref, *, n_cmp, n_sel, n_top, n_win, w_buf, pos0, t_new):
    f32, bf16 = jnp.float32, jnp.bfloat16
    nt = (((1,), (1,)), ((), ()))
    scale = NSA_HEAD_DIM ** -0.5
    rows = q_ref.shape[0]
    per_head = NSA_GQA * t_new
    q = q_ref[...]
    tq = pos0 + (lax.broadcasted_iota(jnp.int32, (rows, 1), 0) & (t_new - 1))

    def heads_out(o_ref, p, v_ref):
        for h in range(NSA_N_KV):
            o_ref[h * per_head:(h + 1) * per_head, :] = jnp.dot(
                p[h * per_head:(h + 1) * per_head].astype(bf16),
                v_ref[:, h * NSA_HEAD_DIM:(h + 1) * NSA_HEAD_DIM], preferred_element_type=f32)

    n_cp = kc_ref.shape[0]
    s = lax.dot_general(q, kc_ref[...], nt, preferred_element_type=f32) * scale
    n_idx = lax.broadcasted_iota(jnp.int32, (rows, n_cp), 1)
    cmask = (n_idx * NSA_CMP_STRIDE + (NSA_CMP_BLOCK - 1) <= tq) & (n_idx < n_cmp)
    p = jnp.where(cmask, _nsa_softmax_rows(s, cmask), 0.0)
    heads_out(ocmp_ref, p, vc_ref)

    psum = jnp.concatenate(
        [sum(p[h * per_head + g * t_new:h * per_head + (g + 1) * t_new] for g in range(NSA_GQA))
         for h in range(NSA_N_KV)], axis=0)
    n_sp = sel_ref.shape[1]
    cols = NSA_N_KV * t_new
    jn = lax.broadcasted_iota(jnp.int32, (n_sp, n_cp), 0)
    nn = lax.broadcasted_iota(jnp.int32, (n_sp, n_cp), 1)
    r = NSA_SEL_BLOCK // NSA_CMP_STRIDE
    span = NSA_CMP_BLOCK // NSA_CMP_STRIDE
    pool = jnp.where((nn >= r * jn - (span - 1)) & (nn <= r * jn + (r - 1)), 1.0, 0.0).astype(bf16)
    imp = jnp.zeros((n_sp, cols), f32)
    rest = psum
    for _ in range(3):
        part = rest.astype(bf16)
        imp = imp + lax.dot_general(pool, part, nt, preferred_element_type=f32)
        rest = rest - part.astype(f32)
    jidx = lax.broadcasted_iota(jnp.int32, (n_sp, cols), 0)
    cur = (pos0 + (lax.broadcasted_iota(jnp.int32, (n_sp, cols), 1) & (t_new - 1))) >> NSA_SEL_SHIFT
    forced = (jidx == 0) | (jidx == cur) | (jidx == cur - 1)
    score = jnp.where(forced, FORCE_SCORE, jnp.where(jidx <= cur, imp, -FORCE_SCORE))
    score_ref[...] = jnp.where(jidx < n_sel, score, -2.0 * FORCE_SCORE)
    rank_ref[...] = jnp.zeros(rank_ref.shape, jnp.int32)

    def rank_body(k, carry):
        row = score_ref[pl.ds(k, 1), :]
        sc = score_ref[...]
        before = (row > sc) | ((row == sc) & (k < jidx))
        rank_ref[...] = rank_ref[...] + jnp.where(before, 1, 0)
        return carry

    lax.fori_loop(0, n_sel, rank_body, 0)
    sel_t = jnp.where((rank_ref[...] < n_top) & (jidx < n_sel), 1.0, 0.0).astype(bf16)
    ri = lax.broadcasted_iota(jnp.int32, (rows, cols), 0)
    ci = lax.broadcasted_iota(jnp.int32, (rows, cols), 1)
    same = (((ri >> _log2(per_head)) == (ci >> _log2(t_new)))
            & ((ri & (t_new - 1)) == (ci & (t_new - 1))))
    spread = jnp.where(same, 1.0, 0.0).astype(bf16)
    sel_ref[...] = lax.dot_general(spread, sel_t, nt, preferred_element_type=f32).astype(bf16)

    s = lax.dot_general(q, wk_ref[...], nt, preferred_element_type=f32) * scale
    kidx = lax.broadcasted_iota(jnp.int32, (rows, wk_ref.shape[0]), 1)
    diff = tq - (pos0 - w_buf + kidx)
    wmask = (diff >= 0) & (diff < NSA_WINDOW) & (kidx < n_win) & (pos0 - w_buf + kidx >= 0)
    heads_out(owin_ref, _nsa_softmax_rows(s, wmask), wv_ref)


def _nsa_decode_slc_kernel(pt_ref, q_ref, sel_ref, new_ref, *refs, pos0, t_new):
    del pt_ref
    pages = refs[:NSA_SLC_PAGES]
    o_ref, m_ref, l_ref, acc_ref = refs[NSA_SLC_PAGES:]
    f32, bf16 = jnp.float32, jnp.bfloat16
    nt = (((1,), (1,)), ((), ()))
    step = pl.program_id(1)
    rows = q_ref.shape[0]
    page = pages[0].shape[0]
    per_head = NSA_GQA * t_new
    n_sp = sel_ref.shape[1]
    q = q_ref[...]
    tq = pos0 + (lax.broadcasted_iota(jnp.int32, (rows, 1), 0) & (t_new - 1))
    row_head = lax.broadcasted_iota(jnp.int32, (rows, page), 0) >> _log2(per_head)
    lane = lax.broadcasted_iota(jnp.int32, (rows, page), 1)
    blocks_per_page = page // NSA_SEL_BLOCK

    def attend(pg_ref, page_index):
        slabs = jnp.swapaxes(pg_ref[...], 0, 1)
        slab = lambda c: slabs[c].astype(bf16)
        kp = jnp.concatenate([slab(h) for h in range(NSA_N_KV)], axis=1)
        v_heads = jnp.concatenate([slab(NSA_N_KV + h) for h in range(NSA_N_KV)], axis=0)
        s = lax.dot_general(q, kp, nt, preferred_element_type=f32) * (NSA_HEAD_DIM ** -0.5)
        jrow = lax.broadcasted_iota(jnp.int32, (n_sp, page), 0)
        jcol = page_index * blocks_per_page + (lax.broadcasted_iota(jnp.int32, (n_sp, page), 1) >> NSA_SEL_SHIFT)
        expand = jnp.where(jrow == jcol, 1.0, 0.0).astype(bf16)
        chosen = jnp.dot(sel_ref[...], expand, preferred_element_type=f32) > 0.5
        s = jnp.where(chosen & (page_index * page + lane <= tq), s, NEG_INF)
        m_old = m_ref[...]
        m_new = jnp.maximum(m_old, jnp.max(s, axis=-1, keepdims=True))
        alpha = jnp.exp(m_old - m_new)
        p = jnp.exp(s - m_new)
        l_ref[...] = alpha * l_ref[...] + jnp.sum(p, axis=-1, keepdims=True)
        p_heads = jnp.concatenate([jnp.where(row_head == h, p, 0.0) for h in range(NSA_N_KV)], axis=1).astype(bf16)
        acc_ref[...] = alpha * acc_ref[...] + jnp.dot(p_heads, v_heads, preferred_element_type=f32)
        m_ref[...] = m_new

    @pl.when(step == 0)
    def _():
        m_ref[...] = jnp.full(m_ref.shape, NEG_INF, f32)
        l_ref[...] = jnp.zeros(l_ref.shape, f32)
        acc_ref[...] = jnp.zeros(acc_ref.shape, f32)
        attend(new_ref, pos0 // page)

    for k, pg in enumerate(pages):
        attend(pg, step * NSA_SLC_PAGES + k)

    @pl.when(step == pl.num_programs(1) - 1)
    def _():
        o_ref[...] = acc_ref[...] / l_ref[...]


def nsa_decode_attention(q, kv, cache, page_ids, win_buf, w_cmp1, w_cmp2, cmp_pe):
    b_, T, _ = q.shape
    n_pages = page_ids.shape[1]
    page = cache.shape[1]
    pos0 = n_pages * page
    w_buf = win_buf.shape[1]
    assert T & (T - 1) == 0 and T <= NSA_SEL_BLOCK and pos0 % NSA_SEL_BLOCK == 0 and page % NSA_SEL_BLOCK == 0
    assert n_pages % NSA_SLC_PAGES == 0
    bf16 = jnp.bfloat16
    lp = _round_up(pos0 + T, NSA_SEL_BLOCK)
    n_sel = lp // NSA_SEL_BLOCK
    n_sp = _round_up(n_sel, 128)
    kv6 = kv.reshape(b_, T, 6, NSA_N_KV, NSA_HEAD_DIM)
    kc, vc = nsa_decode_compress(cache, page_ids, kv6[:, :, 0:2], w_cmp1, w_cmp2, cmp_pe)
    n_cmp = kc.shape[1]
    n_cp = _round_up(n_cmp, 128)
    pad_c = ((0, 0), (0, n_cp - n_cmp), (0, 0))
    kc, vc = jnp.pad(kc, pad_c).astype(bf16), jnp.pad(vc, pad_c).astype(bf16)

    rows = NSA_N_HEADS * T
    q5 = jnp.transpose(q.reshape(b_, T, NSA_N_KV, NSA_GQA, NSA_HEAD_DIM), (0, 2, 3, 1, 4))
    q_blk = jnp.einsum('bhgtd,hk->bhgtkd', q5, jnp.eye(NSA_N_KV, dtype=q.dtype))
    q_blk = q_blk.reshape(b_, rows, NSA_KVW).astype(bf16)

    n_win = w_buf + T
    n_wp = _round_up(n_win, 128)
    wk = jnp.concatenate([win_buf, kv6[:, :, 4:6]], axis=1)
    wk = jnp.pad(wk, ((0, 0), (0, n_wp - n_win), (0, 0), (0, 0), (0, 0))).astype(bf16)
    wkk, wkv = wk[:, :, 0].reshape(b_, n_wp, NSA_KVW), wk[:, :, 1].reshape(b_, n_wp, NSA_KVW)

    per_b = lambda n, w: pl.BlockSpec((None, n, w), lambda b: (b, 0, 0))
    o_shape = jax.ShapeDtypeStruct((b_, rows, NSA_HEAD_DIM), jnp.float32)
    o_cmp, o_win, sel = pl.pallas_call(
        functools.partial(_nsa_decode_select_kernel, n_cmp=n_cmp, n_sel=n_sel, n_top=min(NSA_N_SELECT, n_sel),
                          n_win=n_win, w_buf=w_buf, pos0=pos0, t_new=T),
        grid=(b_,),
        in_specs=[per_b(rows, NSA_KVW), per_b(n_cp, NSA_KVW), per_b(n_cp, NSA_KVW),
                  per_b(n_wp, NSA_KVW), per_b(n_wp, NSA_KVW)],
        out_specs=[per_b(rows, NSA_HEAD_DIM), per_b(rows, NSA_HEAD_DIM), per_b(rows, n_sp)],
        out_shape=[o_shape, o_shape, jax.ShapeDtypeStruct((b_, rows, n_sp), bf16)],
        scratch_shapes=[pltpu.VMEM((n_sp, NSA_N_KV * T), jnp.float32),
                        pltpu.VMEM((n_sp, NSA_N_KV * T), jnp.int32)],
        compiler_params=pltpu.CompilerParams(
            dimension_semantics=("parallel",), vmem_limit_bytes=V7X_VMEM_LIMIT_BYTES),
        name="nsa_decode_select",
    )(q_blk, kc, vc, wkk, wkv)

    new_slc = jnp.pad(kv6[:, :, 2:4].reshape(b_, T, NSA_HALF_SLABS, NSA_HEAD_DIM),
                      ((0, 0), (0, page - T), (0, 0), (0, 0)))
    half_page = (None, page, NSA_HALF_SLABS, NSA_HEAD_DIM)

    def page_spec(k):
        return pl.BlockSpec(half_page, lambda b, s, pt: (pt[b, NSA_SLC_PAGES * s + k], 0, 1, 0))

    bs = lambda n, w: pl.BlockSpec((None, n, w), lambda b, s, pt: (b, 0, 0))
    o_slc = pl.pallas_call(
        functools.partial(_nsa_decode_slc_kernel, pos0=pos0, t_new=T),
        grid_spec=pltpu.PrefetchScalarGridSpec(
            num_scalar_prefetch=1,
            grid=(b_, n_pages // NSA_SLC_PAGES),
            in_specs=[bs(rows, NSA_KVW), bs(rows, n_sp), pl.BlockSpec(half_page, lambda b, s, pt: (b, 0, 0, 0))]
            + [page_spec(k) for k in range(NSA_SLC_PAGES)],
            out_specs=bs(rows, NSA_HEAD_DIM),
            scratch_shapes=[pltpu.VMEM((rows, 1), jnp.float32),
                            pltpu.VMEM((rows, 1), jnp.float32),
                            pltpu.VMEM((rows, NSA_HEAD_DIM), jnp.float32)]),
        out_shape=o_shape,
        compiler_params=pltpu.CompilerParams(
            dimension_semantics=("parallel", "arbitrary"), vmem_limit_bytes=V7X_VMEM_LIMIT_BYTES),
        name="nsa_decode_slc",
    )(page_ids, q_blk, sel, new_slc, *([cache] * NSA_SLC_PAGES))

    def token_major(o):
        o = o.reshape(b_, NSA_N_KV, NSA_GQA, T, NSA_HEAD_DIM)
        return jnp.transpose(o, (0, 3, 1, 2, 4)).reshape(b_, T, NSA_N_HEADS * NSA_HEAD_DIM)

    return token_major(o_cmp), token_major(o_slc), token_major(o_win)


def nsa_decode_mixer(x, cache, page_ids, win_buf, w_q, w_kv, w_gate, b_gate, w_cmp1, w_cmp2, cmp_pe, w_out):
    b_, T, _ = x.shape
    q = matmul(x, w_q)
    kv = matmul(x, w_kv)
    kv6 = kv.reshape(b_, T, 6, NSA_N_KV, NSA_HEAD_DIM)
    o_cmp, o_slc, o_win = nsa_decode_attention(q, kv, cache, page_ids, win_buf, w_cmp1, w_cmp2, cmp_pe)
    gate = jax.nn.sigmoid(matmul(x, w_gate) + b_gate).reshape(b_, T, NSA_N_HEADS, 3)

    def heads(t):
        return t.reshape(b_, T, NSA_N_HEADS, NSA_HEAD_DIM)

    o = gate[..., 0:1] * heads(o_cmp) + gate[..., 1:2] * heads(o_slc) + gate[..., 2:3] * heads(o_win)
    y = Proj(o.reshape(b_, T, NSA_N_HEADS * NSA_HEAD_DIM), w_out)
    win_new = jnp.concatenate([win_buf, kv6[:, :, 4:6]], axis=1)[:, -win_buf.shape[1]:]
    return y, kv6[:, :, :4], win_new


FFN_ROW_TILE = 1024
FFN_COL_TILE = 512
FFN_X_HALO = 16


def _ffn_up_kernel(x_ref, xh_ref, hist_ref, wa_ref, wg_ref, cw_ref, cb_ref, h_ref, tail_ref, ext_ref,
                   *, width, tiles_per_seq):
    f32 = jnp.float32
    rows = x_ref.shape[0]
    x = x_ref[...]
    a = jnp.dot(x, wa_ref[...], preferred_element_type=f32)
    g = jnp.dot(x, wg_ref[...], preferred_element_type=f32)
    before = jnp.dot(xh_ref[...], wa_ref[...], preferred_element_type=f32)[FFN_X_HALO - CONV_HALO:]
    starts_seq = pl.program_id(1) % tiles_per_seq == 0
    ext_ref[0:CONV_HALO, :] = jnp.where(starts_seq, hist_ref[...], before)
    ext_ref[CONV_HALO:, :] = a
    acc = cb_ref[...]
    for k in range(width):
        acc = acc + cw_ref[k:k + 1, :] * ext_ref[pl.ds(CONV_HALO - (width - 1 - k), rows), :]
    h_ref[...] = (jax.nn.gelu(acc) * g).astype(h_ref.dtype)
    tail_ref[...] = a[rows - CONV_HALO:]


def ffn_up_fused(x, hist, w_up, conv_w, conv_b):
    b_, T, K = x.shape
    stack, s = w_up
    width = conv_w.shape[0]
    tm, tn = FFN_ROW_TILE, FFN_COL_TILE
    assert T % tm == 0 and FFN_DIM % tn == 0 and width <= CONV_HALO + 1 and stack.shape[2] == 2 * FFN_DIM
    nj = FFN_DIM // tn
    tiles_per_seq = T // tm
    halo_blocks = tm // FFN_X_HALO
    x2 = x.astype(jnp.bfloat16).reshape(b_ * T, K)
    hist8 = jnp.pad(hist, ((0, 0), (CONV_HALO - (width - 1), 0), (0, 0)))
    per_seq = pl.BlockSpec((None, CONV_HALO, tn), lambda j, i: (i // tiles_per_seq, 0, j))
    h, tail = pl.pallas_call(
        functools.partial(_ffn_up_kernel, width=width, tiles_per_seq=tiles_per_seq),
        grid=(nj, b_ * tiles_per_seq),
        in_specs=[pl.BlockSpec((tm, K), lambda j, i: (i, 0)),
                  pl.BlockSpec((FFN_X_HALO, K), lambda j, i: (jnp.maximum(i * halo_blocks - 1, 0), 0)),
                  per_seq,
                  pl.BlockSpec((None, K, tn), lambda j, i: (s, 0, j)),
                  pl.BlockSpec((None, K, tn), lambda j, i: (s, 0, nj + j)),
                  pl.BlockSpec((width, tn), lambda j, i: (0, j)),
                  pl.BlockSpec((1, tn), lambda j, i: (0, j))],
        out_specs=[pl.BlockSpec((tm, tn), lambda j, i: (i, j)), per_seq],
        out_shape=[jax.ShapeDtypeStruct((b_ * T, FFN_DIM), jnp.bfloat16),
                   jax.ShapeDtypeStruct((b_, CONV_HALO, FFN_DIM), jnp.float32)],
        scratch_shapes=[pltpu.VMEM((CONV_HALO + tm, tn), jnp.float32)],
        compiler_params=pltpu.CompilerParams(
            dimension_semantics=("parallel", "arbitrary"),
            vmem_limit_bytes=V7X_VMEM_LIMIT_BYTES),
        name="ffn_up",
    )(x2, x2, hist8, stack, stack, conv_w, conv_b.reshape(1, FFN_DIM))
    return h.reshape(b_, T, FFN_DIM), tail


def conv_ffn(x, hist, w_up, conv_w, conv_b, w_down):
    keep = hist.shape[1]
    if x.shape[1] % FFN_ROW_TILE == 0:
        h, tail = ffn_up_fused(x, hist, w_up, conv_w, conv_b)
        return Proj(h, w_down), tail[:, -keep:]
    ag = matmul(x, w_up)
    h = conv_act(ag, 0, FFN_DIM, hist, conv_w, conv_b, "gelu_gate", gate_col0=FFN_DIM, out_dtype=jnp.bfloat16)
    return Proj(h, w_down), conv_tail(hist, ag, 0, FFN_DIM)


def kernel(x_prompt, x_sample, cache_nsa, state_nsa_win, state_ssd, state_ssd_conv, state_mlstm_c,
           state_mlstm_n, state_mlstm_m, state_mlstm_conv, state_s5, state_ffn_conv, page_table,
           ln_g, ln_b, ffn_w_up, ffn_conv_w, ffn_conv_b, ffn_w_down,
           ssd_w_in, ssd_conv_w, ssd_conv_b, ssd_dt_bias, ssd_a_log, ssd_d, ssd_norm_g, ssd_w_out,
           mlstm_w_up, mlstm_conv_w, mlstm_conv_b, mlstm_w_q, mlstm_w_k, mlstm_w_v, mlstm_w_if,
           mlstm_b_if, mlstm_skip, mlstm_norm_g, mlstm_w_down,
           s5_a_re, s5_a_im, s5_log_dt, s5_b_re, s5_b_im, s5_c_re, s5_c_im, s5_d, s5_w_glu_a, s5_w_glu_b,
           nsa_w_q, nsa_w_kv, nsa_w_gate, nsa_b_gate, nsa_w_cmp1, nsa_w_cmp2, nsa_cmp_pe, nsa_w_out):

    def bf16_stack(w):
        return w.astype(jnp.bfloat16)

    ffn_w_up, ffn_w_down = bf16_stack(ffn_w_up), bf16_stack(ffn_w_down)
    ssd_w_in, ssd_w_out = bf16_stack(ssd_w_in), bf16_stack(ssd_w_out)
    mlstm_w_up, mlstm_w_down = bf16_stack(mlstm_w_up), bf16_stack(mlstm_w_down)
    mlstm_w_if = bf16_stack(mlstm_w_if).reshape(-1, MLSTM_D_INNER, 2 * MLSTM_N_HEADS)
    s5_w_glu_a, s5_w_glu_b = bf16_stack(s5_w_glu_a), bf16_stack(s5_w_glu_b)
    nsa_w_q, nsa_w_kv, nsa_w_out = bf16_stack(nsa_w_q), bf16_stack(nsa_w_kv), bf16_stack(nsa_w_out)
    nsa_w_gate = bf16_stack(nsa_w_gate)

    def trunk(x, sample):
        b_, T, _ = x.shape
        dt_ = x.dtype
        pos0 = PAST_LEN if sample else 0
        o_nsa, o_win, o_ssd, o_ssdc, o_mc, o_mn, o_mm, o_mconv, o_s5, o_ffn = ([] for _ in range(10))
        xb = x.astype(jnp.bfloat16)

        def residual_norm(x, y, g, b):
            if isinstance(y, Proj):
                return matmul_residual_ln(y.h, y.w, x, g, b)
            out = layer_norm(DEEPNORM_ALPHA * x + y, g, b)
            return out, out.astype(jnp.bfloat16)
        for i in range(DEPTH):
            kind, j = i % N_MIXERS, i // N_MIXERS
            if kind == 0:
                hist = state_ssd_conv[j] if sample else jnp.zeros((b_, SSD_CONV_W - 1, SSD_CONV_DIM), dt_)
                h0 = state_ssd[j] if sample else jnp.zeros((b_, SSD_N_HEADS, SSD_HEADDIM, SSD_D_STATE), dt_)
                y, hist_new, h_new = ssd_mixer(xb, hist, h0, (ssd_w_in, j), ssd_conv_w[j], ssd_conv_b[j],
                                               ssd_dt_bias[j], ssd_a_log[j], ssd_d[j], ssd_norm_g[j],
                                               (ssd_w_out, j))
                o_ssd.append(h_new)
                o_ssdc.append(hist_new)
            elif kind == 1:
                hist = state_mlstm_conv[j] if sample else jnp.zeros((b_, MLSTM_CONV_W - 1, MLSTM_D_INNER), dt_)
                c0 = state_mlstm_c[j] if sample else jnp.zeros((b_, MLSTM_N_HEADS, MLSTM_HEAD_DIM, MLSTM_HEAD_DIM), dt_)
                n0 = state_mlstm_n[j] if sample else jnp.zeros((b_, MLSTM_N_HEADS, MLSTM_HEAD_DIM), dt_)
                m0 = state_mlstm_m[j] if sample else jnp.zeros((b_, MLSTM_N_HEADS), dt_)
                y, hist_new, c, n, m = mlstm_mixer(xb, hist, c0, n0, m0, (mlstm_w_up, j), mlstm_conv_w[j],
                                                   mlstm_conv_b[j], mlstm_w_q[j], mlstm_w_k[j], mlstm_w_v[j],
                                                   [(mlstm_w_if, 3 * j + part) for part in range(3)],
                                                   mlstm_b_if[j], mlstm_skip[j],
                                                   mlstm_norm_g[j], (mlstm_w_down, j))
                o_mc.append(c)
                o_mn.append(n)
                o_mm.append(m)
                o_mconv.append(hist_new)
            elif kind == 2:
                h0 = state_s5[j] if sample else jnp.zeros((b_, S5_N_GROUPS, S5_STATE, 2), dt_)
                y, h_new = s5_mixer(x, h0, s5_a_re[j], s5_a_im[j], s5_log_dt[j], s5_b_re[j], s5_b_im[j],
                                    s5_c_re[j], s5_c_im[j], s5_d[j], (s5_w_glu_a, j), (s5_w_glu_b, j))
                o_s5.append(h_new)
            else:
                nsa_w = ((nsa_w_q, j), (nsa_w_kv, j), (nsa_w_gate, j), nsa_b_gate[j], nsa_w_cmp1[j], nsa_w_cmp2[j],
                         nsa_cmp_pe[j], (nsa_w_out, j))
                if sample:
                    n_pool, page = cache_nsa.shape[1:3]
                    assert pos0 == page_table.shape[1] * page
                    y, rows, win_new = nsa_decode_mixer(
                        xb, cache_nsa.reshape(-1, page, NSA_ROW_SLABS, NSA_HEAD_DIM), page_table + j * n_pool,
                        state_nsa_win[j], *nsa_w)
                else:
                    assert pos0 == 0
                    y, rows, win_new = nsa_prompt_mixer(xb, *nsa_w)
                o_nsa.append(rows)
                o_win.append(win_new)
            x, xb = residual_norm(x, y, ln_g[i, 0], ln_b[i, 0])
            fhist = state_ffn_conv[i] if sample else jnp.zeros((b_, FFN_CONV_W - 1, FFN_DIM), dt_)
            y, fhist_new = conv_ffn(xb, fhist, (ffn_w_up, i), ffn_conv_w[i], ffn_conv_b[i], (ffn_w_down, i))
            o_ffn.append(fhist_new)
            x, xb = residual_norm(x, y, ln_g[i, 1], ln_b[i, 1])
        st = jnp.stack
        return (x, st(o_nsa), st(o_win), st(o_ssd), st(o_ssdc), st(o_mc), st(o_mn), st(o_mm),
                st(o_mconv), st(o_s5), st(o_ffn))

    (y_prompt, nsa_p, win_p, ssd_p, ssdc_p, mc_p, mn_p, mm_p, mconv_p, s5_p, ffn_p) = trunk(x_prompt, False)
    (y_sample, nsa_s, win_s, ssd_s, ssdc_s, mc_s, mn_s, mm_s, mconv_s, s5_s, ffn_s) = trunk(x_sample, True)
    return (y_prompt, y_sample, nsa_p, nsa_s, win_p, win_s, ssd_p, ssd_s, ssdc_p, ssdc_s, mc_p, mc_s,
            mn_p, mn_s, mm_p, mm_s, mconv_p, mconv_s, s5_p, s5_s, ffn_p, ffn_s)
```

```python
import functools
import math
from typing import NamedTuple

import jax
import jax.numpy as jnp
from jax import lax
from jax.experimental import pallas as pl
from jax.experimental.pallas import tpu as pltpu

D_MODEL = 2048
DEPTH = 4
PAST_LEN = 16384
N_MIXERS = 4

DEEPNORM_ALPHA = (2.0 * DEPTH) ** 0.25
LN_EPS = 1e-5
RMS_EPS = 1e-5
NEG_INF = -1e30
FORCE_SCORE = 1e4

SSD_D_INNER = 2 * D_MODEL
SSD_HEADDIM = 64
SSD_N_HEADS = SSD_D_INNER // SSD_HEADDIM
SSD_N_GROUPS = 8
SSD_D_STATE = 128
SSD_CONV_W = 4
SSD_CHUNK = 256
SSD_CONV_DIM = SSD_D_INNER + 2 * SSD_N_GROUPS * SSD_D_STATE

MLSTM_D_INNER = 2 * D_MODEL
MLSTM_N_HEADS = 4
MLSTM_HEAD_DIM = MLSTM_D_INNER // MLSTM_N_HEADS
MLSTM_CONV_W = 4
MLSTM_CHUNK = 64

S5_GROUP = 16
S5_N_GROUPS = D_MODEL // S5_GROUP
S5_STATE = 64

NSA_N_HEADS = 16
NSA_N_KV = 4
NSA_HEAD_DIM = D_MODEL // NSA_N_HEADS
NSA_GQA = NSA_N_HEADS // NSA_N_KV
NSA_CMP_BLOCK = 32
NSA_CMP_STRIDE = 16
NSA_SEL_BLOCK = 64
NSA_N_SELECT = 16
NSA_WINDOW = 512
NSA_QBLOCK = 32
NSA_WBLOCK = 128

FFN_DIM = 5632
FFN_CONV_W = 3

V7X_VMEM_LIMIT_BYTES = 48 * 1024 * 1024


def _mm_kernel(x_ref, w_ref, o_ref):
    o_ref[...] = jnp.dot(x_ref[...], w_ref[...], preferred_element_type=jnp.float32)


def _pick(dim, target):
    if dim <= target:
        return dim
    t = target
    while dim % t:
        t //= 2
    return t


def _mm_tiles(M, K, N):
    tm = _pick(M, 1024)
    tn = N if N <= 512 else (1024 if K <= 2048 else 512)
    double_buffered = 2 * (tm * K * 2 + K * tn * 2 + tm * tn * 4)
    assert double_buffered <= V7X_VMEM_LIMIT_BYTES, (M, K, N)
    return tm, tn


def matmul(x, w):
    stack, s = w if isinstance(w, tuple) else (w[None], 0)
    _, K, N = stack.shape
    lead = x.shape[:-1]
    x2 = x.astype(jnp.bfloat16).reshape(-1, K)
    M = x2.shape[0]
    tm, tn = _mm_tiles(M, K, N)
    out = pl.pallas_call(
        _mm_kernel,
        grid=(M // tm, pl.cdiv(N, tn)),
        in_specs=[pl.BlockSpec((tm, K), lambda i, j: (i, 0)),
                  pl.BlockSpec((None, K, tn), lambda i, j: (s, 0, j))],
        out_specs=pl.BlockSpec((tm, tn), lambda i, j: (i, j)),
        out_shape=jax.ShapeDtypeStruct((M, N), jnp.float32),
        compiler_params=pltpu.CompilerParams(
            dimension_semantics=("parallel", "arbitrary"),
            vmem_limit_bytes=V7X_VMEM_LIMIT_BYTES),
        name="matmul",
    )(x2, stack.astype(jnp.bfloat16))
    return out.reshape(lead + (N,))


class Proj(NamedTuple):
    h: jax.Array
    w: object


LN_ROW_TILE = 512
LN_K_TILE = 1408


def _mm_res_ln_kernel(h_ref, w_ref, x_ref, g_ref, b_ref, o_ref, ob_ref, acc_ref):
    k = pl.program_id(1)

    @pl.when(k == 0)
    def _():
        acc_ref[...] = jnp.zeros_like(acc_ref)

    acc_ref[...] += jnp.dot(h_ref[...], w_ref[...], preferred_element_type=jnp.float32)

    @pl.when(k == pl.num_programs(1) - 1)
    def _():
        z = DEEPNORM_ALPHA * x_ref[...] + acc_ref[...]
        mu = jnp.mean(z, axis=-1, keepdims=True)
        var = jnp.mean(jnp.square(z - mu), axis=-1, keepdims=True)
        out = (z - mu) * lax.rsqrt(var + LN_EPS) * g_ref[...] + b_ref[...]
        o_ref[...] = out
        ob_ref[...] = out.astype(ob_ref.dtype)


def matmul_residual_ln(h, w, x, g, b):
    stack, s = w if isinstance(w, tuple) else (w[None], 0)
    _, K, N = stack.shape
    lead = x.shape[:-1]
    h2 = h.astype(jnp.bfloat16).reshape(-1, K)
    x2 = x.reshape(-1, N)
    M = x2.shape[0]
    tm = _pick(M, LN_ROW_TILE)
    tk = VREG_LANES * max(d for d in range(1, K // VREG_LANES + 1)
                          if (K // VREG_LANES) % d == 0 and VREG_LANES * d <= LN_K_TILE)
    rows = pl.BlockSpec((tm, N), lambda i, k: (i, 0))
    vec = pl.BlockSpec((1, N), lambda i, k: (0, 0))
    out, out_b = pl.pallas_call(
        _mm_res_ln_kernel,
        grid=(M // tm, K // tk),
        in_specs=[pl.BlockSpec((tm, tk), lambda i, k: (i, k)),
                  pl.BlockSpec((None, tk, N), lambda i, k: (s, k, 0)),
                  rows, vec, vec],
        out_specs=[rows, rows],
        out_shape=[jax.ShapeDtypeStruct((M, N), jnp.float32), jax.ShapeDtypeStruct((M, N), jnp.bfloat16)],
        scratch_shapes=[pltpu.VMEM((tm, N), jnp.float32)],
        compiler_params=pltpu.CompilerParams(
            dimension_semantics=("parallel", "arbitrary"),
            vmem_limit_bytes=V7X_VMEM_LIMIT_BYTES),
        name="matmul_residual_ln",
    )(h2, stack.astype(jnp.bfloat16), x2, g.reshape(1, N), b.reshape(1, N))
    return out.reshape(lead + (N,)), out_b.reshape(lead + (N,))


def layer_norm(x, g, b):
    mu = jnp.mean(x, axis=-1, keepdims=True)
    var = jnp.mean(jnp.square(x - mu), axis=-1, keepdims=True)
    return (x - mu) * lax.rsqrt(var + LN_EPS) * g + b


CONV_TIME_TILE = 512
CONV_CHAN_TILE = 2048
QKV_CHAN_TILE = 512
CONV_HALO = 8
VREG_LANES = 128


def _causal_conv_tile(cur_ref, prev_ref, hist_ref, w_ref, b_ref, ext_ref, width):
    rows = cur_ref.shape[0]
    ext_ref[0:CONV_HALO, :] = jnp.where(pl.program_id(1) == 0, hist_ref[...], prev_ref[...])
    ext_ref[CONV_HALO:, :] = cur_ref[...]
    acc = b_ref[...]
    for k in range(width):
        acc = acc + w_ref[k:k + 1, :] * ext_ref[pl.ds(CONV_HALO - (width - 1 - k), rows), :]
    return acc


def _conv_act_kernel(cur_ref, prev_ref, hist_ref, w_ref, b_ref, *rest, width, act):
    acc = _causal_conv_tile(cur_ref, prev_ref, hist_ref, w_ref, b_ref, rest[-1], width)
    if act == "silu":
        o_ref = rest[0]
        o_ref[...] = jax.nn.silu(acc).astype(o_ref.dtype)
    else:
        g_ref, o_ref = rest[:2]
        o_ref[...] = (jax.nn.gelu(acc) * g_ref[...]).astype(o_ref.dtype)


def _conv_tiles(T, chans, offsets, max_lanes):
    lanes = 128
    units = math.gcd(chans // lanes, *[o // lanes for o in offsets])
    ct = lanes * max(d for d in range(1, units + 1) if units % d == 0 and lanes * d <= max_lanes)
    tt = min(T, CONV_TIME_TILE)
    assert T % tt == 0 and tt % CONV_HALO == 0 and chans % lanes == 0 and all(o % lanes == 0 for o in offsets)
    return tt, ct


def conv_act(src, col0, chans, hist, w, b, act, gate_col0=None, out_dtype=jnp.float32):
    b_, T, _ = src.shape
    width = w.shape[0]
    gated = act == "gelu_gate"
    tt, ct = _conv_tiles(T, chans, [col0, gate_col0] if gated else [col0], CONV_CHAN_TILE)
    assert width <= CONV_HALO + 1
    hist8 = jnp.pad(hist, ((0, 0), (CONV_HALO - (width - 1), 0), (0, 0)))
    halo_blocks = tt // CONV_HALO
    cb0 = col0 // ct
    in_specs = [pl.BlockSpec((None, tt, ct), lambda b, t, c: (b, t, cb0 + c)),
                pl.BlockSpec((None, CONV_HALO, ct), lambda b, t, c: (b, jnp.maximum(t * halo_blocks - 1, 0), cb0 + c)),
                pl.BlockSpec((None, CONV_HALO, ct), lambda b, t, c: (b, 0, c)),
                pl.BlockSpec((width, ct), lambda b, t, c: (0, c)),
                pl.BlockSpec((1, ct), lambda b, t, c: (0, c))]
    args = [src, src, hist8, w, b.reshape(1, chans)]
    if gated:
        gb0 = gate_col0 // ct
        in_specs.append(pl.BlockSpec((None, tt, ct), lambda b, t, c: (b, t, gb0 + c)))
        args.append(src)
    return pl.pallas_call(
        functools.partial(_conv_act_kernel, width=width, act=act),
        grid=(b_, T // tt, chans // ct),
        in_specs=in_specs,
        out_specs=pl.BlockSpec((None, tt, ct), lambda b, t, c: (b, t, c)),
        out_shape=jax.ShapeDtypeStruct((b_, T, chans), out_dtype),
        scratch_shapes=[pltpu.VMEM((CONV_HALO + tt, ct), jnp.float32)],
        compiler_params=pltpu.CompilerParams(
            dimension_semantics=("parallel", "parallel", "parallel"),
            vmem_limit_bytes=V7X_VMEM_LIMIT_BYTES),
        name="conv_act",
    )(*args)


def _blockdiag_coefs(w):
    nb, bs, _ = w.shape
    shifts = jnp.stack([jnp.eye(bs, k=d, dtype=w.dtype) for d in range(-(bs - 1), bs)])
    return jnp.einsum('ncd,kcd->knd', w, shifts).reshape(2 * bs - 1, nb * bs)


def _mlstm_qkv_kernel(cur_ref, prev_ref, hist_ref, w_ref, b_ref, cq_ref, ck_ref, cv_ref,
                      xc_ref, q_ref, k_ref, v_ref, ext_ref, *, width, bs):
    xc_ref[...] = jax.nn.silu(_causal_conv_tile(cur_ref, prev_ref, hist_ref, w_ref, b_ref, ext_ref, width))

    def project(x_ref, col, coef_refs, out_refs):
        x = x_ref[:, col]
        outs = [jnp.zeros(x.shape, jnp.float32) for _ in coef_refs]
        for d in range(-(bs - 1), bs):
            moved = x if d == 0 else pltpu.roll(x, d % VREG_LANES, 1)
            row = bs - 1 + d
            outs = [o + c[row:row + 1, col] * moved for o, c in zip(outs, coef_refs)]
        for o, o_ref in zip(outs, out_refs):
            o_ref[:, col] = o.astype(o_ref.dtype)

    for c0 in range(0, cur_ref.shape[1], VREG_LANES):
        col = slice(c0, c0 + VREG_LANES)
        project(xc_ref, col, (cq_ref, ck_ref), (q_ref, k_ref))
        project(cur_ref, col, (cv_ref,), (v_ref,))


def mlstm_qkv(up, hist, conv_w, conv_b, w_q, w_k, w_v):
    b_, T, _ = up.shape
    chans = MLSTM_D_INNER
    width = conv_w.shape[0]
    bs = w_q.shape[1]
    tt, ct = _conv_tiles(T, chans, [0], QKV_CHAN_TILE)
    assert VREG_LANES % bs == 0 and width <= CONV_HALO + 1
    hist8 = jnp.pad(hist, ((0, 0), (CONV_HALO - (width - 1), 0), (0, 0)))
    halo_blocks = tt // CONV_HALO
    tile = pl.BlockSpec((None, tt, ct), lambda b, t, c: (b, t, c))
    coef = pl.BlockSpec((2 * bs - 1, ct), lambda b, t, c: (0, c))
    shape = lambda dt: jax.ShapeDtypeStruct((b_, T, chans), dt)
    return pl.pallas_call(
        functools.partial(_mlstm_qkv_kernel, width=width, bs=bs),
        grid=(b_, T // tt, chans // ct),
        in_specs=[tile,
                  pl.BlockSpec((None, CONV_HALO, ct), lambda b, t, c: (b, jnp.maximum(t * halo_blocks - 1, 0), c)),
                  pl.BlockSpec((None, CONV_HALO, ct), lambda b, t, c: (b, 0, c)),
                  pl.BlockSpec((width, ct), lambda b, t, c: (0, c)),
                  pl.BlockSpec((1, ct), lambda b, t, c: (0, c)),
                  coef, coef, coef],
        out_specs=[tile, tile, tile, tile],
        out_shape=[shape(jnp.float32), shape(jnp.bfloat16), shape(jnp.bfloat16), shape(jnp.bfloat16)],
        scratch_shapes=[pltpu.VMEM((CONV_HALO + tt, ct), jnp.float32)],
        compiler_params=pltpu.CompilerParams(
            dimension_semantics=("parallel", "parallel", "parallel"),
            vmem_limit_bytes=V7X_VMEM_LIMIT_BYTES),
        name="mlstm_qkv",
    )(up, up, hist8, conv_w, conv_b.reshape(1, chans),
      _blockdiag_coefs(w_q), _blockdiag_coefs(w_k), _blockdiag_coefs(w_v))


def conv_tail(hist, src, col0, chans):
    keep = hist.shape[1]
    return jnp.concatenate([hist, src[:, -keep:, col0:col0 + chans]], axis=1)[:, -keep:]


SSD_HEADS_PER_GROUP = SSD_N_HEADS // SSD_N_GROUPS
SSD_GROUP_WIDTH = SSD_HEADS_PER_GROUP * SSD_HEADDIM
SSD_HEAD_SHIFT = SSD_HEADDIM.bit_length() - 1
assert 1 << SSD_HEAD_SHIFT == SSD_HEADDIM and SSD_GROUP_WIDTH == SSD_D_INNER // SSD_N_GROUPS


def _ssd_kernel(x_ref, z_ref, cm_ref, bmt_ref, cols_ref, rows_ref, h0_ref, d_ref, g_ref,
                y_ref, h_out_ref, h_ref):
    f32, bf16 = jnp.float32, jnp.bfloat16
    ck = pl.program_id(2)
    L, W = x_ref.shape
    R = SSD_HEADS_PER_GROUP

    @pl.when(ck == 0)
    def _():
        h_ref[...] = h0_ref[...]

    def per_channel(c):
        spread = jnp.where((lax.broadcasted_iota(jnp.int32, (R, W), 1) >> SSD_HEAD_SHIFT)
                           == lax.broadcasted_iota(jnp.int32, (R, W), 0), 1.0, 0.0).astype(bf16)
        out = jnp.zeros((L, W), f32)
        for _ in range(3):
            part = c.astype(bf16)
            out = out + jnp.dot(part, spread, preferred_element_type=f32)
            c = c - part.astype(f32)
        return out

    x = x_ref[...]
    xb = x.astype(bf16)
    cmb = cm_ref[...].astype(bf16)
    bmt = bmt_ref[...]
    cols = cols_ref[...]
    acs = per_channel(cols[:, :R])
    cb = jnp.dot(cmb, bmt, preferred_element_type=f32)
    causal = (lax.broadcasted_iota(jnp.int32, (L, L), 0) >= lax.broadcasted_iota(jnp.int32, (L, L), 1))
    lane_head = lax.broadcasted_iota(jnp.int32, (L, W), 1) >> SSD_HEAD_SHIFT
    y = jnp.zeros((L, W), f32)
    for r in range(R):
        acs_col = cols[:, r:r + 1]
        acs_row = rows_ref[r:r + 1, :]
        dt_row = rows_ref[SSD_HEADS_PER_GROUP + r:SSD_HEADS_PER_GROUP + r + 1, :]
        decay = jnp.exp(jnp.where(causal, acs_col - acs_row, -jnp.inf))
        w = (cb * decay * dt_row).astype(bf16)
        y = jnp.where(lane_head == r, jnp.dot(w, xb, preferred_element_type=f32), y)
    total = acs[L - 1:L, :]
    xw = (x * (jnp.exp(total - acs) * per_channel(cols[:, R:]))).astype(bf16)
    h_t = h_ref[...]
    y = y + jnp.dot(cmb, h_t.astype(bf16), preferred_element_type=f32) * jnp.exp(acs)
    h_ref[...] = jnp.exp(total) * h_t + jnp.dot(bmt, xw, preferred_element_type=f32)
    y = (y + d_ref[...] * x) * jax.nn.silu(z_ref[...])
    y_ref[...] = y * lax.rsqrt(jnp.mean(y * y, axis=-1, keepdims=True) + RMS_EPS) * g_ref[...]

    @pl.when(ck == pl.num_programs(2) - 1)
    def _():
        h_out_ref[...] = h_ref[...]


def ssd_cell(zx, xbc, dt, a, d_skip, norm_g, h0):
    b_, T, _ = xbc.shape
    G, R, P, N, W = SSD_N_GROUPS, SSD_HEADS_PER_GROUP, SSD_HEADDIM, SSD_D_STATE, SSD_GROUP_WIDTH
    L = math.gcd(T, SSD_CHUNK)
    nc = T // L
    acs = jnp.cumsum((dt * a).reshape(b_, nc, L, SSD_N_HEADS), axis=2)
    dtc = dt.reshape(b_, nc, L, SSD_N_HEADS)

    def rows(t):
        return jnp.transpose(t.reshape(b_, nc, L, G, R), (0, 3, 1, 4, 2))

    def cols(t):
        return jnp.transpose(t.reshape(b_, T, G, R), (0, 2, 1, 3))

    rowpack = jnp.concatenate([rows(acs), rows(dtc)], axis=3)
    colpack = jnp.concatenate([cols(acs), cols(dtc)], axis=3)
    bmt = jnp.transpose(xbc[..., SSD_D_INNER:SSD_D_INNER + G * N].astype(jnp.bfloat16).reshape(b_, T, G, N),
                        (0, 2, 3, 1))
    h0t = jnp.transpose(h0.reshape(b_, G, R, P, N), (0, 1, 4, 2, 3)).reshape(b_, G, N, W)
    chan = lambda b, g, c: (b, c, g)
    cm_block0 = (SSD_D_INNER + G * N) // N
    state_spec = pl.BlockSpec((None, None, N, W), lambda b, g, c: (b, g, 0, 0))
    row_spec = pl.BlockSpec((1, W), lambda b, g, c: (0, g))
    y, ht = pl.pallas_call(
        _ssd_kernel,
        grid=(b_, G, nc),
        in_specs=[pl.BlockSpec((None, L, W), chan),
                  pl.BlockSpec((None, L, W), chan),
                  pl.BlockSpec((None, L, N), lambda b, g, c: (b, c, cm_block0 + g)),
                  pl.BlockSpec((None, None, N, L), lambda b, g, c: (b, g, 0, c)),
                  pl.BlockSpec((None, None, L, 2 * R), lambda b, g, c: (b, g, c, 0)),
                  pl.BlockSpec((None, None, None, 2 * R, L), lambda b, g, c: (b, g, c, 0, 0)),
                  state_spec, row_spec, row_spec],
        out_specs=[pl.BlockSpec((None, L, W), chan), state_spec],
        out_shape=[jax.ShapeDtypeStruct((b_, T, SSD_D_INNER), jnp.float32),
                   jax.ShapeDtypeStruct((b_, G, N, W), jnp.float32)],
        scratch_shapes=[pltpu.VMEM((N, W), jnp.float32)],
        compiler_params=pltpu.CompilerParams(
            dimension_semantics=("parallel", "parallel", "arbitrary"),
            vmem_limit_bytes=V7X_VMEM_LIMIT_BYTES),
        name="ssd_cell",
    )(xbc, zx, xbc, bmt, colpack, rowpack, h0t,
      jnp.repeat(d_skip, P).reshape(1, SSD_D_INNER), norm_g.reshape(1, SSD_D_INNER))
    h_new = jnp.transpose(ht.reshape(b_, G, N, R, P), (0, 1, 3, 4, 2)).reshape(b_, SSD_N_HEADS, P, N)
    return y, h_new


def ssd_mixer(x, conv_hist, h0, w_in, conv_w, conv_b, dt_bias, a_log, d_skip, norm_g, w_out):
    zx = matmul(x, w_in)
    xbc = conv_act(zx, SSD_D_INNER, SSD_CONV_DIM, conv_hist, conv_w, conv_b, "silu")
    dt = jax.nn.softplus(zx[..., SSD_D_INNER + SSD_CONV_DIM:] + dt_bias)
    y, h_new = ssd_cell(zx, xbc, dt, -jnp.exp(a_log), d_skip, norm_g, h0)
    return Proj(y, w_out), conv_tail(conv_hist, zx, SSD_D_INNER, SSD_CONV_DIM), h_new


MLSTM_STEP = 256
MLSTM_NORM_LANES = 128


def _mlstm_kernel(q_ref, kt_ref, v_ref, acol_ref, grow_ref, c0_ref, n0_ref, m0_ref, g_ref,
                  xc_ref, z_ref, skip_ref, h_ref, c_out_ref, n_out_ref, m_out_ref, c_ref, m_ref):
    f32, bf16 = jnp.float32, jnp.bfloat16
    step = pl.program_id(2)
    L, D = q_ref.shape

    @pl.when(step == 0)
    def _():
        c_ref[:, :D] = c0_ref[...]
        c_ref[:, D:] = n0_ref[...]
        m_ref[...] = m0_ref[...]

    q = q_ref[...]
    kt = kt_ref[...]
    a_col = acol_ref[...]
    g_row = grow_ref[...]
    m_prev = m_ref[:, 0:1]
    causal = (lax.broadcasted_iota(jnp.int32, (L, L), 0) >= lax.broadcasted_iota(jnp.int32, (L, L), 1))
    dmat = jnp.where(causal, a_col + g_row, -jnp.inf)
    inter = m_prev + a_col
    m_t = jnp.maximum(inter, jnp.max(dmat, axis=1, keepdims=True))
    s = jnp.dot(q, kt, preferred_element_type=f32) * jnp.exp(dmat - m_t)
    sc_inter = jnp.exp(inter - m_t)
    qc = jnp.dot(q, c_ref[...].astype(bf16), preferred_element_type=f32)
    num = jnp.dot(s.astype(bf16), v_ref[...], preferred_element_type=f32) + sc_inter * qc[:, :D]
    den = jnp.sum(s, axis=1, keepdims=True) + sc_inter * qc[:, D:D + 1]
    h = num / jnp.maximum(jnp.abs(den), jnp.exp(-m_t))
    mu = jnp.mean(h, axis=-1, keepdims=True)
    var = jnp.mean(jnp.square(h - mu), axis=-1, keepdims=True)
    hn = (h - mu) * lax.rsqrt(var + LN_EPS) * g_ref[...]
    h_ref[...] = ((hn + skip_ref[...] * xc_ref[...]) * jax.nn.silu(z_ref[...])).astype(h_ref.dtype)

    m_new = m_t[L - 1:L, :]
    total = a_col[L - 1:L, :]
    decay = jnp.exp(total + g_row - m_new)
    sc_c = jnp.exp(m_prev + total - m_new)
    ktd = (kt.astype(f32) * decay).astype(bf16)
    one_hot = jnp.where(lax.broadcasted_iota(jnp.int32, (L, MLSTM_NORM_LANES), 1) == 0, 1.0, 0.0).astype(bf16)
    v_ext = jnp.concatenate([v_ref[...], one_hot], axis=1)
    c_ref[...] = sc_c * c_ref[...] + jnp.dot(ktd, v_ext, preferred_element_type=f32)
    m_ref[...] = jnp.broadcast_to(m_new, m_ref.shape)

    @pl.when(step == pl.num_programs(2) - 1)
    def _():
        c_out_ref[...] = c_ref[:, :D]
        n_out_ref[...] = c_ref[:, D:]
        m_out_ref[...] = m_ref[...]


def mlstm_cell(q, k, v, i_pre, logf, c0, n0, m0, norm_g, xc, up, skip):
    b_, T, _ = q.shape
    H, D = MLSTM_N_HEADS, MLSTM_HEAD_DIM
    L = min(T, MLSTM_STEP)
    ns = T // L
    bf16 = jnp.bfloat16
    kt = jnp.swapaxes((k * (D ** -0.5)).astype(bf16).reshape(b_, T, H, D), 1, 3)
    kt = jnp.swapaxes(kt, 1, 2)
    bcum = jnp.cumsum(logf.reshape(b_, ns, L, H), axis=2)
    acol = jnp.moveaxis(bcum, 3, 1).reshape(b_, H, T, 1)
    grow = jnp.moveaxis(i_pre.reshape(b_, ns, L, H) - bcum, 3, 1).reshape(b_, H, ns, 1, L)
    n0e = jnp.pad(n0[..., None], ((0, 0), (0, 0), (0, 0), (0, MLSTM_NORM_LANES - 1)))
    m0e = jnp.broadcast_to(m0[:, :, None, None], (b_, H, 1, MLSTM_NORM_LANES))

    tok_spec = pl.BlockSpec((None, L, D), lambda b, h, s: (b, s, h))
    state = lambda w: pl.BlockSpec((None, None, D, w), lambda b, h, s: (b, h, 0, 0))
    m_spec = pl.BlockSpec((None, None, 1, MLSTM_NORM_LANES), lambda b, h, s: (b, h, 0, 0))
    hn, c, n, m = pl.pallas_call(
        _mlstm_kernel,
        grid=(b_, H, ns),
        in_specs=[tok_spec,
                  pl.BlockSpec((None, None, D, L), lambda b, h, s: (b, h, 0, s)),
                  tok_spec,
                  pl.BlockSpec((None, None, L, 1), lambda b, h, s: (b, h, s, 0)),
                  pl.BlockSpec((None, None, None, 1, L), lambda b, h, s: (b, h, s, 0, 0)),
                  state(D), state(MLSTM_NORM_LANES), m_spec,
                  pl.BlockSpec((1, D), lambda b, h, s: (0, h)),
                  tok_spec,
                  pl.BlockSpec((None, L, D), lambda b, h, s: (b, s, H + h)),
                  pl.BlockSpec((1, D), lambda b, h, s: (0, h))],
        out_specs=[tok_spec, state(D), state(MLSTM_NORM_LANES), m_spec],
        out_shape=[jax.ShapeDtypeStruct((b_, T, H * D), bf16),
                   jax.ShapeDtypeStruct((b_, H, D, D), jnp.float32),
                   jax.ShapeDtypeStruct((b_, H, D, MLSTM_NORM_LANES), jnp.float32),
                   jax.ShapeDtypeStruct((b_, H, 1, MLSTM_NORM_LANES), jnp.float32)],
        scratch_shapes=[pltpu.VMEM((D, D + MLSTM_NORM_LANES), jnp.float32),
                        pltpu.VMEM((1, MLSTM_NORM_LANES), jnp.float32)],
        compiler_params=pltpu.CompilerParams(
            dimension_semantics=("parallel", "parallel", "arbitrary"),
            vmem_limit_bytes=V7X_VMEM_LIMIT_BYTES),
        name="mlstm_cell",
    )(q.astype(bf16), kt, v.astype(bf16), acol, grow, c0, n0e, m0e, norm_g.reshape(1, H * D),
      xc, up, skip.reshape(1, H * D))
    return hn, c, n[..., 0], m[:, :, 0, 0]


def mlstm_mixer(x, conv_hist, c0, n0, m0, w_up, conv_w, conv_b, w_q, w_k, w_v, w_if, b_if,
                skip, norm_g, w_down):
    b_, T, _ = x.shape
    up = matmul(x, w_up)
    xc, q, k, v = mlstm_qkv(up, conv_hist, conv_w, conv_b, w_q, w_k, w_v)
    gates = matmul(q, w_if[0]) + matmul(k, w_if[1]) + matmul(v, w_if[2]) + b_if
    i_pre, f_pre = gates[..., :MLSTM_N_HEADS], gates[..., MLSTM_N_HEADS:]
    h, c, n, m = mlstm_cell(q, k, v, i_pre, jax.nn.log_sigmoid(f_pre), c0, n0, m0, norm_g, xc, up, skip)
    return Proj(h, w_down), conv_tail(conv_hist, up, 0, MLSTM_D_INNER), c, n, m


S5_D_STATE = S5_N_GROUPS * S5_STATE
S5_PACK = 8
S5_N_PACKS = S5_N_GROUPS // S5_PACK
S5_SCAN_ROWS = 8
S5_SCAN_LANES = 256
S5_TIME_TILE = 256
S5_SCAN_UNROLL = 4


def _cmul(ar, ai, br, bi):
    return ar * br - ai * bi, ar * bi + ai * br


def _s5_kernel(x_ref, bre_ref, bim_ref, cre_ref, cim_ref, pw_ref, h0r_ref, h0i_ref, d_ref,
               g_ref, hr_out_ref, hi_out_ref, bur_ref, bui_ref, hr_ref, hi_ref):
    tt = pl.program_id(1)
    rows = x_ref.shape[0]
    pk_in = S5_PACK * S5_GROUP
    pk_st = S5_PACK * S5_STATE

    @pl.when(tt == 0)
    def _():
        hr_ref[...] = jnp.broadcast_to(h0r_ref[...], hr_ref.shape)
        hi_ref[...] = jnp.broadcast_to(h0i_ref[...], hi_ref.shape)

    for c in range(S5_N_PACKS):
        xc = x_ref[:, c * pk_in:(c + 1) * pk_in].astype(jnp.bfloat16)
        bur_ref[:, c * pk_st:(c + 1) * pk_st] = jnp.dot(xc, bre_ref[c], preferred_element_type=jnp.float32)
        bui_ref[:, c * pk_st:(c + 1) * pk_st] = jnp.dot(xc, bim_ref[c], preferred_element_type=jnp.float32)

    def col_body(cb, carry):
        cs = pl.ds(pl.multiple_of(cb * S5_SCAN_LANES, S5_SCAN_LANES), S5_SCAN_LANES)
        stages = [(pw_ref[2 * k, :, cs], pw_ref[2 * k + 1, :, cs], 1 << k) for k in range(3)]
        lr, li = pw_ref[6, :, cs], pw_ref[7, :, cs]

        def row_body(r, h):
            hr, hi = h
            rs = pl.ds(pl.multiple_of(r * S5_SCAN_ROWS, S5_SCAN_ROWS), S5_SCAN_ROWS)
            vr, vi = bur_ref[rs, cs], bui_ref[rs, cs]
            for mr, mi, s in stages:
                pr, pi = _cmul(mr, mi, pltpu.roll(vr, s, 0), pltpu.roll(vi, s, 0))
                vr, vi = vr + pr, vi + pi
            pr, pi = _cmul(lr, li, hr, hi)
            vr, vi = vr + pr, vi + pi
            bur_ref[rs, cs] = vr
            bui_ref[rs, cs] = vi
            last = S5_SCAN_ROWS - 1
            return (jnp.broadcast_to(vr[last:, :], vr.shape), jnp.broadcast_to(vi[last:, :], vi.shape))

        hr, hi = lax.fori_loop(0, rows // S5_SCAN_ROWS, row_body, (hr_ref[:, cs], hi_ref[:, cs]),
                               unroll=min(S5_SCAN_UNROLL, rows // S5_SCAN_ROWS))
        hr_ref[:, cs] = hr
        hi_ref[:, cs] = hi
        return carry

    lax.fori_loop(0, S5_D_STATE // S5_SCAN_LANES, col_body, 0)

    for c in range(S5_N_PACKS):
        hr = bur_ref[:, c * pk_st:(c + 1) * pk_st].astype(jnp.bfloat16)
        hi = bui_ref[:, c * pk_st:(c + 1) * pk_st].astype(jnp.bfloat16)
        y = (jnp.dot(hr, cre_ref[c], preferred_element_type=jnp.float32)
             - jnp.dot(hi, cim_ref[c], preferred_element_type=jnp.float32))
        cols = slice(c * pk_in, (c + 1) * pk_in)
        g_ref[:, cols] = jax.nn.gelu(y + d_ref[:, cols] * x_ref[:, cols])

    @pl.when(tt == pl.num_programs(1) - 1)
    def _():
        hr_out_ref[...] = hr_ref[0:1, :]
        hi_out_ref[...] = hi_ref[0:1, :]


def _block_diag_packs(w):
    g, r, c = w.shape
    eye = jnp.eye(S5_PACK, dtype=w.dtype)
    wb = jnp.einsum('kgrc,gh->kgrhc', w.reshape(g // S5_PACK, S5_PACK, r, c), eye)
    return wb.reshape(g // S5_PACK, S5_PACK * r, S5_PACK * c).astype(jnp.bfloat16)


def s5_mixer(x, h0, a_re, a_im, log_dt, b_re, b_im, c_re, c_im, d_skip, w_glu_a, w_glu_b):
    b_, T, _ = x.shape
    step = jnp.exp(log_dt)[:, None]
    mag = jnp.exp(a_re * step)
    ab_re, ab_im = mag * jnp.cos(a_im * step), mag * jnp.sin(a_im * step)
    den = a_re * a_re + a_im * a_im
    nr, ni = ab_re - 1.0, ab_im
    f_re = (nr * a_re + ni * a_im) / den
    f_im = (ni * a_re - nr * a_im) / den
    bb_re = f_re[..., None] * b_re - f_im[..., None] * b_im
    bb_im = f_re[..., None] * b_im + f_im[..., None] * b_re
    bre = _block_diag_packs(jnp.swapaxes(bb_re, 1, 2))
    bim = _block_diag_packs(jnp.swapaxes(bb_im, 1, 2))
    cre = _block_diag_packs(jnp.swapaxes(c_re, 1, 2))
    cim = _block_diag_packs(jnp.swapaxes(c_im, 1, 2))
    l1 = (ab_re.reshape(-1), ab_im.reshape(-1))
    l2 = _cmul(*l1, *l1)
    l4 = _cmul(*l2, *l2)
    row = jnp.arange(S5_SCAN_ROWS)[:, None]
    pw = []
    for s, (pr, pi) in ((1, l1), (2, l2), (4, l4)):
        pw += [jnp.where(row >= s, pr[None, :], 0.0), jnp.where(row >= s, pi[None, :], 0.0)]
    acc = [l1]
    for _ in range(S5_SCAN_ROWS - 1):
        acc.append(_cmul(*acc[-1], *l1))
    pw += [jnp.stack([a[0] for a in acc]), jnp.stack([a[1] for a in acc])]
    pw = jnp.stack(pw)

    tt = min(T, S5_TIME_TILE)
    h0r = h0[..., 0].reshape(b_, 1, S5_D_STATE)
    h0i = h0[..., 1].reshape(b_, 1, S5_D_STATE)
    pk_in, pk_st = S5_PACK * S5_GROUP, S5_PACK * S5_STATE

    def const3(b, t):
        return (0, 0, 0)

    state_spec = pl.BlockSpec((None, 1, S5_D_STATE), lambda b, t: (b, 0, 0))
    g, hr, hi = pl.pallas_call(
        _s5_kernel,
        grid=(b_, T // tt),
        in_specs=[pl.BlockSpec((None, tt, D_MODEL), lambda b, t: (b, t, 0)),
                  pl.BlockSpec((S5_N_PACKS, pk_in, pk_st), const3),
                  pl.BlockSpec((S5_N_PACKS, pk_in, pk_st), const3),
                  pl.BlockSpec((S5_N_PACKS, pk_st, pk_in), const3),
                  pl.BlockSpec((S5_N_PACKS, pk_st, pk_in), const3),
                  pl.BlockSpec((8, S5_SCAN_ROWS, S5_D_STATE), const3),
                  state_spec, state_spec,
                  pl.BlockSpec((1, D_MODEL), lambda b, t: (0, 0))],
        out_specs=[pl.BlockSpec((None, tt, D_MODEL), lambda b, t: (b, t, 0)), state_spec, state_spec],
        out_shape=[jax.ShapeDtypeStruct((b_, T, D_MODEL), jnp.float32),
                   jax.ShapeDtypeStruct((b_, 1, S5_D_STATE), jnp.float32),
                   jax.ShapeDtypeStruct((b_, 1, S5_D_STATE), jnp.float32)],
        scratch_shapes=[pltpu.VMEM((tt, S5_D_STATE), jnp.float32),
                        pltpu.VMEM((tt, S5_D_STATE), jnp.float32),
                        pltpu.VMEM((S5_SCAN_ROWS, S5_D_STATE), jnp.float32),
                        pltpu.VMEM((S5_SCAN_ROWS, S5_D_STATE), jnp.float32)],
        compiler_params=pltpu.CompilerParams(
            dimension_semantics=("parallel", "arbitrary"),
            vmem_limit_bytes=V7X_VMEM_LIMIT_BYTES),
        name="s5_scan",
    )(x, bre, bim, cre, cim, pw, h0r, h0i, d_skip.reshape(1, D_MODEL))
    out = matmul(g, w_glu_a) * jax.nn.sigmoid(matmul(g, w_glu_b))
    h_new = jnp.stack([hr.reshape(b_, S5_N_GROUPS, S5_STATE), hi.reshape(b_, S5_N_GROUPS, S5_STATE)], axis=-1)
    return out, h_new


def nsa_compress(kv, w1, w2, pe):
    b_, L = kv.shape[:2]
    span = NSA_CMP_BLOCK // NSA_CMP_STRIDE
    n_str = L // NSA_CMP_STRIDE
    n_cmp = n_str - span + 1
    chunks = kv.reshape(b_, n_str, NSA_CMP_STRIDE, NSA_N_KV, NSA_HEAD_DIM)
    blocks = jnp.concatenate([chunks[:, s:s + n_cmp] for s in range(span)], axis=2) + pe[:, None, :]
    flat = jnp.moveaxis(blocks, 3, 2).reshape(b_, n_cmp, NSA_N_KV, NSA_CMP_BLOCK * NSA_HEAD_DIM)
    return jax.nn.gelu(flat @ w1) @ w2


NSA_TQ = NSA_WBLOCK
NSA_SLC_CHUNK = 512
NSA_WIN_CHUNK = 256
NSA_ROWS = NSA_GQA * NSA_TQ
NSA_SEL_SHIFT = NSA_SEL_BLOCK.bit_length() - 1
assert 1 << NSA_SEL_SHIFT == NSA_SEL_BLOCK


def _nsa_stream_softmax_t(q_t, k_ref, vt_ref, c_lo, c_hi, chunk, mask_fn, m_ref, l_ref, acc_ref):
    scale = NSA_HEAD_DIM ** -0.5
    m_ref[...] = jnp.full(m_ref.shape, NEG_INF, jnp.float32)
    l_ref[...] = jnp.zeros(l_ref.shape, jnp.float32)
    acc_ref[...] = jnp.zeros(acc_ref.shape, jnp.float32)

    def body(c, carry):
        start = pl.multiple_of(c * chunk, chunk)
        s = jnp.dot(k_ref[pl.ds(start, chunk), :], q_t, preferred_element_type=jnp.float32) * scale
        s = jnp.where(mask_fn(start), s, NEG_INF)
        m_old = m_ref[...]
        m_new = jnp.maximum(m_old, jnp.max(s, axis=0, keepdims=True))
        alpha = jnp.exp(m_old - m_new)
        p = jnp.exp(s - m_new)
        l_ref[...] = alpha * l_ref[...] + jnp.sum(p, axis=0, keepdims=True)
        acc_ref[...] = alpha * acc_ref[...] + jnp.dot(vt_ref[:, pl.ds(start, chunk)], p.astype(jnp.bfloat16),
                                                      preferred_element_type=jnp.float32)
        m_ref[...] = m_new
        return carry

    lax.fori_loop(c_lo, c_hi, body, 0)
    return acc_ref[...] / l_ref[...]


def _nsa_prompt_t_kernel(qt_ref, gate_ref, kc_ref, vct_ref, ks_ref, vst_ref, kw_ref, vwt_ref,
                         o_ref, sel_ref, m_ref, l_ref, acc_ref, mix_ref, *, n_cmp, n_top):
    f32, bf16 = jnp.float32, jnp.bfloat16
    i = pl.program_id(2)
    t0 = i * NSA_TQ
    n_cp = kc_ref.shape[0]
    n_sel = sel_ref.shape[0]
    q_t = qt_ref[...]
    tq = t0 + (lax.broadcasted_iota(jnp.int32, (1, NSA_ROWS), 1) & (NSA_TQ - 1))

    s = jnp.dot(kc_ref[...], q_t, preferred_element_type=f32) * (NSA_HEAD_DIM ** -0.5)
    n_idx = lax.broadcasted_iota(jnp.int32, (n_cp, NSA_ROWS), 0)
    cmask = (n_idx * NSA_CMP_STRIDE + (NSA_CMP_BLOCK - 1) <= tq) & (n_idx < n_cmp)
    s = jnp.where(cmask, s, NEG_INF)
    e = jnp.exp(s - jnp.max(s, axis=0, keepdims=True))
    p = jnp.where(cmask, e / jnp.sum(e, axis=0, keepdims=True), 0.0)
    mix_ref[...] = gate_ref[0:1, :] * jnp.dot(vct_ref[...], p.astype(bf16), preferred_element_type=f32)

    psum = p[:, 0:NSA_TQ]
    for g in range(1, NSA_GQA):
        psum = psum + p[:, g * NSA_TQ:(g + 1) * NSA_TQ]
    jn = lax.broadcasted_iota(jnp.int32, (n_sel, n_cp), 0)
    nn = lax.broadcasted_iota(jnp.int32, (n_sel, n_cp), 1)
    r = NSA_SEL_BLOCK // NSA_CMP_STRIDE
    span = NSA_CMP_BLOCK // NSA_CMP_STRIDE
    pool = jnp.where((nn >= r * jn - (span - 1)) & (nn <= r * jn + (r - 1)), 1.0, 0.0).astype(bf16)
    imp = jnp.zeros((n_sel, NSA_TQ), f32)
    rest = psum
    for _ in range(3):
        part = rest.astype(bf16)
        imp = imp + jnp.dot(pool, part, preferred_element_type=f32)
        rest = rest - part.astype(f32)

    jidx = lax.broadcasted_iota(jnp.int32, (n_sel, NSA_TQ), 0)
    cur = (t0 + lax.broadcasted_iota(jnp.int32, (n_sel, NSA_TQ), 1)) >> NSA_SEL_SHIFT
    forced = (jidx == 0) | (jidx == cur) | (jidx == cur - 1)
    score = jnp.where(forced, FORCE_SCORE, jnp.where(jidx <= cur, imp, -FORCE_SCORE))
    rank = jnp.zeros((n_sel, NSA_TQ), jnp.int32)
    for k in range(n_sel):
        row = score[k:k + 1, :]
        before = (row > score) | ((row == score) & (k < jidx))
        rank = rank + jnp.where(before, 1, 0)
    sel_ref[...] = jnp.where(rank < n_top, 1.0, 0.0)

    def slc_mask(start):
        kpos = start + lax.broadcasted_iota(jnp.int32, (NSA_SLC_CHUNK, NSA_ROWS), 0)
        first = start >> NSA_SEL_SHIFT
        chosen = jnp.concatenate(
            [jnp.broadcast_to(sel_ref[pl.ds(first + j, 1), :], (NSA_SEL_BLOCK, NSA_TQ))
             for j in range(NSA_SLC_CHUNK // NSA_SEL_BLOCK)], axis=0)
        chosen = jnp.concatenate([chosen] * NSA_GQA, axis=1)
        return (chosen > 0.5) & (kpos <= tq)

    mix_ref[...] += gate_ref[1:2, :] * _nsa_stream_softmax_t(
        q_t, ks_ref, vst_ref, 0, (t0 + NSA_TQ - 1) // NSA_SLC_CHUNK + 1, NSA_SLC_CHUNK, slc_mask,
        m_ref, l_ref, acc_ref)

    def win_mask(start):
        diff = tq - (start + lax.broadcasted_iota(jnp.int32, (NSA_WIN_CHUNK, NSA_ROWS), 0))
        return (diff >= 0) & (diff < NSA_WINDOW)

    mixed = mix_ref[...] + gate_ref[2:3, :] * _nsa_stream_softmax_t(
        q_t, kw_ref, vwt_ref, jnp.maximum(t0 - NSA_WINDOW, 0) // NSA_WIN_CHUNK,
        (t0 + NSA_TQ - 1) // NSA_WIN_CHUNK + 1, NSA_WIN_CHUNK, win_mask, m_ref, l_ref, acc_ref)
    for g in range(NSA_GQA):
        o_ref[:, g * NSA_HEAD_DIM:(g + 1) * NSA_HEAD_DIM] = mixed[:, g * NSA_TQ:(g + 1) * NSA_TQ].T.astype(o_ref.dtype)


def nsa_prompt_attention_t(q, kv, kcmp, vcmp, gate):
    b_, T, _ = q.shape
    assert T % NSA_SLC_CHUNK == 0 and (T // NSA_SEL_BLOCK) % 8 == 0
    n_cmp = kcmp.shape[1]
    n_cp = T // NSA_CMP_STRIDE
    n_sel = T // NSA_SEL_BLOCK
    n_tiles = T // NSA_TQ
    bf16 = jnp.bfloat16
    pad_c = ((0, 0), (0, n_cp - n_cmp), (0, 0), (0, 0))
    kc = jnp.transpose(jnp.pad(kcmp, pad_c).astype(bf16), (0, 2, 1, 3))
    vct = jnp.transpose(jnp.pad(vcmp, pad_c).astype(bf16), (0, 2, 3, 1))
    kvb = kv.astype(bf16)
    keys = lambda comp: pl.BlockSpec((None, T, NSA_HEAD_DIM), lambda b, h, i: (b, 0, comp * NSA_N_KV + h))

    def vals_t(comp):
        v = kvb[:, :, comp * NSA_KVW:(comp + 1) * NSA_KVW].reshape(b_, T, NSA_N_KV, NSA_HEAD_DIM)
        return jnp.transpose(v, (0, 2, 3, 1))

    q_t = q.astype(bf16).reshape(b_, n_tiles, NSA_TQ, NSA_N_KV, NSA_GQA, NSA_HEAD_DIM)
    q_t = jnp.transpose(q_t, (0, 3, 1, 5, 4, 2)).reshape(b_, NSA_N_KV, n_tiles, NSA_HEAD_DIM, NSA_ROWS)
    gate_t = gate.reshape(b_, n_tiles, NSA_TQ, NSA_N_KV, NSA_GQA, 3)
    gate_t = jnp.transpose(gate_t, (0, 3, 1, 5, 4, 2)).reshape(b_, NSA_N_KV, n_tiles, 3, NSA_ROWS)

    per_head = lambda r, c: pl.BlockSpec((None, None, r, c), lambda b, h, i: (b, h, 0, 0))
    per_tile = lambda r: pl.BlockSpec((None, None, None, r, NSA_ROWS), lambda b, h, i: (b, h, i, 0, 0))
    return pl.pallas_call(
        functools.partial(_nsa_prompt_t_kernel, n_cmp=n_cmp, n_top=min(NSA_N_SELECT, n_sel)),
        grid=(b_, NSA_N_KV, n_tiles),
        in_specs=[per_tile(NSA_HEAD_DIM), per_tile(3),
                  per_head(n_cp, NSA_HEAD_DIM), per_head(NSA_HEAD_DIM, n_cp),
                  keys(2), per_head(NSA_HEAD_DIM, T), keys(4), per_head(NSA_HEAD_DIM, T)],
        out_specs=pl.BlockSpec((None, NSA_TQ, NSA_GQA * NSA_HEAD_DIM), lambda b, h, i: (b, i, h)),
        out_shape=jax.ShapeDtypeStruct((b_, T, NSA_N_HEADS * NSA_HEAD_DIM), bf16),
        scratch_shapes=[pltpu.VMEM((n_sel, NSA_TQ), jnp.float32),
                        pltpu.VMEM((1, NSA_ROWS), jnp.float32),
                        pltpu.VMEM((1, NSA_ROWS), jnp.float32),
                        pltpu.VMEM((NSA_HEAD_DIM, NSA_ROWS), jnp.float32),
                        pltpu.VMEM((NSA_HEAD_DIM, NSA_ROWS), jnp.float32)],
        compiler_params=pltpu.CompilerParams(
            dimension_semantics=("parallel", "parallel", "arbitrary"),
            vmem_limit_bytes=V7X_VMEM_LIMIT_BYTES),
        name="nsa_prompt",
    )(q_t, gate_t, kc, vct, kvb, vals_t(3), kvb, vals_t(5))


def nsa_prompt_mixer(x, w_q, w_kv, w_gate, b_gate, w_cmp1, w_cmp2, cmp_pe, w_out):
    b_, T, _ = x.shape
    q = matmul(x, w_q)
    kv = matmul(x, w_kv)
    comp = lambda c: kv[:, :, c * NSA_KVW:(c + 1) * NSA_KVW].reshape(b_, T, NSA_N_KV, NSA_HEAD_DIM)
    kcmp = nsa_compress(comp(0), w_cmp1[0], w_cmp2[0], cmp_pe[0])
    vcmp = nsa_compress(comp(1), w_cmp1[1], w_cmp2[1], cmp_pe[1])
    gate = jax.nn.sigmoid(matmul(x, w_gate) + b_gate)
    y = Proj(nsa_prompt_attention_t(q, kv, kcmp, vcmp, gate), w_out)
    rows = kv[:, :, :4 * NSA_KVW].reshape(b_, T, 4, NSA_N_KV, NSA_HEAD_DIM)
    keep = min(NSA_WINDOW, T)
    win_new = kv[:, T - keep:, 4 * NSA_KVW:].reshape(b_, keep, 2, NSA_N_KV, NSA_HEAD_DIM)
    return y, rows, win_new


NSA_ROW_SLABS = 4 * NSA_N_KV
NSA_HALF_SLABS = NSA_ROW_SLABS // 2
NSA_KVW = NSA_N_KV * NSA_HEAD_DIM
NSA_CMP_PAGES = 8
NSA_SLC_PAGES = 4


def _round_up(n, m):
    return -(-n // m) * m


def _log2(n):
    assert n > 0 and n & (n - 1) == 0
    return n.bit_length() - 1


def _nsa_compress_kernel(pt_ref, *refs):
    del pt_ref
    pages = refs[:NSA_CMP_PAGES]
    w1_ref, pe_ref, a_ref, b_ref = refs[NSA_CMP_PAGES:]
    page_rows = pages[0].shape[0]
    per_page = page_rows // NSA_CMP_STRIDE
    rows = NSA_CMP_PAGES * NSA_N_KV * per_page
    half = NSA_CMP_STRIDE * NSA_HEAD_DIM
    slabs = [jnp.swapaxes(pg[...], 0, 1) for pg in pages]
    for comp in range(2):
        acc_a = jnp.zeros((rows, NSA_HEAD_DIM), jnp.float32)
        acc_b = jnp.zeros((rows, NSA_HEAD_DIM), jnp.float32)
        by_row = [jnp.swapaxes(s[comp * NSA_N_KV + h].reshape(per_page, NSA_CMP_STRIDE, NSA_HEAD_DIM), 0, 1)
                  for s in slabs for h in range(NSA_N_KV)]
        for j0 in range(0, NSA_CMP_STRIDE, 2):
            xa, xb = [], []
            for j in (j0, j0 + 1):
                x = jnp.concatenate([t[j] for t in by_row], axis=0)
                xa.append((x + pe_ref[comp, j:j + 1, :]).astype(jnp.bfloat16))
                xb.append((x + pe_ref[comp, NSA_CMP_STRIDE + j:NSA_CMP_STRIDE + j + 1, :]).astype(jnp.bfloat16))
            lo = j0 * NSA_HEAD_DIM
            acc_a = acc_a + jnp.dot(jnp.concatenate(xa, axis=1), w1_ref[comp, lo:lo + 2 * NSA_HEAD_DIM, :],
                                    preferred_element_type=jnp.float32)
            acc_b = acc_b + jnp.dot(jnp.concatenate(xb, axis=1),
                                    w1_ref[comp, half + lo:half + lo + 2 * NSA_HEAD_DIM, :],
                                    preferred_element_type=jnp.float32)
        shape = (NSA_CMP_PAGES, NSA_N_KV, per_page, NSA_HEAD_DIM)
        a_ref[comp] = acc_a.reshape(shape)
        b_ref[comp] = acc_b.reshape(shape)


def nsa_decode_compress(cache, page_ids, new_rows, w_cmp1, w_cmp2, cmp_pe):
    b_, n_pages = page_ids.shape
    page = cache.shape[1]
    T = new_rows.shape[1]
    pos0 = n_pages * page
    lp = _round_up(pos0 + T, NSA_SEL_BLOCK)
    n_cmp = lp // NSA_CMP_STRIDE - (NSA_CMP_BLOCK // NSA_CMP_STRIDE - 1)
    per_page = page // NSA_CMP_STRIDE
    assert n_pages % NSA_CMP_PAGES == 0 and NSA_CMP_BLOCK == 2 * NSA_CMP_STRIDE
    w1 = w_cmp1.astype(jnp.bfloat16)

    def page_spec(k):
        return pl.BlockSpec((None, page, NSA_HALF_SLABS, NSA_HEAD_DIM),
                            lambda b, s, pt: (pt[b, NSA_CMP_PAGES * s + k], 0, 0, 0))

    ab_shape = jax.ShapeDtypeStruct((b_, 2, n_pages, NSA_N_KV, per_page, NSA_HEAD_DIM), jnp.float32)
    ab_spec = pl.BlockSpec((None, 2, NSA_CMP_PAGES, NSA_N_KV, per_page, NSA_HEAD_DIM),
                           lambda b, s, pt: (b, 0, s, 0, 0, 0))
    part_a, part_b = pl.pallas_call(
        _nsa_compress_kernel,
        grid_spec=pltpu.PrefetchScalarGridSpec(
            num_scalar_prefetch=1,
            grid=(b_, n_pages // NSA_CMP_PAGES),
            in_specs=[page_spec(k) for k in range(NSA_CMP_PAGES)]
            + [pl.BlockSpec(w1.shape, lambda b, s, pt: (0, 0, 0)),
               pl.BlockSpec(cmp_pe.shape, lambda b, s, pt: (0, 0, 0))],
            out_specs=[ab_spec, ab_spec]),
        out_shape=[ab_shape, ab_shape],
        compiler_params=pltpu.CompilerParams(
            dimension_semantics=("parallel", "arbitrary"),
            vmem_limit_bytes=V7X_VMEM_LIMIT_BYTES),
        name="nsa_compress_pages",
    )(page_ids, *([cache] * NSA_CMP_PAGES), w1, cmp_pe)

    def strides(t):
        return jnp.transpose(t, (0, 1, 3, 2, 4, 5)).reshape(b_, 2, NSA_N_KV, n_pages * per_page, NSA_HEAD_DIM)

    n_tail = (lp - pos0) // NSA_CMP_STRIDE
    tail = jnp.pad(new_rows, ((0, 0), (0, lp - pos0 - T), (0, 0), (0, 0), (0, 0)))
    tail = jnp.transpose(tail.reshape(b_, n_tail, NSA_CMP_STRIDE, 2, NSA_N_KV, NSA_HEAD_DIM), (0, 3, 4, 1, 2, 5))
    w1s = w_cmp1.reshape(2, 2, NSA_CMP_STRIDE, NSA_HEAD_DIM, -1)
    pes = cmp_pe.reshape(2, 2, NSA_CMP_STRIDE, NSA_HEAD_DIM)
    tail_a = jnp.einsum('bchsjd,cjdk->bchsk', tail + pes[None, :, 0, None, None], w1s[:, 0])
    tail_b = jnp.einsum('bchsjd,cjdk->bchsk', tail + pes[None, :, 1, None, None], w1s[:, 1])
    full_a = jnp.concatenate([strides(part_a), tail_a], axis=3)
    full_b = jnp.concatenate([strides(part_b), tail_b], axis=3)
    hidden = jax.nn.gelu(full_a[:, :, :, :n_cmp] + full_b[:, :, :, 1:n_cmp + 1])
    out = jnp.einsum('bchnk,ckd->bcnhd', hidden, w_cmp2).reshape(b_, 2, n_cmp, NSA_KVW)
    return out[:, 0], out[:, 1]


def _nsa_softmax_rows(s, mask):
    s = jnp.where(mask, s, NEG_INF)
    e = jnp.exp(s - jnp.max(s, axis=-1, keepdims=True))
    return e / jnp.sum(e, axis=-1, keepdims=True)


def _nsa_decode_select_kernel(q_ref, kc_ref, vc_ref, wk_ref, wv_ref, ocmp_ref, owin_ref, sel_ref,
                              score_ref, rank_ref, *, n_cmp, n_sel, n_top, n_win, w_buf, pos0, t_new):
    f32, bf16 = jnp.float32, jnp.bfloat16
    nt = (((1,), (1,)), ((), ()))
    scale = NSA_HEAD_DIM ** -0.5
    rows = q_ref.shape[0]
    per_head = NSA_GQA * t_new
    q = q_ref[...]
    tq = pos0 + (lax.broadcasted_iota(jnp.int32, (rows, 1), 0) & (t_new - 1))

    def heads_out(o_ref, p, v_ref):
        for h in range(NSA_N_KV):
            o_ref[h * per_head:(h + 1) * per_head, :] = jnp.dot(
                p[h * per_head:(h + 1) * per_head].astype(bf16),
                v_ref[:, h * NSA_HEAD_DIM:(h + 1) * NSA_HEAD_DIM], preferred_element_type=f32)

    n_cp = kc_ref.shape[0]
    s = lax.dot_general(q, kc_ref[...], nt, preferred_element_type=f32) * scale
    n_idx = lax.broadcasted_iota(jnp.int32, (rows, n_cp), 1)
    cmask = (n_idx * NSA_CMP_STRIDE + (NSA_CMP_BLOCK - 1) <= tq) & (n_idx < n_cmp)
    p = jnp.where(cmask, _nsa_softmax_rows(s, cmask), 0.0)
    heads_out(ocmp_ref, p, vc_ref)

    psum = jnp.concatenate(
        [sum(p[h * per_head + g * t_new:h * per_head + (g + 1) * t_new] for g in range(NSA_GQA))
         for h in range(NSA_N_KV)], axis=0)
    n_sp = sel_ref.shape[1]
    cols = NSA_N_KV * t_new
    jn = lax.broadcasted_iota(jnp.int32, (n_sp, n_cp), 0)
    nn = lax.broadcasted_iota(jnp.int32, (n_sp, n_cp), 1)
    r = NSA_SEL_BLOCK // NSA_CMP_STRIDE
    span = NSA_CMP_BLOCK // NSA_CMP_STRIDE
    pool = jnp.where((nn >= r * jn - (span - 1)) & (nn <= r * jn + (r - 1)), 1.0, 0.0).astype(bf16)
    imp = jnp.zeros((n_sp, cols), f32)
    rest = psum
    for _ in range(3):
        part = rest.astype(bf16)
        imp = imp + lax.dot_general(pool, part, nt, preferred_element_type=f32)
        rest = rest - part.astype(f32)
    jidx = lax.broadcasted_iota(jnp.int32, (n_sp, cols), 0)
    cur = (pos0 + (lax.broadcasted_iota(jnp.int32, (n_sp, cols), 1) & (t_new - 1))) >> NSA_SEL_SHIFT
    forced = (jidx == 0) | (jidx == cur) | (jidx == cur - 1)
    score = jnp.where(forced, FORCE_SCORE, jnp.where(jidx <= cur, imp, -FORCE_SCORE))
    score_ref[...] = jnp.where(jidx < n_sel, score, -2.0 * FORCE_SCORE)
    rank_ref[...] = jnp.zeros(rank_ref.shape, jnp.int32)

    def rank_body(k, carry):
        row = score_ref[pl.ds(k, 1), :]
        sc = score_ref[...]
        before = (row > sc) | ((row == sc) & (k < jidx))
        rank_ref[...] = rank_ref[...] + jnp.where(before, 1, 0)
        return carry

    lax.fori_loop(0, n_sel, rank_body, 0)
    sel_t = jnp.where((rank_ref[...] < n_top) & (jidx < n_sel), 1.0, 0.0).astype(bf16)
    ri = lax.broadcasted_iota(jnp.int32, (rows, cols), 0)
    ci = lax.broadcasted_iota(jnp.int32, (rows, cols), 1)
    same = (((ri >> _log2(per_head)) == (ci >> _log2(t_new)))
            & ((ri & (t_new - 1)) == (ci & (t_new - 1))))
    spread = jnp.where(same, 1.0, 0.0).astype(bf16)
    sel_ref[...] = lax.dot_general(spread, sel_t, nt, preferred_element_type=f32).astype(bf16)

    s = lax.dot_general(q, wk_ref[...], nt, preferred_element_type=f32) * scale
    kidx = lax.broadcasted_iota(jnp.int32, (rows, wk_ref.shape[0]), 1)
    diff = tq - (pos0 - w_buf + kidx)
    wmask = (diff >= 0) & (diff < NSA_WINDOW) & (kidx < n_win) & (pos0 - w_buf + kidx >= 0)
    heads_out(owin_ref, _nsa_softmax_rows(s, wmask), wv_ref)


def _nsa_decode_slc_kernel(pt_ref, q_ref, sel_ref, new_ref, *refs, pos0, t_new):
    del pt_ref
    pages = refs[:NSA_SLC_PAGES]
    o_ref, m_ref, l_ref, acc_ref = refs[NSA_SLC_PAGES:]
    f32, bf16 = jnp.float32, jnp.bfloat16
    nt = (((1,), (1,)), ((), ()))
    step = pl.program_id(1)
    rows = q_ref.shape[0]
    page = pages[0].shape[0]
    per_head = NSA_GQA * t_new
    n_sp = sel_ref.shape[1]
    q = q_ref[...]
    tq = pos0 + (lax.broadcasted_iota(jnp.int32, (rows, 1), 0) & (t_new - 1))
    row_head = lax.broadcasted_iota(jnp.int32, (rows, page), 0) >> _log2(per_head)
    lane = lax.broadcasted_iota(jnp.int32, (rows, page), 1)
    blocks_per_page = page // NSA_SEL_BLOCK

    def attend(pg_ref, page_index):
        slabs = jnp.swapaxes(pg_ref[...], 0, 1)
        slab = lambda c: slabs[c].astype(bf16)
        kp = jnp.concatenate([slab(h) for h in range(NSA_N_KV)], axis=1)
        v_heads = jnp.concatenate([slab(NSA_N_KV + h) for h in range(NSA_N_KV)], axis=0)
        s = lax.dot_general(q, kp, nt, preferred_element_type=f32) * (NSA_HEAD_DIM ** -0.5)
        jrow = lax.broadcasted_iota(jnp.int32, (n_sp, page), 0)
        jcol = page_index * blocks_per_page + (lax.broadcasted_iota(jnp.int32, (n_sp, page), 1) >> NSA_SEL_SHIFT)
        expand = jnp.where(jrow == jcol, 1.0, 0.0).astype(bf16)
        chosen = jnp.dot(sel_ref[...], expand, preferred_element_type=f32) > 0.5
        s = jnp.where(chosen & (page_index * page + lane <= tq), s, NEG_INF)
        m_old = m_ref[...]
        m_new = jnp.maximum(m_old, jnp.max(s, axis=-1, keepdims=True))
        alpha = jnp.exp(m_old - m_new)
        p = jnp.exp(s - m_new)
        l_ref[...] = alpha * l_ref[...] + jnp.sum(p, axis=-1, keepdims=True)
        p_heads = jnp.concatenate([jnp.where(row_head == h, p, 0.0) for h in range(NSA_N_KV)], axis=1).astype(bf16)
        acc_ref[...] = alpha * acc_ref[...] + jnp.dot(p_heads, v_heads, preferred_element_type=f32)
        m_ref[...] = m_new

    @pl.when(step == 0)
    def _():
        m_ref[...] = jnp.full(m_ref.shape, NEG_INF, f32)
        l_ref[...] = jnp.zeros(l_ref.shape, f32)
        acc_ref[...] = jnp.zeros(acc_ref.shape, f32)
        attend(new_ref, pos0 // page)

    for k, pg in enumerate(pages):
        attend(pg, step * NSA_SLC_PAGES + k)

    @pl.when(step == pl.num_programs(1) - 1)
    def _():
        o_ref[...] = acc_ref[...] / l_ref[...]


def nsa_decode_attention(q, kv, cache, page_ids, win_buf, w_cmp1, w_cmp2, cmp_pe):
    b_, T, _ = q.shape
    n_pages = page_ids.shape[1]
    page = cache.shape[1]
    pos0 = n_pages * page
    w_buf = win_buf.shape[1]
    assert T & (T - 1) == 0 and T <= NSA_SEL_BLOCK and pos0 % NSA_SEL_BLOCK == 0 and page % NSA_SEL_BLOCK == 0
    assert n_pages % NSA_SLC_PAGES == 0
    bf16 = jnp.bfloat16
    lp = _round_up(pos0 + T, NSA_SEL_BLOCK)
    n_sel = lp // NSA_SEL_BLOCK
    n_sp = _round_up(n_sel, 128)
    kv6 = kv.reshape(b_, T, 6, NSA_N_KV, NSA_HEAD_DIM)
    kc, vc = nsa_decode_compress(cache, page_ids, kv6[:, :, 0:2], w_cmp1, w_cmp2, cmp_pe)
    n_cmp = kc.shape[1]
    n_cp = _round_up(n_cmp, 128)
    pad_c = ((0, 0), (0, n_cp - n_cmp), (0, 0))
    kc, vc = jnp.pad(kc, pad_c).astype(bf16), jnp.pad(vc, pad_c).astype(bf16)

    rows = NSA_N_HEADS * T
    q5 = jnp.transpose(q.reshape(b_, T, NSA_N_KV, NSA_GQA, NSA_HEAD_DIM), (0, 2, 3, 1, 4))
    q_blk = jnp.einsum('bhgtd,hk->bhgtkd', q5, jnp.eye(NSA_N_KV, dtype=q.dtype))
    q_blk = q_blk.reshape(b_, rows, NSA_KVW).astype(bf16)

    n_win = w_buf + T
    n_wp = _round_up(n_win, 128)
    wk = jnp.concatenate([win_buf, kv6[:, :, 4:6]], axis=1)
    wk = jnp.pad(wk, ((0, 0), (0, n_wp - n_win), (0, 0), (0, 0), (0, 0))).astype(bf16)
    wkk, wkv = wk[:, :, 0].reshape(b_, n_wp, NSA_KVW), wk[:, :, 1].reshape(b_, n_wp, NSA_KVW)

    per_b = lambda n, w: pl.BlockSpec((None, n, w), lambda b: (b, 0, 0))
    o_shape = jax.ShapeDtypeStruct((b_, rows, NSA_HEAD_DIM), jnp.float32)
    o_cmp, o_win, sel = pl.pallas_call(
        functools.partial(_nsa_decode_select_kernel, n_cmp=n_cmp, n_sel=n_sel, n_top=min(NSA_N_SELECT, n_sel),
                          n_win=n_win, w_buf=w_buf, pos0=pos0, t_new=T),
        grid=(b_,),
        in_specs=[per_b(rows, NSA_KVW), per_b(n_cp, NSA_KVW), per_b(n_cp, NSA_KVW),
                  per_b(n_wp, NSA_KVW), per_b(n_wp, NSA_KVW)],
        out_specs=[per_b(rows, NSA_HEAD_DIM), per_b(rows, NSA_HEAD_DIM), per_b(rows, n_sp)],
        out_shape=[o_shape, o_shape, jax.ShapeDtypeStruct((b_, rows, n_sp), bf16)],
        scratch_shapes=[pltpu.VMEM((n_sp, NSA_N_KV * T), jnp.float32),
                        pltpu.VMEM((n_sp, NSA_N_KV * T), jnp.int32)],
        compiler_params=pltpu.CompilerParams(
            dimension_semantics=("parallel",), vmem_limit_bytes=V7X_VMEM_LIMIT_BYTES),
        name="nsa_decode_select",
    )(q_blk, kc, vc, wkk, wkv)

    new_slc = jnp.pad(kv6[:, :, 2:4].reshape(b_, T, NSA_HALF_SLABS, NSA_HEAD_DIM),
                      ((0, 0), (0, page - T), (0, 0), (0, 0)))
    half_page = (None, page, NSA_HALF_SLABS, NSA_HEAD_DIM)

    def page_spec(k):
        return pl.BlockSpec(half_page, lambda b, s, pt: (pt[b, NSA_SLC_PAGES * s + k], 0, 1, 0))

    bs = lambda n, w: pl.BlockSpec((None, n, w), lambda b, s, pt: (b, 0, 0))
    o_slc = pl.pallas_call(
        functools.partial(_nsa_decode_slc_kernel, pos0=pos0, t_new=T),
        grid_spec=pltpu.PrefetchScalarGridSpec(
            num_scalar_prefetch=1,
            grid=(b_, n_pages // NSA_SLC_PAGES),
            in_specs=[bs(rows, NSA_KVW), bs(rows, n_sp), pl.BlockSpec(half_page, lambda b, s, pt: (b, 0, 0, 0))]
            + [page_spec(k) for k in range(NSA_SLC_PAGES)],
            out_specs=bs(rows, NSA_HEAD_DIM),
            scratch_shapes=[pltpu.VMEM((rows, 1), jnp.float32),
                            pltpu.VMEM((rows, 1), jnp.float32),
                            pltpu.VMEM((rows, NSA_HEAD_DIM), jnp.float32)]),
        out_shape=o_shape,
        compiler_params=pltpu.CompilerParams(
            dimension_semantics=("parallel", "arbitrary"), vmem_limit_bytes=V7X_VMEM_LIMIT_BYTES),
        name="nsa_decode_slc",
    )(page_ids, q_blk, sel, new_slc, *([cache] * NSA_SLC_PAGES))

    def token_major(o):
        o = o.reshape(b_, NSA_N_KV, NSA_GQA, T, NSA_HEAD_DIM)
        return jnp.transpose(o, (0, 3, 1, 2, 4)).reshape(b_, T, NSA_N_HEADS * NSA_HEAD_DIM)

    return token_major(o_cmp), token_major(o_slc), token_major(o_win)


def nsa_decode_mixer(x, cache, page_ids, win_buf, w_q, w_kv, w_gate, b_gate, w_cmp1, w_cmp2, cmp_pe, w_out):
    b_, T, _ = x.shape
    q = matmul(x, w_q)
    kv = matmul(x, w_kv)
    kv6 = kv.reshape(b_, T, 6, NSA_N_KV, NSA_HEAD_DIM)
    o_cmp, o_slc, o_win = nsa_decode_attention(q, kv, cache, page_ids, win_buf, w_cmp1, w_cmp2, cmp_pe)
    gate = jax.nn.sigmoid(matmul(x, w_gate) + b_gate).reshape(b_, T, NSA_N_HEADS, 3)

    def heads(t):
        return t.reshape(b_, T, NSA_N_HEADS, NSA_HEAD_DIM)

    o = gate[..., 0:1] * heads(o_cmp) + gate[..., 1:2] * heads(o_slc) + gate[..., 2:3] * heads(o_win)
    y = Proj(o.reshape(b_, T, NSA_N_HEADS * NSA_HEAD_DIM), w_out)
    win_new = jnp.concatenate([win_buf, kv6[:, :, 4:6]], axis=1)[:, -win_buf.shape[1]:]
    return y, kv6[:, :, :4], win_new


FFN_ROW_TILE = 1024
FFN_COL_TILE = 512
FFN_X_HALO = 16


def _ffn_up_kernel(x_ref, xh_ref, hist_ref, wa_ref, wg_ref, cw_ref, cb_ref, h_ref, tail_ref, ext_ref,
                   *, width, tiles_per_seq):
    f32 = jnp.float32
    rows = x_ref.shape[0]
    x = x_ref[...]
    a = jnp.dot(x, wa_ref[...], preferred_element_type=f32)
    g = jnp.dot(x, wg_ref[...], preferred_element_type=f32)
    before = jnp.dot(xh_ref[...], wa_ref[...], preferred_element_type=f32)[FFN_X_HALO - CONV_HALO:]
    starts_seq = pl.program_id(1) % tiles_per_seq == 0
    ext_ref[0:CONV_HALO, :] = jnp.where(starts_seq, hist_ref[...], before)
    ext_ref[CONV_HALO:, :] = a
    acc = cb_ref[...]
    for k in range(width):
        acc = acc + cw_ref[k:k + 1, :] * ext_ref[pl.ds(CONV_HALO - (width - 1 - k), rows), :]
    h_ref[...] = (jax.nn.gelu(acc) * g).astype(h_ref.dtype)
    tail_ref[...] = a[rows - CONV_HALO:]


def ffn_up_fused(x, hist, w_up, conv_w, conv_b):
    b_, T, K = x.shape
    stack, s = w_up
    width = conv_w.shape[0]
    tm, tn = FFN_ROW_TILE, FFN_COL_TILE
    assert T % tm == 0 and FFN_DIM % tn == 0 and width <= CONV_HALO + 1 and stack.shape[2] == 2 * FFN_DIM
    nj = FFN_DIM // tn
    tiles_per_seq = T // tm
    halo_blocks = tm // FFN_X_HALO
    x2 = x.astype(jnp.bfloat16).reshape(b_ * T, K)
    hist8 = jnp.pad(hist, ((0, 0), (CONV_HALO - (width - 1), 0), (0, 0)))
    per_seq = pl.BlockSpec((None, CONV_HALO, tn), lambda j, i: (i // tiles_per_seq, 0, j))
    h, tail = pl.pallas_call(
        functools.partial(_ffn_up_kernel, width=width, tiles_per_seq=tiles_per_seq),
        grid=(nj, b_ * tiles_per_seq),
        in_specs=[pl.BlockSpec((tm, K), lambda j, i: (i, 0)),
                  pl.BlockSpec((FFN_X_HALO, K), lambda j, i: (jnp.maximum(i * halo_blocks - 1, 0), 0)),
                  per_seq,
                  pl.BlockSpec((None, K, tn), lambda j, i: (s, 0, j)),
                  pl.BlockSpec((None, K, tn), lambda j, i: (s, 0, nj + j)),
                  pl.BlockSpec((width, tn), lambda j, i: (0, j)),
                  pl.BlockSpec((1, tn), lambda j, i: (0, j))],
        out_specs=[pl.BlockSpec((tm, tn), lambda j, i: (i, j)), per_seq],
        out_shape=[jax.ShapeDtypeStruct((b_ * T, FFN_DIM), jnp.bfloat16),
                   jax.ShapeDtypeStruct((b_, CONV_HALO, FFN_DIM), jnp.float32)],
        scratch_shapes=[pltpu.VMEM((CONV_HALO + tm, tn), jnp.float32)],
        compiler_params=pltpu.CompilerParams(
            dimension_semantics=("parallel", "arbitrary"),
            vmem_limit_bytes=V7X_VMEM_LIMIT_BYTES),
        name="ffn_up",
    )(x2, x2, hist8, stack, stack, conv_w, conv_b.reshape(1, FFN_DIM))
    return h.reshape(b_, T, FFN_DIM), tail


def conv_ffn(x, hist, w_up, conv_w, conv_b, w_down):
    keep = hist.shape[1]
    if x.shape[1] % FFN_ROW_TILE == 0:
        h, tail = ffn_up_fused(x, hist, w_up, conv_w, conv_b)
        return Proj(h, w_down), tail[:, -keep:]
    ag = matmul(x, w_up)
    h = conv_act(ag, 0, FFN_DIM, hist, conv_w, conv_b, "gelu_gate", gate_col0=FFN_DIM, out_dtype=jnp.bfloat16)
    return Proj(h, w_down), conv_tail(hist, ag, 0, FFN_DIM)


def kernel(x_prompt, x_sample, cache_nsa, state_nsa_win, state_ssd, state_ssd_conv, state_mlstm_c,
           state_mlstm_n, state_mlstm_m, state_mlstm_conv, state_s5, state_ffn_conv, page_table,
           ln_g, ln_b, ffn_w_up, ffn_conv_w, ffn_conv_b, ffn_w_down,
           ssd_w_in, ssd_conv_w, ssd_conv_b, ssd_dt_bias, ssd_a_log, ssd_d, ssd_norm_g, ssd_w_out,
           mlstm_w_up, mlstm_conv_w, mlstm_conv_b, mlstm_w_q, mlstm_w_k, mlstm_w_v, mlstm_w_if,
           mlstm_b_if, mlstm_skip, mlstm_norm_g, mlstm_w_down,
           s5_a_re, s5_a_im, s5_log_dt, s5_b_re, s5_b_im, s5_c_re, s5_c_im, s5_d, s5_w_glu_a, s5_w_glu_b,
           nsa_w_q, nsa_w_kv, nsa_w_gate, nsa_b_gate, nsa_w_cmp1, nsa_w_cmp2, nsa_cmp_pe, nsa_w_out):

    def bf16_stack(w):
        return w.astype(jnp.bfloat16)

    ffn_w_up, ffn_w_down = bf16_stack(ffn_w_up), bf16_stack(ffn_w_down)
    ssd_w_in, ssd_w_out = bf16_stack(ssd_w_in), bf16_stack(ssd_w_out)
    mlstm_w_up, mlstm_w_down = bf16_stack(mlstm_w_up), bf16_stack(mlstm_w_down)
    mlstm_w_if = bf16_stack(mlstm_w_if).reshape(-1, MLSTM_D_INNER, 2 * MLSTM_N_HEADS)
    s5_w_glu_a, s5_w_glu_b = bf16_stack(s5_w_glu_a), bf16_stack(s5_w_glu_b)
    nsa_w_q, nsa_w_kv, nsa_w_out = bf16_stack(nsa_w_q), bf16_stack(nsa_w_kv), bf16_stack(nsa_w_out)
    nsa_w_gate = bf16_stack(nsa_w_gate)

    def trunk(x, sample):
        b_, T, _ = x.shape
        dt_ = x.dtype
        pos0 = PAST_LEN if sample else 0
        o_nsa, o_win, o_ssd, o_ssdc, o_mc, o_mn, o_mm, o_mconv, o_s5, o_ffn = ([] for _ in range(10))
        xb = x.astype(jnp.bfloat16)

        def residual_norm(x, y, g, b):
            if isinstance(y, Proj):
                return matmul_residual_ln(y.h, y.w, x, g, b)
            out = layer_norm(DEEPNORM_ALPHA * x + y, g, b)
            return out, out.astype(jnp.bfloat16)
        for i in range(DEPTH):
            kind, j = i % N_MIXERS, i // N_MIXERS
            if kind == 0:
                hist = state_ssd_conv[j] if sample else jnp.zeros((b_, SSD_CONV_W - 1, SSD_CONV_DIM), dt_)
                h0 = state_ssd[j] if sample else jnp.zeros((b_, SSD_N_HEADS, SSD_HEADDIM, SSD_D_STATE), dt_)
                y, hist_new, h_new = ssd_mixer(xb, hist, h0, (ssd_w_in, j), ssd_conv_w[j], ssd_conv_b[j],
                                               ssd_dt_bias[j], ssd_a_log[j], ssd_d[j], ssd_norm_g[j],
                                               (ssd_w_out, j))
                o_ssd.append(h_new)
                o_ssdc.append(hist_new)
            elif kind == 1:
                hist = state_mlstm_conv[j] if sample else jnp.zeros((b_, MLSTM_CONV_W - 1, MLSTM_D_INNER), dt_)
                c0 = state_mlstm_c[j] if sample else jnp.zeros((b_, MLSTM_N_HEADS, MLSTM_HEAD_DIM, MLSTM_HEAD_DIM), dt_)
                n0 = state_mlstm_n[j] if sample else jnp.zeros((b_, MLSTM_N_HEADS, MLSTM_HEAD_DIM), dt_)
                m0 = state_mlstm_m[j] if sample else jnp.zeros((b_, MLSTM_N_HEADS), dt_)
                y, hist_new, c, n, m = mlstm_mixer(xb, hist, c0, n0, m0, (mlstm_w_up, j), mlstm_conv_w[j],
                                                   mlstm_conv_b[j], mlstm_w_q[j], mlstm_w_k[j], mlstm_w_v[j],
                                                   [(mlstm_w_if, 3 * j + part) for part in range(3)],
                                                   mlstm_b_if[j], mlstm_skip[j],
                                                   mlstm_norm_g[j], (mlstm_w_down, j))
                o_mc.append(c)
                o_mn.append(n)
                o_mm.append(m)
                o_mconv.append(hist_new)
            elif kind == 2:
                h0 = state_s5[j] if sample else jnp.zeros((b_, S5_N_GROUPS, S5_STATE, 2), dt_)
                y, h_new = s5_mixer(x, h0, s5_a_re[j], s5_a_im[j], s5_log_dt[j], s5_b_re[j], s5_b_im[j],
                                    s5_c_re[j], s5_c_im[j], s5_d[j], (s5_w_glu_a, j), (s5_w_glu_b, j))
                o_s5.append(h_new)
            else:
                nsa_w = ((nsa_w_q, j), (nsa_w_kv, j), (nsa_w_gate, j), nsa_b_gate[j], nsa_w_cmp1[j], nsa_w_cmp2[j],
                         nsa_cmp_pe[j], (nsa_w_out, j))
                if sample:
                    n_pool, page = cache_nsa.shape[1:3]
                    assert pos0 == page_table.shape[1] * page
                    y, rows, win_new = nsa_decode_mixer(
                        xb, cache_nsa.reshape(-1, page, NSA_ROW_SLABS, NSA_HEAD_DIM), page_table + j * n_pool,
                        state_nsa_win[j], *nsa_w)
                else:
                    assert pos0 == 0
                    y, rows, win_new = nsa_prompt_mixer(xb, *nsa_w)
                o_nsa.append(rows)
                o_win.append(win_new)
            x, xb = residual_norm(x, y, ln_g[i, 0], ln_b[i, 0])
            fhist = state_ffn_conv[i] if sample else jnp.zeros((b_, FFN_CONV_W - 1, FFN_DIM), dt_)
            y, fhist_new = conv_ffn(xb, fhist, (ffn_w_up, i), ffn_conv_w[i], ffn_conv_b[i], (ffn_w_down, i))
            o_ffn.append(fhist_new)
            x, xb = residual_norm(x, y, ln_g[i, 1], ln_b[i, 1])
        st = jnp.stack
        return (x, st(o_nsa), st(o_win), st(o_ssd), st(o_ssdc), st(o_mc), st(o_mn), st(o_mm),
                st(o_mconv), st(o_s5), st(o_ffn))

    (y_prompt, nsa_p, win_p, ssd_p, ssdc_p, mc_p, mn_p, mm_p, mconv_p, s5_p, ffn_p) = trunk(x_prompt, False)
    (y_sample, nsa_s, win_s, ssd_s, ssdc_s, mc_s, mn_s, mm_s, mconv_s, s5_s, ffn_s) = trunk(x_sample, True)
    return (y_prompt, y_sample, nsa_p, nsa_s, win_p, win_s, ssd_p, ssd_s, ssdc_p, ssdc_s, mc_p, mc_s,
            mn_p, mn_s, mm_p, mm_s, mconv_p, mconv_s, s5_p, s5_s, ffn_p, ffn_s)
```

```python
import functools
import math
from typing import NamedTuple

import jax
import jax.numpy as jnp
from jax import lax
from jax.experimental import pallas as pl
from jax.experimental.pallas import tpu as pltpu

D_MODEL = 2048
DEPTH = 4
PAST_LEN = 16384
N_MIXERS = 4

DEEPNORM_ALPHA = (2.0 * DEPTH) ** 0.25
LN_EPS = 1e-5
RMS_EPS = 1e-5
NEG_INF = -1e30
FORCE_SCORE = 1e4

SSD_D_INNER = 2 * D_MODEL
SSD_HEADDIM = 64
SSD_N_HEADS = SSD_D_INNER // SSD_HEADDIM
SSD_N_GROUPS = 8
SSD_D_STATE = 128
SSD_CONV_W = 4
SSD_CHUNK = 256
SSD_CONV_DIM = SSD_D_INNER + 2 * SSD_N_GROUPS * SSD_D_STATE

MLSTM_D_INNER = 2 * D_MODEL
MLSTM_N_HEADS = 4
MLSTM_HEAD_DIM = MLSTM_D_INNER // MLSTM_N_HEADS
MLSTM_CONV_W = 4
MLSTM_CHUNK = 64

S5_GROUP = 16
S5_N_GROUPS = D_MODEL // S5_GROUP
S5_STATE = 64

NSA_N_HEADS = 16
NSA_N_KV = 4
NSA_HEAD_DIM = D_MODEL // NSA_N_HEADS
NSA_GQA = NSA_N_HEADS // NSA_N_KV
NSA_CMP_BLOCK = 32
NSA_CMP_STRIDE = 16
NSA_SEL_BLOCK = 64
NSA_N_SELECT = 16
NSA_WINDOW = 512
NSA_QBLOCK = 32
NSA_WBLOCK = 128

FFN_DIM = 5632
FFN_CONV_W = 3

V7X_VMEM_LIMIT_BYTES = 48 * 1024 * 1024


def _mm_kernel(x_ref, w_ref, o_ref):
    o_ref[...] = jnp.dot(x_ref[...], w_ref[...], preferred_element_type=jnp.float32)


def _pick(dim, target):
    if dim <= target:
        return dim
    t = target
    while dim % t:
        t //= 2
    return t


def _mm_tiles(M, K, N):
    tm = _pick(M, 1024)
    tn = N if N <= 512 else (1024 if K <= 2048 else 512)
    double_buffered = 2 * (tm * K * 2 + K * tn * 2 + tm * tn * 4)
    assert double_buffered <= V7X_VMEM_LIMIT_BYTES, (M, K, N)
    return tm, tn


def matmul(x, w):
    stack, s = w if isinstance(w, tuple) else (w[None], 0)
    _, K, N = stack.shape
    lead = x.shape[:-1]
    x2 = x.astype(jnp.bfloat16).reshape(-1, K)
    M = x2.shape[0]
    tm, tn = _mm_tiles(M, K, N)
    out = pl.pallas_call(
        _mm_kernel,
        grid=(M // tm, pl.cdiv(N, tn)),
        in_specs=[pl.BlockSpec((tm, K), lambda i, j: (i, 0)),
                  pl.BlockSpec((None, K, tn), lambda i, j: (s, 0, j))],
        out_specs=pl.BlockSpec((tm, tn), lambda i, j: (i, j)),
        out_shape=jax.ShapeDtypeStruct((M, N), jnp.float32),
        compiler_params=pltpu.CompilerParams(
            dimension_semantics=("parallel", "arbitrary"),
            vmem_limit_bytes=V7X_VMEM_LIMIT_BYTES),
        name="matmul",
    )(x2, stack.astype(jnp.bfloat16))
    return out.reshape(lead + (N,))


class Proj(NamedTuple):
    h: jax.Array
    w: object


LN_ROW_TILE = 512
LN_K_TILE = 1408


def _mm_res_ln_kernel(h_ref, w_ref, x_ref, g_ref, b_ref, o_ref, ob_ref, acc_ref):
    k = pl.program_id(1)

    @pl.when(k == 0)
    def _():
        acc_ref[...] = jnp.zeros_like(acc_ref)

    acc_ref[...] += jnp.dot(h_ref[...], w_ref[...], preferred_element_type=jnp.float32)

    @pl.when(k == pl.num_programs(1) - 1)
    def _():
        z = DEEPNORM_ALPHA * x_ref[...] + acc_ref[...]
        mu = jnp.mean(z, axis=-1, keepdims=True)
        var = jnp.mean(jnp.square(z - mu), axis=-1, keepdims=True)
        out = (z - mu) * lax.rsqrt(var + LN_EPS) * g_ref[...] + b_ref[...]
        o_ref[...] = out
        ob_ref[...] = out.astype(ob_ref.dtype)


def matmul_residual_ln(h, w, x, g, b):
    stack, s = w if isinstance(w, tuple) else (w[None], 0)
    _, K, N = stack.shape
    lead = x.shape[:-1]
    h2 = h.astype(jnp.bfloat16).reshape(-1, K)
    x2 = x.reshape(-1, N)
    M = x2.shape[0]
    tm = _pick(M, LN_ROW_TILE)
    tk = VREG_LANES * max(d for d in range(1, K // VREG_LANES + 1)
                          if (K // VREG_LANES) % d == 0 and VREG_LANES * d <= LN_K_TILE)
    rows = pl.BlockSpec((tm, N), lambda i, k: (i, 0))
    vec = pl.BlockSpec((1, N), lambda i, k: (0, 0))
    out, out_b = pl.pallas_call(
        _mm_res_ln_kernel,
        grid=(M // tm, K // tk),
        in_specs=[pl.BlockSpec((tm, tk), lambda i, k: (i, k)),
                  pl.BlockSpec((None, tk, N), lambda i, k: (s, k, 0)),
                  rows, vec, vec],
        out_specs=[rows, rows],
        out_shape=[jax.ShapeDtypeStruct((M, N), jnp.float32), jax.ShapeDtypeStruct((M, N), jnp.bfloat16)],
        scratch_shapes=[pltpu.VMEM((tm, N), jnp.float32)],
        compiler_params=pltpu.CompilerParams(
            dimension_semantics=("parallel", "arbitrary"),
            vmem_limit_bytes=V7X_VMEM_LIMIT_BYTES),
        name="matmul_residual_ln",
    )(h2, stack.astype(jnp.bfloat16), x2, g.reshape(1, N), b.reshape(1, N))
    return out.reshape(lead + (N,)), out_b.reshape(lead + (N,))


def layer_norm(x, g, b):
    mu = jnp.mean(x, axis=-1, keepdims=True)
    var = jnp.mean(jnp.square(x - mu), axis=-1, keepdims=True)
    return (x - mu) * lax.rsqrt(var + LN_EPS) * g + b


CONV_TIME_TILE = 512
CONV_CHAN_TILE = 2048
QKV_CHAN_TILE = 1024
CONV_HALO = 8
VREG_LANES = 128


def _causal_conv_tile(cur_ref, prev_ref, hist_ref, w_ref, b_ref, ext_ref, width):
    rows = cur_ref.shape[0]
    ext_ref[0:CONV_HALO, :] = jnp.where(pl.program_id(1) == 0, hist_ref[...], prev_ref[...])
    ext_ref[CONV_HALO:, :] = cur_ref[...]
    acc = b_ref[...]
    for k in range(width):
        acc = acc + w_ref[k:k + 1, :] * ext_ref[pl.ds(CONV_HALO - (width - 1 - k), rows), :]
    return acc


def _conv_act_kernel(cur_ref, prev_ref, hist_ref, w_ref, b_ref, *rest, width, act):
    acc = _causal_conv_tile(cur_ref, prev_ref, hist_ref, w_ref, b_ref, rest[-1], width)
    if act == "silu":
        o_ref = rest[0]
        o_ref[...] = jax.nn.silu(acc).astype(o_ref.dtype)
    else:
        g_ref, o_ref = rest[:2]
        o_ref[...] = (jax.nn.gelu(acc) * g_ref[...]).astype(o_ref.dtype)


def _conv_tiles(T, chans, offsets, max_lanes):
    lanes = 128
    units = math.gcd(chans // lanes, *[o // lanes for o in offsets])
    ct = lanes * max(d for d in range(1, units + 1) if units % d == 0 and lanes * d <= max_lanes)
    tt = min(T, CONV_TIME_TILE)
    assert T % tt == 0 and tt % CONV_HALO == 0 and chans % lanes == 0 and all(o % lanes == 0 for o in offsets)
    return tt, ct


def conv_act(src, col0, chans, hist, w, b, act, gate_col0=None, out_dtype=jnp.float32):
    b_, T, _ = src.shape
    width = w.shape[0]
    gated = act == "gelu_gate"
    tt, ct = _conv_tiles(T, chans, [col0, gate_col0] if gated else [col0], CONV_CHAN_TILE)
    assert width <= CONV_HALO + 1
    hist8 = jnp.pad(hist, ((0, 0), (CONV_HALO - (width - 1), 0), (0, 0)))
    halo_blocks = tt // CONV_HALO
    cb0 = col0 // ct
    in_specs = [pl.BlockSpec((None, tt, ct), lambda b, t, c: (b, t, cb0 + c)),
                pl.BlockSpec((None, CONV_HALO, ct), lambda b, t, c: (b, jnp.maximum(t * halo_blocks - 1, 0), cb0 + c)),
                pl.BlockSpec((None, CONV_HALO, ct), lambda b, t, c: (b, 0, c)),
                pl.BlockSpec((width, ct), lambda b, t, c: (0, c)),
                pl.BlockSpec((1, ct), lambda b, t, c: (0, c))]
    args = [src, src, hist8, w, b.reshape(1, chans)]
    if gated:
        gb0 = gate_col0 // ct
        in_specs.append(pl.BlockSpec((None, tt, ct), lambda b, t, c: (b, t, gb0 + c)))
        args.append(src)
    return pl.pallas_call(
        functools.partial(_conv_act_kernel, width=width, act=act),
        grid=(b_, T // tt, chans // ct),
        in_specs=in_specs,
        out_specs=pl.BlockSpec((None, tt, ct), lambda b, t, c: (b, t, c)),
        out_shape=jax.ShapeDtypeStruct((b_, T, chans), out_dtype),
        scratch_shapes=[pltpu.VMEM((CONV_HALO + tt, ct), jnp.float32)],
        compiler_params=pltpu.CompilerParams(
            dimension_semantics=("parallel", "parallel", "parallel"),
            vmem_limit_bytes=V7X_VMEM_LIMIT_BYTES),
        name="conv_act",
    )(*args)


def _blockdiag_coefs(w):
    nb, bs, _ = w.shape
    shifts = jnp.stack([jnp.eye(bs, k=d, dtype=w.dtype) for d in range(-(bs - 1), bs)])
    return jnp.einsum('ncd,kcd->knd', w, shifts).reshape(2 * bs - 1, nb * bs)


def _mlstm_qkv_kernel(cur_ref, prev_ref, hist_ref, w_ref, b_ref, cq_ref, ck_ref, cv_ref,
                      xc_ref, q_ref, k_ref, v_ref, ext_ref, *, width, bs):
    xc_ref[...] = jax.nn.silu(_causal_conv_tile(cur_ref, prev_ref, hist_ref, w_ref, b_ref, ext_ref, width))

    def project(x_ref, col, coef_refs, out_refs):
        x = x_ref[:, col]
        outs = [jnp.zeros(x.shape, jnp.float32) for _ in coef_refs]
        for d in range(-(bs - 1), bs):
            moved = x if d == 0 else pltpu.roll(x, d % VREG_LANES, 1)
            row = bs - 1 + d
            outs = [o + c[row:row + 1, col] * moved for o, c in zip(outs, coef_refs)]
        for o, o_ref in zip(outs, out_refs):
            o_ref[:, col] = o.astype(o_ref.dtype)

    for c0 in range(0, cur_ref.shape[1], VREG_LANES):
        col = slice(c0, c0 + VREG_LANES)
        project(xc_ref, col, (cq_ref, ck_ref), (q_ref, k_ref))
        project(cur_ref, col, (cv_ref,), (v_ref,))


def mlstm_qkv(up, hist, conv_w, conv_b, w_q, w_k, w_v):
    b_, T, _ = up.shape
    chans = MLSTM_D_INNER
    width = conv_w.shape[0]
    bs = w_q.shape[1]
    tt, ct = _conv_tiles(T, chans, [0], QKV_CHAN_TILE)
    assert VREG_LANES % bs == 0 and width <= CONV_HALO + 1
    hist8 = jnp.pad(hist, ((0, 0), (CONV_HALO - (width - 1), 0), (0, 0)))
    halo_blocks = tt // CONV_HALO
    tile = pl.BlockSpec((None, tt, ct), lambda b, t, c: (b, t, c))
    coef = pl.BlockSpec((2 * bs - 1, ct), lambda b, t, c: (0, c))
    shape = lambda dt: jax.ShapeDtypeStruct((b_, T, chans), dt)
    return pl.pallas_call(
        functools.partial(_mlstm_qkv_kernel, width=width, bs=bs),
        grid=(b_, T // tt, chans // ct),
        in_specs=[tile,
                  pl.BlockSpec((None, CONV_HALO, ct), lambda b, t, c: (b, jnp.maximum(t * halo_blocks - 1, 0), c)),
                  pl.BlockSpec((None, CONV_HALO, ct), lambda b, t, c: (b, 0, c)),
                  pl.BlockSpec((width, ct), lambda b, t, c: (0, c)),
                  pl.BlockSpec((1, ct), lambda b, t, c: (0, c)),
                  coef, coef, coef],
        out_specs=[tile, tile, tile, tile],
        out_shape=[shape(jnp.float32), shape(jnp.bfloat16), shape(jnp.bfloat16), shape(jnp.bfloat16)],
        scratch_shapes=[pltpu.VMEM((CONV_HALO + tt, ct), jnp.float32)],
        compiler_params=pltpu.CompilerParams(
            dimension_semantics=("parallel", "parallel", "parallel"),
            vmem_limit_bytes=V7X_VMEM_LIMIT_BYTES),
        name="mlstm_qkv",
    )(up, up, hist8, conv_w, conv_b.reshape(1, chans),
      _blockdiag_coefs(w_q), _blockdiag_coefs(w_k), _blockdiag_coefs(w_v))


def conv_tail(hist, src, col0, chans):
    keep = hist.shape[1]
    return jnp.concatenate([hist, src[:, -keep:, col0:col0 + chans]], axis=1)[:, -keep:]


SSD_HEADS_PER_GROUP = SSD_N_HEADS // SSD_N_GROUPS
SSD_GROUP_WIDTH = SSD_HEADS_PER_GROUP * SSD_HEADDIM
SSD_HEAD_SHIFT = SSD_HEADDIM.bit_length() - 1
assert 1 << SSD_HEAD_SHIFT == SSD_HEADDIM and SSD_GROUP_WIDTH == SSD_D_INNER // SSD_N_GROUPS


def _ssd_kernel(x_ref, z_ref, cm_ref, bmt_ref, cols_ref, rows_ref, h0_ref, d_ref, g_ref,
                y_ref, h_out_ref, h_ref):
    f32, bf16 = jnp.float32, jnp.bfloat16
    ck = pl.program_id(2)
    L, W = x_ref.shape
    R = SSD_HEADS_PER_GROUP

    @pl.when(ck == 0)
    def _():
        h_ref[...] = h0_ref[...]

    def per_channel(c):
        spread = jnp.where((lax.broadcasted_iota(jnp.int32, (R, W), 1) >> SSD_HEAD_SHIFT)
                           == lax.broadcasted_iota(jnp.int32, (R, W), 0), 1.0, 0.0).astype(bf16)
        out = jnp.zeros((L, W), f32)
        for _ in range(3):
            part = c.astype(bf16)
            out = out + jnp.dot(part, spread, preferred_element_type=f32)
            c = c - part.astype(f32)
        return out

    x = x_ref[...]
    xb = x.astype(bf16)
    cmb = cm_ref[...].astype(bf16)
    bmt = bmt_ref[...]
    cols = cols_ref[...]
    acs = per_channel(cols[:, :R])
    cb = jnp.dot(cmb, bmt, preferred_element_type=f32)
    causal = (lax.broadcasted_iota(jnp.int32, (L, L), 0) >= lax.broadcasted_iota(jnp.int32, (L, L), 1))
    lane_head = lax.broadcasted_iota(jnp.int32, (L, W), 1) >> SSD_HEAD_SHIFT
    y = jnp.zeros((L, W), f32)
    for r in range(R):
        acs_col = cols[:, r:r + 1]
        acs_row = rows_ref[r:r + 1, :]
        dt_row = rows_ref[SSD_HEADS_PER_GROUP + r:SSD_HEADS_PER_GROUP + r + 1, :]
        decay = jnp.exp(jnp.where(causal, acs_col - acs_row, -jnp.inf))
        w = (cb * decay * dt_row).astype(bf16)
        y = jnp.where(lane_head == r, jnp.dot(w, xb, preferred_element_type=f32), y)
    total = acs[L - 1:L, :]
    xw = (x * (jnp.exp(total - acs) * per_channel(cols[:, R:]))).astype(bf16)
    h_t = h_ref[...]
    y = y + jnp.dot(cmb, h_t.astype(bf16), preferred_element_type=f32) * jnp.exp(acs)
    h_ref[...] = jnp.exp(total) * h_t + jnp.dot(bmt, xw, preferred_element_type=f32)
    y = (y + d_ref[...] * x) * jax.nn.silu(z_ref[...])
    y_ref[...] = y * lax.rsqrt(jnp.mean(y * y, axis=-1, keepdims=True) + RMS_EPS) * g_ref[...]

    @pl.when(ck == pl.num_programs(2) - 1)
    def _():
        h_out_ref[...] = h_ref[...]


def ssd_cell(zx, xbc, dt, a, d_skip, norm_g, h0):
    b_, T, _ = xbc.shape
    G, R, P, N, W = SSD_N_GROUPS, SSD_HEADS_PER_GROUP, SSD_HEADDIM, SSD_D_STATE, SSD_GROUP_WIDTH
    L = math.gcd(T, SSD_CHUNK)
    nc = T // L
    acs = jnp.cumsum((dt * a).reshape(b_, nc, L, SSD_N_HEADS), axis=2)
    dtc = dt.reshape(b_, nc, L, SSD_N_HEADS)

    def rows(t):
        return jnp.transpose(t.reshape(b_, nc, L, G, R), (0, 3, 1, 4, 2))

    def cols(t):
        return jnp.transpose(t.reshape(b_, T, G, R), (0, 2, 1, 3))

    rowpack = jnp.concatenate([rows(acs), rows(dtc)], axis=3)
    colpack = jnp.concatenate([cols(acs), cols(dtc)], axis=3)
    bmt = jnp.transpose(xbc[..., SSD_D_INNER:SSD_D_INNER + G * N].astype(jnp.bfloat16).reshape(b_, T, G, N),
                        (0, 2, 3, 1))
    h0t = jnp.transpose(h0.reshape(b_, G, R, P, N), (0, 1, 4, 2, 3)).reshape(b_, G, N, W)
    chan = lambda b, g, c: (b, c, g)
    cm_block0 = (SSD_D_INNER + G * N) // N
    state_spec = pl.BlockSpec((None, None, N, W), lambda b, g, c: (b, g, 0, 0))
    row_spec = pl.BlockSpec((1, W), lambda b, g, c: (0, g))
    y, ht = pl.pallas_call(
        _ssd_kernel,
        grid=(b_, G, nc),
        in_specs=[pl.BlockSpec((None, L, W), chan),
                  pl.BlockSpec((None, L, W), chan),
                  pl.BlockSpec((None, L, N), lambda b, g, c: (b, c, cm_block0 + g)),
                  pl.BlockSpec((None, None, N, L), lambda b, g, c: (b, g, 0, c)),
                  pl.BlockSpec((None, None, L, 2 * R), lambda b, g, c: (b, g, c, 0)),
                  pl.BlockSpec((None, None, None, 2 * R, L), lambda b, g, c: (b, g, c, 0, 0)),
                  state_spec, row_spec, row_spec],
        out_specs=[pl.BlockSpec((None, L, W), chan), state_spec],
        out_shape=[jax.ShapeDtypeStruct((b_, T, SSD_D_INNER), jnp.float32),
                   jax.ShapeDtypeStruct((b_, G, N, W), jnp.float32)],
        scratch_shapes=[pltpu.VMEM((N, W), jnp.float32)],
        compiler_params=pltpu.CompilerParams(
            dimension_semantics=("parallel", "parallel", "arbitrary"),
            vmem_limit_bytes=V7X_VMEM_LIMIT_BYTES),
        name="ssd_cell",
    )(xbc, zx, xbc, bmt, colpack, rowpack, h0t,
      jnp.repeat(d_skip, P).reshape(1, SSD_D_INNER), norm_g.reshape(1, SSD_D_INNER))
    h_new = jnp.transpose(ht.reshape(b_, G, N, R, P), (0, 1, 3, 4, 2)).reshape(b_, SSD_N_HEADS, P, N)
    return y, h_new


def ssd_mixer(x, conv_hist, h0, w_in, conv_w, conv_b, dt_bias, a_log, d_skip, norm_g, w_out):
    zx = matmul(x, w_in)
    xbc = conv_act(zx, SSD_D_INNER, SSD_CONV_DIM, conv_hist, conv_w, conv_b, "silu")
    dt = jax.nn.softplus(zx[..., SSD_D_INNER + SSD_CONV_DIM:] + dt_bias)
    y, h_new = ssd_cell(zx, xbc, dt, -jnp.exp(a_log), d_skip, norm_g, h0)
    return Proj(y, w_out), conv_tail(conv_hist, zx, SSD_D_INNER, SSD_CONV_DIM), h_new


MLSTM_STEP = 256
MLSTM_NORM_LANES = 128


def _mlstm_kernel(q_ref, kt_ref, v_ref, acol_ref, grow_ref, c0_ref, n0_ref, m0_ref, g_ref,
                  xc_ref, z_ref, skip_ref, h_ref, c_out_ref, n_out_ref, m_out_ref, c_ref, m_ref):
    f32, bf16 = jnp.float32, jnp.bfloat16
    step = pl.program_id(2)
    L, D = q_ref.shape

    @pl.when(step == 0)
    def _():
        c_ref[:, :D] = c0_ref[...]
        c_ref[:, D:] = n0_ref[...]
        m_ref[...] = m0_ref[...]

    q = q_ref[...]
    kt = kt_ref[...]
    a_col = acol_ref[...]
    g_row = grow_ref[...]
    m_prev = m_ref[:, 0:1]
    causal = (lax.broadcasted_iota(jnp.int32, (L, L), 0) >= lax.broadcasted_iota(jnp.int32, (L, L), 1))
    dmat = jnp.where(causal, a_col + g_row, -jnp.inf)
    inter = m_prev + a_col
    m_t = jnp.maximum(inter, jnp.max(dmat, axis=1, keepdims=True))
    s = jnp.dot(q, kt, preferred_element_type=f32) * jnp.exp(dmat - m_t)
    sc_inter = jnp.exp(inter - m_t)
    qc = jnp.dot(q, c_ref[...].astype(bf16), preferred_element_type=f32)
    num = jnp.dot(s.astype(bf16), v_ref[...], preferred_element_type=f32) + sc_inter * qc[:, :D]
    den = jnp.sum(s, axis=1, keepdims=True) + sc_inter * qc[:, D:D + 1]
    h = num / jnp.maximum(jnp.abs(den), jnp.exp(-m_t))
    mu = jnp.mean(h, axis=-1, keepdims=True)
    var = jnp.mean(jnp.square(h - mu), axis=-1, keepdims=True)
    hn = (h - mu) * lax.rsqrt(var + LN_EPS) * g_ref[...]
    h_ref[...] = ((hn + skip_ref[...] * xc_ref[...]) * jax.nn.silu(z_ref[...])).astype(h_ref.dtype)

    m_new = m_t[L - 1:L, :]
    total = a_col[L - 1:L, :]
    decay = jnp.exp(total + g_row - m_new)
    sc_c = jnp.exp(m_prev + total - m_new)
    ktd = (kt.astype(f32) * decay).astype(bf16)
    one_hot = jnp.where(lax.broadcasted_iota(jnp.int32, (L, MLSTM_NORM_LANES), 1) == 0, 1.0, 0.0).astype(bf16)
    v_ext = jnp.concatenate([v_ref[...], one_hot], axis=1)
    c_ref[...] = sc_c * c_ref[...] + jnp.dot(ktd, v_ext, preferred_element_type=f32)
    m_ref[...] = jnp.broadcast_to(m_new, m_ref.shape)

    @pl.when(step == pl.num_programs(2) - 1)
    def _():
        c_out_ref[...] = c_ref[:, :D]
        n_out_ref[...] = c_ref[:, D:]
        m_out_ref[...] = m_ref[...]


def mlstm_cell(q, k, v, i_pre, logf, c0, n0, m0, norm_g, xc, up, skip):
    b_, T, _ = q.shape
    H, D = MLSTM_N_HEADS, MLSTM_HEAD_DIM
    L = min(T, MLSTM_STEP)
    ns = T // L
    bf16 = jnp.bfloat16
    kt = jnp.swapaxes((k * (D ** -0.5)).astype(bf16).reshape(b_, T, H, D), 1, 3)
    kt = jnp.swapaxes(kt, 1, 2)
    bcum = jnp.cumsum(logf.reshape(b_, ns, L, H), axis=2)
    acol = jnp.moveaxis(bcum, 3, 1).reshape(b_, H, T, 1)
    grow = jnp.moveaxis(i_pre.reshape(b_, ns, L, H) - bcum, 3, 1).reshape(b_, H, ns, 1, L)
    n0e = jnp.pad(n0[..., None], ((0, 0), (0, 0), (0, 0), (0, MLSTM_NORM_LANES - 1)))
    m0e = jnp.broadcast_to(m0[:, :, None, None], (b_, H, 1, MLSTM_NORM_LANES))

    tok_spec = pl.BlockSpec((None, L, D), lambda b, h, s: (b, s, h))
    state = lambda w: pl.BlockSpec((None, None, D, w), lambda b, h, s: (b, h, 0, 0))
    m_spec = pl.BlockSpec((None, None, 1, MLSTM_NORM_LANES), lambda b, h, s: (b, h, 0, 0))
    hn, c, n, m = pl.pallas_call(
        _mlstm_kernel,
        grid=(b_, H, ns),
        in_specs=[tok_spec,
                  pl.BlockSpec((None, None, D, L), lambda b, h, s: (b, h, 0, s)),
                  tok_spec,
                  pl.BlockSpec((None, None, L, 1), lambda b, h, s: (b, h, s, 0)),
                  pl.BlockSpec((None, None, None, 1, L), lambda b, h, s: (b, h, s, 0, 0)),
                  state(D), state(MLSTM_NORM_LANES), m_spec,
                  pl.BlockSpec((1, D), lambda b, h, s: (0, h)),
                  tok_spec,
                  pl.BlockSpec((None, L, D), lambda b, h, s: (b, s, H + h)),
                  pl.BlockSpec((1, D), lambda b, h, s: (0, h))],
        out_specs=[tok_spec, state(D), state(MLSTM_NORM_LANES), m_spec],
        out_shape=[jax.ShapeDtypeStruct((b_, T, H * D), bf16),
                   jax.ShapeDtypeStruct((b_, H, D, D), jnp.float32),
                   jax.ShapeDtypeStruct((b_, H, D, MLSTM_NORM_LANES), jnp.float32),
                   jax.ShapeDtypeStruct((b_, H, 1, MLSTM_NORM_LANES), jnp.float32)],
        scratch_shapes=[pltpu.VMEM((D, D + MLSTM_NORM_LANES), jnp.float32),
                        pltpu.VMEM((1, MLSTM_NORM_LANES), jnp.float32)],
        compiler_params=pltpu.CompilerParams(
            dimension_semantics=("parallel", "parallel", "arbitrary"),
            vmem_limit_bytes=V7X_VMEM_LIMIT_BYTES),
        name="mlstm_cell",
    )(q.astype(bf16), kt, v.astype(bf16), acol, grow, c0, n0e, m0e, norm_g.reshape(1, H * D),
      xc, up, skip.reshape(1, H * D))
    return hn, c, n[..., 0], m[:, :, 0, 0]


def mlstm_mixer(x, conv_hist, c0, n0, m0, w_up, conv_w, conv_b, w_q, w_k, w_v, w_if, b_if,
                skip, norm_g, w_down):
    b_, T, _ = x.shape
    up = matmul(x, w_up)
    xc, q, k, v = mlstm_qkv(up, conv_hist, conv_w, conv_b, w_q, w_k, w_v)
    gates = matmul(q, w_if[0]) + matmul(k, w_if[1]) + matmul(v, w_if[2]) + b_if
    i_pre, f_pre = gates[..., :MLSTM_N_HEADS], gates[..., MLSTM_N_HEADS:]
    h, c, n, m = mlstm_cell(q, k, v, i_pre, jax.nn.log_sigmoid(f_pre), c0, n0, m0, norm_g, xc, up, skip)
    return Proj(h, w_down), conv_tail(conv_hist, up, 0, MLSTM_D_INNER), c, n, m


S5_D_STATE = S5_N_GROUPS * S5_STATE
S5_PACK = 8
S5_N_PACKS = S5_N_GROUPS // S5_PACK
S5_SCAN_ROWS = 8
S5_SCAN_LANES = 256
S5_TIME_TILE = 256
S5_SCAN_UNROLL = 4


def _cmul(ar, ai, br, bi):
    return ar * br - ai * bi, ar * bi + ai * br


def _s5_kernel(x_ref, bre_ref, bim_ref, cre_ref, cim_ref, pw_ref, h0r_ref, h0i_ref, d_ref,
               g_ref, hr_out_ref, hi_out_ref, bur_ref, bui_ref, hr_ref, hi_ref):
    tt = pl.program_id(1)
    rows = x_ref.shape[0]
    pk_in = S5_PACK * S5_GROUP
    pk_st = S5_PACK * S5_STATE

    @pl.when(tt == 0)
    def _():
        hr_ref[...] = jnp.broadcast_to(h0r_ref[...], hr_ref.shape)
        hi_ref[...] = jnp.broadcast_to(h0i_ref[...], hi_ref.shape)

    for c in range(S5_N_PACKS):
        xc = x_ref[:, c * pk_in:(c + 1) * pk_in].astype(jnp.bfloat16)
        bur_ref[:, c * pk_st:(c + 1) * pk_st] = jnp.dot(xc, bre_ref[c], preferred_element_type=jnp.float32)
        bui_ref[:, c * pk_st:(c + 1) * pk_st] = jnp.dot(xc, bim_ref[c], preferred_element_type=jnp.float32)

    def col_body(cb, carry):
        cs = pl.ds(pl.multiple_of(cb * S5_SCAN_LANES, S5_SCAN_LANES), S5_SCAN_LANES)
        stages = [(pw_ref[2 * k, :, cs], pw_ref[2 * k + 1, :, cs], 1 << k) for k in range(3)]
        lr, li = pw_ref[6, :, cs], pw_ref[7, :, cs]

        def row_body(r, h):
            hr, hi = h
            rs = pl.ds(pl.multiple_of(r * S5_SCAN_ROWS, S5_SCAN_ROWS), S5_SCAN_ROWS)
            vr, vi = bur_ref[rs, cs], bui_ref[rs, cs]
            for mr, mi, s in stages:
                pr, pi = _cmul(mr, mi, pltpu.roll(vr, s, 0), pltpu.roll(vi, s, 0))
                vr, vi = vr + pr, vi + pi
            pr, pi = _cmul(lr, li, hr, hi)
            vr, vi = vr + pr, vi + pi
            bur_ref[rs, cs] = vr
            bui_ref[rs, cs] = vi
            last = S5_SCAN_ROWS - 1
            return (jnp.broadcast_to(vr[last:, :], vr.shape), jnp.broadcast_to(vi[last:, :], vi.shape))

        hr, hi = lax.fori_loop(0, rows // S5_SCAN_ROWS, row_body, (hr_ref[:, cs], hi_ref[:, cs]),
                               unroll=min(S5_SCAN_UNROLL, rows // S5_SCAN_ROWS))
        hr_ref[:, cs] = hr
        hi_ref[:, cs] = hi
        return carry

    lax.fori_loop(0, S5_D_STATE // S5_SCAN_LANES, col_body, 0)

    for c in range(S5_N_PACKS):
        hr = bur_ref[:, c * pk_st:(c + 1) * pk_st].astype(jnp.bfloat16)
        hi = bui_ref[:, c * pk_st:(c + 1) * pk_st].astype(jnp.bfloat16)
        y = (jnp.dot(hr, cre_ref[c], preferred_element_type=jnp.float32)
             - jnp.dot(hi, cim_ref[c], preferred_element_type=jnp.float32))
        cols = slice(c * pk_in, (c + 1) * pk_in)
        g_ref[:, cols] = jax.nn.gelu(y + d_ref[:, cols] * x_ref[:, cols])

    @pl.when(tt == pl.num_programs(1) - 1)
    def _():
        hr_out_ref[...] = hr_ref[0:1, :]
        hi_out_ref[...] = hi_ref[0:1, :]


def _block_diag_packs(w):
    g, r, c = w.shape
    eye = jnp.eye(S5_PACK, dtype=w.dtype)
    wb = jnp.einsum('kgrc,gh->kgrhc', w.reshape(g // S5_PACK, S5_PACK, r, c), eye)
    return wb.reshape(g // S5_PACK, S5_PACK * r, S5_PACK * c).astype(jnp.bfloat16)


def s5_mixer(x, h0, a_re, a_im, log_dt, b_re, b_im, c_re, c_im, d_skip, w_glu_a, w_glu_b):
    b_, T, _ = x.shape
    step = jnp.exp(log_dt)[:, None]
    mag = jnp.exp(a_re * step)
    ab_re, ab_im = mag * jnp.cos(a_im * step), mag * jnp.sin(a_im * step)
    den = a_re * a_re + a_im * a_im
    nr, ni = ab_re - 1.0, ab_im
    f_re = (nr * a_re + ni * a_im) / den
    f_im = (ni * a_re - nr * a_im) / den
    bb_re = f_re[..., None] * b_re - f_im[..., None] * b_im
    bb_im = f_re[..., None] * b_im + f_im[..., None] * b_re
    bre = _block_diag_packs(jnp.swapaxes(bb_re, 1, 2))
    bim = _block_diag_packs(jnp.swapaxes(bb_im, 1, 2))
    cre = _block_diag_packs(jnp.swapaxes(c_re, 1, 2))
    cim = _block_diag_packs(jnp.swapaxes(c_im, 1, 2))
    l1 = (ab_re.reshape(-1), ab_im.reshape(-1))
    l2 = _cmul(*l1, *l1)
    l4 = _cmul(*l2, *l2)
    row = jnp.arange(S5_SCAN_ROWS)[:, None]
    pw = []
    for s, (pr, pi) in ((1, l1), (2, l2), (4, l4)):
        pw += [jnp.where(row >= s, pr[None, :], 0.0), jnp.where(row >= s, pi[None, :], 0.0)]
    acc = [l1]
    for _ in range(S5_SCAN_ROWS - 1):
        acc.append(_cmul(*acc[-1], *l1))
    pw += [jnp.stack([a[0] for a in acc]), jnp.stack([a[1] for a in acc])]
    pw = jnp.stack(pw)

    tt = min(T, S5_TIME_TILE)
    h0r = h0[..., 0].reshape(b_, 1, S5_D_STATE)
    h0i = h0[..., 1].reshape(b_, 1, S5_D_STATE)
    pk_in, pk_st = S5_PACK * S5_GROUP, S5_PACK * S5_STATE

    def const3(b, t):
        return (0, 0, 0)

    state_spec = pl.BlockSpec((None, 1, S5_D_STATE), lambda b, t: (b, 0, 0))
    g, hr, hi = pl.pallas_call(
        _s5_kernel,
        grid=(b_, T // tt),
        in_specs=[pl.BlockSpec((None, tt, D_MODEL), lambda b, t: (b, t, 0)),
                  pl.BlockSpec((S5_N_PACKS, pk_in, pk_st), const3),
                  pl.BlockSpec((S5_N_PACKS, pk_in, pk_st), const3),
                  pl.BlockSpec((S5_N_PACKS, pk_st, pk_in), const3),
                  pl.BlockSpec((S5_N_PACKS, pk_st, pk_in), const3),
                  pl.BlockSpec((8, S5_SCAN_ROWS, S5_D_STATE), const3),
                  state_spec, state_spec,
                  pl.BlockSpec((1, D_MODEL), lambda b, t: (0, 0))],
        out_specs=[pl.BlockSpec((None, tt, D_MODEL), lambda b, t: (b, t, 0)), state_spec, state_spec],
        out_shape=[jax.ShapeDtypeStruct((b_, T, D_MODEL), jnp.float32),
                   jax.ShapeDtypeStruct((b_, 1, S5_D_STATE), jnp.float32),
                   jax.ShapeDtypeStruct((b_, 1, S5_D_STATE), jnp.float32)],
        scratch_shapes=[pltpu.VMEM((tt, S5_D_STATE), jnp.float32),
                        pltpu.VMEM((tt, S5_D_STATE), jnp.float32),
                        pltpu.VMEM((S5_SCAN_ROWS, S5_D_STATE), jnp.float32),
                        pltpu.VMEM((S5_SCAN_ROWS, S5_D_STATE), jnp.float32)],
        compiler_params=pltpu.CompilerParams(
            dimension_semantics=("parallel", "arbitrary"),
            vmem_limit_bytes=V7X_VMEM_LIMIT_BYTES),
        name="s5_scan",
    )(x, bre, bim, cre, cim, pw, h0r, h0i, d_skip.reshape(1, D_MODEL))
    out = matmul(g, w_glu_a) * jax.nn.sigmoid(matmul(g, w_glu_b))
    h_new = jnp.stack([hr.reshape(b_, S5_N_GROUPS, S5_STATE), hi.reshape(b_, S5_N_GROUPS, S5_STATE)], axis=-1)
    return out, h_new


def nsa_compress(kv, w1, w2, pe):
    b_, L = kv.shape[:2]
    span = NSA_CMP_BLOCK // NSA_CMP_STRIDE
    n_str = L // NSA_CMP_STRIDE
    n_cmp = n_str - span + 1
    chunks = kv.reshape(b_, n_str, NSA_CMP_STRIDE, NSA_N_KV, NSA_HEAD_DIM)
    blocks = jnp.concatenate([chunks[:, s:s + n_cmp] for s in range(span)], axis=2) + pe[:, None, :]
    flat = jnp.moveaxis(blocks, 3, 2).reshape(b_, n_cmp, NSA_N_KV, NSA_CMP_BLOCK * NSA_HEAD_DIM)
    return jax.nn.gelu(flat @ w1) @ w2


NSA_TQ = NSA_WBLOCK
NSA_SLC_CHUNK = 512
NSA_WIN_CHUNK = 256
NSA_ROWS = NSA_GQA * NSA_TQ
NSA_SEL_SHIFT = NSA_SEL_BLOCK.bit_length() - 1
assert 1 << NSA_SEL_SHIFT == NSA_SEL_BLOCK


def _nsa_stream_softmax_t(q_t, k_ref, vt_ref, c_lo, c_hi, chunk, mask_fn, m_ref, l_ref, acc_ref):
    scale = NSA_HEAD_DIM ** -0.5
    m_ref[...] = jnp.full(m_ref.shape, NEG_INF, jnp.float32)
    l_ref[...] = jnp.zeros(l_ref.shape, jnp.float32)
    acc_ref[...] = jnp.zeros(acc_ref.shape, jnp.float32)

    def body(c, carry):
        start = pl.multiple_of(c * chunk, chunk)
        s = jnp.dot(k_ref[pl.ds(start, chunk), :], q_t, preferred_element_type=jnp.float32) * scale
        s = jnp.where(mask_fn(start), s, NEG_INF)
        m_old = m_ref[...]
        m_new = jnp.maximum(m_old, jnp.max(s, axis=0, keepdims=True))
        alpha = jnp.exp(m_old - m_new)
        p = jnp.exp(s - m_new)
        l_ref[...] = alpha * l_ref[...] + jnp.sum(p, axis=0, keepdims=True)
        acc_ref[...] = alpha * acc_ref[...] + jnp.dot(vt_ref[:, pl.ds(start, chunk)], p.astype(jnp.bfloat16),
                                                      preferred_element_type=jnp.float32)
        m_ref[...] = m_new
        return carry

    lax.fori_loop(c_lo, c_hi, body, 0)
    return acc_ref[...] / l_ref[...]


def _nsa_prompt_t_kernel(qt_ref, gate_ref, kc_ref, vct_ref, ks_ref, vst_ref, kw_ref, vwt_ref,
                         o_ref, sel_ref, m_ref, l_ref, acc_ref, mix_ref, *, n_cmp, n_top):
    f32, bf16 = jnp.float32, jnp.bfloat16
    i = pl.program_id(2)
    t0 = i * NSA_TQ
    n_cp = kc_ref.shape[0]
    n_sel = sel_ref.shape[0]
    q_t = qt_ref[...]
    tq = t0 + (lax.broadcasted_iota(jnp.int32, (1, NSA_ROWS), 1) & (NSA_TQ - 1))

    s = jnp.dot(kc_ref[...], q_t, preferred_element_type=f32) * (NSA_HEAD_DIM ** -0.5)
    n_idx = lax.broadcasted_iota(jnp.int32, (n_cp, NSA_ROWS), 0)
    cmask = (n_idx * NSA_CMP_STRIDE + (NSA_CMP_BLOCK - 1) <= tq) & (n_idx < n_cmp)
    s = jnp.where(cmask, s, NEG_INF)
    e = jnp.exp(s - jnp.max(s, axis=0, keepdims=True))
    p = jnp.where(cmask, e / jnp.sum(e, axis=0, keepdims=True), 0.0)
    mix_ref[...] = gate_ref[0:1, :] * jnp.dot(vct_ref[...], p.astype(bf16), preferred_element_type=f32)

    psum = p[:, 0:NSA_TQ]
    for g in range(1, NSA_GQA):
        psum = psum + p[:, g * NSA_TQ:(g + 1) * NSA_TQ]
    jn = lax.broadcasted_iota(jnp.int32, (n_sel, n_cp), 0)
    nn = lax.broadcasted_iota(jnp.int32, (n_sel, n_cp), 1)
    r = NSA_SEL_BLOCK // NSA_CMP_STRIDE
    span = NSA_CMP_BLOCK // NSA_CMP_STRIDE
    pool = jnp.where((nn >= r * jn - (span - 1)) & (nn <= r * jn + (r - 1)), 1.0, 0.0).astype(bf16)
    imp = jnp.zeros((n_sel, NSA_TQ), f32)
    rest = psum
    for _ in range(3):
        part = rest.astype(bf16)
        imp = imp + jnp.dot(pool, part, preferred_element_type=f32)
        rest = rest - part.astype(f32)

    jidx = lax.broadcasted_iota(jnp.int32, (n_sel, NSA_TQ), 0)
    cur = (t0 + lax.broadcasted_iota(jnp.int32, (n_sel, NSA_TQ), 1)) >> NSA_SEL_SHIFT
    forced = (jidx == 0) | (jidx == cur) | (jidx == cur - 1)
    score = jnp.where(forced, FORCE_SCORE, jnp.where(jidx <= cur, imp, -FORCE_SCORE))
    rank = jnp.zeros((n_sel, NSA_TQ), jnp.int32)
    for k in range(n_sel):
        row = score[k:k + 1, :]
        before = (row > score) | ((row == score) & (k < jidx))
        rank = rank + jnp.where(before, 1, 0)
    sel_ref[...] = jnp.where(rank < n_top, 1.0, 0.0)

    def slc_mask(start):
        kpos = start + lax.broadcasted_iota(jnp.int32, (NSA_SLC_CHUNK, NSA_ROWS), 0)
        first = start >> NSA_SEL_SHIFT
        chosen = jnp.concatenate(
            [jnp.broadcast_to(sel_ref[pl.ds(first + j, 1), :], (NSA_SEL_BLOCK, NSA_TQ))
             for j in range(NSA_SLC_CHUNK // NSA_SEL_BLOCK)], axis=0)
        chosen = jnp.concatenate([chosen] * NSA_GQA, axis=1)
        return (chosen > 0.5) & (kpos <= tq)

    mix_ref[...] += gate_ref[1:2, :] * _nsa_stream_softmax_t(
        q_t, ks_ref, vst_ref, 0, (t0 + NSA_TQ - 1) // NSA_SLC_CHUNK + 1, NSA_SLC_CHUNK, slc_mask,
        m_ref, l_ref, acc_ref)

    def win_mask(start):
        diff = tq - (start + lax.broadcasted_iota(jnp.int32, (NSA_WIN_CHUNK, NSA_ROWS), 0))
        return (diff >= 0) & (diff < NSA_WINDOW)

    mixed = mix_ref[...] + gate_ref[2:3, :] * _nsa_stream_softmax_t(
        q_t, kw_ref, vwt_ref, jnp.maximum(t0 - NSA_WINDOW, 0) // NSA_WIN_CHUNK,
        (t0 + NSA_TQ - 1) // NSA_WIN_CHUNK + 1, NSA_WIN_CHUNK, win_mask, m_ref, l_ref, acc_ref)
    for g in range(NSA_GQA):
        o_ref[:, g * NSA_HEAD_DIM:(g + 1) * NSA_HEAD_DIM] = mixed[:, g * NSA_TQ:(g + 1) * NSA_TQ].T.astype(o_ref.dtype)


def nsa_prompt_attention_t(q, kv, kcmp, vcmp, gate):
    b_, T, _ = q.shape
    assert T % NSA_SLC_CHUNK == 0 and (T // NSA_SEL_BLOCK) % 8 == 0
    n_cmp = kcmp.shape[1]
    n_cp = T // NSA_CMP_STRIDE
    n_sel = T // NSA_SEL_BLOCK
    n_tiles = T // NSA_TQ
    bf16 = jnp.bfloat16
    pad_c = ((0, 0), (0, n_cp - n_cmp), (0, 0), (0, 0))
    kc = jnp.transpose(jnp.pad(kcmp, pad_c).astype(bf16), (0, 2, 1, 3))
    vct = jnp.transpose(jnp.pad(vcmp, pad_c).astype(bf16), (0, 2, 3, 1))
    kvb = kv.astype(bf16)
    keys = lambda comp: pl.BlockSpec((None, T, NSA_HEAD_DIM), lambda b, h, i: (b, 0, comp * NSA_N_KV + h))

    def vals_t(comp):
        v = kvb[:, :, comp * NSA_KVW:(comp + 1) * NSA_KVW].reshape(b_, T, NSA_N_KV, NSA_HEAD_DIM)
        return jnp.transpose(v, (0, 2, 3, 1))

    q_t = q.astype(bf16).reshape(b_, n_tiles, NSA_TQ, NSA_N_KV, NSA_GQA, NSA_HEAD_DIM)
    q_t = jnp.transpose(q_t, (0, 3, 1, 5, 4, 2)).reshape(b_, NSA_N_KV, n_tiles, NSA_HEAD_DIM, NSA_ROWS)
    gate_t = gate.reshape(b_, n_tiles, NSA_TQ, NSA_N_KV, NSA_GQA, 3)
    gate_t = jnp.transpose(gate_t, (0, 3, 1, 5, 4, 2)).reshape(b_, NSA_N_KV, n_tiles, 3, NSA_ROWS)

    per_head = lambda r, c: pl.BlockSpec((None, None, r, c), lambda b, h, i: (b, h, 0, 0))
    per_tile = lambda r: pl.BlockSpec((None, None, None, r, NSA_ROWS), lambda b, h, i: (b, h, i, 0, 0))
    return pl.pallas_call(
        functools.partial(_nsa_prompt_t_kernel, n_cmp=n_cmp, n_top=min(NSA_N_SELECT, n_sel)),
        grid=(b_, NSA_N_KV, n_tiles),
        in_specs=[per_tile(NSA_HEAD_DIM), per_tile(3),
                  per_head(n_cp, NSA_HEAD_DIM), per_head(NSA_HEAD_DIM, n_cp),
                  keys(2), per_head(NSA_HEAD_DIM, T), keys(4), per_head(NSA_HEAD_DIM, T)],
        out_specs=pl.BlockSpec((None, NSA_TQ, NSA_GQA * NSA_HEAD_DIM), lambda b, h, i: (b, i, h)),
        out_shape=jax.ShapeDtypeStruct((b_, T, NSA_N_HEADS * NSA_HEAD_DIM), bf16),
        scratch_shapes=[pltpu.VMEM((n_sel, NSA_TQ), jnp.float32),
                        pltpu.VMEM((1, NSA_ROWS), jnp.float32),
                        pltpu.VMEM((1, NSA_ROWS), jnp.float32),
                        pltpu.VMEM((NSA_HEAD_DIM, NSA_ROWS), jnp.float32),
                        pltpu.VMEM((NSA_HEAD_DIM, NSA_ROWS), jnp.float32)],
        compiler_params=pltpu.CompilerParams(
            dimension_semantics=("parallel", "parallel", "arbitrary"),
            vmem_limit_bytes=V7X_VMEM_LIMIT_BYTES),
        name="nsa_prompt",
    )(q_t, gate_t, kc, vct, kvb, vals_t(3), kvb, vals_t(5))


def nsa_prompt_mixer(x, w_q, w_kv, w_gate, b_gate, w_cmp1, w_cmp2, cmp_pe, w_out):
    b_, T, _ = x.shape
    q = matmul(x, w_q)
    kv = matmul(x, w_kv)
    comp = lambda c: kv[:, :, c * NSA_KVW:(c + 1) * NSA_KVW].reshape(b_, T, NSA_N_KV, NSA_HEAD_DIM)
    kcmp = nsa_compress(comp(0), w_cmp1[0], w_cmp2[0], cmp_pe[0])
    vcmp = nsa_compress(comp(1), w_cmp1[1], w_cmp2[1], cmp_pe[1])
    gate = jax.nn.sigmoid(matmul(x, w_gate) + b_gate)
    y = Proj(nsa_prompt_attention_t(q, kv, kcmp, vcmp, gate), w_out)
    rows = kv[:, :, :4 * NSA_KVW].reshape(b_, T, 4, NSA_N_KV, NSA_HEAD_DIM)
    keep = min(NSA_WINDOW, T)
    win_new = kv[:, T - keep:, 4 * NSA_KVW:].reshape(b_, keep, 2, NSA_N_KV, NSA_HEAD_DIM)
    return y, rows, win_new


NSA_ROW_SLABS = 4 * NSA_N_KV
NSA_HALF_SLABS = NSA_ROW_SLABS // 2
NSA_KVW = NSA_N_KV * NSA_HEAD_DIM
NSA_CMP_PAGES = 8
NSA_SLC_PAGES = 8


def _round_up(n, m):
    return -(-n // m) * m


def _log2(n):
    assert n > 0 and n & (n - 1) == 0
    return n.bit_length() - 1


def _nsa_compress_kernel(pt_ref, *refs):
    del pt_ref
    pages = refs[:NSA_CMP_PAGES]
    w1_ref, pe_ref, a_ref, b_ref = refs[NSA_CMP_PAGES:]
    page_rows = pages[0].shape[0]
    per_page = page_rows // NSA_CMP_STRIDE
    rows = NSA_CMP_PAGES * NSA_N_KV * per_page
    half = NSA_CMP_STRIDE * NSA_HEAD_DIM
    slabs = [jnp.swapaxes(pg[...], 0, 1) for pg in pages]
    for comp in range(2):
        acc_a = jnp.zeros((rows, NSA_HEAD_DIM), jnp.float32)
        acc_b = jnp.zeros((rows, NSA_HEAD_DIM), jnp.float32)
        by_row = [jnp.swapaxes(s[comp * NSA_N_KV + h].reshape(per_page, NSA_CMP_STRIDE, NSA_HEAD_DIM), 0, 1)
                  for s in slabs for h in range(NSA_N_KV)]
        for j0 in range(0, NSA_CMP_STRIDE, 2):
            xa, xb = [], []
            for j in (j0, j0 + 1):
                x = jnp.concatenate([t[j] for t in by_row], axis=0)
                xa.append((x + pe_ref[comp, j:j + 1, :]).astype(jnp.bfloat16))
                xb.append((x + pe_ref[comp, NSA_CMP_STRIDE + j:NSA_CMP_STRIDE + j + 1, :]).astype(jnp.bfloat16))
            lo = j0 * NSA_HEAD_DIM
            acc_a = acc_a + jnp.dot(jnp.concatenate(xa, axis=1), w1_ref[comp, lo:lo + 2 * NSA_HEAD_DIM, :],
                                    preferred_element_type=jnp.float32)
            acc_b = acc_b + jnp.dot(jnp.concatenate(xb, axis=1),
                                    w1_ref[comp, half + lo:half + lo + 2 * NSA_HEAD_DIM, :],
                                    preferred_element_type=jnp.float32)
        shape = (NSA_CMP_PAGES, NSA_N_KV, per_page, NSA_HEAD_DIM)
        a_ref[comp] = acc_a.reshape(shape)
        b_ref[comp] = acc_b.reshape(shape)


def nsa_decode_compress(cache, page_ids, new_rows, w_cmp1, w_cmp2, cmp_pe):
    b_, n_pages = page_ids.shape
    page = cache.shape[1]
    T = new_rows.shape[1]
    pos0 = n_pages * page
    lp = _round_up(pos0 + T, NSA_SEL_BLOCK)
    n_cmp = lp // NSA_CMP_STRIDE - (NSA_CMP_BLOCK // NSA_CMP_STRIDE - 1)
    per_page = page // NSA_CMP_STRIDE
    assert n_pages % NSA_CMP_PAGES == 0 and NSA_CMP_BLOCK == 2 * NSA_CMP_STRIDE
    w1 = w_cmp1.astype(jnp.bfloat16)

    def page_spec(k):
        return pl.BlockSpec((None, page, NSA_HALF_SLABS, NSA_HEAD_DIM),
                            lambda b, s, pt: (pt[b, NSA_CMP_PAGES * s + k], 0, 0, 0))

    ab_shape = jax.ShapeDtypeStruct((b_, 2, n_pages, NSA_N_KV, per_page, NSA_HEAD_DIM), jnp.float32)
    ab_spec = pl.BlockSpec((None, 2, NSA_CMP_PAGES, NSA_N_KV, per_page, NSA_HEAD_DIM),
                           lambda b, s, pt: (b, 0, s, 0, 0, 0))
    part_a, part_b = pl.pallas_call(
        _nsa_compress_kernel,
        grid_spec=pltpu.PrefetchScalarGridSpec(
            num_scalar_prefetch=1,
            grid=(b_, n_pages // NSA_CMP_PAGES),
            in_specs=[page_spec(k) for k in range(NSA_CMP_PAGES)]
            + [pl.BlockSpec(w1.shape, lambda b, s, pt: (0, 0, 0)),
               pl.BlockSpec(cmp_pe.shape, lambda b, s, pt: (0, 0, 0))],
            out_specs=[ab_spec, ab_spec]),
        out_shape=[ab_shape, ab_shape],
        compiler_params=pltpu.CompilerParams(
            dimension_semantics=("parallel", "arbitrary"),
            vmem_limit_bytes=V7X_VMEM_LIMIT_BYTES),
        name="nsa_compress_pages",
    )(page_ids, *([cache] * NSA_CMP_PAGES), w1, cmp_pe)

    def strides(t):
        return jnp.transpose(t, (0, 1, 3, 2, 4, 5)).reshape(b_, 2, NSA_N_KV, n_pages * per_page, NSA_HEAD_DIM)

    n_tail = (lp - pos0) // NSA_CMP_STRIDE
    tail = jnp.pad(new_rows, ((0, 0), (0, lp - pos0 - T), (0, 0), (0, 0), (0, 0)))
    tail = jnp.transpose(tail.reshape(b_, n_tail, NSA_CMP_STRIDE, 2, NSA_N_KV, NSA_HEAD_DIM), (0, 3, 4, 1, 2, 5))
    w1s = w_cmp1.reshape(2, 2, NSA_CMP_STRIDE, NSA_HEAD_DIM, -1)
    pes = cmp_pe.reshape(2, 2, NSA_CMP_STRIDE, NSA_HEAD_DIM)
    tail_a = jnp.einsum('bchsjd,cjdk->bchsk', tail + pes[None, :, 0, None, None], w1s[:, 0])
    tail_b = jnp.einsum('bchsjd,cjdk->bchsk', tail + pes[None, :, 1, None, None], w1s[:, 1])
    full_a = jnp.concatenate([strides(part_a), tail_a], axis=3)
    full_b = jnp.concatenate([strides(part_b), tail_b], axis=3)
    hidden = jax.nn.gelu(full_a[:, :, :, :n_cmp] + full_b[:, :, :, 1:n_cmp + 1])
    out = jnp.einsum('bchnk,ckd->bcnhd', hidden, w_cmp2).reshape(b_, 2, n_cmp, NSA_KVW)
    return out[:, 0], out[:, 1]


def _nsa_softmax_rows(s, mask):
    s = jnp.where(mask, s, NEG_INF)
    e = jnp.exp(s - jnp.max(s, axis=-1, keepdims=True))
    return e / jnp.sum(e, axis=-1, keepdims=True)


def _nsa_decode_select_kernel(q_ref, kc_ref, vc_ref, wk_ref, wv_ref, ocmp_ref, owin_ref, sel_ref,
                              score_ref, rank_ref, *, n_cmp, n_sel, n_top, n_win, w_buf, pos0, t_new):
    f32, bf16 = jnp.float32, jnp.bfloat16
    nt = (((1,), (1,)), ((), ()))
    scale = NSA_HEAD_DIM ** -0.5
    rows = q_ref.shape[0]
    per_head = NSA_GQA * t_new
    q = q_ref[...]
    tq = pos0 + (lax.broadcasted_iota(jnp.int32, (rows, 1), 0) & (t_new - 1))

    def heads_out(o_ref, p, v_ref):
        for h in range(NSA_N_KV):
            o_ref[h * per_head:(h + 1) * per_head, :] = jnp.dot(
                p[h * per_head:(h + 1) * per_head].astype(bf16),
                v_ref[:, h * NSA_HEAD_DIM:(h + 1) * NSA_HEAD_DIM], preferred_element_type=f32)

    n_cp = kc_ref.shape[0]
    s = lax.dot_general(q, kc_ref[...], nt, preferred_element_type=f32) * scale
    n_idx = lax.broadcasted_iota(jnp.int32, (rows, n_cp), 1)
    cmask = (n_idx * NSA_CMP_STRIDE + (NSA_CMP_BLOCK - 1) <= tq) & (n_idx < n_cmp)
    p = jnp.where(cmask, _nsa_softmax_rows(s, cmask), 0.0)
    heads_out(ocmp_ref, p, vc_ref)

    psum = jnp.concatenate(
        [sum(p[h * per_head + g * t_new:h * per_head + (g + 1) * t_new] for g in range(NSA_GQA))
         for h in range(NSA_N_KV)], axis=0)
    n_sp = sel_ref.shape[1]
    cols = NSA_N_KV * t_new
    jn = lax.broadcasted_iota(jnp.int32, (n_sp, n_cp), 0)
    nn = lax.broadcasted_iota(jnp.int32, (n_sp, n_cp), 1)
    r = NSA_SEL_BLOCK // NSA_CMP_STRIDE
    span = NSA_CMP_BLOCK // NSA_CMP_STRIDE
    pool = jnp.where((nn >= r * jn - (span - 1)) & (nn <= r * jn + (r - 1)), 1.0, 0.0).astype(bf16)
    imp = jnp.zeros((n_sp, cols), f32)
    rest = psum
    for _ in range(3):
        part = rest.astype(bf16)
        imp = imp + lax.dot_general(pool, part, nt, preferred_element_type=f32)
        rest = rest - part.astype(f32)
    jidx = lax.broadcasted_iota(jnp.int32, (n_sp, cols), 0)
    cur = (pos0 + (lax.broadcasted_iota(jnp.int32, (n_sp, cols), 1) & (t_new - 1))) >> NSA_SEL_SHIFT
    forced = (jidx == 0) | (jidx == cur) | (jidx == cur - 1)
    score = jnp.where(forced, FORCE_SCORE, jnp.where(jidx <= cur, imp, -FORCE_SCORE))
    score_ref[...] = jnp.where(jidx < n_sel, score, -2.0 * FORCE_SCORE)
    rank_ref[...] = jnp.zeros(rank_ref.shape, jnp.int32)

    def rank_body(k, carry):
        row = score_ref[pl.ds(k, 1), :]
        sc = score_ref[...]
        before = (row > sc) | ((row == sc) & (k < jidx))
        rank_ref[...] = rank_ref[...] + jnp.where(before, 1, 0)
        return carry

    lax.fori_loop(0, n_sel, rank_body, 0)
    sel_t = jnp.where((rank_ref[...] < n_top) & (jidx < n_sel), 1.0, 0.0).astype(bf16)
    ri = lax.broadcasted_iota(jnp.int32, (rows, cols), 0)
    ci = lax.broadcasted_iota(jnp.int32, (rows, cols), 1)
    same = (((ri >> _log2(per_head)) == (ci >> _log2(t_new)))
            & ((ri & (t_new - 1)) == (ci & (t_new - 1))))
    spread = jnp.where(same, 1.0, 0.0).astype(bf16)
    sel_ref[...] = lax.dot_general(spread, sel_t, nt, preferred_element_type=f32).astype(bf16)

    s = lax.dot_general(q, wk_ref[...], nt, preferred_element_type=f32) * scale
    kidx = lax.broadcasted_iota(jnp.int32, (rows, wk_ref.shape[0]), 1)
    diff = tq - (pos0 - w_buf + kidx)
    wmask = (diff >= 0) & (diff < NSA_WINDOW) & (kidx < n_win) & (pos0 - w_buf + kidx >= 0)
    heads_out(owin_ref, _nsa_softmax_rows(s, wmask), wv_ref)


def _nsa_decode_slc_kernel(pt_ref, q_ref, sel_ref, new_ref, *refs, pos0, t_new):
    del pt_ref
    pages = refs[:NSA_SLC_PAGES]
    o_ref, m_ref, l_ref, acc_ref = refs[NSA_SLC_PAGES:]
    f32, bf16 = jnp.float32, jnp.bfloat16
    nt = (((1,), (1,)), ((), ()))
    step = pl.program_id(1)
    rows = q_ref.shape[0]
    page = pages[0].shape[0]
    per_head = NSA_GQA * t_new
    n_sp = sel_ref.shape[1]
    q = q_ref[...]
    tq = pos0 + (lax.broadcasted_iota(jnp.int32, (rows, 1), 0) & (t_new - 1))
    row_head = lax.broadcasted_iota(jnp.int32, (rows, page), 0) >> _log2(per_head)
    lane = lax.broadcasted_iota(jnp.int32, (rows, page), 1)
    blocks_per_page = page // NSA_SEL_BLOCK

    def attend(pg_ref, page_index):
        slabs = jnp.swapaxes(pg_ref[...], 0, 1)
        slab = lambda c: slabs[c].astype(bf16)
        kp = jnp.concatenate([slab(h) for h in range(NSA_N_KV)], axis=1)
        v_heads = jnp.concatenate([slab(NSA_N_KV + h) for h in range(NSA_N_KV)], axis=0)
        s = lax.dot_general(q, kp, nt, preferred_element_type=f32) * (NSA_HEAD_DIM ** -0.5)
        jrow = lax.broadcasted_iota(jnp.int32, (n_sp, page), 0)
        jcol = page_index * blocks_per_page + (lax.broadcasted_iota(jnp.int32, (n_sp, page), 1) >> NSA_SEL_SHIFT)
        expand = jnp.where(jrow == jcol, 1.0, 0.0).astype(bf16)
        chosen = jnp.dot(sel_ref[...], expand, preferred_element_type=f32) > 0.5
        s = jnp.where(chosen & (page_index * page + lane <= tq), s, NEG_INF)
        m_old = m_ref[...]
        m_new = jnp.maximum(m_old, jnp.max(s, axis=-1, keepdims=True))
        alpha = jnp.exp(m_old - m_new)
        p = jnp.exp(s - m_new)
        l_ref[...] = alpha * l_ref[...] + jnp.sum(p, axis=-1, keepdims=True)
        p_heads = jnp.concatenate([jnp.where(row_head == h, p, 0.0) for h in range(NSA_N_KV)], axis=1).astype(bf16)
        acc_ref[...] = alpha * acc_ref[...] + jnp.dot(p_heads, v_heads, preferred_element_type=f32)
        m_ref[...] = m_new

    @pl.when(step == 0)
    def _():
        m_ref[...] = jnp.full(m_ref.shape, NEG_INF, f32)
        l_ref[...] = jnp.zeros(l_ref.shape, f32)
        acc_ref[...] = jnp.zeros(acc_ref.shape, f32)
        attend(new_ref, pos0 // page)

    for k, pg in enumerate(pages):
        attend(pg, step * NSA_SLC_PAGES + k)

    @pl.when(step == pl.num_programs(1) - 1)
    def _():
        o_ref[...] = acc_ref[...] / l_ref[...]


def nsa_decode_attention(q, kv, cache, page_ids, win_buf, w_cmp1, w_cmp2, cmp_pe):
    b_, T, _ = q.shape
    n_pages = page_ids.shape[1]
    page = cache.shape[1]
    pos0 = n_pages * page
    w_buf = win_buf.shape[1]
    assert T & (T - 1) == 0 and T <= NSA_SEL_BLOCK and pos0 % NSA_SEL_BLOCK == 0 and page % NSA_SEL_BLOCK == 0
    assert n_pages % NSA_SLC_PAGES == 0
    bf16 = jnp.bfloat16
    lp = _round_up(pos0 + T, NSA_SEL_BLOCK)
    n_sel = lp // NSA_SEL_BLOCK
    n_sp = _round_up(n_sel, 128)
    kv6 = kv.reshape(b_, T, 6, NSA_N_KV, NSA_HEAD_DIM)
    kc, vc = nsa_decode_compress(cache, page_ids, kv6[:, :, 0:2], w_cmp1, w_cmp2, cmp_pe)
    n_cmp = kc.shape[1]
    n_cp = _round_up(n_cmp, 128)
    pad_c = ((0, 0), (0, n_cp - n_cmp), (0, 0))
    kc, vc = jnp.pad(kc, pad_c).astype(bf16), jnp.pad(vc, pad_c).astype(bf16)

    rows = NSA_N_HEADS * T
    q5 = jnp.transpose(q.reshape(b_, T, NSA_N_KV, NSA_GQA, NSA_HEAD_DIM), (0, 2, 3, 1, 4))
    q_blk = jnp.einsum('bhgtd,hk->bhgtkd', q5, jnp.eye(NSA_N_KV, dtype=q.dtype))
    q_blk = q_blk.reshape(b_, rows, NSA_KVW).astype(bf16)

    n_win = w_buf + T
    n_wp = _round_up(n_win, 128)
    wk = jnp.concatenate([win_buf, kv6[:, :, 4:6]], axis=1)
    wk = jnp.pad(wk, ((0, 0), (0, n_wp - n_win), (0, 0), (0, 0), (0, 0))).astype(bf16)
    wkk, wkv = wk[:, :, 0].reshape(b_, n_wp, NSA_KVW), wk[:, :, 1].reshape(b_, n_wp, NSA_KVW)

    per_b = lambda n, w: pl.BlockSpec((None, n, w), lambda b: (b, 0, 0))
    o_shape = jax.ShapeDtypeStruct((b_, rows, NSA_HEAD_DIM), jnp.float32)
    o_cmp, o_win, sel = pl.pallas_call(
        functools.partial(_nsa_decode_select_kernel, n_cmp=n_cmp, n_sel=n_sel, n_top=min(NSA_N_SELECT, n_sel),
                          n_win=n_win, w_buf=w_buf, pos0=pos0, t_new=T),
        grid=(b_,),
        in_specs=[per_b(rows, NSA_KVW), per_b(n_cp, NSA_KVW), per_b(n_cp, NSA_KVW),
                  per_b(n_wp, NSA_KVW), per_b(n_wp, NSA_KVW)],
        out_specs=[per_b(rows, NSA_HEAD_DIM), per_b(rows, NSA_HEAD_DIM), per_b(rows, n_sp)],
        out_shape=[o_shape, o_shape, jax.ShapeDtypeStruct((b_, rows, n_sp), bf16)],
        scratch_shapes=[pltpu.VMEM((n_sp, NSA_N_KV * T), jnp.float32),
                        pltpu.VMEM((n_sp, NSA_N_KV * T), jnp.int32)],
        compiler_params=pltpu.CompilerParams(
            dimension_semantics=("parallel",), vmem_limit_bytes=V7X_VMEM_LIMIT_BYTES),
        name="nsa_decode_select",
    )(q_blk, kc, vc, wkk, wkv)

    new_slc = jnp.pad(kv6[:, :, 2:4].reshape(b_, T, NSA_HALF_SLABS, NSA_HEAD_DIM),
                      ((0, 0), (0, page - T), (0, 0), (0, 0)))
    half_page = (None, page, NSA_HALF_SLABS, NSA_HEAD_DIM)

    def page_spec(k):
        return pl.BlockSpec(half_page, lambda b, s, pt: (pt[b, NSA_SLC_PAGES * s + k], 0, 1, 0))

    bs = lambda n, w: pl.BlockSpec((None, n, w), lambda b, s, pt: (b, 0, 0))
    o_slc = pl.pallas_call(
        functools.partial(_nsa_decode_slc_kernel, pos0=pos0, t_new=T),
        grid_spec=pltpu.PrefetchScalarGridSpec(
            num_scalar_prefetch=1,
            grid=(b_, n_pages // NSA_SLC_PAGES),
            in_specs=[bs(rows, NSA_KVW), bs(rows, n_sp), pl.BlockSpec(half_page, lambda b, s, pt: (b, 0, 0, 0))]
            + [page_spec(k) for k in range(NSA_SLC_PAGES)],
            out_specs=bs(rows, NSA_HEAD_DIM),
            scratch_shapes=[pltpu.VMEM((rows, 1), jnp.float32),
                            pltpu.VMEM((rows, 1), jnp.float32),
                            pltpu.VMEM((rows, NSA_HEAD_DIM), jnp.float32)]),
        out_shape=o_shape,
        compiler_params=pltpu.CompilerParams(
            dimension_semantics=("parallel", "arbitrary"), vmem_limit_bytes=V7X_VMEM_LIMIT_BYTES),
        name="nsa_decode_slc",
    )(page_ids, q_blk, sel, new_slc, *([cache] * NSA_SLC_PAGES))

    def token_major(o):
        o = o.reshape(b_, NSA_N_KV, NSA_GQA, T, NSA_HEAD_DIM)
        return jnp.transpose(o, (0, 3, 1, 2, 4)).reshape(b_, T, NSA_N_HEADS * NSA_HEAD_DIM)

    return token_major(o_cmp), token_major(o_slc), token_major(o_win)


def nsa_decode_mixer(x, cache, page_ids, win_buf, w_q, w_kv, w_gate, b_gate, w_cmp1, w_cmp2, cmp_pe, w_out):
    b_, T, _ = x.shape
    q = matmul(x, w_q)
    kv = matmul(x, w_kv)
    kv6 = kv.reshape(b_, T, 6, NSA_N_KV, NSA_HEAD_DIM)
    o_cmp, o_slc, o_win = nsa_decode_attention(q, kv, cache, page_ids, win_buf, w_cmp1, w_cmp2, cmp_pe)
    gate = jax.nn.sigmoid(matmul(x, w_gate) + b_gate).reshape(b_, T, NSA_N_HEADS, 3)

    def heads(t):
        return t.reshape(b_, T, NSA_N_HEADS, NSA_HEAD_DIM)

    o = gate[..., 0:1] * heads(o_cmp) + gate[..., 1:2] * heads(o_slc) + gate[..., 2:3] * heads(o_win)
    y = Proj(o.reshape(b_, T, NSA_N_HEADS * NSA_HEAD_DIM), w_out)
    win_new = jnp.concatenate([win_buf, kv6[:, :, 4:6]], axis=1)[:, -win_buf.shape[1]:]
    return y, kv6[:, :, :4], win_new


FFN_ROW_TILE = 1024
FFN_COL_TILE = 512
FFN_X_HALO = 16


def _ffn_up_kernel(x_ref, xh_ref, hist_ref, wa_ref, wg_ref, cw_ref, cb_ref, h_ref, tail_ref, ext_ref,
                   *, width, tiles_per_seq):
    f32 = jnp.float32
    rows = x_ref.shape[0]
    x = x_ref[...]
    a = jnp.dot(x, wa_ref[...], preferred_element_type=f32)
    g = jnp.dot(x, wg_ref[...], preferred_element_type=f32)
    before = jnp.dot(xh_ref[...], wa_ref[...], preferred_element_type=f32)[FFN_X_HALO - CONV_HALO:]
    starts_seq = pl.program_id(1) % tiles_per_seq == 0
    ext_ref[0:CONV_HALO, :] = jnp.where(starts_seq, hist_ref[...], before)
    ext_ref[CONV_HALO:, :] = a
    acc = cb_ref[...]
    for k in range(width):
        acc = acc + cw_ref[k:k + 1, :] * ext_ref[pl.ds(CONV_HALO - (width - 1 - k), rows), :]
    h_ref[...] = (jax.nn.gelu(acc) * g).astype(h_ref.dtype)
    tail_ref[...] = a[rows - CONV_HALO:]


def ffn_up_fused(x, hist, w_up, conv_w, conv_b):
    b_, T, K = x.shape
    stack, s = w_up
    width = conv_w.shape[0]
    tm, tn = FFN_ROW_TILE, FFN_COL_TILE
    assert T % tm == 0 and FFN_DIM % tn == 0 and width <= CONV_HALO + 1 and stack.shape[2] == 2 * FFN_DIM
    nj = FFN_DIM // tn
    tiles_per_seq = T // tm
    halo_blocks = tm // FFN_X_HALO
    x2 = x.astype(jnp.bfloat16).reshape(b_ * T, K)
    hist8 = jnp.pad(hist, ((0, 0), (CONV_HALO - (width - 1), 0), (0, 0)))
    per_seq = pl.BlockSpec((None, CONV_HALO, tn), lambda j, i: (i // tiles_per_seq, 0, j))
    h, tail = pl.pallas_call(
        functools.partial(_ffn_up_kernel, width=width, tiles_per_seq=tiles_per_seq),
        grid=(nj, b_ * tiles_per_seq),
        in_specs=[pl.BlockSpec((tm, K), lambda j, i: (i, 0)),
                  pl.BlockSpec((FFN_X_HALO, K), lambda j, i: (jnp.maximum(i * halo_blocks - 1, 0), 0)),
                  per_seq,
                  pl.BlockSpec((None, K, tn), lambda j, i: (s, 0, j)),
                  pl.BlockSpec((None, K, tn), lambda j, i: (s, 0, nj + j)),
                  pl.BlockSpec((width, tn), lambda j, i: (0, j)),
                  pl.BlockSpec((1, tn), lambda j, i: (0, j))],
        out_specs=[pl.BlockSpec((tm, tn), lambda j, i: (i, j)), per_seq],
        out_shape=[jax.ShapeDtypeStruct((b_ * T, FFN_DIM), jnp.bfloat16),
                   jax.ShapeDtypeStruct((b_, CONV_HALO, FFN_DIM), jnp.float32)],
        scratch_shapes=[pltpu.VMEM((CONV_HALO + tm, tn), jnp.float32)],
        compiler_params=pltpu.CompilerParams(
            dimension_semantics=("parallel", "arbitrary"),
            vmem_limit_bytes=V7X_VMEM_LIMIT_BYTES),
        name="ffn_up",
    )(x2, x2, hist8, stack, stack, conv_w, conv_b.reshape(1, FFN_DIM))
    return h.reshape(b_, T, FFN_DIM), tail


def conv_ffn(x, hist, w_up, conv_w, conv_b, w_down):
    keep = hist.shape[1]
    if x.shape[1] % FFN_ROW_TILE == 0:
        h, tail = ffn_up_fused(x, hist, w_up, conv_w, conv_b)
        return Proj(h, w_down), tail[:, -keep:]
    ag = matmul(x, w_up)
    h = conv_act(ag, 0, FFN_DIM, hist, conv_w, conv_b, "gelu_gate", gate_col0=FFN_DIM, out_dtype=jnp.bfloat16)
    return Proj(h, w_down), conv_tail(hist, ag, 0, FFN_DIM)


def kernel(x_prompt, x_sample, cache_nsa, state_nsa_win, state_ssd, state_ssd_conv, state_mlstm_c,
           state_mlstm_n, state_mlstm_m, state_mlstm_conv, state_s5, state_ffn_conv, page_table,
           ln_g, ln_b, ffn_w_up, ffn_conv_w, ffn_conv_b, ffn_w_down,
           ssd_w_in, ssd_conv_w, ssd_conv_b, ssd_dt_bias, ssd_a_log, ssd_d, ssd_norm_g, ssd_w_out,
           mlstm_w_up, mlstm_conv_w, mlstm_conv_b, mlstm_w_q, mlstm_w_k, mlstm_w_v, mlstm_w_if,
           mlstm_b_if, mlstm_skip, mlstm_norm_g, mlstm_w_down,
           s5_a_re, s5_a_im, s5_log_dt, s5_b_re, s5_b_im, s5_c_re, s5_c_im, s5_d, s5_w_glu_a, s5_w_glu_b,
           nsa_w_q, nsa_w_kv, nsa_w_gate, nsa_b_gate, nsa_w_cmp1, nsa_w_cmp2, nsa_cmp_pe, nsa_w_out):

    def bf16_stack(w):
        return w.astype(jnp.bfloat16)

    ffn_w_up, ffn_w_down = bf16_stack(ffn_w_up), bf16_stack(ffn_w_down)
    ssd_w_in, ssd_w_out = bf16_stack(ssd_w_in), bf16_stack(ssd_w_out)
    mlstm_w_up, mlstm_w_down = bf16_stack(mlstm_w_up), bf16_stack(mlstm_w_down)
    mlstm_w_if = bf16_stack(mlstm_w_if).reshape(-1, MLSTM_D_INNER, 2 * MLSTM_N_HEADS)
    s5_w_glu_a, s5_w_glu_b = bf16_stack(s5_w_glu_a), bf16_stack(s5_w_glu_b)
    nsa_w_q, nsa_w_kv, nsa_w_out = bf16_stack(nsa_w_q), bf16_stack(nsa_w_kv), bf16_stack(nsa_w_out)
    nsa_w_gate = bf16_stack(nsa_w_gate)

    def trunk(x, sample):
        b_, T, _ = x.shape
        dt_ = x.dtype
        pos0 = PAST_LEN if sample else 0
        o_nsa, o_win, o_ssd, o_ssdc, o_mc, o_mn, o_mm, o_mconv, o_s5, o_ffn = ([] for _ in range(10))
        xb = x.astype(jnp.bfloat16)

        def residual_norm(x, y, g, b):
            if isinstance(y, Proj):
                return matmul_residual_ln(y.h, y.w, x, g, b)
            out = layer_norm(DEEPNORM_ALPHA * x + y, g, b)
            return out, out.astype(jnp.bfloat16)
        for i in range(DEPTH):
            kind, j = i % N_MIXERS, i // N_MIXERS
            if kind == 0:
                hist = state_ssd_conv[j] if sample else jnp.zeros((b_, SSD_CONV_W - 1, SSD_CONV_DIM), dt_)
                h0 = state_ssd[j] if sample else jnp.zeros((b_, SSD_N_HEADS, SSD_HEADDIM, SSD_D_STATE), dt_)
                y, hist_new, h_new = ssd_mixer(xb, hist, h0, (ssd_w_in, j), ssd_conv_w[j], ssd_conv_b[j],
                                               ssd_dt_bias[j], ssd_a_log[j], ssd_d[j], ssd_norm_g[j],
                                               (ssd_w_out, j))
                o_ssd.append(h_new)
                o_ssdc.append(hist_new)
            elif kind == 1:
                hist = state_mlstm_conv[j] if sample else jnp.zeros((b_, MLSTM_CONV_W - 1, MLSTM_D_INNER), dt_)
                c0 = state_mlstm_c[j] if sample else jnp.zeros((b_, MLSTM_N_HEADS, MLSTM_HEAD_DIM, MLSTM_HEAD_DIM), dt_)
                n0 = state_mlstm_n[j] if sample else jnp.zeros((b_, MLSTM_N_HEADS, MLSTM_HEAD_DIM), dt_)
                m0 = state_mlstm_m[j] if sample else jnp.zeros((b_, MLSTM_N_HEADS), dt_)
                y, hist_new, c, n, m = mlstm_mixer(xb, hist, c0, n0, m0, (mlstm_w_up, j), mlstm_conv_w[j],
                                                   mlstm_conv_b[j], mlstm_w_q[j], mlstm_w_k[j], mlstm_w_v[j],
                                                   [(mlstm_w_if, 3 * j + part) for part in range(3)],
                                                   mlstm_b_if[j], mlstm_skip[j],
                                                   mlstm_norm_g[j], (mlstm_w_down, j))
                o_mc.append(c)
                o_mn.append(n)
                o_mm.append(m)
                o_mconv.append(hist_new)
            elif kind == 2:
                h0 = state_s5[j] if sample else jnp.zeros((b_, S5_N_GROUPS, S5_STATE, 2), dt_)
                y, h_new = s5_mixer(x, h0, s5_a_re[j], s5_a_im[j], s5_log_dt[j], s5_b_re[j], s5_b_im[j],
                                    s5_c_re[j], s5_c_im[j], s5_d[j], (s5_w_glu_a, j), (s5_w_glu_b, j))
                o_s5.append(h_new)
            else:
                nsa_w = ((nsa_w_q, j), (nsa_w_kv, j), (nsa_w_gate, j), nsa_b_gate[j], nsa_w_cmp1[j], nsa_w_cmp2[j],
                         nsa_cmp_pe[j], (nsa_w_out, j))
                if sample:
                    n_pool, page = cache_nsa.shape[1:3]
                    assert pos0 == page_table.shape[1] * page
                    y, rows, win_new = nsa_decode_mixer(
                        xb, cache_nsa.reshape(-1, page, NSA_ROW_SLABS, NSA_HEAD_DIM), page_table + j * n_pool,
                        state_nsa_win[j], *nsa_w)
                else:
                    assert pos0 == 0
                    y, rows, win_new = nsa_prompt_mixer(xb, *nsa_w)
                o_nsa.append(rows)
                o_win.append(win_new)
            x, xb = residual_norm(x, y, ln_g[i, 0], ln_b[i, 0])
            fhist = state_ffn_conv[i] if sample else jnp.zeros((b_, FFN_CONV_W - 1, FFN_DIM), dt_)
            y, fhist_new = conv_ffn(xb, fhist, (ffn_w_up, i), ffn_conv_w[i], ffn_conv_b[i], (ffn_w_down, i))
            o_ffn.append(fhist_new)
            x, xb = residual_norm(x, y, ln_g[i, 1], ln_b[i, 1])
        st = jnp.stack
        return (x, st(o_nsa), st(o_win), st(o_ssd), st(o_ssdc), st(o_mc), st(o_mn), st(o_mm),
                st(o_mconv), st(o_s5), st(o_ffn))

    (y_prompt, nsa_p, win_p, ssd_p, ssdc_p, mc_p, mn_p, mm_p, mconv_p, s5_p, ffn_p) = trunk(x_prompt, False)
    (y_sample, nsa_s, win_s, ssd_s, ssdc_s, mc_s, mn_s, mm_s, mconv_s, s5_s, ffn_s) = trunk(x_sample, True)
    return (y_prompt, y_sample, nsa_p, nsa_s, win_p, win_s, ssd_p, ssd_s, ssdc_p, ssdc_s, mc_p, mc_s,
            mn_p, mn_s, mm_p, mm_s, mconv_p, mconv_s, s5_p, s5_s, ffn_p, ffn_s)
```
